```python
import math
import jax
import jax.numpy as jnp
from jax import lax
import numpy as np

D_MODEL = 1024
BATCH = 8
SEQ = 2048
DEPTH = 1

CHUNK = 64
EPS = 1e-6
ROPE_THETA = 10000.0
A_HEADS = 8
A_DK = 64
A_DV = 64
CONV_K = 4
B_HEADS = 8
B_KV_HEADS = 2
B_HD = 64
IDX_HEADS = 8
IDX_HD = 64
TOPK_MAX = 256
N_GROUPS = 4
EXPERTS_PER_GROUP = 4
N_EXPERTS = N_GROUPS * EXPERTS_PER_GROUP
TOPK_IN_GROUP = 2
D_EXPERT = 256

A_QK_W = A_HEADS * A_DK
A_V_W = A_HEADS * A_DV
B_Q_W = B_HEADS * B_HD
B_KV_W = B_KV_HEADS * B_HD
CONV_W = 2 * A_QK_W + A_V_W
IN_SPLITS = (
    ("a_q", A_QK_W), ("a_k", A_QK_W), ("a_v", A_V_W), ("a_z", A_V_W),
    ("a_beta", A_HEADS), ("a_alpha", A_HEADS),
    ("b_q", B_Q_W), ("b_k", B_KV_W), ("b_v", B_KV_W),
    ("i_q", IDX_HEADS * IDX_HD), ("i_k", IDX_HD), ("i_w", IDX_HEADS),
    ("gate_a", D_MODEL), ("gate_b", D_MODEL),
)
D_IN = sum(n for _, n in IN_SPLITS)

kernel_name = "hybrid_gdn_dsa_hmoe_block"


def rmsnorm(x, g):
    xf = x.astype(jnp.float32)
    y = xf * lax.rsqrt(jnp.mean(xf * xf, axis=-1, keepdims=True) + EPS)
    return (y * g.astype(jnp.float32)).astype(x.dtype)


def l2norm(x):
    xf = x.astype(jnp.float32)
    return xf * lax.rsqrt(jnp.sum(xf * xf, axis=-1, keepdims=True) + EPS)


def split_cols(z):
    parts, off = {}, 0
    for name, n in IN_SPLITS:
        parts[name] = z[..., off:off + n]
        off += n
    return parts


def rope(x, pos):
    half = x.shape[-1] // 2
    inv = ROPE_THETA ** (-jnp.arange(half, dtype=jnp.float32) / half)
    ang = pos.astype(jnp.float32)[..., None] * inv
    cos = jnp.cos(ang)[:, :, None, :]
    sin = jnp.sin(ang)[:, :, None, :]
    xf = x.astype(jnp.float32)
    x1, x2 = xf[..., :half], xf[..., half:]
    return jnp.concatenate([x1 * cos - x2 * sin, x2 * cos + x1 * sin], axis=-1).astype(x.dtype)


def causal_dwconv(x, w):
    c = x.shape[-1]
    return lax.conv_general_dilated(
        x, w[:, None, :].astype(x.dtype), window_strides=(1,), padding=[(CONV_K - 1, 0)],
        dimension_numbers=("NWC", "WIO", "NWC"), feature_group_count=c)


def gated_delta_rule(q, k, v, g, beta):
    b, s, h, dk = q.shape
    dv = v.shape[-1]
    n = s // CHUNK

    def chunks(t):
        return jnp.moveaxis(t.reshape((b, n, CHUNK, h) + t.shape[3:]), 3, 1)

    qc, kc, vc, gc, bc = chunks(q), chunks(k), chunks(v), chunks(g), chunks(beta)
    G = jnp.cumsum(gc, axis=-1)
    idx = jnp.arange(CHUNK)
    incl = idx[:, None] >= idx[None, :]
    strict = idx[:, None] > idx[None, :]
    decay = jnp.exp(jnp.where(incl, G[..., :, None] - G[..., None, :], -jnp.inf))
    kk = jnp.einsum("bhnid,bhnjd->bhnij", kc, kc)
    a_mat = jnp.where(strict, bc[..., :, None] * kk * decay, 0.0) + jnp.eye(CHUNK, dtype=jnp.float32)
    rhs = jnp.concatenate([vc * bc[..., None], kc * (bc * jnp.exp(G))[..., None]], axis=-1)
    sol = lax.linalg.triangular_solve(a_mat, rhs, left_side=True, lower=True, unit_diagonal=True)
    u, w = sol[..., :dv], sol[..., dv:]
    qk = jnp.einsum("bhnid,bhnjd->bhnij", qc, kc) * decay
    q_dec = qc * jnp.exp(G)[..., None]
    k_dec = kc * jnp.exp(G[..., -1:] - G)[..., None]
    g_last = jnp.exp(G[..., -1])

    def step(state, xs):
        u_n, w_n, q_n, qk_n, k_n, gl_n = xs
        v_new = u_n - jnp.einsum("bhck,bhkv->bhcv", w_n, state)
        o_n = jnp.einsum("bhck,bhkv->bhcv", q_n, state) + jnp.einsum("bhij,bhjv->bhiv", qk_n, v_new)
        state = state * gl_n[..., None, None] + jnp.einsum("bhck,bhcv->bhkv", k_n, v_new)
        return state, o_n

    xs = tuple(jnp.moveaxis(t, 2, 0) for t in (u, w, q_dec, qk, k_dec, g_last))
    s0 = jnp.zeros((b, h, dk, dv), jnp.float32)
    _, o = lax.scan(step, s0, xs)
    return jnp.transpose(o, (1, 0, 3, 2, 4)).reshape(b, s, h, dv)


def dsa_attention(q, k, v, qi, ki, wi):
    b, s, hq, d = q.shape
    hk = k.shape[2]
    grp = hq // hk
    n = s // CHUNK
    topk = min(TOPK_MAX, s // 4)
    kf = k.reshape(b, s, hk * d)
    vf = v.reshape(b, s, hk * d)
    kif = ki.astype(jnp.float32)
    key_pos = jnp.arange(s)
    gather = jax.vmap(lambda table, ids: table[ids])

    def to_blocks(t):
        return jnp.moveaxis(t.reshape((b, n, CHUNK) + t.shape[2:]), 1, 0)

    def block(args):
        q_b, qi_b, wi_b, blk = args
        limit = (blk + 1) * CHUNK
        rel = jax.nn.relu(jnp.einsum("bthd,bsd->bths", qi_b.astype(jnp.float32), kif))
        score = jnp.einsum("bth,bths->bts", wi_b.astype(jnp.float32), rel)
        score = jnp.where(key_pos < limit, score, -jnp.inf)
        _, sel = lax.top_k(score, topk)
        valid = sel < limit
        ks = gather(kf, sel).reshape(b, CHUNK, topk, hk, d)
        vs = gather(vf, sel).reshape(b, CHUNK, topk, hk, d)
        qg = q_b.reshape(b, CHUNK, hk, grp, d)
        logits = jnp.einsum("btkgd,btjkd->btkgj", qg, ks).astype(jnp.float32) * (d ** -0.5)
        logits = jnp.where(valid[:, :, None, None, :], logits, -jnp.inf)
        p = jax.nn.softmax(logits, axis=-1).astype(v.dtype)
        o = jnp.einsum("btkgj,btjkd->btkgd", p, vs)
        return o.reshape(b, CHUNK, hq * d)

    out = lax.map(block, (to_blocks(q), to_blocks(qi), to_blocks(wi), jnp.arange(n)))
    return jnp.moveaxis(out, 0, 1).reshape(b, s, hq * d)


def hier_moe(h, w_rg, b_rg, w_re, b_re, w1, w3, w2):
    bsz, s, d = h.shape
    xt = h.reshape(-1, d)
    glog = (xt @ w_rg + b_rg).astype(jnp.float32)
    gp = jax.nn.softmax(glog, axis=-1)
    gsel = jnp.argmax(glog, axis=-1)
    ggate = jnp.take_along_axis(gp, gsel[:, None], axis=-1)
    elog = (xt @ w_re + b_re).astype(jnp.float32).reshape(-1, N_GROUPS, EXPERTS_PER_GROUP)
    elog_g = jnp.take_along_axis(elog, gsel[:, None, None], axis=1)[:, 0]
    ep = jax.nn.softmax(elog_g, axis=-1)
    top_p, top_i = lax.top_k(ep, TOPK_IN_GROUP)
    top_p = top_p / jnp.sum(top_p, axis=-1, keepdims=True)
    wts = ggate * top_p
    eid = gsel[:, None] * EXPERTS_PER_GROUP + top_i
    comb = jnp.sum(jax.nn.one_hot(eid, N_EXPERTS, dtype=jnp.float32) * wts[..., None], axis=1)
    act = jax.nn.silu(jnp.einsum("nd,edf->nef", xt, w1)) * jnp.einsum("nd,edf->nef", xt, w3)
    act = act * comb[:, :, None].astype(act.dtype)
    y = jnp.einsum("nef,efd->nd", act, w2)
    return y.reshape(bsz, s, d)


def setup_inputs(seed: int = 0) -> dict:
    key = jax.random.key(seed)
    ks = jax.random.split(key, 24)
    f32 = jnp.float32

    def nrm(k, shape, fan):
        return jax.random.normal(k, shape, f32) * (fan ** -0.5)

    def gain(k, shape):
        return 1.0 + 0.02 * jax.random.normal(k, shape, f32)

    x = jax.random.normal(ks[0], (BATCH, SEQ, D_MODEL), f32)
    offs = jax.random.randint(ks[1], (BATCH, 1), 0, 64) * CHUNK
    positions = (offs + jnp.arange(SEQ, dtype=jnp.int32)[None, :]).astype(jnp.int32)
    norm1_g = gain(ks[2], (DEPTH, D_MODEL))
    w_in = nrm(ks[3], (DEPTH, D_MODEL, D_IN), D_MODEL)
    b_gate = 0.02 * jax.random.normal(ks[4], (DEPTH, 2 * D_MODEL), f32)
    conv_w = nrm(ks[5], (DEPTH, CONV_K, CONV_W), CONV_K)
    a_log = jnp.log(jax.random.uniform(ks[6], (DEPTH, A_HEADS), f32, 1.0, 16.0))
    dt = jnp.exp(jax.random.uniform(ks[7], (DEPTH, A_HEADS), f32, math.log(1e-3), math.log(1e-1)))
    dt_bias = dt + jnp.log(-jnp.expm1(-dt))
    a_norm_g = gain(ks[8], (DEPTH, A_DV))
    w_proj_a = nrm(ks[9], (DEPTH, A_V_W, D_MODEL), A_V_W)
    w_proj_b = nrm(ks[10], (DEPTH, B_Q_W, D_MODEL), B_Q_W)
    w_out = nrm(ks[11], (DEPTH, D_MODEL, D_MODEL), D_MODEL)
    norm2_g = gain(ks[12], (DEPTH, D_MODEL))
    w_router_group = nrm(ks[13], (DEPTH, D_MODEL, N_GROUPS), D_MODEL)
    b_router_group = 0.01 * jax.random.normal(ks[14], (DEPTH, N_GROUPS), f32)
    w_router_expert = nrm(ks[15], (DEPTH, D_MODEL, N_EXPERTS), D_MODEL)
    b_router_expert = 0.01 * jax.random.normal(ks[16], (DEPTH, N_EXPERTS), f32)
    w_exp_gate = nrm(ks[17], (DEPTH, N_EXPERTS, D_MODEL, D_EXPERT), D_MODEL)
    w_exp_up = nrm(ks[18], (DEPTH, N_EXPERTS, D_MODEL, D_EXPERT), D_MODEL)
    w_exp_down = nrm(ks[19], (DEPTH, N_EXPERTS, D_EXPERT, D_MODEL), D_EXPERT)
    final_norm_g = gain(ks[20], (D_MODEL,))
    return {
        "x": x, "positions": positions, "norm1_g": norm1_g, "w_in": w_in, "b_gate": b_gate,
        "conv_w": conv_w, "a_log": a_log, "dt_bias": dt_bias, "a_norm_g": a_norm_g,
        "w_proj_a": w_proj_a, "w_proj_b": w_proj_b, "w_out": w_out, "norm2_g": norm2_g,
        "w_router_group": w_router_group, "b_router_group": b_router_group,
        "w_router_expert": w_router_expert, "b_router_expert": b_router_expert,
        "w_exp_gate": w_exp_gate, "w_exp_up": w_exp_up, "w_exp_down": w_exp_down,
        "final_norm_g": final_norm_g,
    }


def reference(x, positions, norm1_g, w_in, b_gate, conv_w, a_log, dt_bias, a_norm_g,
              w_proj_a, w_proj_b, w_out, norm2_g, w_router_group, b_router_group,
              w_router_expert, b_router_expert, w_exp_gate, w_exp_up, w_exp_down, final_norm_g):
    b, s, _ = x.shape
    for l in range(DEPTH):
        h = rmsnorm(x, norm1_g[l])
        parts = split_cols(h @ w_in[l])

        qkv = jnp.concatenate([parts["a_q"], parts["a_k"], parts["a_v"]], axis=-1)
        qkv = jax.nn.silu(causal_dwconv(qkv, conv_w[l]))
        aq = l2norm(qkv[..., :A_QK_W].reshape(b, s, A_HEADS, A_DK)) * (A_DK ** -0.5)
        ak = l2norm(qkv[..., A_QK_W:2 * A_QK_W].reshape(b, s, A_HEADS, A_DK))
        av = qkv[..., 2 * A_QK_W:].reshape(b, s, A_HEADS, A_DV).astype(jnp.float32)
        beta = jax.nn.sigmoid(parts["a_beta"].astype(jnp.float32))
        g = -jnp.exp(a_log[l].astype(jnp.float32)) * jax.nn.softplus(
            parts["a_alpha"].astype(jnp.float32) + dt_bias[l].astype(jnp.float32))
        o_a = gated_delta_rule(aq, ak, av, g, beta)
        o_a = rmsnorm(o_a, a_norm_g[l]) * jax.nn.silu(
            parts["a_z"].reshape(b, s, A_HEADS, A_DV).astype(jnp.float32))
        o_a = o_a.reshape(b, s, A_V_W).astype(x.dtype)

        bq = rope(parts["b_q"].reshape(b, s, B_HEADS, B_HD), positions)
        bk = rope(parts["b_k"].reshape(b, s, B_KV_HEADS, B_HD), positions)
        bv = parts["b_v"].reshape(b, s, B_KV_HEADS, B_HD)
        iq = rope(parts["i_q"].reshape(b, s, IDX_HEADS, IDX_HD), positions)
        ik = rope(parts["i_k"][:, :, None, :], positions)[:, :, 0, :]
        iw = parts["i_w"] * ((IDX_HEADS ** -0.5) * (IDX_HD ** -0.5))
        o_b = dsa_attention(bq, bk, bv, iq, ik, iw)

        gate_a = jax.nn.sigmoid(parts["gate_a"] + b_gate[l][:D_MODEL])
        gate_b = jax.nn.sigmoid(parts["gate_b"] + b_gate[l][D_MODEL:])
        merged = gate_a * (o_a @ w_proj_a[l]) + gate_b * (o_b @ w_proj_b[l])
        x = x + merged @ w_out[l]

        x = x + hier_moe(rmsnorm(x, norm2_g[l]), w_router_group[l], b_router_group[l],
                         w_router_expert[l], b_router_expert[l],
                         w_exp_gate[l], w_exp_up[l], w_exp_down[l])
    return rmsnorm(x, final_norm_g)
```

```python
import numpy as np
import jax
import jax.numpy as jnp
from jax import lax
from jax.experimental import pallas as pl
from jax.experimental.pallas import tpu as pltpu

F32 = jnp.float32
BF16 = jnp.bfloat16

D_MODEL = 1024
CHUNK = 64
EPS = 1e-6
ROPE_THETA = 10000.0
A_HEADS = 8
A_DK = 64
A_DV = 64
CONV_K = 4
B_HEADS = 8
B_KV_HEADS = 2
B_HD = 64
IDX_HEADS = 8
IDX_HD = 64
TOPK_MAX = 256
N_GROUPS = 4
EXPERTS_PER_GROUP = 4
N_EXPERTS = 16
D_EXPERT = 256

O_AQKV, O_AZ, O_GA, O_GB, O_BQ, O_IQ, O_KD, O_VD, O_IK = 0, 1536, 2048, 3072, 4096, 4608, 5120, 5376, 5632
Z_W = 5760
S_BETA, S_ALPHA, S_IW = 0, 8, 16

GROUP_HEADS = 4
BD = GROUP_HEADS * CHUNK
NEG_BIG = -1e30
VMEM_LIMIT = 56 * 1024 * 1024


def _split(x):
    hi = x.astype(BF16)
    lo = (x - hi.astype(F32)).astype(BF16)
    return hi, lo


def _dot(a, b):
    return jnp.dot(a, b, preferred_element_type=F32)


def _mm(a, b):
    return _dot(a.astype(BF16), b.astype(BF16))


def _mm_nt(a, b):
    return lax.dot_general(a.astype(BF16), b.astype(BF16), (((1,), (1,)), ((), ())),
                           preferred_element_type=F32)


def _mm3(a, b):
    ah, al = _split(a)
    bh, bl = _split(b)
    return _dot(ah, bh) + _dot(al, bh) + _dot(ah, bl)


def _mm_exact_lhs(a_bf16, x):
    xh, xl = _split(x)
    return _dot(a_bf16, xh) + _dot(a_bf16, xl)


def _rmsnorm_rows(x, g):
    return x * lax.rsqrt(jnp.mean(x * x, axis=-1, keepdims=True) + EPS) * g


_Z_CHUNKS = tuple((o, min(512, Z_W - o)) for o in range(0, Z_W, 512))


def _in_proj_kernel(x_ref, g_ref, w_ref, ws_ref, z_ref, zs_ref):
    h = _rmsnorm_rows(x_ref[...], g_ref[...])
    hb, hl = _split(h)
    for o, w in _Z_CHUNKS:
        z_ref[:, o:o + w] = _dot(hb, w_ref[:, o:o + w]).astype(BF16)
    s1 = _dot(hb, ws_ref[...])
    s2 = _dot(hl, ws_ref[:, :128])
    zs_ref[...] = s1[:, :128] + s1[:, 128:] + s2


def _in_proj(x2d, g, w_main, w_small, tm=512):
    n = x2d.shape[0]
    return pl.pallas_call(
        _in_proj_kernel,
        grid=(n // tm,),
        in_specs=[
            pl.BlockSpec((tm, D_MODEL), lambda i: (i, 0)),
            pl.BlockSpec((1, D_MODEL), lambda i: (0, 0)),
            pl.BlockSpec((D_MODEL, Z_W), lambda i: (0, 0)),
            pl.BlockSpec((D_MODEL, 256), lambda i: (0, 0)),
        ],
        out_specs=[
            pl.BlockSpec((tm, Z_W), lambda i: (i, 0)),
            pl.BlockSpec((tm, 128), lambda i: (i, 0)),
        ],
        out_shape=[jax.ShapeDtypeStruct((n, Z_W), BF16), jax.ShapeDtypeStruct((n, 128), F32)],
        compiler_params=pltpu.CompilerParams(dimension_semantics=("parallel",),
                                             vmem_limit_bytes=VMEM_LIMIT),
        name="in_proj",
    )(x2d, g, w_main, w_small)


def _gdn_constants():
    r = np.arange(BD)
    same = (r[:, None] // CHUNK) == (r[None, :] // CHUNK)
    incl = same & (r[:, None] >= r[None, :])
    strict = same & (r[:, None] > r[None, :])
    eye = np.eye(BD, dtype=np.float32)
    ll = np.concatenate([incl, same], axis=0).astype(np.float32)
    c = np.arange(A_HEADS * A_DK)
    m64 = ((c[:, None] // A_DK) == (c[None, :] // A_DK)).astype(np.float32)
    sel = np.zeros((4, BD, 128), np.float32)
    for gi in range(2):
        for h in range(GROUP_HEADS):
            sel[gi * 2 + 0, h * CHUNK:(h + 1) * CHUNK, S_BETA + gi * GROUP_HEADS + h] = 1.0
            sel[gi * 2 + 1, h * CHUNK:(h + 1) * CHUNK, S_ALPHA + gi * GROUP_HEADS + h] = 1.0
    return (jnp.asarray(incl, F32), jnp.asarray(strict, F32), jnp.asarray(same, F32), jnp.asarray(eye),
            jnp.asarray(ll, BF16), jnp.asarray(m64, BF16), jnp.asarray(sel))


def _tile4(x):
    return jnp.concatenate([x, x, x, x], axis=0)


def _gdn_kernel(zq_ref, zz_ref, zs_ref, cw_ref, av_ref, ag_ref, incl_ref, strict_ref, bdm_ref, eye_ref,
                ll_ref, m64_ref, sel_ref, o_ref, ext_ref, st_ref):
    c = pl.program_id(1)
    t = zq_ref.shape[0]

    @pl.when(c == 0)
    def _():
        ext_ref[0:8, :] = jnp.zeros((8, ext_ref.shape[1]), F32)
        st_ref[...] = jnp.zeros(st_ref.shape, F32)

    ext_ref[8:8 + t, :] = zq_ref[...].astype(F32)
    cw = cw_ref[...]
    y = cw[0:1, :] * ext_ref[pl.ds(8 - (CONV_K - 1), t), :]
    for j in range(1, CONV_K):
        y = y + cw[j:j + 1, :] * ext_ref[pl.ds(8 - (CONV_K - 1) + j, t), :]
    ext_ref[0:8, :] = ext_ref[t:t + 8, :]
    y = y * jax.nn.sigmoid(y)

    hw = A_HEADS * A_DK
    m64 = m64_ref[...]
    q = y[:, :hw]
    k = y[:, hw:2 * hw]
    v = y[:, 2 * hw:]
    q = q * lax.rsqrt(_mm_exact_lhs_t(q * q, m64) + EPS) * (A_DK ** -0.5)
    k = k * lax.rsqrt(_mm_exact_lhs_t(k * k, m64) + EPS)

    sm = zs_ref[...]
    av = av_ref[...]
    lane = lax.broadcasted_iota(jnp.int32, sm.shape, 1)
    xg = sm + av[1:2, :]
    softplus = jnp.maximum(xg, 0.0) + jnp.log1p(jnp.exp(-jnp.abs(xg)))
    g_all = -jnp.exp(av[0:1, :]) * softplus
    bg = jnp.where(lane < S_ALPHA, jax.nn.sigmoid(sm), g_all)

    incl = incl_ref[...]
    strict = strict_ref[...]
    bdm = bdm_ref[...]
    eye = eye_ref[...]
    ll = ll_ref[...]
    incl_b = ll[:BD]

    outs = []
    for ci in range(t // CHUNK):
        r0 = ci * CHUNK
        bg4 = _tile4(bg[r0:r0 + CHUNK])
        o_groups = []
        for gi in range(A_HEADS // GROUP_HEADS):
            c0 = gi * BD
            beta = jnp.sum(bg4 * sel_ref[gi * 2 + 0], axis=1, keepdims=True)
            gcol = jnp.sum(bg4 * sel_ref[gi * 2 + 1], axis=1, keepdims=True)
            gs = _mm_exact_lhs(ll, jnp.broadcast_to(gcol, (BD, 128)))
            g_cum = gs[:BD, :1]
            g_last = gs[BD:, :1]
            diff = _mm_exact_lhs(incl_b, gcol * strict)
            decay = jnp.where(incl > 0.0, jnp.exp(diff), 0.0)
            e_cum = jnp.exp(g_cum)

            km = _tile4(k[r0:r0 + CHUNK, c0:c0 + BD]) * bdm
            qm = _tile4(q[r0:r0 + CHUNK, c0:c0 + BD]) * bdm
            vm = _tile4(v[r0:r0 + CHUNK, c0:c0 + BD]) * bdm
            kkqk = _mm_nt(jnp.concatenate([km, qm], axis=0), km)
            kk = kkqk[:BD]
            qk = kkqk[BD:] * decay

            m = -(strict * beta * kk * decay)
            inv = eye + m
            sq = CHUNK
            while sq > 2:
                m = _mm3(m, m)
                inv = inv + _mm3(inv, m)
                sq //= 2
            rhs = jnp.concatenate([beta * vm, (beta * e_cum) * km], axis=1)
            uw = _mm3(inv, rhs)
            u = uw[:, :BD]
            w = uw[:, BD:]

            state = st_ref[gi]
            wq = _mm(jnp.concatenate([w, qm * e_cum], axis=0), state)
            v_new = u - wq[:BD]
            o_bd = wq[BD:] + _mm(qk, v_new)
            k_dec = km * jnp.exp(g_last - g_cum)
            st_ref[gi] = state * jnp.exp(g_last) + _mm(k_dec.T, v_new)
            o_groups.append(o_bd[0:CHUNK] + o_bd[CHUNK:2 * CHUNK] + o_bd[2 * CHUNK:3 * CHUNK]
                            + o_bd[3 * CHUNK:])
        outs.append(jnp.concatenate(o_groups, axis=1))
    o = outs[0] if len(outs) == 1 else jnp.concatenate(outs, axis=0)

    ms = _mm_exact_lhs_t(o * o, m64) * (1.0 / A_DV)
    zz = zz_ref[...].astype(F32)
    o = o * lax.rsqrt(ms + EPS) * ag_ref[...] * (zz * jax.nn.sigmoid(zz))
    o_ref[...] = o.astype(BF16)


def _mm_exact_lhs_t(x, m_bf16):
    xh, xl = _split(x)
    return _dot(xh, m_bf16) + _dot(xl, m_bf16)


def _gdn(z3, zs3, conv_w, a_log, dt_bias, a_norm_g, t=CHUNK):
    b, s, _ = z3.shape
    consts = _gdn_constants()
    av = jnp.zeros((2, 128), F32)
    av = av.at[0, S_ALPHA:S_ALPHA + A_HEADS].set(a_log.astype(F32))
    av = av.at[1, S_ALPHA:S_ALPHA + A_HEADS].set(dt_bias.astype(F32))
    ag = jnp.tile(a_norm_g.astype(F32), A_HEADS)[None, :]
    conv_cols = 2 * A_HEADS * A_DK + A_HEADS * A_DV

    def const_spec(a):
        nd = a.ndim
        return pl.BlockSpec(a.shape, lambda bi, ci, _n=nd: (0,) * _n)

    small_in = (conv_w.astype(F32), av, ag) + consts
    return pl.pallas_call(
        _gdn_kernel,
        grid=(b, s // t),
        in_specs=[
            pl.BlockSpec((None, t, conv_cols), lambda bi, ci: (bi, ci, O_AQKV // conv_cols)),
            pl.BlockSpec((None, t, 512), lambda bi, ci: (bi, ci, O_AZ // 512)),
            pl.BlockSpec((None, t, 128), lambda bi, ci: (bi, ci, 0)),
        ] + [const_spec(a) for a in small_in],
        out_specs=pl.BlockSpec((None, t, A_HEADS * A_DV), lambda bi, ci: (bi, ci, 0)),
        out_shape=jax.ShapeDtypeStruct((b, s, A_HEADS * A_DV), BF16),
        scratch_shapes=[pltpu.VMEM((8 + t, conv_cols), F32),
                        pltpu.VMEM((A_HEADS // GROUP_HEADS, BD, BD), F32)],
        compiler_params=pltpu.CompilerParams(dimension_semantics=("parallel", "arbitrary"),
                                             vmem_limit_bytes=VMEM_LIMIT),
        name="gdn",
    )(z3, z3, zs3, *small_in)


def _rope(x, cs, sn, first):
    w = x.shape[1]
    swapped = jnp.where(first, pltpu.roll(x, w - B_HD // 2, 1), pltpu.roll(x, B_HD // 2, 1))
    return x * cs + swapped * sn


def _dsa_kernel(topk, tk, pos_ref, inv_ref, bq_ref, iq_ref, kd_ref, vd_ref, ikd_ref, zs_ref, o_ref,
                cos_ref, sin_ref, ka_ref, kb_ref, ik_ref, sc_ref):
    qb = pl.program_id(1)
    s = kd_ref.shape[0]
    tq = bq_ref.shape[0]
    nkt = s // tk

    @pl.when(qb == 0)
    def _():
        ang = pos_ref[...] * inv_ref[...]
        lane = lax.broadcasted_iota(jnp.int32, ang.shape, 1)
        first = (lane & (B_HD - 1)) < (B_HD // 2)
        cs = jnp.cos(ang)
        sn = jnp.sin(ang)
        sn = jnp.where(first, -sn, sn)
        cos_ref[...] = cs
        sin_ref[...] = sn
        kd = kd_ref[...].astype(F32)
        ka_ref[...] = _rope(kd[:, :128], cs, sn, first).astype(BF16)
        kb_ref[...] = _rope(kd[:, 128:], cs, sn, first).astype(BF16)
        ik_ref[...] = _rope(ikd_ref[...].astype(F32), cs, sn, first).astype(BF16)

    r0 = pl.multiple_of(qb * tq, tq)
    cs1 = cos_ref[pl.ds(r0, tq), :]
    sn1 = sin_ref[pl.ds(r0, tq), :]
    cs = jnp.concatenate([cs1] * 4, axis=1)
    sn = jnp.concatenate([sn1] * 4, axis=1)
    lane512 = lax.broadcasted_iota(jnp.int32, (tq, 4 * 128), 1)
    first512 = (lane512 & (B_HD - 1)) < (B_HD // 2)
    q = _rope(bq_ref[...].astype(F32), cs, sn, first512) * (B_HD ** -0.5)
    iq = _rope(iq_ref[...].astype(F32), cs, sn, first512)

    sm = zs_ref[...]
    lane128 = lax.broadcasted_iota(jnp.int32, (tq, 128), 1)
    iw_scale = (IDX_HEADS ** -0.5) * (IDX_HD ** -0.5)
    wcols = [jnp.sum(jnp.where(lane128 == S_IW + h, sm, 0.0), axis=1, keepdims=True) * iw_scale
             for h in range(IDX_HEADS)]
    low_half = lane128 < B_HD
    high_half = lane128 >= B_HD

    def head_slab(x, h):
        slab = x[:, (h // 2) * 128:(h // 2 + 1) * 128]
        return jnp.where(low_half if h % 2 == 0 else high_half, slab, 0.0)

    row = r0 + lax.broadcasted_iota(jnp.int32, (tq, 1), 0)
    limit = ((row >> 6) + 1) << 6

    for kt in range(nkt):
        ikt = ik_ref[kt * tk:(kt + 1) * tk, :]
        acc = jnp.zeros((tq, tk), F32)
        for h in range(IDX_HEADS):
            sc = _mm_nt(head_slab(iq, h), ikt)
            acc = acc + wcols[h] * jnp.maximum(sc, 0.0)
        kpos = kt * tk + lax.broadcasted_iota(jnp.int32, (tq, tk), 1)
        sc_ref[:, kt * tk:(kt + 1) * tk] = jnp.where(kpos < limit, acc, -jnp.inf)

    kf = float(topk)
    x0 = sc_ref[...]
    row_max = jnp.max(x0, axis=1, keepdims=True)
    row_min = jnp.min(jnp.where(x0 == -jnp.inf, jnp.inf, x0), axis=1, keepdims=True)
    c_max = jnp.sum((x0 >= row_max).astype(F32), axis=1, keepdims=True)
    small = limit <= topk

    def bisect(_, carry):
        lo, hi = carry
        mid = 0.5 * (lo + hi)
        cnt = jnp.sum((sc_ref[...] >= mid).astype(F32), axis=1, keepdims=True)
        ge = cnt >= kf
        return jnp.where(ge, mid, lo), jnp.where(ge, hi, mid)

    lo, hi = lax.fori_loop(0, 30, bisect, (row_min, row_max))
    top_tied = c_max >= kf
    done0 = jnp.where(small | top_tied, 1.0, 0.0)
    thr0 = jnp.where(top_tied, row_max, lo)

    def peel_cond(carry):
        done, _, _ = carry
        return jnp.sum(1.0 - done) > 0.0

    def peel(carry):
        done, thr, hi_c = carry
        x = sc_ref[...]
        v1 = jnp.max(jnp.where(x < hi_c, x, -jnp.inf), axis=1, keepdims=True)
        c1 = jnp.sum((x >= v1).astype(F32), axis=1, keepdims=True)
        ok = c1 >= kf
        act = done < 0.5
        thr = jnp.where(act & ok, v1, thr)
        hi_c = jnp.where(act & (~ok), v1, hi_c)
        done = jnp.where(ok, 1.0, done)
        return done, thr, hi_c

    _, thr, _ = lax.while_loop(peel_cond, peel, (done0, thr0, hi))

    x = sc_ref[...]
    idx = lax.broadcasted_iota(jnp.int32, (tq, s), 1).astype(F32)
    gt = x > thr
    eq = x == thr
    need = kf - jnp.sum(gt.astype(F32), axis=1, keepdims=True)

    def tie_search(_, carry):
        jlo, jhi = carry
        mid = jnp.floor(0.5 * (jlo + jhi))
        cnt = jnp.sum((eq & (idx <= mid)).astype(F32), axis=1, keepdims=True)
        ge = cnt >= need
        return jnp.where(ge, jlo, mid), jnp.where(ge, mid, jhi)

    n_iter = int(np.ceil(np.log2(s))) + 1
    _, jcut = lax.fori_loop(0, n_iter, tie_search,
                            (jnp.full((tq, 1), -1.0, F32), jnp.full((tq, 1), float(s - 1), F32)))
    sel = (x > -jnp.inf) & (small | gt | (eq & (idx <= jcut)))
    sc_ref[...] = jnp.where(sel, 0.0, NEG_BIG)

    heads_per_kv = B_HEADS // B_KV_HEADS
    for g in range(B_KV_HEADS):
        k_ref = ka_ref if g == 0 else kb_ref
        qrows = jnp.concatenate([head_slab(q, g * heads_per_kv + j) for j in range(heads_per_kv)],
                                axis=0).astype(BF16)
        nr = heads_per_kv * tq
        m_run = jnp.full((nr, 1), NEG_BIG, F32)
        l_run = jnp.zeros((nr, 1), F32)
        acc = jnp.zeros((nr, 128), F32)
        for kt in range(nkt):
            logits = lax.dot_general(qrows, k_ref[kt * tk:(kt + 1) * tk, :], (((1,), (1,)), ((), ())),
                                     preferred_element_type=F32)
            bias = sc_ref[:, kt * tk:(kt + 1) * tk]
            logits = logits + jnp.concatenate([bias] * heads_per_kv, axis=0)
            m_new = jnp.maximum(m_run, jnp.max(logits, axis=1, keepdims=True))
            p = jnp.exp(logits - m_new)
            alpha = jnp.exp(m_run - m_new)
            l_run = alpha * l_run + jnp.sum(p, axis=1, keepdims=True)
            acc = alpha * acc + _dot(p.astype(BF16), vd_ref[kt * tk:(kt + 1) * tk, g * 128:(g + 1) * 128])
            m_run = m_new
        og = acc / l_run
        for pp in range(heads_per_kv // 2):
            even = og[(2 * pp) * tq:(2 * pp + 1) * tq]
            odd = og[(2 * pp + 1) * tq:(2 * pp + 2) * tq]
            col = (g * (heads_per_kv // 2) + pp) * 128
            o_ref[:, col:col + 128] = jnp.where(low_half, even, odd).astype(BF16)


def _dsa(z3, zs3, positions, tq=128, tk=512):
    b, s, _ = z3.shape
    topk = min(TOPK_MAX, s // 4)
    half = B_HD // 2
    inv = ROPE_THETA ** (-jnp.arange(half, dtype=F32) / half)
    inv128 = jnp.tile(inv, 4)[None, :]
    pos = positions.astype(F32)[:, :, None]
    kernel = lambda *refs: _dsa_kernel(topk, tk, *refs)
    return pl.pallas_call(
        kernel,
        grid=(b, s // tq),
        in_specs=[
            pl.BlockSpec((None, s, 1), lambda bi, qi: (bi, 0, 0)),
            pl.BlockSpec((1, 128), lambda bi, qi: (0, 0)),
            pl.BlockSpec((None, tq, 512), lambda bi, qi: (bi, qi, O_BQ // 512)),
            pl.BlockSpec((None, tq, 512), lambda bi, qi: (bi, qi, O_IQ // 512)),
            pl.BlockSpec((None, s, 256), lambda bi, qi: (bi, 0, O_KD // 256)),
            pl.BlockSpec((None, s, 256), lambda bi, qi: (bi, 0, O_VD // 256)),
            pl.BlockSpec((None, s, 128), lambda bi, qi: (bi, 0, O_IK // 128)),
            pl.BlockSpec((None, tq, 128), lambda bi, qi: (bi, qi, 0)),
        ],
        out_specs=pl.BlockSpec((None, tq, B_HEADS * B_HD), lambda bi, qi: (bi, qi, 0)),
        out_shape=jax.ShapeDtypeStruct((b, s, B_HEADS * B_HD), BF16),
        scratch_shapes=[pltpu.VMEM((s, 128), F32), pltpu.VMEM((s, 128), F32),
                        pltpu.VMEM((s, 128), BF16), pltpu.VMEM((s, 128), BF16), pltpu.VMEM((s, 128), BF16),
                        pltpu.VMEM((tq, s), F32)],
        compiler_params=pltpu.CompilerParams(dimension_semantics=("parallel", "arbitrary"),
                                             vmem_limit_bytes=VMEM_LIMIT),
        name="dsa",
    )(pos, inv128, z3, z3, z3, z3, z3, zs3)


def _merge_kernel(x_ref, oa_ref, ob_ref, ga_ref, gb_ref, bg_ref, wa_ref, wb_ref, wo_ref, n2_ref, wr_ref,
                  br_ref, x1_ref, h2_ref, rl_ref):
    pa = _dot(oa_ref[...], wa_ref[...])
    pb = _dot(ob_ref[...], wb_ref[...])
    bgv = bg_ref[...]
    ga = jax.nn.sigmoid(ga_ref[...].astype(F32) + bgv[:, :D_MODEL])
    gb = jax.nn.sigmoid(gb_ref[...].astype(F32) + bgv[:, D_MODEL:])
    merged = ga * pa + gb * pb
    x1 = x_ref[...] + _mm(merged, wo_ref[...])
    x1_ref[...] = x1
    h2 = _rmsnorm_rows(x1, n2_ref[...])
    hb, hl = _split(h2)
    h2_ref[...] = hb
    s1 = _dot(hb, wr_ref[...])
    s2 = _dot(hl, wr_ref[:, :128])
    rl_ref[...] = s1[:, :128] + s1[:, 128:] + s2 + br_ref[...]


def _merge(x2d, oa, ob, z2d, b_gate, wa, wb, wo, n2, wr, br, tm=512):
    n = x2d.shape[0]
    full = lambda shape: pl.BlockSpec(shape, lambda i: (0, 0))
    return pl.pallas_call(
        _merge_kernel,
        grid=(n // tm,),
        in_specs=[
            pl.BlockSpec((tm, D_MODEL), lambda i: (i, 0)),
            pl.BlockSpec((tm, 512), lambda i: (i, 0)),
            pl.BlockSpec((tm, 512), lambda i: (i, 0)),
            pl.BlockSpec((tm, D_MODEL), lambda i: (i, O_GA // D_MODEL)),
            pl.BlockSpec((tm, D_MODEL), lambda i: (i, O_GB // D_MODEL)),
            full((1, 2 * D_MODEL)), full((512, D_MODEL)), full((512, D_MODEL)), full((D_MODEL, D_MODEL)),
            full((1, D_MODEL)), full((D_MODEL, 256)), full((1, 128)),
        ],
        out_specs=[
            pl.BlockSpec((tm, D_MODEL), lambda i: (i, 0)),
            pl.BlockSpec((tm, D_MODEL), lambda i: (i, 0)),
            pl.BlockSpec((tm, 128), lambda i: (i, 0)),
        ],
        out_shape=[jax.ShapeDtypeStruct((n, D_MODEL), F32), jax.ShapeDtypeStruct((n, D_MODEL), BF16),
                   jax.ShapeDtypeStruct((n, 128), F32)],
        compiler_params=pltpu.CompilerParams(dimension_semantics=("parallel",),
                                             vmem_limit_bytes=VMEM_LIMIT),
        name="merge",
    )(x2d, oa, ob, z2d, z2d, b_gate, wa, wb, wo, n2, wr, br)


R_GROUP, R_EXPERT = 0, N_GROUPS


def _routing_weights(rl):
    lane = lax.broadcasted_iota(jnp.int32, rl.shape, 1)
    big = jnp.int32(1 << 20)
    gmask = lane < N_GROUPS
    gl = jnp.where(gmask, rl, -jnp.inf)
    gmax = jnp.max(gl, axis=1, keepdims=True)
    gsel = jnp.min(jnp.where(gmask & (rl == gmax), lane, big), axis=1, keepdims=True)
    ggate = 1.0 / jnp.sum(jnp.where(gmask, jnp.exp(gl - gmax), 0.0), axis=1, keepdims=True)
    e_lo = R_EXPERT + gsel * EXPERTS_PER_GROUP
    emask = (lane >= e_lo) & (lane < e_lo + EXPERTS_PER_GROUP)
    el = jnp.where(emask, rl, -jnp.inf)
    emax = jnp.max(el, axis=1, keepdims=True)
    ee = jnp.where(emask, jnp.exp(el - emax), 0.0)
    ep = jnp.where(emask, ee / jnp.sum(ee, axis=1, keepdims=True), -1.0)
    p1 = jnp.max(ep, axis=1, keepdims=True)
    i1 = jnp.min(jnp.where(ep == p1, lane, big), axis=1, keepdims=True)
    ep2 = jnp.where(lane == i1, -1.0, ep)
    p2 = jnp.max(ep2, axis=1, keepdims=True)
    i2 = jnp.min(jnp.where(ep2 == p2, lane, big), axis=1, keepdims=True)
    tot = p1 + p2
    return (jnp.where(lane == i1, ggate * (p1 / tot), 0.0)
            + jnp.where(lane == i2, ggate * (p2 / tot), 0.0))


def _moe_kernel(x1_ref, h2_ref, rl_ref, w1_ref, w3_ref, w2_ref, fg_ref, o_ref, y_ref, comb_ref):
    e = pl.program_id(1)

    @pl.when(e == 0)
    def _():
        comb_ref[...] = _routing_weights(rl_ref[...])
        y_ref[...] = jnp.zeros(y_ref.shape, F32)

    h = h2_ref[...]
    a = _dot(h, w1_ref[...])
    b = _dot(h, w3_ref[...])
    comb = comb_ref[...]
    lane = lax.broadcasted_iota(jnp.int32, comb.shape, 1)
    ce = jnp.sum(jnp.where(lane == R_EXPERT + e, comb, 0.0), axis=1, keepdims=True)
    act = (a * jax.nn.sigmoid(a)) * b * ce
    y_ref[...] += _dot(act.astype(BF16), w2_ref[...])

    @pl.when(e == pl.num_programs(1) - 1)
    def _():
        o_ref[...] = _rmsnorm_rows(x1_ref[...] + y_ref[...], fg_ref[...])


def _moe(x1, h2, rl, w1, w3, w2, fg, tm=1024):
    n = x1.shape[0]
    return pl.pallas_call(
        _moe_kernel,
        grid=(n // tm, N_EXPERTS),
        in_specs=[
            pl.BlockSpec((tm, D_MODEL), lambda i, e: (i, 0)),
            pl.BlockSpec((tm, D_MODEL), lambda i, e: (i, 0)),
            pl.BlockSpec((tm, 128), lambda i, e: (i, 0)),
            pl.BlockSpec((None, D_MODEL, D_EXPERT), lambda i, e: (e, 0, 0)),
            pl.BlockSpec((None, D_MODEL, D_EXPERT), lambda i, e: (e, 0, 0)),
            pl.BlockSpec((None, D_EXPERT, D_MODEL), lambda i, e: (e, 0, 0)),
            pl.BlockSpec((1, D_MODEL), lambda i, e: (0, 0)),
        ],
        out_specs=pl.BlockSpec((tm, D_MODEL), lambda i, e: (i, 0)),
        out_shape=jax.ShapeDtypeStruct((n, D_MODEL), F32),
        scratch_shapes=[pltpu.VMEM((tm, D_MODEL), F32), pltpu.VMEM((tm, 128), F32)],
        compiler_params=pltpu.CompilerParams(dimension_semantics=("parallel", "arbitrary"),
                                             vmem_limit_bytes=VMEM_LIMIT),
        name="moe",
    )(x1, h2, rl, w1, w3, w2, fg)


_W_OFF = {}
_off = 0
for _name, _n in (("a_q", 512), ("a_k", 512), ("a_v", 512), ("a_z", 512), ("a_beta", 8), ("a_alpha", 8),
                  ("b_q", 512), ("b_k", 128), ("b_v", 128), ("i_q", 512), ("i_k", 64), ("i_w", 8),
                  ("gate_a", 1024), ("gate_b", 1024)):
    _W_OFF[_name] = (_off, _off + _n)
    _off += _n


def _cols(w, name, lo=0, hi=None):
    a, b = _W_OFF[name]
    return w[:, a + lo:(b if hi is None else a + hi)]


def _hi_lo_pair(w_small):
    pad = jnp.zeros((w_small.shape[0], 128 - w_small.shape[1]), F32)
    w = jnp.concatenate([w_small.astype(F32), pad], axis=1)
    hi, lo = _split(w)
    return jnp.concatenate([hi, lo], axis=1)


def _layout_w_in(w):
    k0, k1 = _cols(w, "b_k", 0, 64), _cols(w, "b_k", 64, 128)
    v0, v1 = _cols(w, "b_v", 0, 64), _cols(w, "b_v", 64, 128)
    ik = _cols(w, "i_k")
    main = jnp.concatenate([
        _cols(w, "a_q"), _cols(w, "a_k"), _cols(w, "a_v"), _cols(w, "a_z"), _cols(w, "gate_a"),
        _cols(w, "gate_b"), _cols(w, "b_q"), _cols(w, "i_q"), k0, k0, k1, k1, v0, v0, v1, v1, ik, ik],
        axis=1).astype(BF16)
    small = _hi_lo_pair(jnp.concatenate([_cols(w, "a_beta"), _cols(w, "a_alpha"), _cols(w, "i_w")], axis=1))
    return main, small


def kernel(x, positions, norm1_g, w_in, b_gate, conv_w, a_log, dt_bias, a_norm_g, w_proj_a, w_proj_b, w_out,
           norm2_g, w_router_group, b_router_group, w_router_expert, b_router_expert, w_exp_gate, w_exp_up,
           w_exp_down, final_norm_g):
    b, s, d = x.shape
    n = b * s
    depth = w_in.shape[0]
    xc = x.reshape(n, d).astype(F32)
    for l in range(depth):
        w_main, w_small = _layout_w_in(w_in[l])
        z, zs = _in_proj(xc, norm1_g[l][None, :].astype(F32), w_main, w_small)
        z3 = z.reshape(b, s, Z_W)
        zs3 = zs.reshape(b, s, 128)
        o_a = _gdn(z3, zs3, conv_w[l], a_log[l], dt_bias[l], a_norm_g[l])
        o_b = _dsa(z3, zs3, positions)
        wr = _hi_lo_pair(jnp.concatenate([w_router_group[l], w_router_expert[l]], axis=1))
        br = jnp.concatenate([b_router_group[l], b_router_expert[l],
                              jnp.zeros((128 - N_GROUPS - N_EXPERTS,), F32)])[None, :].astype(F32)
        x1, h2, rl = _merge(xc, o_a.reshape(n, -1), o_b.reshape(n, -1), z, b_gate[l][None, :].astype(F32),
                            w_proj_a[l].astype(BF16), w_proj_b[l].astype(BF16), w_out[l].astype(BF16),
                            norm2_g[l][None, :].astype(F32), wr, br)
        last = l == depth - 1
        fg = final_norm_g[None, :].astype(F32) if last else None
        if not last:
            raise NotImplementedError("only the final layer fuses the output norm")
        xc = _moe(x1, h2, rl, w_exp_gate[l].astype(BF16), w_exp_up[l].astype(BF16),
                  w_exp_down[l].astype(BF16), fg)
    return xc.reshape(b, s, d).astype(x.dtype)
```

```python
import numpy as np
import jax
import jax.numpy as jnp
from jax import lax
from jax.experimental import pallas as pl
from jax.experimental.pallas import tpu as pltpu

F32 = jnp.float32
BF16 = jnp.bfloat16

D_MODEL = 1024
CHUNK = 64
EPS = 1e-6
ROPE_THETA = 10000.0
A_HEADS = 8
A_DK = 64
A_DV = 64
CONV_K = 4
B_HEADS = 8
B_KV_HEADS = 2
B_HD = 64
IDX_HEADS = 8
IDX_HD = 64
TOPK_MAX = 256
N_GROUPS = 4
EXPERTS_PER_GROUP = 4
N_EXPERTS = 16
D_EXPERT = 256

O_AQKV, O_AZ, O_GA, O_GB, O_BQ, O_IQ, O_KD, O_VD, O_IK = 0, 1536, 2048, 3072, 4096, 4608, 5120, 5376, 5632
Z_W = 5760
S_BETA, S_ALPHA, S_IW = 0, 8, 16

GROUP_HEADS = 4
BD = GROUP_HEADS * CHUNK
NEG_BIG = -1e30
BISECT_STEPS = 18
VMEM_LIMIT = 56 * 1024 * 1024


def _split(x):
    hi = x.astype(BF16)
    lo = (x - hi.astype(F32)).astype(BF16)
    return hi, lo


def _dot(a, b):
    return jnp.dot(a, b, preferred_element_type=F32)


def _mm(a, b):
    return _dot(a.astype(BF16), b.astype(BF16))


def _mm_nt(a, b):
    return lax.dot_general(a.astype(BF16), b.astype(BF16), (((1,), (1,)), ((), ())),
                           preferred_element_type=F32)


def _mm_exact_lhs(a_bf16, x):
    xh, xl = _split(x)
    return _dot(a_bf16, xh) + _dot(a_bf16, xl)


def _rmsnorm_rows(x, g):
    return x * lax.rsqrt(jnp.mean(x * x, axis=-1, keepdims=True) + EPS) * g


_Z_CHUNKS = tuple((o, min(512, Z_W - o)) for o in range(0, Z_W, 512))


def _in_proj_kernel(x_ref, g_ref, w_ref, ws_ref, z_ref, zs_ref):
    h = _rmsnorm_rows(x_ref[...], g_ref[...])
    hb, hl = _split(h)
    for o, w in _Z_CHUNKS:
        z_ref[:, o:o + w] = _dot(hb, w_ref[:, o:o + w]).astype(BF16)
    s1 = _dot(hb, ws_ref[...])
    s2 = _dot(hl, ws_ref[:, :128])
    zs_ref[...] = s1[:, :128] + s1[:, 128:] + s2


def _in_proj(x2d, g, w_main, w_small, tm=512):
    n = x2d.shape[0]
    return pl.pallas_call(
        _in_proj_kernel,
        grid=(n // tm,),
        in_specs=[
            pl.BlockSpec((tm, D_MODEL), lambda i: (i, 0)),
            pl.BlockSpec((1, D_MODEL), lambda i: (0, 0)),
            pl.BlockSpec((D_MODEL, Z_W), lambda i: (0, 0)),
            pl.BlockSpec((D_MODEL, 256), lambda i: (0, 0)),
        ],
        out_specs=[
            pl.BlockSpec((tm, Z_W), lambda i: (i, 0)),
            pl.BlockSpec((tm, 128), lambda i: (i, 0)),
        ],
        out_shape=[jax.ShapeDtypeStruct((n, Z_W), BF16), jax.ShapeDtypeStruct((n, 128), F32)],
        compiler_params=pltpu.CompilerParams(dimension_semantics=("parallel",),
                                             vmem_limit_bytes=VMEM_LIMIT),
        name="in_proj",
    )(x2d, g, w_main, w_small)


def _gdn_constants():
    r = np.arange(BD)
    same = (r[:, None] // CHUNK) == (r[None, :] // CHUNK)
    incl = same & (r[:, None] >= r[None, :])
    strict = same & (r[:, None] > r[None, :])
    eye = np.eye(BD, dtype=np.float32)
    ll = np.concatenate([incl, same], axis=0).astype(np.float32)
    c = np.arange(A_HEADS * A_DK)
    m64 = ((c[:, None] // A_DK) == (c[None, :] // A_DK)).astype(np.float32)
    sel = np.zeros((4, BD, 128), np.float32)
    for gi in range(2):
        for h in range(GROUP_HEADS):
            sel[gi * 2 + 0, h * CHUNK:(h + 1) * CHUNK, S_BETA + gi * GROUP_HEADS + h] = 1.0
            sel[gi * 2 + 1, h * CHUNK:(h + 1) * CHUNK, S_ALPHA + gi * GROUP_HEADS + h] = 1.0
    return (jnp.asarray(incl, F32), jnp.asarray(strict, F32), jnp.asarray(same, F32), jnp.asarray(eye),
            jnp.asarray(ll, BF16), jnp.asarray(m64, BF16), jnp.asarray(sel))


def _tile4(x):
    return jnp.concatenate([x, x, x, x], axis=0)


def _gdn_kernel(zq_ref, zz_ref, zs_ref, cw_ref, av_ref, ag_ref, incl_ref, strict_ref, bdm_ref, eye_ref,
                ll_ref, m64_ref, sel_ref, o_ref, ext_ref, st_ref):
    c = pl.program_id(1)
    t = zq_ref.shape[0]

    @pl.when(c == 0)
    def _():
        ext_ref[0:8, :] = jnp.zeros((8, ext_ref.shape[1]), F32)
        st_ref[...] = jnp.zeros(st_ref.shape, F32)

    ext_ref[8:8 + t, :] = zq_ref[...].astype(F32)
    cw = cw_ref[...]
    y = cw[0:1, :] * ext_ref[pl.ds(8 - (CONV_K - 1), t), :]
    for j in range(1, CONV_K):
        y = y + cw[j:j + 1, :] * ext_ref[pl.ds(8 - (CONV_K - 1) + j, t), :]
    ext_ref[0:8, :] = ext_ref[t:t + 8, :]
    y = y * jax.nn.sigmoid(y)

    hw = A_HEADS * A_DK
    m64 = m64_ref[...]
    q = y[:, :hw]
    k = y[:, hw:2 * hw]
    v = y[:, 2 * hw:]
    q = q * lax.rsqrt(_mm_exact_lhs_t(q * q, m64) + EPS) * (A_DK ** -0.5)
    k = k * lax.rsqrt(_mm_exact_lhs_t(k * k, m64) + EPS)

    sm = zs_ref[...]
    av = av_ref[...]
    lane = lax.broadcasted_iota(jnp.int32, sm.shape, 1)
    xg = sm + av[1:2, :]
    softplus = jnp.maximum(xg, 0.0) + jnp.log1p(jnp.exp(-jnp.abs(xg)))
    g_all = -jnp.exp(av[0:1, :]) * softplus
    bg = jnp.where(lane < S_ALPHA, jax.nn.sigmoid(sm), g_all)

    incl = incl_ref[...]
    strict = strict_ref[...]
    bdm = bdm_ref[...]
    eye = eye_ref[...]
    ll = ll_ref[...]
    incl_b = ll[:BD]

    outs = []
    for ci in range(t // CHUNK):
        r0 = ci * CHUNK
        bg4 = _tile4(bg[r0:r0 + CHUNK])
        o_groups = []
        for gi in range(A_HEADS // GROUP_HEADS):
            c0 = gi * BD
            beta = jnp.sum(bg4 * sel_ref[gi * 2 + 0], axis=1, keepdims=True)
            gcol = jnp.sum(bg4 * sel_ref[gi * 2 + 1], axis=1, keepdims=True)
            gs = _mm_exact_lhs(ll, jnp.broadcast_to(gcol, (BD, 128)))
            g_cum = gs[:BD, :1]
            g_last = gs[BD:, :1]
            diff = _mm_exact_lhs(incl_b, gcol * strict)
            decay = jnp.where(incl > 0.0, jnp.exp(diff), 0.0)
            e_cum = jnp.exp(g_cum)

            km = _tile4(k[r0:r0 + CHUNK, c0:c0 + BD]) * bdm
            qm = _tile4(q[r0:r0 + CHUNK, c0:c0 + BD]) * bdm
            vm = _tile4(v[r0:r0 + CHUNK, c0:c0 + BD]) * bdm
            kkqk = _mm_nt(jnp.concatenate([km, qm], axis=0), km)
            kk = kkqk[:BD]
            qk = kkqk[BD:] * decay

            m = -(strict * beta * kk * decay)
            inv = eye + m
            sq = CHUNK
            while sq > 2:
                m = _mm(m, m)
                inv = inv + _mm(inv, m)
                sq //= 2
            rhs = jnp.concatenate([beta * vm, (beta * e_cum) * km], axis=1)
            uw = _mm(inv, rhs)
            u = uw[:, :BD]
            w = uw[:, BD:]

            state = st_ref[gi]
            wq = _mm(jnp.concatenate([w, qm * e_cum], axis=0), state)
            v_new = u - wq[:BD]
            o_bd = wq[BD:] + _mm(qk, v_new)
            k_dec = km * jnp.exp(g_last - g_cum)
            st_ref[gi] = state * jnp.exp(g_last) + _mm(k_dec.T, v_new)
            o_groups.append(o_bd[0:CHUNK] + o_bd[CHUNK:2 * CHUNK] + o_bd[2 * CHUNK:3 * CHUNK]
                            + o_bd[3 * CHUNK:])
        outs.append(jnp.concatenate(o_groups, axis=1))
    o = outs[0] if len(outs) == 1 else jnp.concatenate(outs, axis=0)

    ms = _mm_exact_lhs_t(o * o, m64) * (1.0 / A_DV)
    zz = zz_ref[...].astype(F32)
    o = o * lax.rsqrt(ms + EPS) * ag_ref[...] * (zz * jax.nn.sigmoid(zz))
    o_ref[...] = o.astype(BF16)


def _mm_exact_lhs_t(x, m_bf16):
    xh, xl = _split(x)
    return _dot(xh, m_bf16) + _dot(xl, m_bf16)


def _gdn(z3, zs3, conv_w, a_log, dt_bias, a_norm_g, t=CHUNK):
    b, s, _ = z3.shape
    consts = _gdn_constants()
    av = jnp.zeros((2, 128), F32)
    av = av.at[0, S_ALPHA:S_ALPHA + A_HEADS].set(a_log.astype(F32))
    av = av.at[1, S_ALPHA:S_ALPHA + A_HEADS].set(dt_bias.astype(F32))
    ag = jnp.tile(a_norm_g.astype(F32), A_HEADS)[None, :]
    conv_cols = 2 * A_HEADS * A_DK + A_HEADS * A_DV

    def const_spec(a):
        nd = a.ndim
        return pl.BlockSpec(a.shape, lambda bi, ci, _n=nd: (0,) * _n)

    small_in = (conv_w.astype(F32), av, ag) + consts
    return pl.pallas_call(
        _gdn_kernel,
        grid=(b, s // t),
        in_specs=[
            pl.BlockSpec((None, t, conv_cols), lambda bi, ci: (bi, ci, O_AQKV // conv_cols)),
            pl.BlockSpec((None, t, 512), lambda bi, ci: (bi, ci, O_AZ // 512)),
            pl.BlockSpec((None, t, 128), lambda bi, ci: (bi, ci, 0)),
        ] + [const_spec(a) for a in small_in],
        out_specs=pl.BlockSpec((None, t, A_HEADS * A_DV), lambda bi, ci: (bi, ci, 0)),
        out_shape=jax.ShapeDtypeStruct((b, s, A_HEADS * A_DV), BF16),
        scratch_shapes=[pltpu.VMEM((8 + t, conv_cols), F32),
                        pltpu.VMEM((A_HEADS // GROUP_HEADS, BD, BD), F32)],
        compiler_params=pltpu.CompilerParams(dimension_semantics=("parallel", "arbitrary"),
                                             vmem_limit_bytes=VMEM_LIMIT),
        name="gdn",
    )(z3, z3, zs3, *small_in)


def _rope(x, cs, sn, first):
    w = x.shape[1]
    swapped = jnp.where(first, pltpu.roll(x, w - B_HD // 2, 1), pltpu.roll(x, B_HD // 2, 1))
    return x * cs + swapped * sn


def _dsa_kernel(topk, tk, pos_ref, inv_ref, bq_ref, iq_ref, kd_ref, vd_ref, ikd_ref, zs_ref, o_ref,
                cos_ref, sin_ref, ka_ref, kb_ref, ik_ref, sc_ref):
    qb = pl.program_id(1)
    s = kd_ref.shape[0]
    tq = bq_ref.shape[0]
    nkt = s // tk

    @pl.when(qb == 0)
    def _():
        ang = pos_ref[...] * inv_ref[...]
        lane = lax.broadcasted_iota(jnp.int32, ang.shape, 1)
        first = (lane & (B_HD - 1)) < (B_HD // 2)
        cs = jnp.cos(ang)
        sn = jnp.sin(ang)
        sn = jnp.where(first, -sn, sn)
        cos_ref[...] = cs
        sin_ref[...] = sn
        kd = kd_ref[...].astype(F32)
        ka_ref[...] = _rope(kd[:, :128], cs, sn, first).astype(BF16)
        kb_ref[...] = _rope(kd[:, 128:], cs, sn, first).astype(BF16)
        ik_ref[...] = _rope(ikd_ref[...].astype(F32), cs, sn, first).astype(BF16)

    r0 = pl.multiple_of(qb * tq, tq)
    cs1 = cos_ref[pl.ds(r0, tq), :]
    sn1 = sin_ref[pl.ds(r0, tq), :]
    cs = jnp.concatenate([cs1] * 4, axis=1)
    sn = jnp.concatenate([sn1] * 4, axis=1)
    lane512 = lax.broadcasted_iota(jnp.int32, (tq, 4 * 128), 1)
    first512 = (lane512 & (B_HD - 1)) < (B_HD // 2)
    q = _rope(bq_ref[...].astype(F32), cs, sn, first512) * (B_HD ** -0.5)
    iq = _rope(iq_ref[...].astype(F32), cs, sn, first512)

    sm = zs_ref[...]
    lane128 = lax.broadcasted_iota(jnp.int32, (tq, 128), 1)
    iw_scale = (IDX_HEADS ** -0.5) * (IDX_HD ** -0.5)
    wcols = [jnp.sum(jnp.where(lane128 == S_IW + h, sm, 0.0), axis=1, keepdims=True) * iw_scale
             for h in range(IDX_HEADS)]
    low_half = lane128 < B_HD
    high_half = lane128 >= B_HD

    def head_slab(x, h):
        slab = x[:, (h // 2) * 128:(h // 2 + 1) * 128]
        return jnp.where(low_half if h % 2 == 0 else high_half, slab, 0.0)

    row = r0 + lax.broadcasted_iota(jnp.int32, (tq, 1), 0)
    limit = ((row >> 6) + 1) << 6

    ctx = (q, iq, wcols, limit, low_half, head_slab)
    refs = (vd_ref, o_ref, ka_ref, kb_ref, ik_ref, sc_ref)
    per = tk // tq
    for j in range(nkt):
        @pl.when(jnp.logical_and(qb >= j * per, qb < (j + 1) * per))
        def _(j=j):
            _dsa_block(topk, tk, (j + 1) * tk, ctx, refs)


def _dsa_block(topk, tk, width, ctx, refs):
    q, iq, wcols, limit, low_half, head_slab = ctx
    vd_ref, o_ref, ka_ref, kb_ref, ik_ref, sc_ref = refs
    tq = q.shape[0]
    nkt = width // tk

    for kt in range(nkt):
        ikt = ik_ref[kt * tk:(kt + 1) * tk, :]
        acc = jnp.zeros((tq, tk), F32)
        for h in range(IDX_HEADS):
            sc = _mm_nt(head_slab(iq, h), ikt)
            acc = acc + wcols[h] * jnp.maximum(sc, 0.0)
        kpos = kt * tk + lax.broadcasted_iota(jnp.int32, (tq, tk), 1)
        sc_ref[:, kt * tk:(kt + 1) * tk] = jnp.where(kpos < limit, acc, -jnp.inf)

    kf = float(topk)
    x0 = sc_ref[:, :width]
    row_max = jnp.max(x0, axis=1, keepdims=True)
    row_min = jnp.min(jnp.where(x0 == -jnp.inf, jnp.inf, x0), axis=1, keepdims=True)
    c_max = jnp.sum((x0 >= row_max).astype(F32), axis=1, keepdims=True)
    small = limit <= topk

    def bisect(_, carry):
        lo, hi = carry
        mid = 0.5 * (lo + hi)
        cnt = jnp.sum((sc_ref[:, :width] >= mid).astype(F32), axis=1, keepdims=True)
        ge = cnt >= kf
        return jnp.where(ge, mid, lo), jnp.where(ge, hi, mid)

    lo, hi = lax.fori_loop(0, BISECT_STEPS, bisect, (row_min, row_max))
    top_tied = c_max >= kf
    done0 = jnp.where(small | top_tied, 1.0, 0.0)
    thr0 = jnp.where(top_tied, row_max, lo)

    def peel_cond(carry):
        done, _, _ = carry
        return jnp.sum(1.0 - done) > 0.0

    def peel(carry):
        done, thr, hi_c = carry
        x = sc_ref[:, :width]
        v1 = jnp.max(jnp.where(x < hi_c, x, -jnp.inf), axis=1, keepdims=True)
        c1 = jnp.sum((x >= v1).astype(F32), axis=1, keepdims=True)
        ok = c1 >= kf
        act = done < 0.5
        thr = jnp.where(act & ok, v1, thr)
        hi_c = jnp.where(act & (~ok), v1, hi_c)
        done = jnp.where(ok, 1.0, done)
        return done, thr, hi_c

    _, thr, _ = lax.while_loop(peel_cond, peel, (done0, thr0, hi))

    x = sc_ref[:, :width]
    idx = lax.broadcasted_iota(jnp.int32, (tq, width), 1).astype(F32)
    gt = x > thr
    eq = x == thr
    need = kf - jnp.sum(gt.astype(F32), axis=1, keepdims=True)
    n_eq = jnp.sum(eq.astype(F32), axis=1, keepdims=True)
    contested = jnp.sum(jnp.where((n_eq > need) & (~small), 1.0, 0.0)) > 0.0

    def tie_search(_, carry):
        jlo, jhi = carry
        mid = jnp.floor(0.5 * (jlo + jhi))
        e = sc_ref[:, :width] == thr
        cnt = jnp.sum((e & (idx <= mid)).astype(F32), axis=1, keepdims=True)
        ge = cnt >= need
        return jnp.where(ge, jlo, mid), jnp.where(ge, mid, jhi)

    def tie_cut():
        n_iter = int(np.ceil(np.log2(width))) + 1
        return lax.fori_loop(0, n_iter, tie_search,
                             (jnp.full((tq, 1), -1.0, F32), jnp.full((tq, 1), float(width - 1), F32)))[1]

    jcut = lax.cond(contested, tie_cut, lambda: jnp.full((tq, 1), float(width - 1), F32))
    sel = (x > -jnp.inf) & (small | gt | (eq & (idx <= jcut)))
    sc_ref[:, :width] = jnp.where(sel, 0.0, NEG_BIG)

    heads_per_kv = B_HEADS // B_KV_HEADS
    for g in range(B_KV_HEADS):
        k_ref = ka_ref if g == 0 else kb_ref
        qrows = jnp.concatenate([head_slab(q, g * heads_per_kv + j) for j in range(heads_per_kv)],
                                axis=0).astype(BF16)
        nr = heads_per_kv * tq
        m_run = jnp.full((nr, 1), NEG_BIG, F32)
        l_run = jnp.zeros((nr, 1), F32)
        acc = jnp.zeros((nr, 128), F32)
        for kt in range(nkt):
            logits = lax.dot_general(qrows, k_ref[kt * tk:(kt + 1) * tk, :], (((1,), (1,)), ((), ())),
                                     preferred_element_type=F32)
            bias = sc_ref[:, kt * tk:(kt + 1) * tk]
            logits = logits + jnp.concatenate([bias] * heads_per_kv, axis=0)
            m_new = jnp.maximum(m_run, jnp.max(logits, axis=1, keepdims=True))
            p = jnp.exp(logits - m_new)
            alpha = jnp.exp(m_run - m_new)
            l_run = alpha * l_run + jnp.sum(p, axis=1, keepdims=True)
            acc = alpha * acc + _dot(p.astype(BF16), vd_ref[kt * tk:(kt + 1) * tk, g * 128:(g + 1) * 128])
            m_run = m_new
        og = acc / l_run
        for pp in range(heads_per_kv // 2):
            even = og[(2 * pp) * tq:(2 * pp + 1) * tq]
            odd = og[(2 * pp + 1) * tq:(2 * pp + 2) * tq]
            col = (g * (heads_per_kv // 2) + pp) * 128
            o_ref[:, col:col + 128] = jnp.where(low_half, even, odd).astype(BF16)


def _dsa(z3, zs3, positions, tq=128, tk=512):
    b, s, _ = z3.shape
    topk = min(TOPK_MAX, s // 4)
    half = B_HD // 2
    inv = ROPE_THETA ** (-jnp.arange(half, dtype=F32) / half)
    inv128 = jnp.tile(inv, 4)[None, :]
    pos = positions.astype(F32)[:, :, None]
    kernel = lambda *refs: _dsa_kernel(topk, tk, *refs)
    return pl.pallas_call(
        kernel,
        grid=(b, s // tq),
        in_specs=[
            pl.BlockSpec((None, s, 1), lambda bi, qi: (bi, 0, 0)),
            pl.BlockSpec((1, 128), lambda bi, qi: (0, 0)),
            pl.BlockSpec((None, tq, 512), lambda bi, qi: (bi, qi, O_BQ // 512)),
            pl.BlockSpec((None, tq, 512), lambda bi, qi: (bi, qi, O_IQ // 512)),
            pl.BlockSpec((None, s, 256), lambda bi, qi: (bi, 0, O_KD // 256)),
            pl.BlockSpec((None, s, 256), lambda bi, qi: (bi, 0, O_VD // 256)),
            pl.BlockSpec((None, s, 128), lambda bi, qi: (bi, 0, O_IK // 128)),
            pl.BlockSpec((None, tq, 128), lambda bi, qi: (bi, qi, 0)),
        ],
        out_specs=pl.BlockSpec((None, tq, B_HEADS * B_HD), lambda bi, qi: (bi, qi, 0)),
        out_shape=jax.ShapeDtypeStruct((b, s, B_HEADS * B_HD), BF16),
        scratch_shapes=[pltpu.VMEM((s, 128), F32), pltpu.VMEM((s, 128), F32),
                        pltpu.VMEM((s, 128), BF16), pltpu.VMEM((s, 128), BF16), pltpu.VMEM((s, 128), BF16),
                        pltpu.VMEM((tq, s), F32)],
        compiler_params=pltpu.CompilerParams(dimension_semantics=("parallel", "arbitrary"),
                                             vmem_limit_bytes=VMEM_LIMIT),
        name="dsa",
    )(pos, inv128, z3, z3, z3, z3, z3, zs3)


def _merge_kernel(x_ref, oa_ref, ob_ref, ga_ref, gb_ref, bg_ref, wa_ref, wb_ref, wo_ref, n2_ref, wr_ref,
                  br_ref, x1_ref, h2_ref, rl_ref):
    pa = _dot(oa_ref[...], wa_ref[...])
    pb = _dot(ob_ref[...], wb_ref[...])
    bgv = bg_ref[...]
    ga = jax.nn.sigmoid(ga_ref[...].astype(F32) + bgv[:, :D_MODEL])
    gb = jax.nn.sigmoid(gb_ref[...].astype(F32) + bgv[:, D_MODEL:])
    merged = ga * pa + gb * pb
    x1 = x_ref[...] + _mm(merged, wo_ref[...])
    x1_ref[...] = x1
    h2 = _rmsnorm_rows(x1, n2_ref[...])
    hb, hl = _split(h2)
    h2_ref[...] = hb
    s1 = _dot(hb, wr_ref[...])
    s2 = _dot(hl, wr_ref[:, :128])
    rl_ref[...] = s1[:, :128] + s1[:, 128:] + s2 + br_ref[...]


def _merge(x2d, oa, ob, z2d, b_gate, wa, wb, wo, n2, wr, br, tm=512):
    n = x2d.shape[0]
    full = lambda shape: pl.BlockSpec(shape, lambda i: (0, 0))
    return pl.pallas_call(
        _merge_kernel,
        grid=(n // tm,),
        in_specs=[
            pl.BlockSpec((tm, D_MODEL), lambda i: (i, 0)),
            pl.BlockSpec((tm, 512), lambda i: (i, 0)),
            pl.BlockSpec((tm, 512), lambda i: (i, 0)),
            pl.BlockSpec((tm, D_MODEL), lambda i: (i, O_GA // D_MODEL)),
            pl.BlockSpec((tm, D_MODEL), lambda i: (i, O_GB // D_MODEL)),
            full((1, 2 * D_MODEL)), full((512, D_MODEL)), full((512, D_MODEL)), full((D_MODEL, D_MODEL)),
            full((1, D_MODEL)), full((D_MODEL, 256)), full((1, 128)),
        ],
        out_specs=[
            pl.BlockSpec((tm, D_MODEL), lambda i: (i, 0)),
            pl.BlockSpec((tm, D_MODEL), lambda i: (i, 0)),
            pl.BlockSpec((tm, 128), lambda i: (i, 0)),
        ],
        out_shape=[jax.ShapeDtypeStruct((n, D_MODEL), F32), jax.ShapeDtypeStruct((n, D_MODEL), BF16),
                   jax.ShapeDtypeStruct((n, 128), F32)],
        compiler_params=pltpu.CompilerParams(dimension_semantics=("parallel",),
                                             vmem_limit_bytes=VMEM_LIMIT),
        name="merge",
    )(x2d, oa, ob, z2d, z2d, b_gate, wa, wb, wo, n2, wr, br)


R_GROUP, R_EXPERT = 0, N_GROUPS


def _routing_weights(rl):
    lane = lax.broadcasted_iota(jnp.int32, rl.shape, 1)
    big = jnp.int32(1 << 20)
    gmask = lane < N_GROUPS
    gl = jnp.where(gmask, rl, -jnp.inf)
    gmax = jnp.max(gl, axis=1, keepdims=True)
    gsel = jnp.min(jnp.where(gmask & (rl == gmax), lane, big), axis=1, keepdims=True)
    ggate = 1.0 / jnp.sum(jnp.where(gmask, jnp.exp(gl - gmax), 0.0), axis=1, keepdims=True)
    e_lo = R_EXPERT + gsel * EXPERTS_PER_GROUP
    emask = (lane >= e_lo) & (lane < e_lo + EXPERTS_PER_GROUP)
    el = jnp.where(emask, rl, -jnp.inf)
    emax = jnp.max(el, axis=1, keepdims=True)
    ee = jnp.where(emask, jnp.exp(el - emax), 0.0)
    ep = jnp.where(emask, ee / jnp.sum(ee, axis=1, keepdims=True), -1.0)
    p1 = jnp.max(ep, axis=1, keepdims=True)
    i1 = jnp.min(jnp.where(ep == p1, lane, big), axis=1, keepdims=True)
    ep2 = jnp.where(lane == i1, -1.0, ep)
    p2 = jnp.max(ep2, axis=1, keepdims=True)
    i2 = jnp.min(jnp.where(ep2 == p2, lane, big), axis=1, keepdims=True)
    tot = p1 + p2
    return (jnp.where(lane == i1, ggate * (p1 / tot), 0.0)
            + jnp.where(lane == i2, ggate * (p2 / tot), 0.0))


def _moe_kernel(x1_ref, h2_ref, rl_ref, w1_ref, w3_ref, w2_ref, fg_ref, o_ref, y_ref, comb_ref):
    e = pl.program_id(1)

    @pl.when(e == 0)
    def _():
        comb_ref[...] = _routing_weights(rl_ref[...])
        y_ref[...] = jnp.zeros(y_ref.shape, F32)

    h = h2_ref[...]
    a = _dot(h, w1_ref[...])
    b = _dot(h, w3_ref[...])
    comb = comb_ref[...]
    lane = lax.broadcasted_iota(jnp.int32, comb.shape, 1)
    ce = jnp.sum(jnp.where(lane == R_EXPERT + e, comb, 0.0), axis=1, keepdims=True)
    act = (a * jax.nn.sigmoid(a)) * b * ce
    y_ref[...] += _dot(act.astype(BF16), w2_ref[...])

    @pl.when(e == pl.num_programs(1) - 1)
    def _():
        o_ref[...] = _rmsnorm_rows(x1_ref[...] + y_ref[...], fg_ref[...])


def _moe(x1, h2, rl, w1, w3, w2, fg, tm=1024):
    n = x1.shape[0]
    return pl.pallas_call(
        _moe_kernel,
        grid=(n // tm, N_EXPERTS),
        in_specs=[
            pl.BlockSpec((tm, D_MODEL), lambda i, e: (i, 0)),
            pl.BlockSpec((tm, D_MODEL), lambda i, e: (i, 0)),
            pl.BlockSpec((tm, 128), lambda i, e: (i, 0)),
            pl.BlockSpec((None, D_MODEL, D_EXPERT), lambda i, e: (e, 0, 0)),
            pl.BlockSpec((None, D_MODEL, D_EXPERT), lambda i, e: (e, 0, 0)),
            pl.BlockSpec((None, D_EXPERT, D_MODEL), lambda i, e: (e, 0, 0)),
            pl.BlockSpec((1, D_MODEL), lambda i, e: (0, 0)),
        ],
        out_specs=pl.BlockSpec((tm, D_MODEL), lambda i, e: (i, 0)),
        out_shape=jax.ShapeDtypeStruct((n, D_MODEL), F32),
        scratch_shapes=[pltpu.VMEM((tm, D_MODEL), F32), pltpu.VMEM((tm, 128), F32)],
        compiler_params=pltpu.CompilerParams(dimension_semantics=("parallel", "arbitrary"),
                                             vmem_limit_bytes=VMEM_LIMIT),
        name="moe",
    )(x1, h2, rl, w1, w3, w2, fg)


_W_OFF = {}
_off = 0
for _name, _n in (("a_q", 512), ("a_k", 512), ("a_v", 512), ("a_z", 512), ("a_beta", 8), ("a_alpha", 8),
                  ("b_q", 512), ("b_k", 128), ("b_v", 128), ("i_q", 512), ("i_k", 64), ("i_w", 8),
                  ("gate_a", 1024), ("gate_b", 1024)):
    _W_OFF[_name] = (_off, _off + _n)
    _off += _n


def _cols(w, name, lo=0, hi=None):
    a, b = _W_OFF[name]
    return w[:, a + lo:(b if hi is None else a + hi)]


def _hi_lo_pair(w_small):
    pad = jnp.zeros((w_small.shape[0], 128 - w_small.shape[1]), F32)
    w = jnp.concatenate([w_small.astype(F32), pad], axis=1)
    hi, lo = _split(w)
    return jnp.concatenate([hi, lo], axis=1)


def _layout_w_in(w):
    k0, k1 = _cols(w, "b_k", 0, 64), _cols(w, "b_k", 64, 128)
    v0, v1 = _cols(w, "b_v", 0, 64), _cols(w, "b_v", 64, 128)
    ik = _cols(w, "i_k")
    main = jnp.concatenate([
        _cols(w, "a_q"), _cols(w, "a_k"), _cols(w, "a_v"), _cols(w, "a_z"), _cols(w, "gate_a"),
        _cols(w, "gate_b"), _cols(w, "b_q"), _cols(w, "i_q"), k0, k0, k1, k1, v0, v0, v1, v1, ik, ik],
        axis=1).astype(BF16)
    small = _hi_lo_pair(jnp.concatenate([_cols(w, "a_beta"), _cols(w, "a_alpha"), _cols(w, "i_w")], axis=1))
    return main, small


def kernel(x, positions, norm1_g, w_in, b_gate, conv_w, a_log, dt_bias, a_norm_g, w_proj_a, w_proj_b, w_out,
           norm2_g, w_router_group, b_router_group, w_router_expert, b_router_expert, w_exp_gate, w_exp_up,
           w_exp_down, final_norm_g):
    b, s, d = x.shape
    n = b * s
    depth = w_in.shape[0]
    xc = x.reshape(n, d).astype(F32)
    for l in range(depth):
        w_main, w_small = _layout_w_in(w_in[l])
        z, zs = _in_proj(xc, norm1_g[l][None, :].astype(F32), w_main, w_small)
        z3 = z.reshape(b, s, Z_W)
        zs3 = zs.reshape(b, s, 128)
        o_a = _gdn(z3, zs3, conv_w[l], a_log[l], dt_bias[l], a_norm_g[l])
        o_b = _dsa(z3, zs3, positions)
        wr = _hi_lo_pair(jnp.concatenate([w_router_group[l], w_router_expert[l]], axis=1))
        br = jnp.concatenate([b_router_group[l], b_router_expert[l],
                              jnp.zeros((128 - N_GROUPS - N_EXPERTS,), F32)])[None, :].astype(F32)
        x1, h2, rl = _merge(xc, o_a.reshape(n, -1), o_b.reshape(n, -1), z, b_gate[l][None, :].astype(F32),
                            w_proj_a[l].astype(BF16), w_proj_b[l].astype(BF16), w_out[l].astype(BF16),
                            norm2_g[l][None, :].astype(F32), wr, br)
        last = l == depth - 1
        fg = final_norm_g[None, :].astype(F32) if last else None
        if not last:
            raise NotImplementedError("only the final layer fuses the output norm")
        xc = _moe(x1, h2, rl, w_exp_gate[l].astype(BF16), w_exp_up[l].astype(BF16),
                  w_exp_down[l].astype(BF16), fg)
    return xc.reshape(b, s, d).astype(x.dtype)
```

```python
import numpy as np
import jax
import jax.numpy as jnp
from jax import lax
from jax.experimental import pallas as pl
from jax.experimental.pallas import tpu as pltpu

F32 = jnp.float32
BF16 = jnp.bfloat16

D_MODEL = 1024
CHUNK = 64
EPS = 1e-6
ROPE_THETA = 10000.0
A_HEADS = 8
A_DK = 64
A_DV = 64
CONV_K = 4
B_HEADS = 8
B_KV_HEADS = 2
B_HD = 64
IDX_HEADS = 8
IDX_HD = 64
TOPK_MAX = 256
N_GROUPS = 4
EXPERTS_PER_GROUP = 4
N_EXPERTS = 16
D_EXPERT = 256

O_AQKV, O_AZ, O_GA, O_GB, O_BQ, O_IQ, O_KD, O_VD, O_IK = 0, 1536, 2048, 3072, 4096, 4608, 5120, 5376, 5632
Z_W = 5760
S_BETA, S_ALPHA, S_IW = 0, 8, 16

GROUP_HEADS = 4
BD = GROUP_HEADS * CHUNK
NEG_BIG = -1e30
BISECT_STEPS = 18
VMEM_LIMIT = 56 * 1024 * 1024


def _split(x):
    hi = x.astype(BF16)
    lo = (x - hi.astype(F32)).astype(BF16)
    return hi, lo


def _dot(a, b):
    return jnp.dot(a, b, preferred_element_type=F32)


def _mm(a, b):
    return _dot(a.astype(BF16), b.astype(BF16))


def _mm_nt(a, b):
    return lax.dot_general(a.astype(BF16), b.astype(BF16), (((1,), (1,)), ((), ())),
                           preferred_element_type=F32)


def _mm_exact_lhs(a_bf16, x):
    xh, xl = _split(x)
    return _dot(a_bf16, xh) + _dot(a_bf16, xl)


def _rmsnorm_rows(x, g):
    return x * lax.rsqrt(jnp.mean(x * x, axis=-1, keepdims=True) + EPS) * g


_Z_CHUNKS = tuple((o, min(512, Z_W - o)) for o in range(0, Z_W, 512))


def _in_proj_kernel(x_ref, g_ref, w_ref, ws_ref, z_ref, zs_ref):
    h = _rmsnorm_rows(x_ref[...], g_ref[...])
    hb, hl = _split(h)
    for o, w in _Z_CHUNKS:
        z_ref[:, o:o + w] = _dot(hb, w_ref[:, o:o + w]).astype(BF16)
    s1 = _dot(hb, ws_ref[...])
    s2 = _dot(hl, ws_ref[:, :128])
    zs_ref[...] = s1[:, :128] + s1[:, 128:] + s2


def _in_proj(x2d, g, w_main, w_small, tm=512):
    n = x2d.shape[0]
    return pl.pallas_call(
        _in_proj_kernel,
        grid=(n // tm,),
        in_specs=[
            pl.BlockSpec((tm, D_MODEL), lambda i: (i, 0)),
            pl.BlockSpec((1, D_MODEL), lambda i: (0, 0)),
            pl.BlockSpec((D_MODEL, Z_W), lambda i: (0, 0)),
            pl.BlockSpec((D_MODEL, 256), lambda i: (0, 0)),
        ],
        out_specs=[
            pl.BlockSpec((tm, Z_W), lambda i: (i, 0)),
            pl.BlockSpec((tm, 128), lambda i: (i, 0)),
        ],
        out_shape=[jax.ShapeDtypeStruct((n, Z_W), BF16), jax.ShapeDtypeStruct((n, 128), F32)],
        compiler_params=pltpu.CompilerParams(dimension_semantics=("parallel",),
                                             vmem_limit_bytes=VMEM_LIMIT),
        name="in_proj",
    )(x2d, g, w_main, w_small)


def _gdn_constants():
    r = np.arange(BD)
    same = (r[:, None] // CHUNK) == (r[None, :] // CHUNK)
    incl = same & (r[:, None] >= r[None, :])
    strict = same & (r[:, None] > r[None, :])
    eye = np.eye(BD, dtype=np.float32)
    ll = np.concatenate([incl, same], axis=0).astype(np.float32)
    sel = np.zeros((4, BD, 128), np.float32)
    for gi in range(2):
        for h in range(GROUP_HEADS):
            sel[gi * 2 + 0, h * CHUNK:(h + 1) * CHUNK, S_BETA + gi * GROUP_HEADS + h] = 1.0
            sel[gi * 2 + 1, h * CHUNK:(h + 1) * CHUNK, S_ALPHA + gi * GROUP_HEADS + h] = 1.0
    return (jnp.asarray(incl, F32), jnp.asarray(strict, F32), jnp.asarray(same, F32), jnp.asarray(eye),
            jnp.asarray(ll, BF16), jnp.asarray(sel))


def _tile4(x):
    return jnp.concatenate([x, x, x, x], axis=0)


def _gdn_kernel(zq_ref, zz_ref, zs_ref, cw_ref, av_ref, ag_ref, incl_ref, strict_ref, bdm_ref, eye_ref,
                ll_ref, sel_ref, o_ref, ext_ref, st_ref):
    c = pl.program_id(1)
    t = zq_ref.shape[0]

    @pl.when(c == 0)
    def _():
        ext_ref[0:8, :] = jnp.zeros((8, ext_ref.shape[1]), F32)
        st_ref[...] = jnp.zeros(st_ref.shape, F32)

    ext_ref[8:8 + t, :] = zq_ref[...].astype(F32)
    cw = cw_ref[...]
    y = cw[0:1, :] * ext_ref[pl.ds(8 - (CONV_K - 1), t), :]
    for j in range(1, CONV_K):
        y = y + cw[j:j + 1, :] * ext_ref[pl.ds(8 - (CONV_K - 1) + j, t), :]
    ext_ref[0:8, :] = ext_ref[t:t + 8, :]
    y = y * jax.nn.sigmoid(y)

    hw = A_HEADS * A_DK
    q_all = y[:, :hw]
    k_all = y[:, hw:2 * hw]
    v_all = y[:, 2 * hw:]

    sm = zs_ref[...]
    av = av_ref[...]
    lane = lax.broadcasted_iota(jnp.int32, sm.shape, 1)
    xg = sm + av[1:2, :]
    softplus = jnp.maximum(xg, 0.0) + jnp.log1p(jnp.exp(-jnp.abs(xg)))
    g_all = -jnp.exp(av[0:1, :]) * softplus
    bg = jnp.where(lane < S_ALPHA, jax.nn.sigmoid(sm), g_all)

    incl = incl_ref[...]
    strict = strict_ref[...]
    bdm = bdm_ref[...]
    eye = eye_ref[...]
    ll = ll_ref[...]
    incl_b = ll[:BD]

    n_chunks = t // CHUNK
    n_groups = A_HEADS // GROUP_HEADS
    chains = [(ci, gi) for ci in range(n_chunks) for gi in range(n_groups)]

    pre = {}
    for ci, gi in chains:
        r0, c0 = ci * CHUNK, gi * BD
        bg4 = _tile4(bg[r0:r0 + CHUNK])
        beta = jnp.sum(bg4 * sel_ref[gi * 2 + 0], axis=1, keepdims=True)
        gcol = jnp.sum(bg4 * sel_ref[gi * 2 + 1], axis=1, keepdims=True)
        gs = _mm_exact_lhs(ll, jnp.broadcast_to(gcol, (BD, 128)))
        g_cum = gs[:BD, :1]
        g_last = gs[BD:, :1]
        diff = _mm_exact_lhs(incl_b, gcol * strict)
        decay = jnp.where(incl > 0.0, jnp.exp(diff), 0.0)
        e_cum = jnp.exp(g_cum)
        kr = _tile4(k_all[r0:r0 + CHUNK, c0:c0 + BD]) * bdm
        qr = _tile4(q_all[r0:r0 + CHUNK, c0:c0 + BD]) * bdm
        vm = _tile4(v_all[r0:r0 + CHUNK, c0:c0 + BD]) * bdm
        km = kr * lax.rsqrt(jnp.sum(kr * kr, axis=1, keepdims=True) + EPS)
        qm = qr * (lax.rsqrt(jnp.sum(qr * qr, axis=1, keepdims=True) + EPS) * (A_DK ** -0.5))
        kkqk = _mm_nt(jnp.concatenate([km, qm], axis=0), km)
        m = -(strict * beta * kkqk[:BD] * decay)
        pre[ci, gi] = dict(beta=beta, g_cum=g_cum, g_last=g_last, e_cum=e_cum, km=km, qm=qm, vm=vm,
                           qk=kkqk[BD:] * decay, m=m, inv=eye + m)

    sq = CHUNK
    while sq > 2:
        for key in chains:
            p = pre[key]
            p["m"] = _mm(p["m"], p["m"])
        for key in chains:
            p = pre[key]
            p["inv"] = p["inv"] + _mm(p["inv"], p["m"])
        sq //= 2

    outs = []
    for ci in range(n_chunks):
        o_groups = []
        for gi in range(n_groups):
            p = pre[ci, gi]
            state = st_ref[gi]
            kq_s = _mm(jnp.concatenate([p["km"] * p["e_cum"], p["qm"] * p["e_cum"]], axis=0), state)
            v_new = _mm(p["inv"], p["beta"] * (p["vm"] - kq_s[:BD]))
            o_bd = kq_s[BD:] + _mm(p["qk"], v_new)
            k_dec = p["km"] * jnp.exp(p["g_last"] - p["g_cum"])
            st_ref[gi] = state * jnp.exp(p["g_last"]) + _mm(k_dec.T, v_new)
            o_bd = o_bd * lax.rsqrt(jnp.sum(o_bd * o_bd, axis=1, keepdims=True) * (1.0 / A_DV) + EPS)
            o_groups.append(o_bd[0:CHUNK] + o_bd[CHUNK:2 * CHUNK] + o_bd[2 * CHUNK:3 * CHUNK]
                            + o_bd[3 * CHUNK:])
        outs.append(jnp.concatenate(o_groups, axis=1))
    o = outs[0] if len(outs) == 1 else jnp.concatenate(outs, axis=0)

    zz = zz_ref[...].astype(F32)
    o_ref[...] = (o * ag_ref[...] * (zz * jax.nn.sigmoid(zz))).astype(BF16)


def _gdn(z3, zs3, conv_w, a_log, dt_bias, a_norm_g, t=4 * CHUNK):
    b, s, _ = z3.shape
    consts = _gdn_constants()
    av = jnp.zeros((2, 128), F32)
    av = av.at[0, S_ALPHA:S_ALPHA + A_HEADS].set(a_log.astype(F32))
    av = av.at[1, S_ALPHA:S_ALPHA + A_HEADS].set(dt_bias.astype(F32))
    ag = jnp.tile(a_norm_g.astype(F32), A_HEADS)[None, :]
    conv_cols = 2 * A_HEADS * A_DK + A_HEADS * A_DV

    def const_spec(a):
        nd = a.ndim
        return pl.BlockSpec(a.shape, lambda bi, ci, _n=nd: (0,) * _n)

    small_in = (conv_w.astype(F32), av, ag) + consts
    return pl.pallas_call(
        _gdn_kernel,
        grid=(b, s // t),
        in_specs=[
            pl.BlockSpec((None, t, conv_cols), lambda bi, ci: (bi, ci, O_AQKV // conv_cols)),
            pl.BlockSpec((None, t, 512), lambda bi, ci: (bi, ci, O_AZ // 512)),
            pl.BlockSpec((None, t, 128), lambda bi, ci: (bi, ci, 0)),
        ] + [const_spec(a) for a in small_in],
        out_specs=pl.BlockSpec((None, t, A_HEADS * A_DV), lambda bi, ci: (bi, ci, 0)),
        out_shape=jax.ShapeDtypeStruct((b, s, A_HEADS * A_DV), BF16),
        scratch_shapes=[pltpu.VMEM((8 + t, conv_cols), F32),
                        pltpu.VMEM((A_HEADS // GROUP_HEADS, BD, BD), F32)],
        compiler_params=pltpu.CompilerParams(dimension_semantics=("parallel", "arbitrary"),
                                             vmem_limit_bytes=VMEM_LIMIT),
        name="gdn",
    )(z3, z3, zs3, *small_in)


def _rope(x, cs, sn, first):
    w = x.shape[1]
    swapped = jnp.where(first, pltpu.roll(x, w - B_HD // 2, 1), pltpu.roll(x, B_HD // 2, 1))
    return x * cs + swapped * sn


def _dsa_kernel(topk, tk, pos_ref, inv_ref, bq_ref, iq_ref, kd_ref, vd_ref, ikd_ref, zs_ref, o_ref,
                cos_ref, sin_ref, ka_ref, kb_ref, ik_ref, sc_ref):
    qb = pl.program_id(1)
    s = kd_ref.shape[0]
    tq = bq_ref.shape[0]
    nkt = s // tk

    @pl.when(qb == 0)
    def _():
        ang = pos_ref[...] * inv_ref[...]
        lane = lax.broadcasted_iota(jnp.int32, ang.shape, 1)
        first = (lane & (B_HD - 1)) < (B_HD // 2)
        cs = jnp.cos(ang)
        sn = jnp.sin(ang)
        sn = jnp.where(first, -sn, sn)
        cos_ref[...] = cs
        sin_ref[...] = sn
        kd = kd_ref[...].astype(F32)
        ka_ref[...] = _rope(kd[:, :128], cs, sn, first).astype(BF16)
        kb_ref[...] = _rope(kd[:, 128:], cs, sn, first).astype(BF16)
        ik_ref[...] = _rope(ikd_ref[...].astype(F32), cs, sn, first).astype(BF16)

    r0 = pl.multiple_of(qb * tq, tq)
    cs1 = cos_ref[pl.ds(r0, tq), :]
    sn1 = sin_ref[pl.ds(r0, tq), :]
    cs = jnp.concatenate([cs1] * 4, axis=1)
    sn = jnp.concatenate([sn1] * 4, axis=1)
    lane512 = lax.broadcasted_iota(jnp.int32, (tq, 4 * 128), 1)
    first512 = (lane512 & (B_HD - 1)) < (B_HD // 2)
    q = _rope(bq_ref[...].astype(F32), cs, sn, first512) * (B_HD ** -0.5)
    iq = _rope(iq_ref[...].astype(F32), cs, sn, first512)

    sm = zs_ref[...]
    lane128 = lax.broadcasted_iota(jnp.int32, (tq, 128), 1)
    iw_scale = (IDX_HEADS ** -0.5) * (IDX_HD ** -0.5)
    wcols = [jnp.sum(jnp.where(lane128 == S_IW + h, sm, 0.0), axis=1, keepdims=True) * iw_scale
             for h in range(IDX_HEADS)]
    low_half = lane128 < B_HD
    high_half = lane128 >= B_HD

    def head_slab(x, h):
        slab = x[:, (h // 2) * 128:(h // 2 + 1) * 128]
        return jnp.where(low_half if h % 2 == 0 else high_half, slab, 0.0)

    row = r0 + lax.broadcasted_iota(jnp.int32, (tq, 1), 0)
    limit = ((row >> 6) + 1) << 6

    ctx = (q, iq, wcols, limit, low_half, head_slab)
    refs = (vd_ref, o_ref, ka_ref, kb_ref, ik_ref, sc_ref)
    per = tk // tq
    for j in range(nkt):
        @pl.when(jnp.logical_and(qb >= j * per, qb < (j + 1) * per))
        def _(j=j):
            _dsa_block(topk, tk, (j + 1) * tk, ctx, refs)


def _dsa_block(topk, tk, width, ctx, refs):
    q, iq, wcols, limit, low_half, head_slab = ctx
    vd_ref, o_ref, ka_ref, kb_ref, ik_ref, sc_ref = refs
    tq = q.shape[0]
    nkt = width // tk

    for kt in range(nkt):
        ikt = ik_ref[kt * tk:(kt + 1) * tk, :]
        acc = jnp.zeros((tq, tk), F32)
        for h in range(IDX_HEADS):
            sc = _mm_nt(head_slab(iq, h), ikt)
            acc = acc + wcols[h] * jnp.maximum(sc, 0.0)
        kpos = kt * tk + lax.broadcasted_iota(jnp.int32, (tq, tk), 1)
        sc_ref[:, kt * tk:(kt + 1) * tk] = jnp.where(kpos < limit, acc, -jnp.inf)

    kf = float(topk)
    x0 = sc_ref[:, :width]
    row_max = jnp.max(x0, axis=1, keepdims=True)
    row_min = jnp.min(jnp.where(x0 == -jnp.inf, jnp.inf, x0), axis=1, keepdims=True)
    c_max = jnp.sum((x0 >= row_max).astype(F32), axis=1, keepdims=True)
    small = limit <= topk

    def bisect(_, carry):
        lo, hi = carry
        mid = 0.5 * (lo + hi)
        cnt = jnp.sum((sc_ref[:, :width] >= mid).astype(F32), axis=1, keepdims=True)
        ge = cnt >= kf
        return jnp.where(ge, mid, lo), jnp.where(ge, hi, mid)

    lo, hi = lax.fori_loop(0, BISECT_STEPS, bisect, (row_min, row_max))
    top_tied = c_max >= kf
    done0 = jnp.where(small | top_tied, 1.0, 0.0)
    thr0 = jnp.where(top_tied, row_max, lo)

    def peel_cond(carry):
        done, _, _ = carry
        return jnp.sum(1.0 - done) > 0.0

    def peel(carry):
        done, thr, hi_c = carry
        x = sc_ref[:, :width]
        v1 = jnp.max(jnp.where(x < hi_c, x, -jnp.inf), axis=1, keepdims=True)
        c1 = jnp.sum((x >= v1).astype(F32), axis=1, keepdims=True)
        ok = c1 >= kf
        act = done < 0.5
        thr = jnp.where(act & ok, v1, thr)
        hi_c = jnp.where(act & (~ok), v1, hi_c)
        done = jnp.where(ok, 1.0, done)
        return done, thr, hi_c

    _, thr, _ = lax.while_loop(peel_cond, peel, (done0, thr0, hi))

    x = sc_ref[:, :width]
    idx = lax.broadcasted_iota(jnp.int32, (tq, width), 1).astype(F32)
    gt = x > thr
    eq = x == thr
    need = kf - jnp.sum(gt.astype(F32), axis=1, keepdims=True)
    n_eq = jnp.sum(eq.astype(F32), axis=1, keepdims=True)
    contested = jnp.sum(jnp.where((n_eq > need) & (~small), 1.0, 0.0)) > 0.0

    def tie_search(_, carry):
        jlo, jhi = carry
        mid = jnp.floor(0.5 * (jlo + jhi))
        e = sc_ref[:, :width] == thr
        cnt = jnp.sum((e & (idx <= mid)).astype(F32), axis=1, keepdims=True)
        ge = cnt >= need
        return jnp.where(ge, jlo, mid), jnp.where(ge, mid, jhi)

    def tie_cut():
        n_iter = int(np.ceil(np.log2(width))) + 1
        return lax.fori_loop(0, n_iter, tie_search,
                             (jnp.full((tq, 1), -1.0, F32), jnp.full((tq, 1), float(width - 1), F32)))[1]

    jcut = lax.cond(contested, tie_cut, lambda: jnp.full((tq, 1), float(width - 1), F32))
    sel = (x > -jnp.inf) & (small | gt | (eq & (idx <= jcut)))
    sc_ref[:, :width] = jnp.where(sel, 0.0, NEG_BIG)

    heads_per_kv = B_HEADS // B_KV_HEADS
    for g in range(B_KV_HEADS):
        k_ref = ka_ref if g == 0 else kb_ref
        qrows = jnp.concatenate([head_slab(q, g * heads_per_kv + j) for j in range(heads_per_kv)],
                                axis=0).astype(BF16)
        nr = heads_per_kv * tq
        m_run = jnp.full((nr, 1), NEG_BIG, F32)
        l_run = jnp.zeros((nr, 1), F32)
        acc = jnp.zeros((nr, 128), F32)
        for kt in range(nkt):
            logits = lax.dot_general(qrows, k_ref[kt * tk:(kt + 1) * tk, :], (((1,), (1,)), ((), ())),
                                     preferred_element_type=F32)
            bias = sc_ref[:, kt * tk:(kt + 1) * tk]
            logits = logits + jnp.concatenate([bias] * heads_per_kv, axis=0)
            m_new = jnp.maximum(m_run, jnp.max(logits, axis=1, keepdims=True))
            p = jnp.exp(logits - m_new)
            alpha = jnp.exp(m_run - m_new)
            l_run = alpha * l_run + jnp.sum(p, axis=1, keepdims=True)
            acc = alpha * acc + _dot(p.astype(BF16), vd_ref[kt * tk:(kt + 1) * tk, g * 128:(g + 1) * 128])
            m_run = m_new
        og = acc / l_run
        for pp in range(heads_per_kv // 2):
            even = og[(2 * pp) * tq:(2 * pp + 1) * tq]
            odd = og[(2 * pp + 1) * tq:(2 * pp + 2) * tq]
            col = (g * (heads_per_kv // 2) + pp) * 128
            o_ref[:, col:col + 128] = jnp.where(low_half, even, odd).astype(BF16)


def _dsa(z3, zs3, positions, tq=256, tk=512):
    b, s, _ = z3.shape
    topk = min(TOPK_MAX, s // 4)
    half = B_HD // 2
    inv = ROPE_THETA ** (-jnp.arange(half, dtype=F32) / half)
    inv128 = jnp.tile(inv, 4)[None, :]
    pos = positions.astype(F32)[:, :, None]
    kernel = lambda *refs: _dsa_kernel(topk, tk, *refs)
    return pl.pallas_call(
        kernel,
        grid=(b, s // tq),
        in_specs=[
            pl.BlockSpec((None, s, 1), lambda bi, qi: (bi, 0, 0)),
            pl.BlockSpec((1, 128), lambda bi, qi: (0, 0)),
            pl.BlockSpec((None, tq, 512), lambda bi, qi: (bi, qi, O_BQ // 512)),
            pl.BlockSpec((None, tq, 512), lambda bi, qi: (bi, qi, O_IQ // 512)),
            pl.BlockSpec((None, s, 256), lambda bi, qi: (bi, 0, O_KD // 256)),
            pl.BlockSpec((None, s, 256), lambda bi, qi: (bi, 0, O_VD // 256)),
            pl.BlockSpec((None, s, 128), lambda bi, qi: (bi, 0, O_IK // 128)),
            pl.BlockSpec((None, tq, 128), lambda bi, qi: (bi, qi, 0)),
        ],
        out_specs=pl.BlockSpec((None, tq, B_HEADS * B_HD), lambda bi, qi: (bi, qi, 0)),
        out_shape=jax.ShapeDtypeStruct((b, s, B_HEADS * B_HD), BF16),
        scratch_shapes=[pltpu.VMEM((s, 128), F32), pltpu.VMEM((s, 128), F32),
                        pltpu.VMEM((s, 128), BF16), pltpu.VMEM((s, 128), BF16), pltpu.VMEM((s, 128), BF16),
                        pltpu.VMEM((tq, s), F32)],
        compiler_params=pltpu.CompilerParams(dimension_semantics=("parallel", "arbitrary"),
                                             vmem_limit_bytes=VMEM_LIMIT),
        name="dsa",
    )(pos, inv128, z3, z3, z3, z3, z3, zs3)


def _merge_kernel(x_ref, oa_ref, ob_ref, ga_ref, gb_ref, bg_ref, wa_ref, wb_ref, wo_ref, n2_ref, wr_ref,
                  br_ref, x1_ref, h2_ref, rl_ref):
    pa = _dot(oa_ref[...], wa_ref[...])
    pb = _dot(ob_ref[...], wb_ref[...])
    bgv = bg_ref[...]
    ga = jax.nn.sigmoid(ga_ref[...].astype(F32) + bgv[:, :D_MODEL])
    gb = jax.nn.sigmoid(gb_ref[...].astype(F32) + bgv[:, D_MODEL:])
    merged = ga * pa + gb * pb
    x1 = x_ref[...] + _mm(merged, wo_ref[...])
    x1_ref[...] = x1
    h2 = _rmsnorm_rows(x1, n2_ref[...])
    hb, hl = _split(h2)
    h2_ref[...] = hb
    s1 = _dot(hb, wr_ref[...])
    s2 = _dot(hl, wr_ref[:, :128])
    rl_ref[...] = s1[:, :128] + s1[:, 128:] + s2 + br_ref[...]


def _merge(x2d, oa, ob, z2d, b_gate, wa, wb, wo, n2, wr, br, tm=512):
    n = x2d.shape[0]
    full = lambda shape: pl.BlockSpec(shape, lambda i: (0, 0))
    return pl.pallas_call(
        _merge_kernel,
        grid=(n // tm,),
        in_specs=[
            pl.BlockSpec((tm, D_MODEL), lambda i: (i, 0)),
            pl.BlockSpec((tm, 512), lambda i: (i, 0)),
            pl.BlockSpec((tm, 512), lambda i: (i, 0)),
            pl.BlockSpec((tm, D_MODEL), lambda i: (i, O_GA // D_MODEL)),
            pl.BlockSpec((tm, D_MODEL), lambda i: (i, O_GB // D_MODEL)),
            full((1, 2 * D_MODEL)), full((512, D_MODEL)), full((512, D_MODEL)), full((D_MODEL, D_MODEL)),
            full((1, D_MODEL)), full((D_MODEL, 256)), full((1, 128)),
        ],
        out_specs=[
            pl.BlockSpec((tm, D_MODEL), lambda i: (i, 0)),
            pl.BlockSpec((tm, D_MODEL), lambda i: (i, 0)),
            pl.BlockSpec((tm, 128), lambda i: (i, 0)),
        ],
        out_shape=[jax.ShapeDtypeStruct((n, D_MODEL), F32), jax.ShapeDtypeStruct((n, D_MODEL), BF16),
                   jax.ShapeDtypeStruct((n, 128), F32)],
        compiler_params=pltpu.CompilerParams(dimension_semantics=("parallel",),
                                             vmem_limit_bytes=VMEM_LIMIT),
        name="merge",
    )(x2d, oa, ob, z2d, z2d, b_gate, wa, wb, wo, n2, wr, br)


R_GROUP, R_EXPERT = 0, N_GROUPS


def _routing_weights(rl):
    lane = lax.broadcasted_iota(jnp.int32, rl.shape, 1)
    big = jnp.int32(1 << 20)
    gmask = lane < N_GROUPS
    gl = jnp.where(gmask, rl, -jnp.inf)
    gmax = jnp.max(gl, axis=1, keepdims=True)
    gsel = jnp.min(jnp.where(gmask & (rl == gmax), lane, big), axis=1, keepdims=True)
    ggate = 1.0 / jnp.sum(jnp.where(gmask, jnp.exp(gl - gmax), 0.0), axis=1, keepdims=True)
    e_lo = R_EXPERT + gsel * EXPERTS_PER_GROUP
    emask = (lane >= e_lo) & (lane < e_lo + EXPERTS_PER_GROUP)
    el = jnp.where(emask, rl, -jnp.inf)
    emax = jnp.max(el, axis=1, keepdims=True)
    ee = jnp.where(emask, jnp.exp(el - emax), 0.0)
    ep = jnp.where(emask, ee / jnp.sum(ee, axis=1, keepdims=True), -1.0)
    p1 = jnp.max(ep, axis=1, keepdims=True)
    i1 = jnp.min(jnp.where(ep == p1, lane, big), axis=1, keepdims=True)
    ep2 = jnp.where(lane == i1, -1.0, ep)
    p2 = jnp.max(ep2, axis=1, keepdims=True)
    i2 = jnp.min(jnp.where(ep2 == p2, lane, big), axis=1, keepdims=True)
    tot = p1 + p2
    return (jnp.where(lane == i1, ggate * (p1 / tot), 0.0)
            + jnp.where(lane == i2, ggate * (p2 / tot), 0.0))


def _moe_kernel(x1_ref, h2_ref, rl_ref, w1_ref, w3_ref, w2_ref, fg_ref, o_ref, y_ref, comb_ref):
    e = pl.program_id(1)

    @pl.when(e == 0)
    def _():
        comb_ref[...] = _routing_weights(rl_ref[...])
        y_ref[...] = jnp.zeros(y_ref.shape, F32)

    h = h2_ref[...]
    a = _dot(h, w1_ref[...])
    b = _dot(h, w3_ref[...])
    comb = comb_ref[...]
    lane = lax.broadcasted_iota(jnp.int32, comb.shape, 1)
    ce = jnp.sum(jnp.where(lane == R_EXPERT + e, comb, 0.0), axis=1, keepdims=True)
    act = (a * jax.nn.sigmoid(a)) * b * ce
    y_ref[...] += _dot(act.astype(BF16), w2_ref[...])

    @pl.when(e == pl.num_programs(1) - 1)
    def _():
        o_ref[...] = _rmsnorm_rows(x1_ref[...] + y_ref[...], fg_ref[...])


def _moe(x1, h2, rl, w1, w3, w2, fg, tm=1024):
    n = x1.shape[0]
    return pl.pallas_call(
        _moe_kernel,
        grid=(n // tm, N_EXPERTS),
        in_specs=[
            pl.BlockSpec((tm, D_MODEL), lambda i, e: (i, 0)),
            pl.BlockSpec((tm, D_MODEL), lambda i, e: (i, 0)),
            pl.BlockSpec((tm, 128), lambda i, e: (i, 0)),
            pl.BlockSpec((None, D_MODEL, D_EXPERT), lambda i, e: (e, 0, 0)),
            pl.BlockSpec((None, D_MODEL, D_EXPERT), lambda i, e: (e, 0, 0)),
            pl.BlockSpec((None, D_EXPERT, D_MODEL), lambda i, e: (e, 0, 0)),
            pl.BlockSpec((1, D_MODEL), lambda i, e: (0, 0)),
        ],
        out_specs=pl.BlockSpec((tm, D_MODEL), lambda i, e: (i, 0)),
        out_shape=jax.ShapeDtypeStruct((n, D_MODEL), F32),
        scratch_shapes=[pltpu.VMEM((tm, D_MODEL), F32), pltpu.VMEM((tm, 128), F32)],
        compiler_params=pltpu.CompilerParams(dimension_semantics=("parallel", "arbitrary"),
                                             vmem_limit_bytes=VMEM_LIMIT),
        name="moe",
    )(x1, h2, rl, w1, w3, w2, fg)


_W_OFF = {}
_off = 0
for _name, _n in (("a_q", 512), ("a_k", 512), ("a_v", 512), ("a_z", 512), ("a_beta", 8), ("a_alpha", 8),
                  ("b_q", 512), ("b_k", 128), ("b_v", 128), ("i_q", 512), ("i_k", 64), ("i_w", 8),
                  ("gate_a", 1024), ("gate_b", 1024)):
    _W_OFF[_name] = (_off, _off + _n)
    _off += _n


def _cols(w, name, lo=0, hi=None):
    a, b = _W_OFF[name]
    return w[:, a + lo:(b if hi is None else a + hi)]


def _hi_lo_pair(w_small):
    pad = jnp.zeros((w_small.shape[0], 128 - w_small.shape[1]), F32)
    w = jnp.concatenate([w_small.astype(F32), pad], axis=1)
    hi, lo = _split(w)
    return jnp.concatenate([hi, lo], axis=1)


def _layout_w_in(w):
    k0, k1 = _cols(w, "b_k", 0, 64), _cols(w, "b_k", 64, 128)
    v0, v1 = _cols(w, "b_v", 0, 64), _cols(w, "b_v", 64, 128)
    ik = _cols(w, "i_k")
    main = jnp.concatenate([
        _cols(w, "a_q"), _cols(w, "a_k"), _cols(w, "a_v"), _cols(w, "a_z"), _cols(w, "gate_a"),
        _cols(w, "gate_b"), _cols(w, "b_q"), _cols(w, "i_q"), k0, k0, k1, k1, v0, v0, v1, v1, ik, ik],
        axis=1).astype(BF16)
    small = _hi_lo_pair(jnp.concatenate([_cols(w, "a_beta"), _cols(w, "a_alpha"), _cols(w, "i_w")], axis=1))
    return main, small


def kernel(x, positions, norm1_g, w_in, b_gate, conv_w, a_log, dt_bias, a_norm_g, w_proj_a, w_proj_b, w_out,
           norm2_g, w_router_group, b_router_group, w_router_expert, b_router_expert, w_exp_gate, w_exp_up,
           w_exp_down, final_norm_g):
    b, s, d = x.shape
    n = b * s
    depth = w_in.shape[0]
    xc = x.reshape(n, d).astype(F32)
    for l in range(depth):
        w_main, w_small = _layout_w_in(w_in[l])
        z, zs = _in_proj(xc, norm1_g[l][None, :].astype(F32), w_main, w_small)
        z3 = z.reshape(b, s, Z_W)
        zs3 = zs.reshape(b, s, 128)
        o_a = _gdn(z3, zs3, conv_w[l], a_log[l], dt_bias[l], a_norm_g[l])
        o_b = _dsa(z3, zs3, positions)
        wr = _hi_lo_pair(jnp.concatenate([w_router_group[l], w_router_expert[l]], axis=1))
        br = jnp.concatenate([b_router_group[l], b_router_expert[l],
                              jnp.zeros((128 - N_GROUPS - N_EXPERTS,), F32)])[None, :].astype(F32)
        x1, h2, rl = _merge(xc, o_a.reshape(n, -1), o_b.reshape(n, -1), z, b_gate[l][None, :].astype(F32),
                            w_proj_a[l].astype(BF16), w_proj_b[l].astype(BF16), w_out[l].astype(BF16),
                            norm2_g[l][None, :].astype(F32), wr, br)
        last = l == depth - 1
        fg = final_norm_g[None, :].astype(F32) if last else None
        if not last:
            raise NotImplementedError("only the final layer fuses the output norm")
        xc = _moe(x1, h2, rl, w_exp_gate[l].astype(BF16), w_exp_up[l].astype(BF16),
                  w_exp_down[l].astype(BF16), fg)
    return xc.reshape(b, s, d).astype(x.dtype)
```

```python
import numpy as np
import jax
import jax.numpy as jnp
from jax import lax
from jax.experimental import pallas as pl
from jax.experimental.pallas import tpu as pltpu

F32 = jnp.float32
BF16 = jnp.bfloat16

D_MODEL = 1024
CHUNK = 64
EPS = 1e-6
ROPE_THETA = 10000.0
A_HEADS = 8
A_DK = 64
A_DV = 64
CONV_K = 4
B_HEADS = 8
B_KV_HEADS = 2
B_HD = 64
IDX_HEADS = 8
IDX_HD = 64
TOPK_MAX = 256
N_GROUPS = 4
EXPERTS_PER_GROUP = 4
N_EXPERTS = 16
D_EXPERT = 256

O_AQKV, O_AZ, O_GA, O_GB, O_BQ, O_IQ, O_KD, O_VD, O_IK = 0, 1536, 2048, 3072, 4096, 4608, 5120, 5376, 5632
Z_W = 5760
S_BETA, S_ALPHA, S_IW = 0, 8, 16

GROUP_HEADS = 4
BD = GROUP_HEADS * CHUNK
NEG_BIG = -1e30
BISECT_STEPS = 18
VMEM_LIMIT = 56 * 1024 * 1024


def _split(x):
    hi = x.astype(BF16)
    lo = (x - hi.astype(F32)).astype(BF16)
    return hi, lo


def _dot(a, b):
    return jnp.dot(a, b, preferred_element_type=F32)


def _mm(a, b):
    return _dot(a.astype(BF16), b.astype(BF16))


def _mm_nt(a, b):
    return lax.dot_general(a.astype(BF16), b.astype(BF16), (((1,), (1,)), ((), ())),
                           preferred_element_type=F32)


def _mm_exact_lhs(a_bf16, x):
    xh, xl = _split(x)
    return _dot(a_bf16, xh) + _dot(a_bf16, xl)


def _rmsnorm_rows(x, g):
    return x * lax.rsqrt(jnp.mean(x * x, axis=-1, keepdims=True) + EPS) * g


_Z_CHUNKS = tuple((o, min(512, Z_W - o)) for o in range(0, Z_W, 512))


def _in_proj_kernel(x_ref, g_ref, w_ref, ws_ref, z_ref, zs_ref):
    h = _rmsnorm_rows(x_ref[...], g_ref[...])
    hb, hl = _split(h)
    for o, w in _Z_CHUNKS:
        z_ref[:, o:o + w] = _dot(hb, w_ref[:, o:o + w]).astype(BF16)
    s1 = _dot(hb, ws_ref[...])
    s2 = _dot(hl, ws_ref[:, :128])
    zs_ref[...] = s1[:, :128] + s1[:, 128:] + s2


def _in_proj(x2d, g, w_main, w_small, tm=512):
    n = x2d.shape[0]
    return pl.pallas_call(
        _in_proj_kernel,
        grid=(n // tm,),
        in_specs=[
            pl.BlockSpec((tm, D_MODEL), lambda i: (i, 0)),
            pl.BlockSpec((1, D_MODEL), lambda i: (0, 0)),
            pl.BlockSpec((D_MODEL, Z_W), lambda i: (0, 0)),
            pl.BlockSpec((D_MODEL, 256), lambda i: (0, 0)),
        ],
        out_specs=[
            pl.BlockSpec((tm, Z_W), lambda i: (i, 0)),
            pl.BlockSpec((tm, 128), lambda i: (i, 0)),
        ],
        out_shape=[jax.ShapeDtypeStruct((n, Z_W), BF16), jax.ShapeDtypeStruct((n, 128), F32)],
        compiler_params=pltpu.CompilerParams(dimension_semantics=("parallel",),
                                             vmem_limit_bytes=VMEM_LIMIT),
        name="in_proj",
    )(x2d, g, w_main, w_small)


def _gdn_constants():
    r = np.arange(BD)
    same = (r[:, None] // CHUNK) == (r[None, :] // CHUNK)
    incl = same & (r[:, None] >= r[None, :])
    strict = same & (r[:, None] > r[None, :])
    eye = np.eye(BD, dtype=np.float32)
    ll = np.concatenate([incl, same], axis=0).astype(np.float32)
    sel = np.zeros((4, BD, 128), np.float32)
    for gi in range(2):
        for h in range(GROUP_HEADS):
            sel[gi * 2 + 0, h * CHUNK:(h + 1) * CHUNK, S_BETA + gi * GROUP_HEADS + h] = 1.0
            sel[gi * 2 + 1, h * CHUNK:(h + 1) * CHUNK, S_ALPHA + gi * GROUP_HEADS + h] = 1.0
    return (jnp.asarray(incl, F32), jnp.asarray(strict, F32), jnp.asarray(same, F32), jnp.asarray(eye),
            jnp.asarray(ll, BF16), jnp.asarray(sel))


def _tile4(x):
    return jnp.concatenate([x, x, x, x], axis=0)


def _gdn_kernel(zq_ref, zz_ref, zs_ref, cw_ref, av_ref, ag_ref, incl_ref, strict_ref, bdm_ref, eye_ref,
                ll_ref, sel_ref, o_ref, ext_ref, st_ref):
    c = pl.program_id(1)
    t = zq_ref.shape[0]

    @pl.when(c == 0)
    def _():
        ext_ref[0:8, :] = jnp.zeros((8, ext_ref.shape[1]), F32)
        st_ref[...] = jnp.zeros(st_ref.shape, F32)

    ext_ref[8:8 + t, :] = zq_ref[...].astype(F32)
    cw = cw_ref[...]
    y = cw[0:1, :] * ext_ref[pl.ds(8 - (CONV_K - 1), t), :]
    for j in range(1, CONV_K):
        y = y + cw[j:j + 1, :] * ext_ref[pl.ds(8 - (CONV_K - 1) + j, t), :]
    ext_ref[0:8, :] = ext_ref[t:t + 8, :]
    y = y * jax.nn.sigmoid(y)

    hw = A_HEADS * A_DK
    q_all = y[:, :hw]
    k_all = y[:, hw:2 * hw]
    v_all = y[:, 2 * hw:]

    sm = zs_ref[...]
    av = av_ref[...]
    lane = lax.broadcasted_iota(jnp.int32, sm.shape, 1)
    xg = sm + av[1:2, :]
    softplus = jnp.maximum(xg, 0.0) + jnp.log1p(jnp.exp(-jnp.abs(xg)))
    g_all = -jnp.exp(av[0:1, :]) * softplus
    bg = jnp.where(lane < S_ALPHA, jax.nn.sigmoid(sm), g_all)

    incl = incl_ref[...]
    strict = strict_ref[...]
    bdm = bdm_ref[...]
    eye = eye_ref[...]
    ll = ll_ref[...]
    incl_b = ll[:BD]

    n_chunks = t // CHUNK
    n_groups = A_HEADS // GROUP_HEADS
    chains = [(ci, gi) for ci in range(n_chunks) for gi in range(n_groups)]

    pre = {}
    for ci, gi in chains:
        r0, c0 = ci * CHUNK, gi * BD
        bg4 = _tile4(bg[r0:r0 + CHUNK])
        beta = jnp.sum(bg4 * sel_ref[gi * 2 + 0], axis=1, keepdims=True)
        gcol = jnp.sum(bg4 * sel_ref[gi * 2 + 1], axis=1, keepdims=True)
        gs = _mm_exact_lhs(ll, jnp.broadcast_to(gcol, (BD, 128)))
        g_cum = gs[:BD, :1]
        g_last = gs[BD:, :1]
        diff = _mm_exact_lhs(incl_b, gcol * strict)
        decay = jnp.where(incl > 0.0, jnp.exp(diff), 0.0)
        e_cum = jnp.exp(g_cum)
        kr = _tile4(k_all[r0:r0 + CHUNK, c0:c0 + BD]) * bdm
        qr = _tile4(q_all[r0:r0 + CHUNK, c0:c0 + BD]) * bdm
        vm = _tile4(v_all[r0:r0 + CHUNK, c0:c0 + BD]) * bdm
        km = kr * lax.rsqrt(jnp.sum(kr * kr, axis=1, keepdims=True) + EPS)
        qm = qr * (lax.rsqrt(jnp.sum(qr * qr, axis=1, keepdims=True) + EPS) * (A_DK ** -0.5))
        kkqk = _mm_nt(jnp.concatenate([km, qm], axis=0), km)
        m = -(strict * beta * kkqk[:BD] * decay)
        pre[ci, gi] = dict(beta=beta, g_cum=g_cum, g_last=g_last, e_cum=e_cum, km=km, qm=qm, vm=vm,
                           qk=kkqk[BD:] * decay, m=m, inv=eye + m)

    sq = CHUNK
    while sq > 2:
        for key in chains:
            p = pre[key]
            p["m"] = _mm(p["m"], p["m"])
        for key in chains:
            p = pre[key]
            p["inv"] = p["inv"] + _mm(p["inv"], p["m"])
        sq //= 2

    outs = []
    for ci in range(n_chunks):
        o_groups = []
        for gi in range(n_groups):
            p = pre[ci, gi]
            state = st_ref[gi]
            kq_s = _mm(jnp.concatenate([p["km"] * p["e_cum"], p["qm"] * p["e_cum"]], axis=0), state)
            v_new = _mm(p["inv"], p["beta"] * (p["vm"] - kq_s[:BD]))
            o_bd = kq_s[BD:] + _mm(p["qk"], v_new)
            k_dec = p["km"] * jnp.exp(p["g_last"] - p["g_cum"])
            st_ref[gi] = state * jnp.exp(p["g_last"]) + _mm(k_dec.T, v_new)
            o_bd = o_bd * lax.rsqrt(jnp.sum(o_bd * o_bd, axis=1, keepdims=True) * (1.0 / A_DV) + EPS)
            o_groups.append(o_bd[0:CHUNK] + o_bd[CHUNK:2 * CHUNK] + o_bd[2 * CHUNK:3 * CHUNK]
                            + o_bd[3 * CHUNK:])
        outs.append(jnp.concatenate(o_groups, axis=1))
    o = outs[0] if len(outs) == 1 else jnp.concatenate(outs, axis=0)

    zz = zz_ref[...].astype(F32)
    o_ref[...] = (o * ag_ref[...] * (zz * jax.nn.sigmoid(zz))).astype(BF16)


def _gdn(z3, zs3, conv_w, a_log, dt_bias, a_norm_g, t=4 * CHUNK):
    b, s, _ = z3.shape
    consts = _gdn_constants()
    av = jnp.zeros((2, 128), F32)
    av = av.at[0, S_ALPHA:S_ALPHA + A_HEADS].set(a_log.astype(F32))
    av = av.at[1, S_ALPHA:S_ALPHA + A_HEADS].set(dt_bias.astype(F32))
    ag = jnp.tile(a_norm_g.astype(F32), A_HEADS)[None, :]
    conv_cols = 2 * A_HEADS * A_DK + A_HEADS * A_DV

    def const_spec(a):
        nd = a.ndim
        return pl.BlockSpec(a.shape, lambda bi, ci, _n=nd: (0,) * _n)

    small_in = (conv_w.astype(F32), av, ag) + consts
    return pl.pallas_call(
        _gdn_kernel,
        grid=(b, s // t),
        in_specs=[
            pl.BlockSpec((None, t, conv_cols), lambda bi, ci: (bi, ci, O_AQKV // conv_cols)),
            pl.BlockSpec((None, t, 512), lambda bi, ci: (bi, ci, O_AZ // 512)),
            pl.BlockSpec((None, t, 128), lambda bi, ci: (bi, ci, 0)),
        ] + [const_spec(a) for a in small_in],
        out_specs=pl.BlockSpec((None, t, A_HEADS * A_DV), lambda bi, ci: (bi, ci, 0)),
        out_shape=jax.ShapeDtypeStruct((b, s, A_HEADS * A_DV), BF16),
        scratch_shapes=[pltpu.VMEM((8 + t, conv_cols), F32),
                        pltpu.VMEM((A_HEADS // GROUP_HEADS, BD, BD), F32)],
        compiler_params=pltpu.CompilerParams(dimension_semantics=("parallel", "arbitrary"),
                                             vmem_limit_bytes=VMEM_LIMIT),
        name="gdn",
    )(z3, z3, zs3, *small_in)


def _rope(x, cs, sn, first):
    w = x.shape[1]
    swapped = jnp.where(first, pltpu.roll(x, w - B_HD // 2, 1), pltpu.roll(x, B_HD // 2, 1))
    return x * cs + swapped * sn


def _dsa_kernel(topk, tk, pos_ref, inv_ref, bq_ref, iq_ref, kd_ref, vd_ref, ikd_ref, zs_ref, o_ref,
                cos_ref, sin_ref, ka_ref, kb_ref, ik_ref, sc_ref):
    qb = pl.program_id(1)
    s = kd_ref.shape[0]
    tq = bq_ref.shape[0]
    nkt = s // tk

    @pl.when(qb == 0)
    def _():
        ang = pos_ref[...] * inv_ref[...]
        lane = lax.broadcasted_iota(jnp.int32, ang.shape, 1)
        first = (lane & (B_HD - 1)) < (B_HD // 2)
        cs = jnp.cos(ang)
        sn = jnp.sin(ang)
        sn = jnp.where(first, -sn, sn)
        cos_ref[...] = cs
        sin_ref[...] = sn
        kd = kd_ref[...].astype(F32)
        ka_ref[...] = _rope(kd[:, :128], cs, sn, first).astype(BF16)
        kb_ref[...] = _rope(kd[:, 128:], cs, sn, first).astype(BF16)
        ik_ref[...] = _rope(ikd_ref[...].astype(F32), cs, sn, first).astype(BF16)

    r0 = pl.multiple_of(qb * tq, tq)
    cs1 = cos_ref[pl.ds(r0, tq), :]
    sn1 = sin_ref[pl.ds(r0, tq), :]
    cs = jnp.concatenate([cs1] * 4, axis=1)
    sn = jnp.concatenate([sn1] * 4, axis=1)
    lane512 = lax.broadcasted_iota(jnp.int32, (tq, 4 * 128), 1)
    first512 = (lane512 & (B_HD - 1)) < (B_HD // 2)
    q = _rope(bq_ref[...].astype(F32), cs, sn, first512) * (B_HD ** -0.5)
    iq = _rope(iq_ref[...].astype(F32), cs, sn, first512)

    sm = zs_ref[...]
    lane128 = lax.broadcasted_iota(jnp.int32, (tq, 128), 1)
    iw_scale = (IDX_HEADS ** -0.5) * (IDX_HD ** -0.5)
    wcols = [jnp.sum(jnp.where(lane128 == S_IW + h, sm, 0.0), axis=1, keepdims=True) * iw_scale
             for h in range(IDX_HEADS)]
    low_half = lane128 < B_HD
    high_half = lane128 >= B_HD

    def head_slab(x, h):
        slab = x[:, (h // 2) * 128:(h // 2 + 1) * 128]
        return jnp.where(low_half if h % 2 == 0 else high_half, slab, 0.0)

    row = r0 + lax.broadcasted_iota(jnp.int32, (tq, 1), 0)
    limit = ((row >> 6) + 1) << 6

    ctx = (q, iq, wcols, limit, low_half, head_slab)
    refs = (vd_ref, o_ref, ka_ref, kb_ref, ik_ref, sc_ref)
    per = tk // tq
    for j in range(nkt):
        @pl.when(jnp.logical_and(qb >= j * per, qb < (j + 1) * per))
        def _(j=j):
            _dsa_block(topk, tk, (j + 1) * tk, ctx, refs)


def _dsa_block(topk, tk, width, ctx, refs):
    q, iq, wcols, limit, low_half, head_slab = ctx
    vd_ref, o_ref, ka_ref, kb_ref, ik_ref, sc_ref = refs
    tq = q.shape[0]
    nkt = width // tk

    for kt in range(nkt):
        ikt = ik_ref[kt * tk:(kt + 1) * tk, :]
        acc = jnp.zeros((tq, tk), F32)
        for h in range(IDX_HEADS):
            sc = _mm_nt(head_slab(iq, h), ikt)
            acc = acc + wcols[h] * jnp.maximum(sc, 0.0)
        kpos = kt * tk + lax.broadcasted_iota(jnp.int32, (tq, tk), 1)
        sc_ref[:, kt * tk:(kt + 1) * tk] = jnp.where(kpos < limit, acc, -jnp.inf)

    kf = float(topk)
    x0 = sc_ref[:, :width]
    row_max = jnp.max(x0, axis=1, keepdims=True)
    row_min = jnp.min(jnp.where(x0 == -jnp.inf, jnp.inf, x0), axis=1, keepdims=True)
    c_max = jnp.sum((x0 >= row_max).astype(F32), axis=1, keepdims=True)
    small = limit <= topk

    def bisect(_, carry):
        lo, hi = carry
        mid = 0.5 * (lo + hi)
        cnt = jnp.sum((sc_ref[:, :width] >= mid).astype(F32), axis=1, keepdims=True)
        ge = cnt >= kf
        return jnp.where(ge, mid, lo), jnp.where(ge, hi, mid)

    lo, hi = lax.fori_loop(0, BISECT_STEPS, bisect, (row_min, row_max))
    top_tied = c_max >= kf
    done0 = jnp.where(small | top_tied, 1.0, 0.0)
    thr0 = jnp.where(top_tied, row_max, lo)

    def peel_cond(carry):
        done, _, _ = carry
        return jnp.sum(1.0 - done) > 0.0

    def peel(carry):
        done, thr, hi_c = carry
        x = sc_ref[:, :width]
        v1 = jnp.max(jnp.where(x < hi_c, x, -jnp.inf), axis=1, keepdims=True)
        c1 = jnp.sum((x >= v1).astype(F32), axis=1, keepdims=True)
        ok = c1 >= kf
        act = done < 0.5
        thr = jnp.where(act & ok, v1, thr)
        hi_c = jnp.where(act & (~ok), v1, hi_c)
        done = jnp.where(ok, 1.0, done)
        return done, thr, hi_c

    _, thr, _ = lax.while_loop(peel_cond, peel, (done0, thr0, hi))

    x = sc_ref[:, :width]
    idx = lax.broadcasted_iota(jnp.int32, (tq, width), 1).astype(F32)
    gt = x > thr
    eq = x == thr
    need = kf - jnp.sum(gt.astype(F32), axis=1, keepdims=True)
    n_eq = jnp.sum(eq.astype(F32), axis=1, keepdims=True)
    contested = jnp.sum(jnp.where((n_eq > need) & (~small), 1.0, 0.0)) > 0.0

    def tie_search(_, carry):
        jlo, jhi = carry
        mid = jnp.floor(0.5 * (jlo + jhi))
        e = sc_ref[:, :width] == thr
        cnt = jnp.sum((e & (idx <= mid)).astype(F32), axis=1, keepdims=True)
        ge = cnt >= need
        return jnp.where(ge, jlo, mid), jnp.where(ge, mid, jhi)

    def tie_cut():
        n_iter = int(np.ceil(np.log2(width))) + 1
        return lax.fori_loop(0, n_iter, tie_search,
                             (jnp.full((tq, 1), -1.0, F32), jnp.full((tq, 1), float(width - 1), F32)))[1]

    jcut = lax.cond(contested, tie_cut, lambda: jnp.full((tq, 1), float(width - 1), F32))
    sel = (x > -jnp.inf) & (small | gt | (eq & (idx <= jcut)))
    sc_ref[:, :width] = jnp.where(sel, 0.0, NEG_BIG)

    heads_per_kv = B_HEADS // B_KV_HEADS
    for g in range(B_KV_HEADS):
        k_ref = ka_ref if g == 0 else kb_ref
        qrows = jnp.concatenate([head_slab(q, g * heads_per_kv + j) for j in range(heads_per_kv)],
                                axis=0).astype(BF16)
        nr = heads_per_kv * tq
        m_run = jnp.full((nr, 1), NEG_BIG, F32)
        l_run = jnp.zeros((nr, 1), F32)
        acc = jnp.zeros((nr, 128), F32)
        for kt in range(nkt):
            logits = lax.dot_general(qrows, k_ref[kt * tk:(kt + 1) * tk, :], (((1,), (1,)), ((), ())),
                                     preferred_element_type=F32)
            bias = sc_ref[:, kt * tk:(kt + 1) * tk]
            logits = logits + jnp.concatenate([bias] * heads_per_kv, axis=0)
            m_new = jnp.maximum(m_run, jnp.max(logits, axis=1, keepdims=True))
            p = jnp.exp(logits - m_new)
            alpha = jnp.exp(m_run - m_new)
            l_run = alpha * l_run + jnp.sum(p, axis=1, keepdims=True)
            acc = alpha * acc + _dot(p.astype(BF16), vd_ref[kt * tk:(kt + 1) * tk, g * 128:(g + 1) * 128])
            m_run = m_new
        og = acc / l_run
        for pp in range(heads_per_kv // 2):
            even = og[(2 * pp) * tq:(2 * pp + 1) * tq]
            odd = og[(2 * pp + 1) * tq:(2 * pp + 2) * tq]
            col = (g * (heads_per_kv // 2) + pp) * 128
            o_ref[:, col:col + 128] = jnp.where(low_half, even, odd).astype(BF16)


def _dsa(z3, zs3, positions, tq=128, tk=512):
    b, s, _ = z3.shape
    topk = min(TOPK_MAX, s // 4)
    half = B_HD // 2
    inv = ROPE_THETA ** (-jnp.arange(half, dtype=F32) / half)
    inv128 = jnp.tile(inv, 4)[None, :]
    pos = positions.astype(F32)[:, :, None]
    kernel = lambda *refs: _dsa_kernel(topk, tk, *refs)
    return pl.pallas_call(
        kernel,
        grid=(b, s // tq),
        in_specs=[
            pl.BlockSpec((None, s, 1), lambda bi, qi: (bi, 0, 0)),
            pl.BlockSpec((1, 128), lambda bi, qi: (0, 0)),
            pl.BlockSpec((None, tq, 512), lambda bi, qi: (bi, qi, O_BQ // 512)),
            pl.BlockSpec((None, tq, 512), lambda bi, qi: (bi, qi, O_IQ // 512)),
            pl.BlockSpec((None, s, 256), lambda bi, qi: (bi, 0, O_KD // 256)),
            pl.BlockSpec((None, s, 256), lambda bi, qi: (bi, 0, O_VD // 256)),
            pl.BlockSpec((None, s, 128), lambda bi, qi: (bi, 0, O_IK // 128)),
            pl.BlockSpec((None, tq, 128), lambda bi, qi: (bi, qi, 0)),
        ],
        out_specs=pl.BlockSpec((None, tq, B_HEADS * B_HD), lambda bi, qi: (bi, qi, 0)),
        out_shape=jax.ShapeDtypeStruct((b, s, B_HEADS * B_HD), BF16),
        scratch_shapes=[pltpu.VMEM((s, 128), F32), pltpu.VMEM((s, 128), F32),
                        pltpu.VMEM((s, 128), BF16), pltpu.VMEM((s, 128), BF16), pltpu.VMEM((s, 128), BF16),
                        pltpu.VMEM((tq, s), F32)],
        compiler_params=pltpu.CompilerParams(dimension_semantics=("parallel", "arbitrary"),
                                             vmem_limit_bytes=VMEM_LIMIT),
        name="dsa",
    )(pos, inv128, z3, z3, z3, z3, z3, zs3)


def _merge_kernel(x_ref, oa_ref, ob_ref, ga_ref, gb_ref, bg_ref, wa_ref, wb_ref, wo_ref, n2_ref, wr_ref,
                  br_ref, x1_ref, h2_ref, rl_ref):
    pa = _dot(oa_ref[...], wa_ref[...])
    pb = _dot(ob_ref[...], wb_ref[...])
    bgv = bg_ref[...]
    ga = jax.nn.sigmoid(ga_ref[...].astype(F32) + bgv[:, :D_MODEL])
    gb = jax.nn.sigmoid(gb_ref[...].astype(F32) + bgv[:, D_MODEL:])
    merged = ga * pa + gb * pb
    x1 = x_ref[...] + _mm(merged, wo_ref[...])
    x1_ref[...] = x1
    h2 = _rmsnorm_rows(x1, n2_ref[...])
    hb, hl = _split(h2)
    h2_ref[...] = hb
    s1 = _dot(hb, wr_ref[...])
    s2 = _dot(hl, wr_ref[:, :128])
    rl_ref[...] = s1[:, :128] + s1[:, 128:] + s2 + br_ref[...]


def _merge(x2d, oa, ob, z2d, b_gate, wa, wb, wo, n2, wr, br, tm=512):
    n = x2d.shape[0]
    full = lambda shape: pl.BlockSpec(shape, lambda i: (0, 0))
    return pl.pallas_call(
        _merge_kernel,
        grid=(n // tm,),
        in_specs=[
            pl.BlockSpec((tm, D_MODEL), lambda i: (i, 0)),
            pl.BlockSpec((tm, 512), lambda i: (i, 0)),
            pl.BlockSpec((tm, 512), lambda i: (i, 0)),
            pl.BlockSpec((tm, D_MODEL), lambda i: (i, O_GA // D_MODEL)),
            pl.BlockSpec((tm, D_MODEL), lambda i: (i, O_GB // D_MODEL)),
            full((1, 2 * D_MODEL)), full((512, D_MODEL)), full((512, D_MODEL)), full((D_MODEL, D_MODEL)),
            full((1, D_MODEL)), full((D_MODEL, 256)), full((1, 128)),
        ],
        out_specs=[
            pl.BlockSpec((tm, D_MODEL), lambda i: (i, 0)),
            pl.BlockSpec((tm, D_MODEL), lambda i: (i, 0)),
            pl.BlockSpec((tm, 128), lambda i: (i, 0)),
        ],
        out_shape=[jax.ShapeDtypeStruct((n, D_MODEL), F32), jax.ShapeDtypeStruct((n, D_MODEL), BF16),
                   jax.ShapeDtypeStruct((n, 128), F32)],
        compiler_params=pltpu.CompilerParams(dimension_semantics=("parallel",),
                                             vmem_limit_bytes=VMEM_LIMIT),
        name="merge",
    )(x2d, oa, ob, z2d, z2d, b_gate, wa, wb, wo, n2, wr, br)


R_GROUP, R_EXPERT = 0, N_GROUPS


def _routing_weights(rl):
    lane = lax.broadcasted_iota(jnp.int32, rl.shape, 1)
    big = jnp.int32(1 << 20)
    gmask = lane < N_GROUPS
    gl = jnp.where(gmask, rl, -jnp.inf)
    gmax = jnp.max(gl, axis=1, keepdims=True)
    gsel = jnp.min(jnp.where(gmask & (rl == gmax), lane, big), axis=1, keepdims=True)
    ggate = 1.0 / jnp.sum(jnp.where(gmask, jnp.exp(gl - gmax), 0.0), axis=1, keepdims=True)
    e_lo = R_EXPERT + gsel * EXPERTS_PER_GROUP
    emask = (lane >= e_lo) & (lane < e_lo + EXPERTS_PER_GROUP)
    el = jnp.where(emask, rl, -jnp.inf)
    emax = jnp.max(el, axis=1, keepdims=True)
    ee = jnp.where(emask, jnp.exp(el - emax), 0.0)
    ep = jnp.where(emask, ee / jnp.sum(ee, axis=1, keepdims=True), -1.0)
    p1 = jnp.max(ep, axis=1, keepdims=True)
    i1 = jnp.min(jnp.where(ep == p1, lane, big), axis=1, keepdims=True)
    ep2 = jnp.where(lane == i1, -1.0, ep)
    p2 = jnp.max(ep2, axis=1, keepdims=True)
    i2 = jnp.min(jnp.where(ep2 == p2, lane, big), axis=1, keepdims=True)
    tot = p1 + p2
    return (jnp.where(lane == i1, ggate * (p1 / tot), 0.0)
            + jnp.where(lane == i2, ggate * (p2 / tot), 0.0))


def _moe_kernel(x1_ref, h2_ref, rl_ref, w1_ref, w3_ref, w2_ref, fg_ref, o_ref, y_ref, comb_ref):
    e = pl.program_id(1)

    @pl.when(e == 0)
    def _():
        comb_ref[...] = _routing_weights(rl_ref[...])
        y_ref[...] = jnp.zeros(y_ref.shape, F32)

    h = h2_ref[...]
    a = _dot(h, w1_ref[...])
    b = _dot(h, w3_ref[...])
    comb = comb_ref[...]
    lane = lax.broadcasted_iota(jnp.int32, comb.shape, 1)
    ce = jnp.sum(jnp.where(lane == R_EXPERT + e, comb, 0.0), axis=1, keepdims=True)
    act = (a * jax.nn.sigmoid(a)) * b * ce
    y_ref[...] += _dot(act.astype(BF16), w2_ref[...])

    @pl.when(e == pl.num_programs(1) - 1)
    def _():
        o_ref[...] = _rmsnorm_rows(x1_ref[...] + y_ref[...], fg_ref[...])


def _moe(x1, h2, rl, w1, w3, w2, fg, tm=1024):
    n = x1.shape[0]
    return pl.pallas_call(
        _moe_kernel,
        grid=(n // tm, N_EXPERTS),
        in_specs=[
            pl.BlockSpec((tm, D_MODEL), lambda i, e: (i, 0)),
            pl.BlockSpec((tm, D_MODEL), lambda i, e: (i, 0)),
            pl.BlockSpec((tm, 128), lambda i, e: (i, 0)),
            pl.BlockSpec((None, D_MODEL, D_EXPERT), lambda i, e: (e, 0, 0)),
            pl.BlockSpec((None, D_MODEL, D_EXPERT), lambda i, e: (e, 0, 0)),
            pl.BlockSpec((None, D_EXPERT, D_MODEL), lambda i, e: (e, 0, 0)),
            pl.BlockSpec((1, D_MODEL), lambda i, e: (0, 0)),
        ],
        out_specs=pl.BlockSpec((tm, D_MODEL), lambda i, e: (i, 0)),
        out_shape=jax.ShapeDtypeStruct((n, D_MODEL), F32),
        scratch_shapes=[pltpu.VMEM((tm, D_MODEL), F32), pltpu.VMEM((tm, 128), F32)],
        compiler_params=pltpu.CompilerParams(dimension_semantics=("parallel", "arbitrary"),
                                             vmem_limit_bytes=VMEM_LIMIT),
        name="moe",
    )(x1, h2, rl, w1, w3, w2, fg)


_W_OFF = {}
_off = 0
for _name, _n in (("a_q", 512), ("a_k", 512), ("a_v", 512), ("a_z", 512), ("a_beta", 8), ("a_alpha", 8),
                  ("b_q", 512), ("b_k", 128), ("b_v", 128), ("i_q", 512), ("i_k", 64), ("i_w", 8),
                  ("gate_a", 1024), ("gate_b", 1024)):
    _W_OFF[_name] = (_off, _off + _n)
    _off += _n


def _cols(w, name, lo=0, hi=None):
    a, b = _W_OFF[name]
    return w[:, a + lo:(b if hi is None else a + hi)]


def _hi_lo_pair(w_small):
    pad = jnp.zeros((w_small.shape[0], 128 - w_small.shape[1]), F32)
    w = jnp.concatenate([w_small.astype(F32), pad], axis=1)
    hi, lo = _split(w)
    return jnp.concatenate([hi, lo], axis=1)


def _layout_w_in(w):
    wb = w.astype(BF16)
    k0, k1 = _cols(wb, "b_k", 0, 64), _cols(wb, "b_k", 64, 128)
    v0, v1 = _cols(wb, "b_v", 0, 64), _cols(wb, "b_v", 64, 128)
    ik = _cols(wb, "i_k")
    main = jnp.concatenate([
        _cols(wb, "a_q"), _cols(wb, "a_k"), _cols(wb, "a_v"), _cols(wb, "a_z"), _cols(wb, "gate_a"),
        _cols(wb, "gate_b"), _cols(wb, "b_q"), _cols(wb, "i_q"), k0, k0, k1, k1, v0, v0, v1, v1, ik, ik],
        axis=1)
    small = _hi_lo_pair(jnp.concatenate([_cols(w, "a_beta"), _cols(w, "a_alpha"), _cols(w, "i_w")], axis=1))
    return main, small


def kernel(x, positions, norm1_g, w_in, b_gate, conv_w, a_log, dt_bias, a_norm_g, w_proj_a, w_proj_b, w_out,
           norm2_g, w_router_group, b_router_group, w_router_expert, b_router_expert, w_exp_gate, w_exp_up,
           w_exp_down, final_norm_g):
    b, s, d = x.shape
    n = b * s
    depth = w_in.shape[0]
    xc = x.reshape(n, d).astype(F32)
    for l in range(depth):
        w_main, w_small = _layout_w_in(w_in[l])
        z, zs = _in_proj(xc, norm1_g[l][None, :].astype(F32), w_main, w_small)
        z3 = z.reshape(b, s, Z_W)
        zs3 = zs.reshape(b, s, 128)
        o_a = _gdn(z3, zs3, conv_w[l], a_log[l], dt_bias[l], a_norm_g[l])
        o_b = _dsa(z3, zs3, positions)
        wr = _hi_lo_pair(jnp.concatenate([w_router_group[l], w_router_expert[l]], axis=1))
        br = jnp.concatenate([b_router_group[l], b_router_expert[l],
                              jnp.zeros((128 - N_GROUPS - N_EXPERTS,), F32)])[None, :].astype(F32)
        x1, h2, rl = _merge(xc, o_a.reshape(n, -1), o_b.reshape(n, -1), z, b_gate[l][None, :].astype(F32),
                            w_proj_a[l].astype(BF16), w_proj_b[l].astype(BF16), w_out[l].astype(BF16),
                            norm2_g[l][None, :].astype(F32), wr, br)
        last = l == depth - 1
        fg = final_norm_g[None, :].astype(F32) if last else None
        if not last:
            raise NotImplementedError("only the final layer fuses the output norm")
        xc = _moe(x1, h2, rl, w_exp_gate[l].astype(BF16), w_exp_up[l].astype(BF16),
                  w_exp_down[l].astype(BF16), fg)
    return xc.reshape(b, s, d).astype(x.dtype)
```

```python
import numpy as np
import jax
import jax.numpy as jnp
from jax import lax
from jax.experimental import pallas as pl
from jax.experimental.pallas import tpu as pltpu

F32 = jnp.float32
BF16 = jnp.bfloat16

D_MODEL = 1024
CHUNK = 64
EPS = 1e-6
ROPE_THETA = 10000.0
A_HEADS = 8
A_DK = 64
A_DV = 64
CONV_K = 4
B_HEADS = 8
B_KV_HEADS = 2
B_HD = 64
IDX_HEADS = 8
IDX_HD = 64
TOPK_MAX = 256
N_GROUPS = 4
EXPERTS_PER_GROUP = 4
N_EXPERTS = 16
D_EXPERT = 256

O_AQKV, O_AZ, O_GA, O_GB, O_BQ, O_IQ, O_KD, O_VD, O_IK = 0, 1536, 2048, 3072, 4096, 4608, 5120, 5376, 5632
Z_W = 5760
S_BETA, S_ALPHA, S_IW = 0, 8, 16

GROUP_HEADS = 4
BD = GROUP_HEADS * CHUNK
NEG_BIG = -1e30
LOG2E = 1.4426950408889634
BISECT_STEPS = 18
VMEM_LIMIT = 56 * 1024 * 1024


def _split(x):
    hi = x.astype(BF16)
    lo = (x - hi.astype(F32)).astype(BF16)
    return hi, lo


def _dot(a, b):
    return jnp.dot(a, b, preferred_element_type=F32)


def _mm(a, b):
    return _dot(a.astype(BF16), b.astype(BF16))


def _mm_nt(a, b):
    return lax.dot_general(a.astype(BF16), b.astype(BF16), (((1,), (1,)), ((), ())),
                           preferred_element_type=F32)


def _mm_exact_lhs(a_bf16, x):
    xh, xl = _split(x)
    return _dot(a_bf16, xh) + _dot(a_bf16, xl)


def _rmsnorm_rows(x, g):
    return x * lax.rsqrt(jnp.mean(x * x, axis=-1, keepdims=True) + EPS) * g


_Z_CHUNKS = tuple((o, min(512, Z_W - o)) for o in range(0, Z_W, 512))


def _in_proj_kernel(x_ref, g_ref, w_ref, ws_ref, z_ref, zs_ref):
    h = _rmsnorm_rows(x_ref[...], g_ref[...])
    hb, hl = _split(h)
    for o, w in _Z_CHUNKS:
        z_ref[:, o:o + w] = _dot(hb, w_ref[:, o:o + w]).astype(BF16)
    s1 = _dot(hb, ws_ref[...])
    s2 = _dot(hl, ws_ref[:, :128])
    zs_ref[...] = s1[:, :128] + s1[:, 128:] + s2


def _in_proj(x2d, g, w_main, w_small, tm=512):
    n = x2d.shape[0]
    return pl.pallas_call(
        _in_proj_kernel,
        grid=(n // tm,),
        in_specs=[
            pl.BlockSpec((tm, D_MODEL), lambda i: (i, 0)),
            pl.BlockSpec((1, D_MODEL), lambda i: (0, 0)),
            pl.BlockSpec((D_MODEL, Z_W), lambda i: (0, 0)),
            pl.BlockSpec((D_MODEL, 256), lambda i: (0, 0)),
        ],
        out_specs=[
            pl.BlockSpec((tm, Z_W), lambda i: (i, 0)),
            pl.BlockSpec((tm, 128), lambda i: (i, 0)),
        ],
        out_shape=[jax.ShapeDtypeStruct((n, Z_W), BF16), jax.ShapeDtypeStruct((n, 128), F32)],
        compiler_params=pltpu.CompilerParams(dimension_semantics=("parallel",),
                                             vmem_limit_bytes=VMEM_LIMIT),
        name="in_proj",
    )(x2d, g, w_main, w_small)


def _gdn_constants():
    r = np.arange(BD)
    same = (r[:, None] // CHUNK) == (r[None, :] // CHUNK)
    incl = same & (r[:, None] >= r[None, :])
    strict = same & (r[:, None] > r[None, :])
    eye = np.eye(BD, dtype=np.float32)
    ll = np.concatenate([incl, same], axis=0).astype(np.float32)
    sel = np.zeros((4, BD, 128), np.float32)
    for gi in range(2):
        for h in range(GROUP_HEADS):
            sel[gi * 2 + 0, h * CHUNK:(h + 1) * CHUNK, S_BETA + gi * GROUP_HEADS + h] = 1.0
            sel[gi * 2 + 1, h * CHUNK:(h + 1) * CHUNK, S_ALPHA + gi * GROUP_HEADS + h] = 1.0
    return (jnp.asarray(incl, F32), jnp.asarray(strict, F32), jnp.asarray(same, F32), jnp.asarray(eye),
            jnp.asarray(ll, BF16), jnp.asarray(sel))


def _tile4(x):
    return jnp.concatenate([x, x, x, x], axis=0)


def _gdn_kernel(zq_ref, zz_ref, zs_ref, cw_ref, av_ref, ag_ref, incl_ref, strict_ref, bdm_ref, eye_ref,
                ll_ref, sel_ref, o_ref, ext_ref, st_ref):
    c = pl.program_id(1)
    t = zq_ref.shape[0]

    @pl.when(c == 0)
    def _():
        ext_ref[0:8, :] = jnp.zeros((8, ext_ref.shape[1]), F32)
        st_ref[...] = jnp.zeros(st_ref.shape, F32)

    ext_ref[8:8 + t, :] = zq_ref[...].astype(F32)
    cw = cw_ref[...]
    y = cw[0:1, :] * ext_ref[pl.ds(8 - (CONV_K - 1), t), :]
    for j in range(1, CONV_K):
        y = y + cw[j:j + 1, :] * ext_ref[pl.ds(8 - (CONV_K - 1) + j, t), :]
    ext_ref[0:8, :] = ext_ref[t:t + 8, :]
    y = y * jax.nn.sigmoid(y)

    hw = A_HEADS * A_DK
    q_all = y[:, :hw]
    k_all = y[:, hw:2 * hw]
    v_all = y[:, 2 * hw:]

    sm = zs_ref[...]
    av = av_ref[...]
    lane = lax.broadcasted_iota(jnp.int32, sm.shape, 1)
    xg = sm + av[1:2, :]
    softplus = jnp.maximum(xg, 0.0) + jnp.log1p(jnp.exp(-jnp.abs(xg)))
    g_all = -jnp.exp(av[0:1, :]) * softplus
    bg = jnp.where(lane < S_ALPHA, jax.nn.sigmoid(sm), g_all)

    incl = incl_ref[...]
    strict = strict_ref[...]
    bdm = bdm_ref[...]
    eye = eye_ref[...]
    ll = ll_ref[...]
    incl_b = ll[:BD]

    n_chunks = t // CHUNK
    n_groups = A_HEADS // GROUP_HEADS
    chains = [(ci, gi) for ci in range(n_chunks) for gi in range(n_groups)]

    pre = {}
    for ci, gi in chains:
        r0, c0 = ci * CHUNK, gi * BD
        bg4 = _tile4(bg[r0:r0 + CHUNK])
        beta = jnp.sum(bg4 * sel_ref[gi * 2 + 0], axis=1, keepdims=True)
        gcol = jnp.sum(bg4 * sel_ref[gi * 2 + 1], axis=1, keepdims=True)
        gs = _mm_exact_lhs(ll, jnp.broadcast_to(gcol, (BD, 128)))
        g_cum = gs[:BD, :1]
        g_last = gs[BD:, :1]
        diff = _mm_exact_lhs(incl_b, gcol * strict)
        decay = jnp.where(incl > 0.0, jnp.exp(diff), 0.0)
        e_cum = jnp.exp(g_cum)
        kr = _tile4(k_all[r0:r0 + CHUNK, c0:c0 + BD]) * bdm
        qr = _tile4(q_all[r0:r0 + CHUNK, c0:c0 + BD]) * bdm
        vm = _tile4(v_all[r0:r0 + CHUNK, c0:c0 + BD]) * bdm
        km = kr * lax.rsqrt(jnp.sum(kr * kr, axis=1, keepdims=True) + EPS)
        qm = qr * (lax.rsqrt(jnp.sum(qr * qr, axis=1, keepdims=True) + EPS) * (A_DK ** -0.5))
        kkqk = _mm_nt(jnp.concatenate([km, qm], axis=0), km)
        m = -(strict * beta * kkqk[:BD] * decay)
        pre[ci, gi] = dict(beta=beta, g_cum=g_cum, g_last=g_last, e_cum=e_cum, km=km, qm=qm, vm=vm,
                           qk=kkqk[BD:] * decay, m=m, inv=eye + m)

    sq = CHUNK
    while sq > 2:
        for key in chains:
            p = pre[key]
            p["m"] = _mm(p["m"], p["m"])
        for key in chains:
            p = pre[key]
            p["inv"] = p["inv"] + _mm(p["inv"], p["m"])
        sq //= 2

    outs = []
    for ci in range(n_chunks):
        o_groups = []
        for gi in range(n_groups):
            p = pre[ci, gi]
            state = st_ref[gi]
            kq_s = _mm(jnp.concatenate([p["km"] * p["e_cum"], p["qm"] * p["e_cum"]], axis=0), state)
            v_new = _mm(p["inv"], p["beta"] * (p["vm"] - kq_s[:BD]))
            o_bd = kq_s[BD:] + _mm(p["qk"], v_new)
            k_dec = p["km"] * jnp.exp(p["g_last"] - p["g_cum"])
            st_ref[gi] = state * jnp.exp(p["g_last"]) + _mm(k_dec.T, v_new)
            o_bd = o_bd * lax.rsqrt(jnp.sum(o_bd * o_bd, axis=1, keepdims=True) * (1.0 / A_DV) + EPS)
            o_groups.append(o_bd[0:CHUNK] + o_bd[CHUNK:2 * CHUNK] + o_bd[2 * CHUNK:3 * CHUNK]
                            + o_bd[3 * CHUNK:])
        outs.append(jnp.concatenate(o_groups, axis=1))
    o = outs[0] if len(outs) == 1 else jnp.concatenate(outs, axis=0)

    zz = zz_ref[...].astype(F32)
    o_ref[...] = (o * ag_ref[...] * (zz * jax.nn.sigmoid(zz))).astype(BF16)


def _gdn(z3, zs3, conv_w, a_log, dt_bias, a_norm_g, t=4 * CHUNK):
    b, s, _ = z3.shape
    consts = _gdn_constants()
    av = jnp.zeros((2, 128), F32)
    av = av.at[0, S_ALPHA:S_ALPHA + A_HEADS].set(a_log.astype(F32))
    av = av.at[1, S_ALPHA:S_ALPHA + A_HEADS].set(dt_bias.astype(F32))
    ag = jnp.tile(a_norm_g.astype(F32), A_HEADS)[None, :]
    conv_cols = 2 * A_HEADS * A_DK + A_HEADS * A_DV

    def const_spec(a):
        nd = a.ndim
        return pl.BlockSpec(a.shape, lambda bi, ci, _n=nd: (0,) * _n)

    small_in = (conv_w.astype(F32), av, ag) + consts
    return pl.pallas_call(
        _gdn_kernel,
        grid=(b, s // t),
        in_specs=[
            pl.BlockSpec((None, t, conv_cols), lambda bi, ci: (bi, ci, O_AQKV // conv_cols)),
            pl.BlockSpec((None, t, 512), lambda bi, ci: (bi, ci, O_AZ // 512)),
            pl.BlockSpec((None, t, 128), lambda bi, ci: (bi, ci, 0)),
        ] + [const_spec(a) for a in small_in],
        out_specs=pl.BlockSpec((None, t, A_HEADS * A_DV), lambda bi, ci: (bi, ci, 0)),
        out_shape=jax.ShapeDtypeStruct((b, s, A_HEADS * A_DV), BF16),
        scratch_shapes=[pltpu.VMEM((8 + t, conv_cols), F32),
                        pltpu.VMEM((A_HEADS // GROUP_HEADS, BD, BD), F32)],
        compiler_params=pltpu.CompilerParams(dimension_semantics=("parallel", "arbitrary"),
                                             vmem_limit_bytes=VMEM_LIMIT),
        name="gdn",
    )(z3, z3, zs3, *small_in)


def _rope(x, cs, sn, first):
    w = x.shape[1]
    swapped = jnp.where(first, pltpu.roll(x, w - B_HD // 2, 1), pltpu.roll(x, B_HD // 2, 1))
    return x * cs + swapped * sn


def _rope_kernel(pos_ref, inv_ref, bq_ref, iq_ref, kd_ref, ikd_ref, q_out, iq_out, ka_out, kb_out, ik_out):
    ang = pos_ref[...] * inv_ref[...]
    lane = lax.broadcasted_iota(jnp.int32, ang.shape, 1)
    first = (lane & (B_HD - 1)) < (B_HD // 2)
    cs = jnp.cos(ang)
    sn = jnp.sin(ang)
    sn = jnp.where(first, -sn, sn)
    kd = kd_ref[...].astype(F32)
    ka_out[...] = _rope(kd[:, :128], cs, sn, first).astype(BF16)
    kb_out[...] = _rope(kd[:, 128:], cs, sn, first).astype(BF16)
    ik_out[...] = _rope(ikd_ref[...].astype(F32), cs, sn, first).astype(BF16)
    cs4 = jnp.concatenate([cs] * 4, axis=1)
    sn4 = jnp.concatenate([sn] * 4, axis=1)
    first4 = jnp.concatenate([first] * 4, axis=1)
    q_out[...] = (_rope(bq_ref[...].astype(F32), cs4, sn4, first4) * (B_HD ** -0.5 * LOG2E)).astype(BF16)
    iq_out[...] = _rope(iq_ref[...].astype(F32), cs4, sn4, first4).astype(BF16)


def _rope_qk(z3, positions, tr=512):
    b, s, _ = z3.shape
    half = B_HD // 2
    inv = ROPE_THETA ** (-jnp.arange(half, dtype=F32) / half)
    inv128 = jnp.tile(inv, 4)[None, :]
    pos = positions.astype(F32)[:, :, None]
    row = lambda w, cb: pl.BlockSpec((None, tr, w), lambda bi, ri, _c=cb: (bi, ri, _c))
    shp = lambda w: jax.ShapeDtypeStruct((b, s, w), BF16)
    return pl.pallas_call(
        _rope_kernel,
        grid=(b, s // tr),
        in_specs=[row(1, 0), pl.BlockSpec((1, 128), lambda bi, ri: (0, 0)),
                  row(512, O_BQ // 512), row(512, O_IQ // 512), row(256, O_KD // 256), row(128, O_IK // 128)],
        out_specs=[row(512, 0), row(512, 0), row(128, 0), row(128, 0), row(128, 0)],
        out_shape=[shp(512), shp(512), shp(128), shp(128), shp(128)],
        compiler_params=pltpu.CompilerParams(dimension_semantics=("parallel", "parallel"),
                                             vmem_limit_bytes=VMEM_LIMIT),
        name="rope",
    )(pos, inv128, z3, z3, z3, z3)


def _dsa_kernel(topk, tk, q_ref, iq_ref, ka_ref, kb_ref, vd_ref, ik_ref, zs_ref, o_ref, sc_ref):
    qb = pl.program_id(1)
    s = ka_ref.shape[0]
    tq = q_ref.shape[0]
    nkt_max = s // tk
    r0 = qb * tq
    n_tiles = (r0 + tq + tk - 1) // tk

    lane128 = lax.broadcasted_iota(jnp.int32, (tq, 128), 1)
    low_half = lane128 < B_HD
    high_half = lane128 >= B_HD

    def head_slab(x, h):
        slab = x[:, (h // 2) * 128:(h // 2 + 1) * 128]
        return jnp.where(low_half if h % 2 == 0 else high_half, slab, jnp.zeros_like(slab))

    sm = zs_ref[...]
    iw_scale = (IDX_HEADS ** -0.5) * (IDX_HD ** -0.5)
    wcols = [jnp.sum(jnp.where(lane128 == S_IW + h, sm, 0.0), axis=1, keepdims=True) * iw_scale
             for h in range(IDX_HEADS)]
    row = r0 + lax.broadcasted_iota(jnp.int32, (tq, 1), 0)
    limit = ((row >> 6) + 1) << 6

    iq = iq_ref[...]
    iq_heads = [head_slab(iq, h) for h in range(IDX_HEADS)]

    def index_tile(kt, carry):
        off = pl.multiple_of(kt * tk, tk)
        ikt = ik_ref[pl.ds(off, tk), :]
        acc = jnp.zeros((tq, tk), F32)
        for h in range(IDX_HEADS):
            sc = lax.dot_general(iq_heads[h], ikt, (((1,), (1,)), ((), ())), preferred_element_type=F32)
            acc = acc + wcols[h] * jnp.maximum(sc, 0.0)
        kpos = off + lax.broadcasted_iota(jnp.int32, (tq, tk), 1)
        sc_ref[kt] = jnp.where(kpos < limit, acc, -jnp.inf)
        return carry

    lax.fori_loop(0, n_tiles, index_tile, 0)

    for j in range(nkt_max):
        @pl.when(n_tiles == j + 1)
        def _(j=j):
            _topk_mask(topk, j + 1, limit, sc_ref)

    q = q_ref[...]
    heads_per_kv = B_HEADS // B_KV_HEADS
    nr = heads_per_kv * tq
    qrows = [jnp.concatenate([head_slab(q, g * heads_per_kv + j) for j in range(heads_per_kv)], axis=0)
             for g in range(B_KV_HEADS)]
    k_refs = (ka_ref, kb_ref)

    def attend_tile(kt, carry):
        off = pl.multiple_of(kt * tk, tk)
        bias = sc_ref[kt]
        bias4 = jnp.concatenate([bias] * heads_per_kv, axis=0)
        out = []
        for g in range(B_KV_HEADS):
            m_run, l_run, acc = carry[g]
            logits = lax.dot_general(qrows[g], k_refs[g][pl.ds(off, tk), :], (((1,), (1,)), ((), ())),
                                     preferred_element_type=F32) + bias4
            m_new = jnp.maximum(m_run, jnp.max(logits, axis=1, keepdims=True))
            p = jnp.exp2(logits - m_new)
            alpha = jnp.exp2(m_run - m_new)
            l_new = alpha * l_run + jnp.sum(p, axis=1, keepdims=True)
            acc = alpha * acc + _dot(p.astype(BF16), vd_ref[pl.ds(off, tk), g * 128:(g + 1) * 128])
            out.append((m_new, l_new, acc))
        return tuple(out)

    init = tuple((jnp.full((nr, 1), NEG_BIG, F32), jnp.zeros((nr, 1), F32), jnp.zeros((nr, 128), F32))
                 for _ in range(B_KV_HEADS))
    fin = lax.fori_loop(0, n_tiles, attend_tile, init)
    for g in range(B_KV_HEADS):
        _, l_run, acc = fin[g]
        og = acc / l_run
        for pp in range(heads_per_kv // 2):
            even = og[(2 * pp) * tq:(2 * pp + 1) * tq]
            odd = og[(2 * pp + 1) * tq:(2 * pp + 2) * tq]
            col = (g * (heads_per_kv // 2) + pp) * 128
            o_ref[:, col:col + 128] = jnp.where(low_half, even, odd).astype(BF16)


def _topk_mask(topk, nt, limit, sc_ref):
    _, tq, tk = sc_ref.shape
    kf = float(topk)

    def tiles():
        return [sc_ref[t] for t in range(nt)]

    def row_sum(pred):
        tot = None
        for t, x in enumerate(tiles()):
            c = jnp.sum(pred(x, t).astype(F32), axis=1, keepdims=True)
            tot = c if tot is None else tot + c
        return tot

    def row_max(val):
        best = None
        for t, x in enumerate(tiles()):
            c = jnp.max(val(x, t), axis=1, keepdims=True)
            best = c if best is None else jnp.maximum(best, c)
        return best

    r_max = row_max(lambda x, t: x)
    r_min = -row_max(lambda x, t: jnp.where(x == -jnp.inf, -jnp.inf, -x))
    c_max = row_sum(lambda x, t: x >= r_max)
    small = limit <= topk

    def bisect(_, carry):
        lo, hi = carry
        mid = 0.5 * (lo + hi)
        ge = row_sum(lambda x, t: x >= mid) >= kf
        return jnp.where(ge, mid, lo), jnp.where(ge, hi, mid)

    lo, hi = lax.fori_loop(0, BISECT_STEPS, bisect, (r_min, r_max))
    top_tied = c_max >= kf
    done0 = jnp.where(small | top_tied, 1.0, 0.0)
    thr0 = jnp.where(top_tied, r_max, lo)

    def peel_cond(carry):
        return jnp.sum(1.0 - carry[0]) > 0.0

    def peel(carry):
        done, thr, hi_c = carry
        v1 = row_max(lambda x, t: jnp.where(x < hi_c, x, -jnp.inf))
        ok = row_sum(lambda x, t: x >= v1) >= kf
        act = done < 0.5
        thr = jnp.where(act & ok, v1, thr)
        hi_c = jnp.where(act & (~ok), v1, hi_c)
        return jnp.where(ok, 1.0, done), thr, hi_c

    _, thr, _ = lax.while_loop(peel_cond, peel, (done0, thr0, hi))

    def key_index(t):
        return (t * tk + lax.broadcasted_iota(jnp.int32, (tq, tk), 1)).astype(F32)

    need = kf - row_sum(lambda x, t: x > thr)
    n_eq = row_sum(lambda x, t: x == thr)
    contested = jnp.sum(jnp.where((n_eq > need) & (~small), 1.0, 0.0)) > 0.0
    last = float(nt * tk - 1)

    def tie_search(_, carry):
        jlo, jhi = carry
        mid = jnp.floor(0.5 * (jlo + jhi))
        ge = row_sum(lambda x, t: (x == thr) & (key_index(t) <= mid)) >= need
        return jnp.where(ge, jlo, mid), jnp.where(ge, mid, jhi)

    def tie_cut():
        n_iter = int(np.ceil(np.log2(nt * tk))) + 1
        return lax.fori_loop(0, n_iter, tie_search,
                             (jnp.full((tq, 1), -1.0, F32), jnp.full((tq, 1), last, F32)))[1]

    jcut = lax.cond(contested, tie_cut, lambda: jnp.full((tq, 1), last, F32))
    for t, x in enumerate(tiles()):
        sel = (x > -jnp.inf) & (small | (x > thr) | ((x == thr) & (key_index(t) <= jcut)))
        sc_ref[t] = jnp.where(sel, 0.0, NEG_BIG)


def _dsa(z3, zs3, positions, tq=256, tk=512):
    b, s, _ = z3.shape
    topk = min(TOPK_MAX, s // 4)
    qr, iqr, ka, kb, ik = _rope_qk(z3, positions)
    kernel = lambda *refs: _dsa_kernel(topk, tk, *refs)
    qblock = lambda w: pl.BlockSpec((None, tq, w), lambda bi, qi: (bi, qi, 0))
    keys = lambda w, cb=0: pl.BlockSpec((None, s, w), lambda bi, qi, _c=cb: (bi, 0, _c))
    return pl.pallas_call(
        kernel,
        grid=(b, s // tq),
        in_specs=[qblock(512), qblock(512), keys(128), keys(128), keys(256, O_VD // 256), keys(128),
                  qblock(128)],
        out_specs=pl.BlockSpec((None, tq, B_HEADS * B_HD), lambda bi, qi: (bi, qi, 0)),
        out_shape=jax.ShapeDtypeStruct((b, s, B_HEADS * B_HD), BF16),
        scratch_shapes=[pltpu.VMEM((s // tk, tq, tk), F32)],
        compiler_params=pltpu.CompilerParams(dimension_semantics=("parallel", "arbitrary"),
                                             vmem_limit_bytes=VMEM_LIMIT),
        name="dsa",
    )(qr, iqr, ka, kb, z3, ik, zs3)


def _merge_kernel(x_ref, oa_ref, ob_ref, ga_ref, gb_ref, bg_ref, wa_ref, wb_ref, wo_ref, n2_ref, wr_ref,
                  br_ref, x1_ref, h2_ref, rl_ref):
    pa = _dot(oa_ref[...], wa_ref[...])
    pb = _dot(ob_ref[...], wb_ref[...])
    bgv = bg_ref[...]
    ga = jax.nn.sigmoid(ga_ref[...].astype(F32) + bgv[:, :D_MODEL])
    gb = jax.nn.sigmoid(gb_ref[...].astype(F32) + bgv[:, D_MODEL:])
    merged = ga * pa + gb * pb
    x1 = x_ref[...] + _mm(merged, wo_ref[...])
    x1_ref[...] = x1
    h2 = _rmsnorm_rows(x1, n2_ref[...])
    hb, hl = _split(h2)
    h2_ref[...] = hb
    s1 = _dot(hb, wr_ref[...])
    s2 = _dot(hl, wr_ref[:, :128])
    rl_ref[...] = s1[:, :128] + s1[:, 128:] + s2 + br_ref[...]


def _merge(x2d, oa, ob, z2d, b_gate, wa, wb, wo, n2, wr, br, tm=512):
    n = x2d.shape[0]
    full = lambda shape: pl.BlockSpec(shape, lambda i: (0, 0))
    return pl.pallas_call(
        _merge_kernel,
        grid=(n // tm,),
        in_specs=[
            pl.BlockSpec((tm, D_MODEL), lambda i: (i, 0)),
            pl.BlockSpec((tm, 512), lambda i: (i, 0)),
            pl.BlockSpec((tm, 512), lambda i: (i, 0)),
            pl.BlockSpec((tm, D_MODEL), lambda i: (i, O_GA // D_MODEL)),
            pl.BlockSpec((tm, D_MODEL), lambda i: (i, O_GB // D_MODEL)),
            full((1, 2 * D_MODEL)), full((512, D_MODEL)), full((512, D_MODEL)), full((D_MODEL, D_MODEL)),
            full((1, D_MODEL)), full((D_MODEL, 256)), full((1, 128)),
        ],
        out_specs=[
            pl.BlockSpec((tm, D_MODEL), lambda i: (i, 0)),
            pl.BlockSpec((tm, D_MODEL), lambda i: (i, 0)),
            pl.BlockSpec((tm, 128), lambda i: (i, 0)),
        ],
        out_shape=[jax.ShapeDtypeStruct((n, D_MODEL), F32), jax.ShapeDtypeStruct((n, D_MODEL), BF16),
                   jax.ShapeDtypeStruct((n, 128), F32)],
        compiler_params=pltpu.CompilerParams(dimension_semantics=("parallel",),
                                             vmem_limit_bytes=VMEM_LIMIT),
        name="merge",
    )(x2d, oa, ob, z2d, z2d, b_gate, wa, wb, wo, n2, wr, br)


R_GROUP, R_EXPERT = 0, N_GROUPS


def _routing_weights(rl):
    lane = lax.broadcasted_iota(jnp.int32, rl.shape, 1)
    big = jnp.int32(1 << 20)
    gmask = lane < N_GROUPS
    gl = jnp.where(gmask, rl, -jnp.inf)
    gmax = jnp.max(gl, axis=1, keepdims=True)
    gsel = jnp.min(jnp.where(gmask & (rl == gmax), lane, big), axis=1, keepdims=True)
    ggate = 1.0 / jnp.sum(jnp.where(gmask, jnp.exp(gl - gmax), 0.0), axis=1, keepdims=True)
    e_lo = R_EXPERT + gsel * EXPERTS_PER_GROUP
    emask = (lane >= e_lo) & (lane < e_lo + EXPERTS_PER_GROUP)
    el = jnp.where(emask, rl, -jnp.inf)
    emax = jnp.max(el, axis=1, keepdims=True)
    ee = jnp.where(emask, jnp.exp(el - emax), 0.0)
    ep = jnp.where(emask, ee / jnp.sum(ee, axis=1, keepdims=True), -1.0)
    p1 = jnp.max(ep, axis=1, keepdims=True)
    i1 = jnp.min(jnp.where(ep == p1, lane, big), axis=1, keepdims=True)
    ep2 = jnp.where(lane == i1, -1.0, ep)
    p2 = jnp.max(ep2, axis=1, keepdims=True)
    i2 = jnp.min(jnp.where(ep2 == p2, lane, big), axis=1, keepdims=True)
    tot = p1 + p2
    return (jnp.where(lane == i1, ggate * (p1 / tot), 0.0)
            + jnp.where(lane == i2, ggate * (p2 / tot), 0.0))


def _moe_kernel(x1_ref, h2_ref, rl_ref, w1_ref, w3_ref, w2_ref, fg_ref, o_ref, y_ref, comb_ref):
    e = pl.program_id(1)

    @pl.when(e == 0)
    def _():
        comb_ref[...] = _routing_weights(rl_ref[...])
        y_ref[...] = jnp.zeros(y_ref.shape, F32)

    h = h2_ref[...]
    a = _dot(h, w1_ref[...])
    b = _dot(h, w3_ref[...])
    comb = comb_ref[...]
    lane = lax.broadcasted_iota(jnp.int32, comb.shape, 1)
    ce = jnp.sum(jnp.where(lane == R_EXPERT + e, comb, 0.0), axis=1, keepdims=True)
    act = (a * jax.nn.sigmoid(a)) * b * ce
    y_ref[...] += _dot(act.astype(BF16), w2_ref[...])

    @pl.when(e == pl.num_programs(1) - 1)
    def _():
        o_ref[...] = _rmsnorm_rows(x1_ref[...] + y_ref[...], fg_ref[...])


def _moe(x1, h2, rl, w1, w3, w2, fg, tm=1024):
    n = x1.shape[0]
    return pl.pallas_call(
        _moe_kernel,
        grid=(n // tm, N_EXPERTS),
        in_specs=[
            pl.BlockSpec((tm, D_MODEL), lambda i, e: (i, 0)),
            pl.BlockSpec((tm, D_MODEL), lambda i, e: (i, 0)),
            pl.BlockSpec((tm, 128), lambda i, e: (i, 0)),
            pl.BlockSpec((None, D_MODEL, D_EXPERT), lambda i, e: (e, 0, 0)),
            pl.BlockSpec((None, D_MODEL, D_EXPERT), lambda i, e: (e, 0, 0)),
            pl.BlockSpec((None, D_EXPERT, D_MODEL), lambda i, e: (e, 0, 0)),
            pl.BlockSpec((1, D_MODEL), lambda i, e: (0, 0)),
        ],
        out_specs=pl.BlockSpec((tm, D_MODEL), lambda i, e: (i, 0)),
        out_shape=jax.ShapeDtypeStruct((n, D_MODEL), F32),
        scratch_shapes=[pltpu.VMEM((tm, D_MODEL), F32), pltpu.VMEM((tm, 128), F32)],
        compiler_params=pltpu.CompilerParams(dimension_semantics=("parallel", "arbitrary"),
                                             vmem_limit_bytes=VMEM_LIMIT),
        name="moe",
    )(x1, h2, rl, w1, w3, w2, fg)


_W_OFF = {}
_off = 0
for _name, _n in (("a_q", 512), ("a_k", 512), ("a_v", 512), ("a_z", 512), ("a_beta", 8), ("a_alpha", 8),
                  ("b_q", 512), ("b_k", 128), ("b_v", 128), ("i_q", 512), ("i_k", 64), ("i_w", 8),
                  ("gate_a", 1024), ("gate_b", 1024)):
    _W_OFF[_name] = (_off, _off + _n)
    _off += _n


def _cols(w, name, lo=0, hi=None):
    a, b = _W_OFF[name]
    return w[:, a + lo:(b if hi is None else a + hi)]


def _hi_lo_pair(w_small):
    pad = jnp.zeros((w_small.shape[0], 128 - w_small.shape[1]), F32)
    w = jnp.concatenate([w_small.astype(F32), pad], axis=1)
    hi, lo = _split(w)
    return jnp.concatenate([hi, lo], axis=1)


def _layout_w_in(w):
    wb = w.astype(BF16)
    k0, k1 = _cols(wb, "b_k", 0, 64), _cols(wb, "b_k", 64, 128)
    v0, v1 = _cols(wb, "b_v", 0, 64), _cols(wb, "b_v", 64, 128)
    ik = _cols(wb, "i_k")
    main = jnp.concatenate([
        _cols(wb, "a_q"), _cols(wb, "a_k"), _cols(wb, "a_v"), _cols(wb, "a_z"), _cols(wb, "gate_a"),
        _cols(wb, "gate_b"), _cols(wb, "b_q"), _cols(wb, "i_q"), k0, k0, k1, k1, v0, v0, v1, v1, ik, ik],
        axis=1)
    small = _hi_lo_pair(jnp.concatenate([_cols(w, "a_beta"), _cols(w, "a_alpha"), _cols(w, "i_w")], axis=1))
    return main, small


def kernel(x, positions, norm1_g, w_in, b_gate, conv_w, a_log, dt_bias, a_norm_g, w_proj_a, w_proj_b, w_out,
           norm2_g, w_router_group, b_router_group, w_router_expert, b_router_expert, w_exp_gate, w_exp_up,
           w_exp_down, final_norm_g):
    b, s, d = x.shape
    n = b * s
    depth = w_in.shape[0]
    xc = x.reshape(n, d).astype(F32)
    for l in range(depth):
        w_main, w_small = _layout_w_in(w_in[l])
        z, zs = _in_proj(xc, norm1_g[l][None, :].astype(F32), w_main, w_small)
        z3 = z.reshape(b, s, Z_W)
        zs3 = zs.reshape(b, s, 128)
        o_a = _gdn(z3, zs3, conv_w[l], a_log[l], dt_bias[l], a_norm_g[l])
        o_b = _dsa(z3, zs3, positions)
        wr = _hi_lo_pair(jnp.concatenate([w_router_group[l], w_router_expert[l]], axis=1))
        br = jnp.concatenate([b_router_group[l], b_router_expert[l],
                              jnp.zeros((128 - N_GROUPS - N_EXPERTS,), F32)])[None, :].astype(F32)
        x1, h2, rl = _merge(xc, o_a.reshape(n, -1), o_b.reshape(n, -1), z, b_gate[l][None, :].astype(F32),
                            w_proj_a[l].astype(BF16), w_proj_b[l].astype(BF16), w_out[l].astype(BF16),
                            norm2_g[l][None, :].astype(F32), wr, br)
        last = l == depth - 1
        fg = final_norm_g[None, :].astype(F32) if last else None
        if not last:
            raise NotImplementedError("only the final layer fuses the output norm")
        xc = _moe(x1, h2, rl, w_exp_gate[l].astype(BF16), w_exp_up[l].astype(BF16),
                  w_exp_down[l].astype(BF16), fg)
    return xc.reshape(b, s, d).astype(x.dtype)
```

```python
import numpy as np
import jax
import jax.numpy as jnp
from jax import lax
from jax.experimental import pallas as pl
from jax.experimental.pallas import tpu as pltpu

F32 = jnp.float32
BF16 = jnp.bfloat16

D_MODEL = 1024
CHUNK = 64
EPS = 1e-6
ROPE_THETA = 10000.0
A_HEADS = 8
A_DK = 64
A_DV = 64
CONV_K = 4
B_HEADS = 8
B_KV_HEADS = 2
B_HD = 64
IDX_HEADS = 8
IDX_HD = 64
TOPK_MAX = 256
N_GROUPS = 4
EXPERTS_PER_GROUP = 4
N_EXPERTS = 16
D_EXPERT = 256

O_AQKV, O_AZ, O_GA, O_GB, O_BQ, O_IQ, O_KD, O_VD, O_IK = 0, 1536, 2048, 3072, 4096, 4608, 5120, 5376, 5632
Z_W = 5760
S_BETA, S_ALPHA, S_IW = 0, 8, 16

GROUP_HEADS = 4
BD = GROUP_HEADS * CHUNK
NEG_BIG = -1e30
LOG2E = 1.4426950408889634
BISECT_STEPS = 18
VMEM_LIMIT = 56 * 1024 * 1024


def _split(x):
    hi = x.astype(BF16)
    lo = (x - hi.astype(F32)).astype(BF16)
    return hi, lo


def _dot(a, b):
    return jnp.dot(a, b, preferred_element_type=F32)


def _mm(a, b):
    return _dot(a.astype(BF16), b.astype(BF16))


def _mm_nt(a, b):
    return lax.dot_general(a.astype(BF16), b.astype(BF16), (((1,), (1,)), ((), ())),
                           preferred_element_type=F32)


def _mm_exact_lhs(a_bf16, x):
    xh, xl = _split(x)
    return _dot(a_bf16, xh) + _dot(a_bf16, xl)


def _rmsnorm_rows(x, g):
    return x * lax.rsqrt(jnp.mean(x * x, axis=-1, keepdims=True) + EPS) * g


_Z_CHUNKS = tuple((o, min(512, Z_W - o)) for o in range(0, Z_W, 512))


def _in_proj_kernel(x_ref, g_ref, w_ref, ws_ref, z_ref, zs_ref):
    h = _rmsnorm_rows(x_ref[...], g_ref[...])
    hb, hl = _split(h)
    for o, w in _Z_CHUNKS:
        z_ref[:, o:o + w] = _dot(hb, w_ref[:, o:o + w]).astype(BF16)
    s1 = _dot(hb, ws_ref[...])
    s2 = _dot(hl, ws_ref[:, :128])
    zs_ref[...] = s1[:, :128] + s1[:, 128:] + s2


def _in_proj(x2d, g, w_main, w_small, tm=512):
    n = x2d.shape[0]
    return pl.pallas_call(
        _in_proj_kernel,
        grid=(n // tm,),
        in_specs=[
            pl.BlockSpec((tm, D_MODEL), lambda i: (i, 0)),
            pl.BlockSpec((1, D_MODEL), lambda i: (0, 0)),
            pl.BlockSpec((D_MODEL, Z_W), lambda i: (0, 0)),
            pl.BlockSpec((D_MODEL, 256), lambda i: (0, 0)),
        ],
        out_specs=[
            pl.BlockSpec((tm, Z_W), lambda i: (i, 0)),
            pl.BlockSpec((tm, 128), lambda i: (i, 0)),
        ],
        out_shape=[jax.ShapeDtypeStruct((n, Z_W), BF16), jax.ShapeDtypeStruct((n, 128), F32)],
        compiler_params=pltpu.CompilerParams(dimension_semantics=("parallel",),
                                             vmem_limit_bytes=VMEM_LIMIT),
        name="in_proj",
    )(x2d, g, w_main, w_small)


def _gdn_constants():
    r = np.arange(BD)
    same = (r[:, None] // CHUNK) == (r[None, :] // CHUNK)
    incl = same & (r[:, None] >= r[None, :])
    strict = same & (r[:, None] > r[None, :])
    eye = np.eye(BD, dtype=np.float32)
    ll = np.concatenate([incl, same], axis=0).astype(np.float32)
    sel = np.zeros((4, BD, 128), np.float32)
    for gi in range(2):
        for h in range(GROUP_HEADS):
            sel[gi * 2 + 0, h * CHUNK:(h + 1) * CHUNK, S_BETA + gi * GROUP_HEADS + h] = 1.0
            sel[gi * 2 + 1, h * CHUNK:(h + 1) * CHUNK, S_ALPHA + gi * GROUP_HEADS + h] = 1.0
    return (jnp.asarray(incl, F32), jnp.asarray(strict, F32), jnp.asarray(same, F32), jnp.asarray(eye),
            jnp.asarray(ll, BF16), jnp.asarray(sel))


def _tile4(x):
    return jnp.concatenate([x, x, x, x], axis=0)


def _gdn_kernel(zq_ref, zz_ref, zs_ref, cw_ref, av_ref, ag_ref, incl_ref, strict_ref, bdm_ref, eye_ref,
                ll_ref, sel_ref, o_ref, ext_ref, st_ref):
    c = pl.program_id(1)
    t = zq_ref.shape[0]

    @pl.when(c == 0)
    def _():
        ext_ref[0:8, :] = jnp.zeros((8, ext_ref.shape[1]), F32)
        st_ref[...] = jnp.zeros(st_ref.shape, F32)

    ext_ref[8:8 + t, :] = zq_ref[...].astype(F32)
    cw = cw_ref[...]
    y = cw[0:1, :] * ext_ref[pl.ds(8 - (CONV_K - 1), t), :]
    for j in range(1, CONV_K):
        y = y + cw[j:j + 1, :] * ext_ref[pl.ds(8 - (CONV_K - 1) + j, t), :]
    ext_ref[0:8, :] = ext_ref[t:t + 8, :]
    y = y * jax.nn.sigmoid(y)

    hw = A_HEADS * A_DK
    q_all = y[:, :hw]
    k_all = y[:, hw:2 * hw]
    v_all = y[:, 2 * hw:]

    sm = zs_ref[...]
    av = av_ref[...]
    lane = lax.broadcasted_iota(jnp.int32, sm.shape, 1)
    xg = sm + av[1:2, :]
    softplus = jnp.maximum(xg, 0.0) + jnp.log1p(jnp.exp(-jnp.abs(xg)))
    g_all = -jnp.exp(av[0:1, :]) * softplus
    bg = jnp.where(lane < S_ALPHA, jax.nn.sigmoid(sm), g_all)

    incl = incl_ref[...]
    strict = strict_ref[...]
    bdm = bdm_ref[...]
    eye = eye_ref[...]
    ll = ll_ref[...]
    incl_b = ll[:BD]

    n_chunks = t // CHUNK
    n_groups = A_HEADS // GROUP_HEADS
    chains = [(ci, gi) for ci in range(n_chunks) for gi in range(n_groups)]

    pre = {}
    for ci, gi in chains:
        r0, c0 = ci * CHUNK, gi * BD
        bg4 = _tile4(bg[r0:r0 + CHUNK])
        beta = jnp.sum(bg4 * sel_ref[gi * 2 + 0], axis=1, keepdims=True)
        gcol = jnp.sum(bg4 * sel_ref[gi * 2 + 1], axis=1, keepdims=True)
        gs = _mm_exact_lhs(ll, jnp.broadcast_to(gcol, (BD, 128)))
        g_cum = gs[:BD, :1]
        g_last = gs[BD:, :1]
        diff = _mm_exact_lhs(incl_b, gcol * strict)
        decay = jnp.where(incl > 0.0, jnp.exp(diff), 0.0)
        e_cum = jnp.exp(g_cum)
        kr = _tile4(k_all[r0:r0 + CHUNK, c0:c0 + BD]) * bdm
        qr = _tile4(q_all[r0:r0 + CHUNK, c0:c0 + BD]) * bdm
        vm = _tile4(v_all[r0:r0 + CHUNK, c0:c0 + BD]) * bdm
        km = kr * lax.rsqrt(jnp.sum(kr * kr, axis=1, keepdims=True) + EPS)
        qm = qr * (lax.rsqrt(jnp.sum(qr * qr, axis=1, keepdims=True) + EPS) * (A_DK ** -0.5))
        kkqk = _mm_nt(jnp.concatenate([km, qm], axis=0), km)
        m = -(strict * beta * kkqk[:BD] * decay)
        pre[ci, gi] = dict(beta=beta, g_cum=g_cum, g_last=g_last, e_cum=e_cum, km=km, qm=qm, vm=vm,
                           qk=kkqk[BD:] * decay, m=m, inv=eye + m)

    sq = CHUNK
    while sq > 2:
        for key in chains:
            p = pre[key]
            p["m"] = _mm(p["m"], p["m"])
        for key in chains:
            p = pre[key]
            p["inv"] = p["inv"] + _mm(p["inv"], p["m"])
        sq //= 2

    outs = []
    for ci in range(n_chunks):
        o_groups = []
        for gi in range(n_groups):
            p = pre[ci, gi]
            state = st_ref[gi]
            kq_s = _mm(jnp.concatenate([p["km"] * p["e_cum"], p["qm"] * p["e_cum"]], axis=0), state)
            v_new = _mm(p["inv"], p["beta"] * (p["vm"] - kq_s[:BD]))
            o_bd = kq_s[BD:] + _mm(p["qk"], v_new)
            k_dec = p["km"] * jnp.exp(p["g_last"] - p["g_cum"])
            st_ref[gi] = state * jnp.exp(p["g_last"]) + _mm(k_dec.T, v_new)
            o_bd = o_bd * lax.rsqrt(jnp.sum(o_bd * o_bd, axis=1, keepdims=True) * (1.0 / A_DV) + EPS)
            o_groups.append(o_bd[0:CHUNK] + o_bd[CHUNK:2 * CHUNK] + o_bd[2 * CHUNK:3 * CHUNK]
                            + o_bd[3 * CHUNK:])
        outs.append(jnp.concatenate(o_groups, axis=1))
    o = outs[0] if len(outs) == 1 else jnp.concatenate(outs, axis=0)

    zz = zz_ref[...].astype(F32)
    o_ref[...] = (o * ag_ref[...] * (zz * jax.nn.sigmoid(zz))).astype(BF16)


def _gdn(z3, zs3, conv_w, a_log, dt_bias, a_norm_g, t=4 * CHUNK):
    b, s, _ = z3.shape
    consts = _gdn_constants()
    av = jnp.zeros((2, 128), F32)
    av = av.at[0, S_ALPHA:S_ALPHA + A_HEADS].set(a_log.astype(F32))
    av = av.at[1, S_ALPHA:S_ALPHA + A_HEADS].set(dt_bias.astype(F32))
    ag = jnp.tile(a_norm_g.astype(F32), A_HEADS)[None, :]
    conv_cols = 2 * A_HEADS * A_DK + A_HEADS * A_DV

    def const_spec(a):
        nd = a.ndim
        return pl.BlockSpec(a.shape, lambda bi, ci, _n=nd: (0,) * _n)

    small_in = (conv_w.astype(F32), av, ag) + consts
    return pl.pallas_call(
        _gdn_kernel,
        grid=(b, s // t),
        in_specs=[
            pl.BlockSpec((None, t, conv_cols), lambda bi, ci: (bi, ci, O_AQKV // conv_cols)),
            pl.BlockSpec((None, t, 512), lambda bi, ci: (bi, ci, O_AZ // 512)),
            pl.BlockSpec((None, t, 128), lambda bi, ci: (bi, ci, 0)),
        ] + [const_spec(a) for a in small_in],
        out_specs=pl.BlockSpec((None, t, A_HEADS * A_DV), lambda bi, ci: (bi, ci, 0)),
        out_shape=jax.ShapeDtypeStruct((b, s, A_HEADS * A_DV), BF16),
        scratch_shapes=[pltpu.VMEM((8 + t, conv_cols), F32),
                        pltpu.VMEM((A_HEADS // GROUP_HEADS, BD, BD), F32)],
        compiler_params=pltpu.CompilerParams(dimension_semantics=("parallel", "arbitrary"),
                                             vmem_limit_bytes=VMEM_LIMIT),
        name="gdn",
    )(z3, z3, zs3, *small_in)


def _rope(x, cs, sn, first):
    w = x.shape[1]
    swapped = jnp.where(first, pltpu.roll(x, w - B_HD // 2, 1), pltpu.roll(x, B_HD // 2, 1))
    return x * cs + swapped * sn


def _rope_kernel(pos_ref, inv_ref, bq_ref, iq_ref, kd_ref, ikd_ref, q_out, iq_out, ka_out, kb_out, ik_out):
    ang = pos_ref[...] * inv_ref[...]
    lane = lax.broadcasted_iota(jnp.int32, ang.shape, 1)
    first = (lane & (B_HD - 1)) < (B_HD // 2)
    cs = jnp.cos(ang)
    sn = jnp.sin(ang)
    sn = jnp.where(first, -sn, sn)
    kd = kd_ref[...].astype(F32)
    ka_out[...] = _rope(kd[:, :128], cs, sn, first).astype(BF16)
    kb_out[...] = _rope(kd[:, 128:], cs, sn, first).astype(BF16)
    ik_out[...] = _rope(ikd_ref[...].astype(F32), cs, sn, first).astype(BF16)
    cs4 = jnp.concatenate([cs] * 4, axis=1)
    sn4 = jnp.concatenate([sn] * 4, axis=1)
    first4 = jnp.concatenate([first] * 4, axis=1)
    q_out[...] = (_rope(bq_ref[...].astype(F32), cs4, sn4, first4) * (B_HD ** -0.5 * LOG2E)).astype(BF16)
    iq_out[...] = _rope(iq_ref[...].astype(F32), cs4, sn4, first4).astype(BF16)


def _rope_qk(z3, positions, tr=512):
    b, s, _ = z3.shape
    half = B_HD // 2
    inv = ROPE_THETA ** (-jnp.arange(half, dtype=F32) / half)
    inv128 = jnp.tile(inv, 4)[None, :]
    pos = positions.astype(F32)[:, :, None]
    row = lambda w, cb: pl.BlockSpec((None, tr, w), lambda bi, ri, _c=cb: (bi, ri, _c))
    shp = lambda w: jax.ShapeDtypeStruct((b, s, w), BF16)
    return pl.pallas_call(
        _rope_kernel,
        grid=(b, s // tr),
        in_specs=[row(1, 0), pl.BlockSpec((1, 128), lambda bi, ri: (0, 0)),
                  row(512, O_BQ // 512), row(512, O_IQ // 512), row(256, O_KD // 256), row(128, O_IK // 128)],
        out_specs=[row(512, 0), row(512, 0), row(128, 0), row(128, 0), row(128, 0)],
        out_shape=[shp(512), shp(512), shp(128), shp(128), shp(128)],
        compiler_params=pltpu.CompilerParams(dimension_semantics=("parallel", "parallel"),
                                             vmem_limit_bytes=VMEM_LIMIT),
        name="rope",
    )(pos, inv128, z3, z3, z3, z3)


def _dsa_kernel(topk, tk, q_ref, iq_ref, ka_ref, kb_ref, vd_ref, ik_ref, zs_ref, o_ref, sc_ref):
    qb = pl.program_id(1)
    s = ka_ref.shape[0]
    tq = q_ref.shape[0]
    nkt_max = s // tk
    r0 = qb * tq
    n_tiles = (r0 + tq + tk - 1) // tk

    lane128 = lax.broadcasted_iota(jnp.int32, (tq, 128), 1)
    low_half = lane128 < B_HD
    high_half = lane128 >= B_HD

    def head_slab(x, h):
        slab = x[:, (h // 2) * 128:(h // 2 + 1) * 128]
        return jnp.where(low_half if h % 2 == 0 else high_half, slab, jnp.zeros_like(slab))

    sm = zs_ref[...]
    iw_scale = (IDX_HEADS ** -0.5) * (IDX_HD ** -0.5)
    wcols = [jnp.sum(jnp.where(lane128 == S_IW + h, sm, 0.0), axis=1, keepdims=True) * iw_scale
             for h in range(IDX_HEADS)]
    row = r0 + lax.broadcasted_iota(jnp.int32, (tq, 1), 0)
    limit = ((row >> 6) + 1) << 6

    iq = iq_ref[...]
    iq_heads = [head_slab(iq, h) for h in range(IDX_HEADS)]

    def index_tile(kt, carry):
        off = pl.multiple_of(kt * tk, tk)
        ikt = ik_ref[pl.ds(off, tk), :]
        acc = jnp.zeros((tq, tk), F32)
        for h in range(IDX_HEADS):
            sc = lax.dot_general(iq_heads[h], ikt, (((1,), (1,)), ((), ())), preferred_element_type=F32)
            acc = acc + wcols[h] * jnp.maximum(sc, 0.0)
        kpos = off + lax.broadcasted_iota(jnp.int32, (tq, tk), 1)
        sc_ref[kt] = jnp.where(kpos < limit, acc, -jnp.inf)
        return carry

    lax.fori_loop(0, n_tiles, index_tile, 0)

    for j in range(nkt_max):
        @pl.when(n_tiles == j + 1)
        def _(j=j):
            _topk_mask(topk, j + 1, limit, sc_ref)

    q = q_ref[...]
    heads_per_kv = B_HEADS // B_KV_HEADS
    nr = heads_per_kv * tq
    qrows = [jnp.concatenate([head_slab(q, g * heads_per_kv + j) for j in range(heads_per_kv)], axis=0)
             for g in range(B_KV_HEADS)]
    k_refs = (ka_ref, kb_ref)

    def attend_tile(kt, carry):
        off = pl.multiple_of(kt * tk, tk)
        bias = sc_ref[kt]
        bias4 = jnp.concatenate([bias] * heads_per_kv, axis=0)
        out = []
        for g in range(B_KV_HEADS):
            m_run, l_run, acc = carry[g]
            logits = lax.dot_general(qrows[g], k_refs[g][pl.ds(off, tk), :], (((1,), (1,)), ((), ())),
                                     preferred_element_type=F32) + bias4
            m_new = jnp.maximum(m_run, jnp.max(logits, axis=1, keepdims=True))
            p = jnp.exp2(logits - m_new)
            alpha = jnp.exp2(m_run - m_new)
            l_new = alpha * l_run + jnp.sum(p, axis=1, keepdims=True)
            acc = alpha * acc + _dot(p.astype(BF16), vd_ref[pl.ds(off, tk), g * 128:(g + 1) * 128])
            out.append((m_new, l_new, acc))
        return tuple(out)

    init = tuple((jnp.full((nr, 1), NEG_BIG, F32), jnp.zeros((nr, 1), F32), jnp.zeros((nr, 128), F32))
                 for _ in range(B_KV_HEADS))
    fin = lax.fori_loop(0, n_tiles, attend_tile, init)
    for g in range(B_KV_HEADS):
        _, l_run, acc = fin[g]
        og = acc / l_run
        for pp in range(heads_per_kv // 2):
            even = og[(2 * pp) * tq:(2 * pp + 1) * tq]
            odd = og[(2 * pp + 1) * tq:(2 * pp + 2) * tq]
            col = (g * (heads_per_kv // 2) + pp) * 128
            o_ref[:, col:col + 128] = jnp.where(low_half, even, odd).astype(BF16)


def _topk_mask(topk, nt, limit, sc_ref):
    _, tq, tk = sc_ref.shape
    kf = float(topk)

    def tiles():
        return [sc_ref[t] for t in range(nt)]

    def row_sum(pred):
        tot = None
        for t, x in enumerate(tiles()):
            c = jnp.sum(pred(x, t).astype(F32), axis=1, keepdims=True)
            tot = c if tot is None else tot + c
        return tot

    def row_max(val):
        best = None
        for t, x in enumerate(tiles()):
            c = jnp.max(val(x, t), axis=1, keepdims=True)
            best = c if best is None else jnp.maximum(best, c)
        return best

    r_max = row_max(lambda x, t: x)
    r_min = -row_max(lambda x, t: jnp.where(x == -jnp.inf, -jnp.inf, -x))
    c_max = row_sum(lambda x, t: x >= r_max)
    small = limit <= topk

    def bisect(_, carry):
        lo, hi = carry
        mid = 0.5 * (lo + hi)
        ge = row_sum(lambda x, t: x >= mid) >= kf
        return jnp.where(ge, mid, lo), jnp.where(ge, hi, mid)

    lo, hi = lax.fori_loop(0, BISECT_STEPS, bisect, (r_min, r_max))
    top_tied = c_max >= kf
    done0 = jnp.where(small | top_tied, 1.0, 0.0)
    thr0 = jnp.where(top_tied, r_max, lo)

    def peel_cond(carry):
        return jnp.sum(1.0 - carry[0]) > 0.0

    def peel(carry):
        done, thr, hi_c = carry
        v1 = row_max(lambda x, t: jnp.where(x < hi_c, x, -jnp.inf))
        ok = row_sum(lambda x, t: x >= v1) >= kf
        act = done < 0.5
        thr = jnp.where(act & ok, v1, thr)
        hi_c = jnp.where(act & (~ok), v1, hi_c)
        return jnp.where(ok, 1.0, done), thr, hi_c

    _, thr, _ = lax.while_loop(peel_cond, peel, (done0, thr0, hi))

    def key_index(t):
        return (t * tk + lax.broadcasted_iota(jnp.int32, (tq, tk), 1)).astype(F32)

    need = kf - row_sum(lambda x, t: x > thr)
    n_eq = row_sum(lambda x, t: x == thr)
    contested = jnp.sum(jnp.where((n_eq > need) & (~small), 1.0, 0.0)) > 0.0
    last = float(nt * tk - 1)

    def tie_search(_, carry):
        jlo, jhi = carry
        mid = jnp.floor(0.5 * (jlo + jhi))
        ge = row_sum(lambda x, t: (x == thr) & (key_index(t) <= mid)) >= need
        return jnp.where(ge, jlo, mid), jnp.where(ge, mid, jhi)

    def tie_cut():
        n_iter = int(np.ceil(np.log2(nt * tk))) + 1
        return lax.fori_loop(0, n_iter, tie_search,
                             (jnp.full((tq, 1), -1.0, F32), jnp.full((tq, 1), last, F32)))[1]

    jcut = lax.cond(contested, tie_cut, lambda: jnp.full((tq, 1), last, F32))
    for t, x in enumerate(tiles()):
        sel = (x > -jnp.inf) & (small | (x > thr) | ((x == thr) & (key_index(t) <= jcut)))
        sc_ref[t] = jnp.where(sel, 0.0, NEG_BIG)


def _dsa(z3, zs3, positions, tq=256, tk=512):
    b, s, _ = z3.shape
    topk = min(TOPK_MAX, s // 4)
    qr, iqr, ka, kb, ik = _rope_qk(z3, positions)
    kernel = lambda *refs: _dsa_kernel(topk, tk, *refs)
    qblock = lambda w: pl.BlockSpec((None, tq, w), lambda bi, qi: (bi, qi, 0))
    keys = lambda w, cb=0: pl.BlockSpec((None, s, w), lambda bi, qi, _c=cb: (bi, 0, _c))
    return pl.pallas_call(
        kernel,
        grid=(b, s // tq),
        in_specs=[qblock(512), qblock(512), keys(128), keys(128), keys(256, O_VD // 256), keys(128),
                  qblock(128)],
        out_specs=pl.BlockSpec((None, tq, B_HEADS * B_HD), lambda bi, qi: (bi, qi, 0)),
        out_shape=jax.ShapeDtypeStruct((b, s, B_HEADS * B_HD), BF16),
        scratch_shapes=[pltpu.VMEM((s // tk, tq, tk), F32)],
        compiler_params=pltpu.CompilerParams(dimension_semantics=("parallel", "arbitrary"),
                                             vmem_limit_bytes=VMEM_LIMIT),
        name="dsa",
    )(qr, iqr, ka, kb, z3, ik, zs3)


def _merge_kernel(x_ref, oa_ref, ob_ref, ga_ref, gb_ref, bg_ref, wa_ref, wb_ref, wo_ref, n2_ref, wr_ref,
                  br_ref, x1_ref, h2_ref, rl_ref):
    pa = _dot(oa_ref[...], wa_ref[...])
    pb = _dot(ob_ref[...], wb_ref[...])
    bgv = bg_ref[...]
    ga = jax.nn.sigmoid(ga_ref[...].astype(F32) + bgv[:, :D_MODEL])
    gb = jax.nn.sigmoid(gb_ref[...].astype(F32) + bgv[:, D_MODEL:])
    merged = ga * pa + gb * pb
    x1 = x_ref[...] + _mm(merged, wo_ref[...])
    x1_ref[...] = x1
    h2 = _rmsnorm_rows(x1, n2_ref[...])
    hb, hl = _split(h2)
    h2_ref[...] = hb
    s1 = _dot(hb, wr_ref[...])
    s2 = _dot(hl, wr_ref[:, :128])
    rl_ref[...] = s1[:, :128] + s1[:, 128:] + s2 + br_ref[...]


def _merge(x2d, oa, ob, z2d, b_gate, wa, wb, wo, n2, wr, br, tm=512):
    n = x2d.shape[0]
    full = lambda shape: pl.BlockSpec(shape, lambda i: (0, 0))
    return pl.pallas_call(
        _merge_kernel,
        grid=(n // tm,),
        in_specs=[
            pl.BlockSpec((tm, D_MODEL), lambda i: (i, 0)),
            pl.BlockSpec((tm, 512), lambda i: (i, 0)),
            pl.BlockSpec((tm, 512), lambda i: (i, 0)),
            pl.BlockSpec((tm, D_MODEL), lambda i: (i, O_GA // D_MODEL)),
            pl.BlockSpec((tm, D_MODEL), lambda i: (i, O_GB // D_MODEL)),
            full((1, 2 * D_MODEL)), full((512, D_MODEL)), full((512, D_MODEL)), full((D_MODEL, D_MODEL)),
            full((1, D_MODEL)), full((D_MODEL, 256)), full((1, 128)),
        ],
        out_specs=[
            pl.BlockSpec((tm, D_MODEL), lambda i: (i, 0)),
            pl.BlockSpec((tm, D_MODEL), lambda i: (i, 0)),
            pl.BlockSpec((tm, 128), lambda i: (i, 0)),
        ],
        out_shape=[jax.ShapeDtypeStruct((n, D_MODEL), F32), jax.ShapeDtypeStruct((n, D_MODEL), BF16),
                   jax.ShapeDtypeStruct((n, 128), F32)],
        compiler_params=pltpu.CompilerParams(dimension_semantics=("parallel",),
                                             vmem_limit_bytes=VMEM_LIMIT),
        name="merge",
    )(x2d, oa, ob, z2d, z2d, b_gate, wa, wb, wo, n2, wr, br)


R_GROUP, R_EXPERT = 0, N_GROUPS


def _routing_weights(rl):
    lane = lax.broadcasted_iota(jnp.int32, rl.shape, 1)
    big = jnp.int32(1 << 20)
    gmask = lane < N_GROUPS
    gl = jnp.where(gmask, rl, -jnp.inf)
    gmax = jnp.max(gl, axis=1, keepdims=True)
    gsel = jnp.min(jnp.where(gmask & (rl == gmax), lane, big), axis=1, keepdims=True)
    ggate = 1.0 / jnp.sum(jnp.where(gmask, jnp.exp(gl - gmax), 0.0), axis=1, keepdims=True)
    e_lo = R_EXPERT + gsel * EXPERTS_PER_GROUP
    emask = (lane >= e_lo) & (lane < e_lo + EXPERTS_PER_GROUP)
    el = jnp.where(emask, rl, -jnp.inf)
    emax = jnp.max(el, axis=1, keepdims=True)
    ee = jnp.where(emask, jnp.exp(el - emax), 0.0)
    ep = jnp.where(emask, ee / jnp.sum(ee, axis=1, keepdims=True), -1.0)
    p1 = jnp.max(ep, axis=1, keepdims=True)
    i1 = jnp.min(jnp.where(ep == p1, lane, big), axis=1, keepdims=True)
    ep2 = jnp.where(lane == i1, -1.0, ep)
    p2 = jnp.max(ep2, axis=1, keepdims=True)
    i2 = jnp.min(jnp.where(ep2 == p2, lane, big), axis=1, keepdims=True)
    tot = p1 + p2
    return (jnp.where(lane == i1, ggate * (p1 / tot), 0.0)
            + jnp.where(lane == i2, ggate * (p2 / tot), 0.0))


def _moe_kernel(x1_ref, h2_ref, rl_ref, w1_ref, w3_ref, w2_ref, fg_ref, o_ref, y_ref, comb_ref):
    step = pl.program_id(1)
    per_step = w1_ref.shape[0]

    @pl.when(step == 0)
    def _():
        comb_ref[...] = _routing_weights(rl_ref[...])
        y_ref[...] = jnp.zeros(y_ref.shape, F32)

    h = h2_ref[...]
    comb = comb_ref[...]
    lane = lax.broadcasted_iota(jnp.int32, comb.shape, 1)
    acts = []
    for j in range(per_step):
        a = _dot(h, w1_ref[j])
        b = _dot(h, w3_ref[j])
        ce = jnp.sum(jnp.where(lane == R_EXPERT + step * per_step + j, comb, 0.0), axis=1, keepdims=True)
        acts.append(((a * jax.nn.sigmoid(a)) * b * ce).astype(BF16))
    y_ref[...] += _dot(jnp.concatenate(acts, axis=1), w2_ref[...])

    @pl.when(step == pl.num_programs(1) - 1)
    def _():
        o_ref[...] = _rmsnorm_rows(x1_ref[...] + y_ref[...], fg_ref[...])


def _moe(x1, h2, rl, w1, w3, w2, fg, tm=1024, per_step=4):
    n = x1.shape[0]
    w2g = w2.reshape(N_EXPERTS // per_step, per_step * D_EXPERT, D_MODEL)
    return pl.pallas_call(
        _moe_kernel,
        grid=(n // tm, N_EXPERTS // per_step),
        in_specs=[
            pl.BlockSpec((tm, D_MODEL), lambda i, e: (i, 0)),
            pl.BlockSpec((tm, D_MODEL), lambda i, e: (i, 0)),
            pl.BlockSpec((tm, 128), lambda i, e: (i, 0)),
            pl.BlockSpec((per_step, D_MODEL, D_EXPERT), lambda i, e: (e, 0, 0)),
            pl.BlockSpec((per_step, D_MODEL, D_EXPERT), lambda i, e: (e, 0, 0)),
            pl.BlockSpec((None, per_step * D_EXPERT, D_MODEL), lambda i, e: (e, 0, 0)),
            pl.BlockSpec((1, D_MODEL), lambda i, e: (0, 0)),
        ],
        out_specs=pl.BlockSpec((tm, D_MODEL), lambda i, e: (i, 0)),
        out_shape=jax.ShapeDtypeStruct((n, D_MODEL), F32),
        scratch_shapes=[pltpu.VMEM((tm, D_MODEL), F32), pltpu.VMEM((tm, 128), F32)],
        compiler_params=pltpu.CompilerParams(dimension_semantics=("parallel", "arbitrary"),
                                             vmem_limit_bytes=VMEM_LIMIT),
        name="moe",
    )(x1, h2, rl, w1, w3, w2g, fg)


_W_OFF = {}
_off = 0
for _name, _n in (("a_q", 512), ("a_k", 512), ("a_v", 512), ("a_z", 512), ("a_beta", 8), ("a_alpha", 8),
                  ("b_q", 512), ("b_k", 128), ("b_v", 128), ("i_q", 512), ("i_k", 64), ("i_w", 8),
                  ("gate_a", 1024), ("gate_b", 1024)):
    _W_OFF[_name] = (_off, _off + _n)
    _off += _n


def _cols(w, name, lo=0, hi=None):
    a, b = _W_OFF[name]
    return w[:, a + lo:(b if hi is None else a + hi)]


def _hi_lo_pair(w_small):
    pad = jnp.zeros((w_small.shape[0], 128 - w_small.shape[1]), F32)
    w = jnp.concatenate([w_small.astype(F32), pad], axis=1)
    hi, lo = _split(w)
    return jnp.concatenate([hi, lo], axis=1)


def _layout_w_in(w):
    wb = w.astype(BF16)
    k0, k1 = _cols(wb, "b_k", 0, 64), _cols(wb, "b_k", 64, 128)
    v0, v1 = _cols(wb, "b_v", 0, 64), _cols(wb, "b_v", 64, 128)
    ik = _cols(wb, "i_k")
    main = jnp.concatenate([
        _cols(wb, "a_q"), _cols(wb, "a_k"), _cols(wb, "a_v"), _cols(wb, "a_z"), _cols(wb, "gate_a"),
        _cols(wb, "gate_b"), _cols(wb, "b_q"), _cols(wb, "i_q"), k0, k0, k1, k1, v0, v0, v1, v1, ik, ik],
        axis=1)
    small = _hi_lo_pair(jnp.concatenate([_cols(w, "a_beta"), _cols(w, "a_alpha"), _cols(w, "i_w")], axis=1))
    return main, small


def kernel(x, positions, norm1_g, w_in, b_gate, conv_w, a_log, dt_bias, a_norm_g, w_proj_a, w_proj_b, w_out,
           norm2_g, w_router_group, b_router_group, w_router_expert, b_router_expert, w_exp_gate, w_exp_up,
           w_exp_down, final_norm_g):
    b, s, d = x.shape
    n = b * s
    depth = w_in.shape[0]
    xc = x.reshape(n, d).astype(F32)
    for l in range(depth):
        w_main, w_small = _layout_w_in(w_in[l])
        z, zs = _in_proj(xc, norm1_g[l][None, :].astype(F32), w_main, w_small)
        z3 = z.reshape(b, s, Z_W)
        zs3 = zs.reshape(b, s, 128)
        o_a = _gdn(z3, zs3, conv_w[l], a_log[l], dt_bias[l], a_norm_g[l])
        o_b = _dsa(z3, zs3, positions)
        wr = _hi_lo_pair(jnp.concatenate([w_router_group[l], w_router_expert[l]], axis=1))
        br = jnp.concatenate([b_router_group[l], b_router_expert[l],
                              jnp.zeros((128 - N_GROUPS - N_EXPERTS,), F32)])[None, :].astype(F32)
        x1, h2, rl = _merge(xc, o_a.reshape(n, -1), o_b.reshape(n, -1), z, b_gate[l][None, :].astype(F32),
                            w_proj_a[l].astype(BF16), w_proj_b[l].astype(BF16), w_out[l].astype(BF16),
                            norm2_g[l][None, :].astype(F32), wr, br)
        last = l == depth - 1
        fg = final_norm_g[None, :].astype(F32) if last else None
        if not last:
            raise NotImplementedError("only the final layer fuses the output norm")
        xc = _moe(x1, h2, rl, w_exp_gate[l].astype(BF16), w_exp_up[l].astype(BF16),
                  w_exp_down[l].astype(BF16), fg)
    return xc.reshape(b, s, d).astype(x.dtype)
```

```python
import numpy as np
import jax
import jax.numpy as jnp
from jax import lax
from jax.experimental import pallas as pl
from jax.experimental.pallas import tpu as pltpu

F32 = jnp.float32
BF16 = jnp.bfloat16

D_MODEL = 1024
CHUNK = 64
EPS = 1e-6
ROPE_THETA = 10000.0
A_HEADS = 8
A_DK = 64
A_DV = 64
CONV_K = 4
B_HEADS = 8
B_KV_HEADS = 2
B_HD = 64
IDX_HEADS = 8
IDX_HD = 64
TOPK_MAX = 256
N_GROUPS = 4
EXPERTS_PER_GROUP = 4
N_EXPERTS = 16
D_EXPERT = 256

O_AQKV, O_AZ, O_GA, O_GB, O_BQ, O_IQ, O_KD, O_VD, O_IK = 0, 1536, 2048, 3072, 4096, 4608, 5120, 5376, 5632
Z_W = 5760
S_BETA, S_ALPHA, S_IW = 0, 8, 16

GROUP_HEADS = 2
BD = GROUP_HEADS * CHUNK
NEG_BIG = -1e30
LOG2E = 1.4426950408889634
BISECT_STEPS = 18
VMEM_LIMIT = 56 * 1024 * 1024


def _split(x):
    hi = x.astype(BF16)
    lo = (x - hi.astype(F32)).astype(BF16)
    return hi, lo


def _dot(a, b):
    return jnp.dot(a, b, preferred_element_type=F32)


def _mm(a, b):
    return _dot(a.astype(BF16), b.astype(BF16))


def _mm_nt(a, b):
    return lax.dot_general(a.astype(BF16), b.astype(BF16), (((1,), (1,)), ((), ())),
                           preferred_element_type=F32)


def _mm_exact_lhs(a_bf16, x):
    xh, xl = _split(x)
    return _dot(a_bf16, xh) + _dot(a_bf16, xl)


def _rmsnorm_rows(x, g):
    return x * lax.rsqrt(jnp.mean(x * x, axis=-1, keepdims=True) + EPS) * g


_Z_CHUNKS = tuple((o, min(512, Z_W - o)) for o in range(0, Z_W, 512))


def _in_proj_kernel(x_ref, g_ref, w_ref, ws_ref, z_ref, zs_ref):
    h = _rmsnorm_rows(x_ref[...], g_ref[...])
    hb, hl = _split(h)
    for o, w in _Z_CHUNKS:
        z_ref[:, o:o + w] = _dot(hb, w_ref[:, o:o + w]).astype(BF16)
    s1 = _dot(hb, ws_ref[...])
    s2 = _dot(hl, ws_ref[:, :128])
    zs_ref[...] = s1[:, :128] + s1[:, 128:] + s2


def _in_proj(x2d, g, w_main, w_small, tm=512):
    n = x2d.shape[0]
    return pl.pallas_call(
        _in_proj_kernel,
        grid=(n // tm,),
        in_specs=[
            pl.BlockSpec((tm, D_MODEL), lambda i: (i, 0)),
            pl.BlockSpec((1, D_MODEL), lambda i: (0, 0)),
            pl.BlockSpec((D_MODEL, Z_W), lambda i: (0, 0)),
            pl.BlockSpec((D_MODEL, 256), lambda i: (0, 0)),
        ],
        out_specs=[
            pl.BlockSpec((tm, Z_W), lambda i: (i, 0)),
            pl.BlockSpec((tm, 128), lambda i: (i, 0)),
        ],
        out_shape=[jax.ShapeDtypeStruct((n, Z_W), BF16), jax.ShapeDtypeStruct((n, 128), F32)],
        compiler_params=pltpu.CompilerParams(dimension_semantics=("parallel",),
                                             vmem_limit_bytes=VMEM_LIMIT),
        name="in_proj",
    )(x2d, g, w_main, w_small)


def _gdn_constants():
    r = np.arange(BD)
    same = (r[:, None] // CHUNK) == (r[None, :] // CHUNK)
    incl = same & (r[:, None] >= r[None, :])
    strict = same & (r[:, None] > r[None, :])
    eye = np.eye(BD, dtype=np.float32)
    ll = np.concatenate([incl, same], axis=0).astype(np.float32)
    n_groups = A_HEADS // GROUP_HEADS
    sel = np.zeros((2 * n_groups, BD, 128), np.float32)
    for gi in range(n_groups):
        for h in range(GROUP_HEADS):
            sel[gi * 2 + 0, h * CHUNK:(h + 1) * CHUNK, S_BETA + gi * GROUP_HEADS + h] = 1.0
            sel[gi * 2 + 1, h * CHUNK:(h + 1) * CHUNK, S_ALPHA + gi * GROUP_HEADS + h] = 1.0
    return (jnp.asarray(incl, F32), jnp.asarray(strict, F32), jnp.asarray(same, F32), jnp.asarray(eye),
            jnp.asarray(ll, BF16), jnp.asarray(sel))


def _tile_heads(x):
    return jnp.concatenate([x] * GROUP_HEADS, axis=0)


def _gdn_kernel(zq_ref, zz_ref, zs_ref, cw_ref, av_ref, ag_ref, incl_ref, strict_ref, bdm_ref, eye_ref,
                ll_ref, sel_ref, o_ref, ext_ref, st_ref):
    c = pl.program_id(1)
    t = zq_ref.shape[0]

    @pl.when(c == 0)
    def _():
        ext_ref[0:8, :] = jnp.zeros((8, ext_ref.shape[1]), F32)
        st_ref[...] = jnp.zeros(st_ref.shape, F32)

    ext_ref[8:8 + t, :] = zq_ref[...].astype(F32)
    cw = cw_ref[...]
    y = cw[0:1, :] * ext_ref[pl.ds(8 - (CONV_K - 1), t), :]
    for j in range(1, CONV_K):
        y = y + cw[j:j + 1, :] * ext_ref[pl.ds(8 - (CONV_K - 1) + j, t), :]
    ext_ref[0:8, :] = ext_ref[t:t + 8, :]
    y = y * jax.nn.sigmoid(y)

    hw = A_HEADS * A_DK
    q_all = y[:, :hw]
    k_all = y[:, hw:2 * hw]
    v_all = y[:, 2 * hw:]

    sm = zs_ref[...]
    av = av_ref[...]
    lane = lax.broadcasted_iota(jnp.int32, sm.shape, 1)
    xg = sm + av[1:2, :]
    softplus = jnp.maximum(xg, 0.0) + jnp.log1p(jnp.exp(-jnp.abs(xg)))
    g_all = -jnp.exp(av[0:1, :]) * softplus
    bg = jnp.where(lane < S_ALPHA, jax.nn.sigmoid(sm), g_all)

    incl = incl_ref[...]
    strict = strict_ref[...]
    bdm = bdm_ref[...]
    eye = eye_ref[...]
    ll = ll_ref[...]
    incl_b = ll[:BD]

    n_chunks = t // CHUNK
    n_groups = A_HEADS // GROUP_HEADS
    chains = [(ci, gi) for ci in range(n_chunks) for gi in range(n_groups)]

    pre = {}
    for ci, gi in chains:
        r0, c0 = ci * CHUNK, gi * BD
        bg4 = _tile_heads(bg[r0:r0 + CHUNK])
        beta = jnp.sum(bg4 * sel_ref[gi * 2 + 0], axis=1, keepdims=True)
        gcol = jnp.sum(bg4 * sel_ref[gi * 2 + 1], axis=1, keepdims=True)
        gs = _mm_exact_lhs(ll, jnp.broadcast_to(gcol, (BD, 128)))
        g_cum = gs[:BD, :1]
        g_last = gs[BD:, :1]
        diff = _mm_exact_lhs(incl_b, gcol * strict)
        decay = jnp.where(incl > 0.0, jnp.exp(diff), 0.0)
        e_cum = jnp.exp(g_cum)
        kr = _tile_heads(k_all[r0:r0 + CHUNK, c0:c0 + BD]) * bdm
        qr = _tile_heads(q_all[r0:r0 + CHUNK, c0:c0 + BD]) * bdm
        vm = _tile_heads(v_all[r0:r0 + CHUNK, c0:c0 + BD]) * bdm
        km = kr * lax.rsqrt(jnp.sum(kr * kr, axis=1, keepdims=True) + EPS)
        qm = qr * (lax.rsqrt(jnp.sum(qr * qr, axis=1, keepdims=True) + EPS) * (A_DK ** -0.5))
        kkqk = _mm_nt(jnp.concatenate([km, qm], axis=0), km)
        m = -(strict * beta * kkqk[:BD] * decay)
        pre[ci, gi] = dict(beta=beta, g_cum=g_cum, g_last=g_last, e_cum=e_cum, km=km, qm=qm, vm=vm,
                           qk=kkqk[BD:] * decay, m=m, inv=eye + m)

    sq = CHUNK
    while sq > 2:
        for key in chains:
            p = pre[key]
            p["m"] = _mm(p["m"], p["m"])
        for key in chains:
            p = pre[key]
            p["inv"] = p["inv"] + _mm(p["inv"], p["m"])
        sq //= 2

    outs = []
    for ci in range(n_chunks):
        o_groups = []
        for gi in range(n_groups):
            p = pre[ci, gi]
            state = st_ref[gi]
            kq_s = _mm(jnp.concatenate([p["km"] * p["e_cum"], p["qm"] * p["e_cum"]], axis=0), state)
            v_new = _mm(p["inv"], p["beta"] * (p["vm"] - kq_s[:BD]))
            o_bd = kq_s[BD:] + _mm(p["qk"], v_new)
            k_dec = p["km"] * jnp.exp(p["g_last"] - p["g_cum"])
            st_ref[gi] = state * jnp.exp(p["g_last"]) + _mm(k_dec.T, v_new)
            o_bd = o_bd * lax.rsqrt(jnp.sum(o_bd * o_bd, axis=1, keepdims=True) * (1.0 / A_DV) + EPS)
            o_groups.append(sum(o_bd[h * CHUNK:(h + 1) * CHUNK] for h in range(GROUP_HEADS)))
        outs.append(jnp.concatenate(o_groups, axis=1))
    o = outs[0] if len(outs) == 1 else jnp.concatenate(outs, axis=0)

    zz = zz_ref[...].astype(F32)
    o_ref[...] = (o * ag_ref[...] * (zz * jax.nn.sigmoid(zz))).astype(BF16)


def _gdn(z3, zs3, conv_w, a_log, dt_bias, a_norm_g, t=4 * CHUNK):
    b, s, _ = z3.shape
    consts = _gdn_constants()
    av = jnp.zeros((2, 128), F32)
    av = av.at[0, S_ALPHA:S_ALPHA + A_HEADS].set(a_log.astype(F32))
    av = av.at[1, S_ALPHA:S_ALPHA + A_HEADS].set(dt_bias.astype(F32))
    ag = jnp.tile(a_norm_g.astype(F32), A_HEADS)[None, :]
    conv_cols = 2 * A_HEADS * A_DK + A_HEADS * A_DV

    def const_spec(a):
        nd = a.ndim
        return pl.BlockSpec(a.shape, lambda bi, ci, _n=nd: (0,) * _n)

    small_in = (conv_w.astype(F32), av, ag) + consts
    return pl.pallas_call(
        _gdn_kernel,
        grid=(b, s // t),
        in_specs=[
            pl.BlockSpec((None, t, conv_cols), lambda bi, ci: (bi, ci, O_AQKV // conv_cols)),
            pl.BlockSpec((None, t, 512), lambda bi, ci: (bi, ci, O_AZ // 512)),
            pl.BlockSpec((None, t, 128), lambda bi, ci: (bi, ci, 0)),
        ] + [const_spec(a) for a in small_in],
        out_specs=pl.BlockSpec((None, t, A_HEADS * A_DV), lambda bi, ci: (bi, ci, 0)),
        out_shape=jax.ShapeDtypeStruct((b, s, A_HEADS * A_DV), BF16),
        scratch_shapes=[pltpu.VMEM((8 + t, conv_cols), F32),
                        pltpu.VMEM((A_HEADS // GROUP_HEADS, BD, BD), F32)],
        compiler_params=pltpu.CompilerParams(dimension_semantics=("parallel", "arbitrary"),
                                             vmem_limit_bytes=VMEM_LIMIT),
        name="gdn",
    )(z3, z3, zs3, *small_in)


def _rope(x, cs, sn, first):
    w = x.shape[1]
    swapped = jnp.where(first, pltpu.roll(x, w - B_HD // 2, 1), pltpu.roll(x, B_HD // 2, 1))
    return x * cs + swapped * sn


def _rope_kernel(pos_ref, inv_ref, bq_ref, iq_ref, kd_ref, ikd_ref, q_out, iq_out, ka_out, kb_out, ik_out):
    ang = pos_ref[...] * inv_ref[...]
    lane = lax.broadcasted_iota(jnp.int32, ang.shape, 1)
    first = (lane & (B_HD - 1)) < (B_HD // 2)
    cs = jnp.cos(ang)
    sn = jnp.sin(ang)
    sn = jnp.where(first, -sn, sn)
    kd = kd_ref[...].astype(F32)
    ka_out[...] = _rope(kd[:, :128], cs, sn, first).astype(BF16)
    kb_out[...] = _rope(kd[:, 128:], cs, sn, first).astype(BF16)
    ik_out[...] = _rope(ikd_ref[...].astype(F32), cs, sn, first).astype(BF16)
    cs4 = jnp.concatenate([cs] * 4, axis=1)
    sn4 = jnp.concatenate([sn] * 4, axis=1)
    first4 = jnp.concatenate([first] * 4, axis=1)
    q_out[...] = (_rope(bq_ref[...].astype(F32), cs4, sn4, first4) * (B_HD ** -0.5 * LOG2E)).astype(BF16)
    iq_out[...] = _rope(iq_ref[...].astype(F32), cs4, sn4, first4).astype(BF16)


def _rope_qk(z3, positions, tr=512):
    b, s, _ = z3.shape
    half = B_HD // 2
    inv = ROPE_THETA ** (-jnp.arange(half, dtype=F32) / half)
    inv128 = jnp.tile(inv, 4)[None, :]
    pos = positions.astype(F32)[:, :, None]
    row = lambda w, cb: pl.BlockSpec((None, tr, w), lambda bi, ri, _c=cb: (bi, ri, _c))
    shp = lambda w: jax.ShapeDtypeStruct((b, s, w), BF16)
    return pl.pallas_call(
        _rope_kernel,
        grid=(b, s // tr),
        in_specs=[row(1, 0), pl.BlockSpec((1, 128), lambda bi, ri: (0, 0)),
                  row(512, O_BQ // 512), row(512, O_IQ // 512), row(256, O_KD // 256), row(128, O_IK // 128)],
        out_specs=[row(512, 0), row(512, 0), row(128, 0), row(128, 0), row(128, 0)],
        out_shape=[shp(512), shp(512), shp(128), shp(128), shp(128)],
        compiler_params=pltpu.CompilerParams(dimension_semantics=("parallel", "parallel"),
                                             vmem_limit_bytes=VMEM_LIMIT),
        name="rope",
    )(pos, inv128, z3, z3, z3, z3)


def _dsa_kernel(topk, tk, q_ref, iq_ref, ka_ref, kb_ref, vd_ref, ik_ref, zs_ref, o_ref, sc_ref):
    qb = pl.program_id(1)
    s = ka_ref.shape[0]
    tq = q_ref.shape[0]
    nkt_max = s // tk
    r0 = qb * tq
    n_tiles = (r0 + tq + tk - 1) // tk

    lane128 = lax.broadcasted_iota(jnp.int32, (tq, 128), 1)
    low_half = lane128 < B_HD
    high_half = lane128 >= B_HD

    def head_slab(x, h):
        slab = x[:, (h // 2) * 128:(h // 2 + 1) * 128]
        return jnp.where(low_half if h % 2 == 0 else high_half, slab, jnp.zeros_like(slab))

    sm = zs_ref[...]
    iw_scale = (IDX_HEADS ** -0.5) * (IDX_HD ** -0.5)
    wcols = [jnp.sum(jnp.where(lane128 == S_IW + h, sm, 0.0), axis=1, keepdims=True) * iw_scale
             for h in range(IDX_HEADS)]
    row = r0 + lax.broadcasted_iota(jnp.int32, (tq, 1), 0)
    limit = ((row >> 6) + 1) << 6

    iq = iq_ref[...]
    iq_heads = [head_slab(iq, h) for h in range(IDX_HEADS)]

    def index_tile(kt, carry):
        off = pl.multiple_of(kt * tk, tk)
        ikt = ik_ref[pl.ds(off, tk), :]
        acc = jnp.zeros((tq, tk), F32)
        for h in range(IDX_HEADS):
            sc = lax.dot_general(iq_heads[h], ikt, (((1,), (1,)), ((), ())), preferred_element_type=F32)
            acc = acc + wcols[h] * jnp.maximum(sc, 0.0)
        kpos = off + lax.broadcasted_iota(jnp.int32, (tq, tk), 1)
        sc_ref[kt] = jnp.where(kpos < limit, acc, -jnp.inf)
        return carry

    lax.fori_loop(0, n_tiles, index_tile, 0)

    all_kept = r0 + tq <= min(topk, tk)

    @pl.when(all_kept)
    def _():
        sc_ref[0] = jnp.where(sc_ref[0] > -jnp.inf, 0.0, NEG_BIG)

    for j in range(nkt_max):
        @pl.when(jnp.logical_and(n_tiles == j + 1, jnp.logical_not(all_kept)))
        def _(j=j):
            _topk_mask(topk, j + 1, limit, sc_ref)

    q = q_ref[...]
    heads_per_kv = B_HEADS // B_KV_HEADS
    nr = heads_per_kv * tq
    qrows = [jnp.concatenate([head_slab(q, g * heads_per_kv + j) for j in range(heads_per_kv)], axis=0)
             for g in range(B_KV_HEADS)]
    k_refs = (ka_ref, kb_ref)

    def attend_tile(kt, carry):
        off = pl.multiple_of(kt * tk, tk)
        bias = sc_ref[kt]
        bias4 = jnp.concatenate([bias] * heads_per_kv, axis=0)
        out = []
        for g in range(B_KV_HEADS):
            m_run, l_run, acc = carry[g]
            logits = lax.dot_general(qrows[g], k_refs[g][pl.ds(off, tk), :], (((1,), (1,)), ((), ())),
                                     preferred_element_type=F32) + bias4
            m_new = jnp.maximum(m_run, jnp.max(logits, axis=1, keepdims=True))
            p = jnp.exp2(logits - m_new)
            alpha = jnp.exp2(m_run - m_new)
            l_new = alpha * l_run + jnp.sum(p, axis=1, keepdims=True)
            acc = alpha * acc + _dot(p.astype(BF16), vd_ref[pl.ds(off, tk), g * 128:(g + 1) * 128])
            out.append((m_new, l_new, acc))
        return tuple(out)

    init = tuple((jnp.full((nr, 1), NEG_BIG, F32), jnp.zeros((nr, 1), F32), jnp.zeros((nr, 128), F32))
                 for _ in range(B_KV_HEADS))
    fin = lax.fori_loop(0, n_tiles, attend_tile, init)
    for g in range(B_KV_HEADS):
        _, l_run, acc = fin[g]
        og = acc / l_run
        for pp in range(heads_per_kv // 2):
            even = og[(2 * pp) * tq:(2 * pp + 1) * tq]
            odd = og[(2 * pp + 1) * tq:(2 * pp + 2) * tq]
            col = (g * (heads_per_kv // 2) + pp) * 128
            o_ref[:, col:col + 128] = jnp.where(low_half, even, odd).astype(BF16)


def _topk_mask(topk, nt, limit, sc_ref):
    _, tq, tk = sc_ref.shape
    kf = float(topk)

    def tiles():
        return [sc_ref[t] for t in range(nt)]

    def row_sum(pred):
        tot = None
        for t, x in enumerate(tiles()):
            c = jnp.sum(pred(x, t).astype(F32), axis=1, keepdims=True)
            tot = c if tot is None else tot + c
        return tot

    def row_max(val):
        best = None
        for t, x in enumerate(tiles()):
            c = jnp.max(val(x, t), axis=1, keepdims=True)
            best = c if best is None else jnp.maximum(best, c)
        return best

    r_max = row_max(lambda x, t: x)
    r_min = -row_max(lambda x, t: jnp.where(x == -jnp.inf, -jnp.inf, -x))
    c_max = row_sum(lambda x, t: x >= r_max)
    small = limit <= topk

    def bisect(_, carry):
        lo, hi = carry
        mid = 0.5 * (lo + hi)
        ge = row_sum(lambda x, t: x >= mid) >= kf
        return jnp.where(ge, mid, lo), jnp.where(ge, hi, mid)

    lo, hi = lax.fori_loop(0, BISECT_STEPS, bisect, (r_min, r_max))
    top_tied = c_max >= kf
    done0 = jnp.where(small | top_tied, 1.0, 0.0)
    thr0 = jnp.where(top_tied, r_max, lo)

    def peel_cond(carry):
        return jnp.sum(1.0 - carry[0]) > 0.0

    def peel(carry):
        done, thr, hi_c = carry
        v1 = row_max(lambda x, t: jnp.where(x < hi_c, x, -jnp.inf))
        ok = row_sum(lambda x, t: x >= v1) >= kf
        act = done < 0.5
        thr = jnp.where(act & ok, v1, thr)
        hi_c = jnp.where(act & (~ok), v1, hi_c)
        return jnp.where(ok, 1.0, done), thr, hi_c

    _, thr, _ = lax.while_loop(peel_cond, peel, (done0, thr0, hi))

    def key_index(t):
        return (t * tk + lax.broadcasted_iota(jnp.int32, (tq, tk), 1)).astype(F32)

    need = kf - row_sum(lambda x, t: x > thr)
    n_eq = row_sum(lambda x, t: x == thr)
    contested = jnp.sum(jnp.where((n_eq > need) & (~small), 1.0, 0.0)) > 0.0
    last = float(nt * tk - 1)

    def tie_search(_, carry):
        jlo, jhi = carry
        mid = jnp.floor(0.5 * (jlo + jhi))
        ge = row_sum(lambda x, t: (x == thr) & (key_index(t) <= mid)) >= need
        return jnp.where(ge, jlo, mid), jnp.where(ge, mid, jhi)

    def tie_cut():
        n_iter = int(np.ceil(np.log2(nt * tk))) + 1
        return lax.fori_loop(0, n_iter, tie_search,
                             (jnp.full((tq, 1), -1.0, F32), jnp.full((tq, 1), last, F32)))[1]

    jcut = lax.cond(contested, tie_cut, lambda: jnp.full((tq, 1), last, F32))
    for t, x in enumerate(tiles()):
        sel = (x > -jnp.inf) & (small | (x > thr) | ((x == thr) & (key_index(t) <= jcut)))
        sc_ref[t] = jnp.where(sel, 0.0, NEG_BIG)


def _dsa(z3, zs3, positions, tq=256, tk=512):
    b, s, _ = z3.shape
    topk = min(TOPK_MAX, s // 4)
    qr, iqr, ka, kb, ik = _rope_qk(z3, positions)
    kernel = lambda *refs: _dsa_kernel(topk, tk, *refs)
    qblock = lambda w: pl.BlockSpec((None, tq, w), lambda bi, qi: (bi, qi, 0))
    keys = lambda w, cb=0: pl.BlockSpec((None, s, w), lambda bi, qi, _c=cb: (bi, 0, _c))
    return pl.pallas_call(
        kernel,
        grid=(b, s // tq),
        in_specs=[qblock(512), qblock(512), keys(128), keys(128), keys(256, O_VD // 256), keys(128),
                  qblock(128)],
        out_specs=pl.BlockSpec((None, tq, B_HEADS * B_HD), lambda bi, qi: (bi, qi, 0)),
        out_shape=jax.ShapeDtypeStruct((b, s, B_HEADS * B_HD), BF16),
        scratch_shapes=[pltpu.VMEM((s // tk, tq, tk), F32)],
        compiler_params=pltpu.CompilerParams(dimension_semantics=("parallel", "arbitrary"),
                                             vmem_limit_bytes=VMEM_LIMIT),
        name="dsa",
    )(qr, iqr, ka, kb, z3, ik, zs3)


def _merge_kernel(x_ref, oa_ref, ob_ref, ga_ref, gb_ref, bg_ref, wa_ref, wb_ref, wo_ref, n2_ref, wr_ref,
                  br_ref, x1_ref, h2_ref, rl_ref):
    pa = _dot(oa_ref[...], wa_ref[...])
    pb = _dot(ob_ref[...], wb_ref[...])
    bgv = bg_ref[...]
    ga = jax.nn.sigmoid(ga_ref[...].astype(F32) + bgv[:, :D_MODEL])
    gb = jax.nn.sigmoid(gb_ref[...].astype(F32) + bgv[:, D_MODEL:])
    merged = ga * pa + gb * pb
    x1 = x_ref[...] + _mm(merged, wo_ref[...])
    x1_ref[...] = x1
    h2 = _rmsnorm_rows(x1, n2_ref[...])
    hb, hl = _split(h2)
    h2_ref[...] = hb
    s1 = _dot(hb, wr_ref[...])
    s2 = _dot(hl, wr_ref[:, :128])
    rl_ref[...] = s1[:, :128] + s1[:, 128:] + s2 + br_ref[...]


def _merge(x2d, oa, ob, z2d, b_gate, wa, wb, wo, n2, wr, br, tm=512):
    n = x2d.shape[0]
    full = lambda shape: pl.BlockSpec(shape, lambda i: (0, 0))
    return pl.pallas_call(
        _merge_kernel,
        grid=(n // tm,),
        in_specs=[
            pl.BlockSpec((tm, D_MODEL), lambda i: (i, 0)),
            pl.BlockSpec((tm, 512), lambda i: (i, 0)),
            pl.BlockSpec((tm, 512), lambda i: (i, 0)),
            pl.BlockSpec((tm, D_MODEL), lambda i: (i, O_GA // D_MODEL)),
            pl.BlockSpec((tm, D_MODEL), lambda i: (i, O_GB // D_MODEL)),
            full((1, 2 * D_MODEL)), full((512, D_MODEL)), full((512, D_MODEL)), full((D_MODEL, D_MODEL)),
            full((1, D_MODEL)), full((D_MODEL, 256)), full((1, 128)),
        ],
        out_specs=[
            pl.BlockSpec((tm, D_MODEL), lambda i: (i, 0)),
            pl.BlockSpec((tm, D_MODEL), lambda i: (i, 0)),
            pl.BlockSpec((tm, 128), lambda i: (i, 0)),
        ],
        out_shape=[jax.ShapeDtypeStruct((n, D_MODEL), F32), jax.ShapeDtypeStruct((n, D_MODEL), BF16),
                   jax.ShapeDtypeStruct((n, 128), F32)],
        compiler_params=pltpu.CompilerParams(dimension_semantics=("parallel",),
                                             vmem_limit_bytes=VMEM_LIMIT),
        name="merge",
    )(x2d, oa, ob, z2d, z2d, b_gate, wa, wb, wo, n2, wr, br)


R_GROUP, R_EXPERT = 0, N_GROUPS


def _routing_weights(rl):
    lane = lax.broadcasted_iota(jnp.int32, rl.shape, 1)
    big = jnp.int32(1 << 20)
    gmask = lane < N_GROUPS
    gl = jnp.where(gmask, rl, -jnp.inf)
    gmax = jnp.max(gl, axis=1, keepdims=True)
    gsel = jnp.min(jnp.where(gmask & (rl == gmax), lane, big), axis=1, keepdims=True)
    ggate = 1.0 / jnp.sum(jnp.where(gmask, jnp.exp(gl - gmax), 0.0), axis=1, keepdims=True)
    e_lo = R_EXPERT + gsel * EXPERTS_PER_GROUP
    emask = (lane >= e_lo) & (lane < e_lo + EXPERTS_PER_GROUP)
    el = jnp.where(emask, rl, -jnp.inf)
    emax = jnp.max(el, axis=1, keepdims=True)
    ee = jnp.where(emask, jnp.exp(el - emax), 0.0)
    ep = jnp.where(emask, ee / jnp.sum(ee, axis=1, keepdims=True), -1.0)
    p1 = jnp.max(ep, axis=1, keepdims=True)
    i1 = jnp.min(jnp.where(ep == p1, lane, big), axis=1, keepdims=True)
    ep2 = jnp.where(lane == i1, -1.0, ep)
    p2 = jnp.max(ep2, axis=1, keepdims=True)
    i2 = jnp.min(jnp.where(ep2 == p2, lane, big), axis=1, keepdims=True)
    tot = p1 + p2
    return (jnp.where(lane == i1, ggate * (p1 / tot), 0.0)
            + jnp.where(lane == i2, ggate * (p2 / tot), 0.0))


def _moe_kernel(x1_ref, h2_ref, rl_ref, w1_ref, w3_ref, w2_ref, fg_ref, o_ref, y_ref, comb_ref):
    step = pl.program_id(1)
    per_step = w1_ref.shape[0]

    @pl.when(step == 0)
    def _():
        comb_ref[...] = _routing_weights(rl_ref[...])
        y_ref[...] = jnp.zeros(y_ref.shape, F32)

    h = h2_ref[...]
    comb = comb_ref[...]
    lane = lax.broadcasted_iota(jnp.int32, comb.shape, 1)
    acts = []
    for j in range(per_step):
        a = _dot(h, w1_ref[j])
        b = _dot(h, w3_ref[j])
        ce = jnp.sum(jnp.where(lane == R_EXPERT + step * per_step + j, comb, 0.0), axis=1, keepdims=True)
        acts.append(((a * jax.nn.sigmoid(a)) * b * ce).astype(BF16))
    y_ref[...] += _dot(jnp.concatenate(acts, axis=1), w2_ref[...])

    @pl.when(step == pl.num_programs(1) - 1)
    def _():
        o_ref[...] = _rmsnorm_rows(x1_ref[...] + y_ref[...], fg_ref[...])


def _moe(x1, h2, rl, w1, w3, w2, fg, tm=1024, per_step=4):
    n = x1.shape[0]
    w2g = w2.reshape(N_EXPERTS // per_step, per_step * D_EXPERT, D_MODEL)
    return pl.pallas_call(
        _moe_kernel,
        grid=(n // tm, N_EXPERTS // per_step),
        in_specs=[
            pl.BlockSpec((tm, D_MODEL), lambda i, e: (i, 0)),
            pl.BlockSpec((tm, D_MODEL), lambda i, e: (i, 0)),
            pl.BlockSpec((tm, 128), lambda i, e: (i, 0)),
            pl.BlockSpec((per_step, D_MODEL, D_EXPERT), lambda i, e: (e, 0, 0)),
            pl.BlockSpec((per_step, D_MODEL, D_EXPERT), lambda i, e: (e, 0, 0)),
            pl.BlockSpec((None, per_step * D_EXPERT, D_MODEL), lambda i, e: (e, 0, 0)),
            pl.BlockSpec((1, D_MODEL), lambda i, e: (0, 0)),
        ],
        out_specs=pl.BlockSpec((tm, D_MODEL), lambda i, e: (i, 0)),
        out_shape=jax.ShapeDtypeStruct((n, D_MODEL), F32),
        scratch_shapes=[pltpu.VMEM((tm, D_MODEL), F32), pltpu.VMEM((tm, 128), F32)],
        compiler_params=pltpu.CompilerParams(dimension_semantics=("parallel", "arbitrary"),
                                             vmem_limit_bytes=VMEM_LIMIT),
        name="moe",
    )(x1, h2, rl, w1, w3, w2g, fg)


_W_OFF = {}
_off = 0
for _name, _n in (("a_q", 512), ("a_k", 512), ("a_v", 512), ("a_z", 512), ("a_beta", 8), ("a_alpha", 8),
                  ("b_q", 512), ("b_k", 128), ("b_v", 128), ("i_q", 512), ("i_k", 64), ("i_w", 8),
                  ("gate_a", 1024), ("gate_b", 1024)):
    _W_OFF[_name] = (_off, _off + _n)
    _off += _n


def _cols(w, name, lo=0, hi=None):
    a, b = _W_OFF[name]
    return w[:, a + lo:(b if hi is None else a + hi)]


def _hi_lo_pair(w_small):
    pad = jnp.zeros((w_small.shape[0], 128 - w_small.shape[1]), F32)
    w = jnp.concatenate([w_small.astype(F32), pad], axis=1)
    hi, lo = _split(w)
    return jnp.concatenate([hi, lo], axis=1)


def _layout_w_in(w):
    wb = w.astype(BF16)
    k0, k1 = _cols(wb, "b_k", 0, 64), _cols(wb, "b_k", 64, 128)
    v0, v1 = _cols(wb, "b_v", 0, 64), _cols(wb, "b_v", 64, 128)
    ik = _cols(wb, "i_k")
    main = jnp.concatenate([
        _cols(wb, "a_q"), _cols(wb, "a_k"), _cols(wb, "a_v"), _cols(wb, "a_z"), _cols(wb, "gate_a"),
        _cols(wb, "gate_b"), _cols(wb, "b_q"), _cols(wb, "i_q"), k0, k0, k1, k1, v0, v0, v1, v1, ik, ik],
        axis=1)
    small = _hi_lo_pair(jnp.concatenate([_cols(w, "a_beta"), _cols(w, "a_alpha"), _cols(w, "i_w")], axis=1))
    return main, small


def kernel(x, positions, norm1_g, w_in, b_gate, conv_w, a_log, dt_bias, a_norm_g, w_proj_a, w_proj_b, w_out,
           norm2_g, w_router_group, b_router_group, w_router_expert, b_router_expert, w_exp_gate, w_exp_up,
           w_exp_down, final_norm_g):
    b, s, d = x.shape
    n = b * s
    depth = w_in.shape[0]
    xc = x.reshape(n, d).astype(F32)
    for l in range(depth):
        w_main, w_small = _layout_w_in(w_in[l])
        z, zs = _in_proj(xc, norm1_g[l][None, :].astype(F32), w_main, w_small)
        z3 = z.reshape(b, s, Z_W)
        zs3 = zs.reshape(b, s, 128)
        o_a = _gdn(z3, zs3, conv_w[l], a_log[l], dt_bias[l], a_norm_g[l])
        o_b = _dsa(z3, zs3, positions)
        wr = _hi_lo_pair(jnp.concatenate([w_router_group[l], w_router_expert[l]], axis=1))
        br = jnp.concatenate([b_router_group[l], b_router_expert[l],
                              jnp.zeros((128 - N_GROUPS - N_EXPERTS,), F32)])[None, :].astype(F32)
        x1, h2, rl = _merge(xc, o_a.reshape(n, -1), o_b.reshape(n, -1), z, b_gate[l][None, :].astype(F32),
                            w_proj_a[l].astype(BF16), w_proj_b[l].astype(BF16), w_out[l].astype(BF16),
                            norm2_g[l][None, :].astype(F32), wr, br)
        last = l == depth - 1
        fg = final_norm_g[None, :].astype(F32) if last else None
        if not last:
            raise NotImplementedError("only the final layer fuses the output norm")
        xc = _moe(x1, h2, rl, w_exp_gate[l].astype(BF16), w_exp_up[l].astype(BF16),
                  w_exp_down[l].astype(BF16), fg)
    return xc.reshape(b, s, d).astype(x.dtype)
```

```python
import numpy as np
import jax
import jax.numpy as jnp
from jax import lax
from jax.experimental import pallas as pl
from jax.experimental.pallas import tpu as pltpu

F32 = jnp.float32
BF16 = jnp.bfloat16

D_MODEL = 1024
CHUNK = 64
EPS = 1e-6
ROPE_THETA = 10000.0
A_HEADS = 8
A_DK = 64
A_DV = 64
CONV_K = 4
B_HEADS = 8
B_KV_HEADS = 2
B_HD = 64
IDX_HEADS = 8
IDX_HD = 64
TOPK_MAX = 256
N_GROUPS = 4
EXPERTS_PER_GROUP = 4
N_EXPERTS = 16
D_EXPERT = 256

O_AQKV, O_AZ, O_GA, O_GB, O_BQ, O_IQ, O_KD, O_VD, O_IK = 0, 1536, 2048, 3072, 4096, 4608, 5120, 5376, 5632
Z_W = 5760
S_BETA, S_ALPHA, S_IW = 0, 8, 16

GROUP_HEADS = 2
BD = GROUP_HEADS * CHUNK
NEG_BIG = -1e30
LOG2E = 1.4426950408889634
BISECT_STEPS = 18
VMEM_LIMIT = 56 * 1024 * 1024


def _split(x):
    hi = x.astype(BF16)
    lo = (x - hi.astype(F32)).astype(BF16)
    return hi, lo


def _dot(a, b):
    return jnp.dot(a, b, preferred_element_type=F32)


def _mm(a, b):
    return _dot(a.astype(BF16), b.astype(BF16))


def _mm_nt(a, b):
    return lax.dot_general(a.astype(BF16), b.astype(BF16), (((1,), (1,)), ((), ())),
                           preferred_element_type=F32)


def _mm_exact_lhs(a_bf16, x):
    xh, xl = _split(x)
    return _dot(a_bf16, xh) + _dot(a_bf16, xl)


def _rmsnorm_rows(x, g):
    return x * lax.rsqrt(jnp.mean(x * x, axis=-1, keepdims=True) + EPS) * g


_Z_CHUNKS = tuple((o, min(512, Z_W - o)) for o in range(0, Z_W, 512))


def _rope(x, cs, sn, first):
    w = x.shape[1]
    rep = w // 128
    if rep > 1:
        cs, sn, first = (jnp.concatenate([a] * rep, axis=1) for a in (cs, sn, first))
    swapped = jnp.where(first, pltpu.roll(x, w - B_HD // 2, 1), pltpu.roll(x, B_HD // 2, 1))
    return x * cs + swapped * sn


def _in_proj_kernel(x_ref, g_ref, w_ref, ws_ref, pos_ref, inv_ref, z_ref, zs_ref):
    h = _rmsnorm_rows(x_ref[...], g_ref[...])
    hb, hl = _split(h)

    ang = pos_ref[...] * inv_ref[...]
    lane = lax.broadcasted_iota(jnp.int32, ang.shape, 1)
    first = (lane & (B_HD - 1)) < (B_HD // 2)
    cs = jnp.cos(ang)
    sn = jnp.sin(ang)
    sn = jnp.where(first, -sn, sn)

    for o, w in _Z_CHUNKS:
        r = _dot(hb, w_ref[:, o:o + w])
        if o == O_BQ:
            r = _rope(r, cs, sn, first) * (B_HD ** -0.5 * LOG2E)
        elif o == O_IQ or o == O_IK:
            r = _rope(r, cs, sn, first)
        elif o == O_KD:
            kw = O_VD - O_KD
            r = jnp.concatenate([_rope(r[:, :kw], cs, sn, first), r[:, kw:]], axis=1)
        z_ref[:, o:o + w] = r.astype(BF16)
    s1 = _dot(hb, ws_ref[...])
    s2 = _dot(hl, ws_ref[:, :128])
    zs_ref[...] = s1[:, :128] + s1[:, 128:] + s2


def _in_proj(x2d, g, w_main, w_small, positions, tm=512):
    n = x2d.shape[0]
    half = B_HD // 2
    inv = ROPE_THETA ** (-jnp.arange(half, dtype=F32) / half)
    inv128 = jnp.tile(inv, 4)[None, :]
    pos = positions.astype(F32).reshape(n, 1)
    return pl.pallas_call(
        _in_proj_kernel,
        grid=(n // tm,),
        in_specs=[
            pl.BlockSpec((tm, D_MODEL), lambda i: (i, 0)),
            pl.BlockSpec((1, D_MODEL), lambda i: (0, 0)),
            pl.BlockSpec((D_MODEL, Z_W), lambda i: (0, 0)),
            pl.BlockSpec((D_MODEL, 256), lambda i: (0, 0)),
            pl.BlockSpec((tm, 1), lambda i: (i, 0)),
            pl.BlockSpec((1, 128), lambda i: (0, 0)),
        ],
        out_specs=[
            pl.BlockSpec((tm, Z_W), lambda i: (i, 0)),
            pl.BlockSpec((tm, 128), lambda i: (i, 0)),
        ],
        out_shape=[jax.ShapeDtypeStruct((n, Z_W), BF16), jax.ShapeDtypeStruct((n, 128), F32)],
        compiler_params=pltpu.CompilerParams(dimension_semantics=("parallel",),
                                             vmem_limit_bytes=VMEM_LIMIT),
        name="in_proj",
    )(x2d, g, w_main, w_small, pos, inv128)


def _gdn_constants():
    r = np.arange(BD)
    same = (r[:, None] // CHUNK) == (r[None, :] // CHUNK)
    incl = same & (r[:, None] >= r[None, :])
    strict = same & (r[:, None] > r[None, :])
    eye = np.eye(BD, dtype=np.float32)
    ll = np.concatenate([incl, same], axis=0).astype(np.float32)
    n_groups = A_HEADS // GROUP_HEADS
    sel = np.zeros((2 * n_groups, BD, 128), np.float32)
    for gi in range(n_groups):
        for h in range(GROUP_HEADS):
            sel[gi * 2 + 0, h * CHUNK:(h + 1) * CHUNK, S_BETA + gi * GROUP_HEADS + h] = 1.0
            sel[gi * 2 + 1, h * CHUNK:(h + 1) * CHUNK, S_ALPHA + gi * GROUP_HEADS + h] = 1.0
    return (jnp.asarray(incl, F32), jnp.asarray(strict, F32), jnp.asarray(same, F32), jnp.asarray(eye),
            jnp.asarray(ll, BF16), jnp.asarray(sel))


def _tile_heads(x):
    return jnp.concatenate([x] * GROUP_HEADS, axis=0)


def _gdn_kernel(zq_ref, zz_ref, zs_ref, cw_ref, av_ref, ag_ref, incl_ref, strict_ref, bdm_ref, eye_ref,
                ll_ref, sel_ref, o_ref, ext_ref, st_ref):
    c = pl.program_id(1)
    t = zq_ref.shape[0]

    @pl.when(c == 0)
    def _():
        ext_ref[0:8, :] = jnp.zeros((8, ext_ref.shape[1]), F32)
        st_ref[...] = jnp.zeros(st_ref.shape, F32)

    ext_ref[8:8 + t, :] = zq_ref[...].astype(F32)
    cw = cw_ref[...]
    y = cw[0:1, :] * ext_ref[pl.ds(8 - (CONV_K - 1), t), :]
    for j in range(1, CONV_K):
        y = y + cw[j:j + 1, :] * ext_ref[pl.ds(8 - (CONV_K - 1) + j, t), :]
    ext_ref[0:8, :] = ext_ref[t:t + 8, :]
    y = y * jax.nn.sigmoid(y)

    hw = A_HEADS * A_DK
    q_all = y[:, :hw]
    k_all = y[:, hw:2 * hw]
    v_all = y[:, 2 * hw:]

    sm = zs_ref[...]
    av = av_ref[...]
    lane = lax.broadcasted_iota(jnp.int32, sm.shape, 1)
    xg = sm + av[1:2, :]
    softplus = jnp.maximum(xg, 0.0) + jnp.log1p(jnp.exp(-jnp.abs(xg)))
    g_all = -jnp.exp(av[0:1, :]) * softplus
    bg = jnp.where(lane < S_ALPHA, jax.nn.sigmoid(sm), g_all)

    incl = incl_ref[...]
    strict = strict_ref[...]
    bdm = bdm_ref[...]
    eye = eye_ref[...]
    ll = ll_ref[...]
    incl_b = ll[:BD]

    n_chunks = t // CHUNK
    n_groups = A_HEADS // GROUP_HEADS
    chains = [(ci, gi) for ci in range(n_chunks) for gi in range(n_groups)]

    pre = {}
    for ci, gi in chains:
        r0, c0 = ci * CHUNK, gi * BD
        bg4 = _tile_heads(bg[r0:r0 + CHUNK])
        beta = jnp.sum(bg4 * sel_ref[gi * 2 + 0], axis=1, keepdims=True)
        gcol = jnp.sum(bg4 * sel_ref[gi * 2 + 1], axis=1, keepdims=True)
        gs = _mm_exact_lhs(ll, jnp.broadcast_to(gcol, (BD, 128)))
        g_cum = gs[:BD, :1]
        g_last = gs[BD:, :1]
        diff = _mm_exact_lhs(incl_b, gcol * strict)
        decay = jnp.where(incl > 0.0, jnp.exp(diff), 0.0)
        e_cum = jnp.exp(g_cum)
        kr = _tile_heads(k_all[r0:r0 + CHUNK, c0:c0 + BD]) * bdm
        qr = _tile_heads(q_all[r0:r0 + CHUNK, c0:c0 + BD]) * bdm
        vm = _tile_heads(v_all[r0:r0 + CHUNK, c0:c0 + BD]) * bdm
        km = kr * lax.rsqrt(jnp.sum(kr * kr, axis=1, keepdims=True) + EPS)
        qm = qr * (lax.rsqrt(jnp.sum(qr * qr, axis=1, keepdims=True) + EPS) * (A_DK ** -0.5))
        kkqk = _mm_nt(jnp.concatenate([km, qm], axis=0), km)
        m = -(strict * beta * kkqk[:BD] * decay)
        pre[ci, gi] = dict(beta=beta, g_cum=g_cum, g_last=g_last, e_cum=e_cum, km=km, qm=qm, vm=vm,
                           qk=kkqk[BD:] * decay, m=m, inv=eye + m)

    sq = CHUNK
    while sq > 2:
        for key in chains:
            p = pre[key]
            p["m"] = _mm(p["m"], p["m"])
        for key in chains:
            p = pre[key]
            p["inv"] = p["inv"] + _mm(p["inv"], p["m"])
        sq //= 2

    outs = []
    for ci in range(n_chunks):
        o_groups = []
        for gi in range(n_groups):
            p = pre[ci, gi]
            state = st_ref[gi]
            kq_s = _mm(jnp.concatenate([p["km"] * p["e_cum"], p["qm"] * p["e_cum"]], axis=0), state)
            v_new = _mm(p["inv"], p["beta"] * (p["vm"] - kq_s[:BD]))
            o_bd = kq_s[BD:] + _mm(p["qk"], v_new)
            k_dec = p["km"] * jnp.exp(p["g_last"] - p["g_cum"])
            st_ref[gi] = state * jnp.exp(p["g_last"]) + _mm(k_dec.T, v_new)
            o_bd = o_bd * lax.rsqrt(jnp.sum(o_bd * o_bd, axis=1, keepdims=True) * (1.0 / A_DV) + EPS)
            o_groups.append(sum(o_bd[h * CHUNK:(h + 1) * CHUNK] for h in range(GROUP_HEADS)))
        outs.append(jnp.concatenate(o_groups, axis=1))
    o = outs[0] if len(outs) == 1 else jnp.concatenate(outs, axis=0)

    zz = zz_ref[...].astype(F32)
    o_ref[...] = (o * ag_ref[...] * (zz * jax.nn.sigmoid(zz))).astype(BF16)


def _gdn(z3, zs3, conv_w, a_log, dt_bias, a_norm_g, t=4 * CHUNK):
    b, s, _ = z3.shape
    consts = _gdn_constants()
    av = jnp.zeros((2, 128), F32)
    av = av.at[0, S_ALPHA:S_ALPHA + A_HEADS].set(a_log.astype(F32))
    av = av.at[1, S_ALPHA:S_ALPHA + A_HEADS].set(dt_bias.astype(F32))
    ag = jnp.tile(a_norm_g.astype(F32), A_HEADS)[None, :]
    conv_cols = 2 * A_HEADS * A_DK + A_HEADS * A_DV

    def const_spec(a):
        nd = a.ndim
        return pl.BlockSpec(a.shape, lambda bi, ci, _n=nd: (0,) * _n)

    small_in = (conv_w.astype(F32), av, ag) + consts
    return pl.pallas_call(
        _gdn_kernel,
        grid=(b, s // t),
        in_specs=[
            pl.BlockSpec((None, t, conv_cols), lambda bi, ci: (bi, ci, O_AQKV // conv_cols)),
            pl.BlockSpec((None, t, 512), lambda bi, ci: (bi, ci, O_AZ // 512)),
            pl.BlockSpec((None, t, 128), lambda bi, ci: (bi, ci, 0)),
        ] + [const_spec(a) for a in small_in],
        out_specs=pl.BlockSpec((None, t, A_HEADS * A_DV), lambda bi, ci: (bi, ci, 0)),
        out_shape=jax.ShapeDtypeStruct((b, s, A_HEADS * A_DV), BF16),
        scratch_shapes=[pltpu.VMEM((8 + t, conv_cols), F32),
                        pltpu.VMEM((A_HEADS // GROUP_HEADS, BD, BD), F32)],
        compiler_params=pltpu.CompilerParams(dimension_semantics=("parallel", "arbitrary"),
                                             vmem_limit_bytes=VMEM_LIMIT),
        name="gdn",
    )(z3, z3, zs3, *small_in)


def _dsa_kernel(topk, tk, q_ref, iq_ref, ka_ref, kb_ref, vd_ref, ik_ref, zs_ref, o_ref, sc_ref):
    qb = pl.program_id(1)
    s = ka_ref.shape[0]
    tq = q_ref.shape[0]
    nkt_max = s // tk
    r0 = qb * tq
    n_tiles = (r0 + tq + tk - 1) // tk

    lane128 = lax.broadcasted_iota(jnp.int32, (tq, 128), 1)
    low_half = lane128 < B_HD
    high_half = lane128 >= B_HD

    def head_slab(x, h):
        slab = x[:, (h // 2) * 128:(h // 2 + 1) * 128]
        return jnp.where(low_half if h % 2 == 0 else high_half, slab, jnp.zeros_like(slab))

    sm = zs_ref[...]
    iw_scale = (IDX_HEADS ** -0.5) * (IDX_HD ** -0.5)
    wcols = [jnp.sum(jnp.where(lane128 == S_IW + h, sm, 0.0), axis=1, keepdims=True) * iw_scale
             for h in range(IDX_HEADS)]
    row = r0 + lax.broadcasted_iota(jnp.int32, (tq, 1), 0)
    limit = ((row >> 6) + 1) << 6

    iq = iq_ref[...]
    iq_heads = [head_slab(iq, h) for h in range(IDX_HEADS)]

    def index_tile(kt, carry):
        r_max, r_min = carry
        off = pl.multiple_of(kt * tk, tk)
        ikt = ik_ref[pl.ds(off, tk), :]
        acc = jnp.zeros((tq, tk), F32)
        for h in range(IDX_HEADS):
            sc = lax.dot_general(iq_heads[h], ikt, (((1,), (1,)), ((), ())), preferred_element_type=F32)
            acc = acc + wcols[h] * jnp.maximum(sc, 0.0)
        valid = off + lax.broadcasted_iota(jnp.int32, (tq, tk), 1) < limit
        masked = jnp.where(valid, acc, -jnp.inf)
        sc_ref[kt] = masked
        r_max = jnp.maximum(r_max, jnp.max(masked, axis=1, keepdims=True))
        r_min = jnp.minimum(r_min, jnp.min(acc, axis=1, keepdims=True))
        return r_max, r_min

    r_max, r_min = lax.fori_loop(0, n_tiles, index_tile,
                                 (jnp.full((tq, 1), -jnp.inf, F32), jnp.full((tq, 1), jnp.inf, F32)))

    all_kept = r0 + tq <= min(topk, tk)

    @pl.when(all_kept)
    def _():
        sc_ref[0] = jnp.where(sc_ref[0] > -jnp.inf, 0.0, NEG_BIG)

    for j in range(nkt_max):
        @pl.when(jnp.logical_and(n_tiles == j + 1, jnp.logical_not(all_kept)))
        def _(j=j):
            _topk_mask(topk, j + 1, limit, r_max, r_min, sc_ref)

    q = q_ref[...]
    heads_per_kv = B_HEADS // B_KV_HEADS
    nr = heads_per_kv * tq
    qrows = [jnp.concatenate([head_slab(q, g * heads_per_kv + j) for j in range(heads_per_kv)], axis=0)
             for g in range(B_KV_HEADS)]
    k_refs = (ka_ref, kb_ref)

    def attend_tile(kt, carry):
        off = pl.multiple_of(kt * tk, tk)
        bias = sc_ref[kt]
        out = []
        for g in range(B_KV_HEADS):
            m_run, l_run, acc = carry[g]
            logits = lax.dot_general(qrows[g], k_refs[g][pl.ds(off, tk), :], (((1,), (1,)), ((), ())),
                                     preferred_element_type=F32)
            logits = (logits.reshape(heads_per_kv, tq, tk) + bias[None]).reshape(nr, tk)
            m_new = jnp.maximum(m_run, jnp.max(logits, axis=1, keepdims=True))
            p = jnp.exp2(logits - m_new)
            alpha = jnp.exp2(m_run - m_new)
            l_new = alpha * l_run + jnp.sum(p, axis=1, keepdims=True)
            acc = alpha * acc + _dot(p.astype(BF16), vd_ref[pl.ds(off, tk), g * 128:(g + 1) * 128])
            out.append((m_new, l_new, acc))
        return tuple(out)

    init = tuple((jnp.full((nr, 1), NEG_BIG, F32), jnp.zeros((nr, 1), F32), jnp.zeros((nr, 128), F32))
                 for _ in range(B_KV_HEADS))
    fin = lax.fori_loop(0, n_tiles, attend_tile, init)
    for g in range(B_KV_HEADS):
        _, l_run, acc = fin[g]
        og = acc / l_run
        for pp in range(heads_per_kv // 2):
            even = og[(2 * pp) * tq:(2 * pp + 1) * tq]
            odd = og[(2 * pp + 1) * tq:(2 * pp + 2) * tq]
            col = (g * (heads_per_kv // 2) + pp) * 128
            o_ref[:, col:col + 128] = jnp.where(low_half, even, odd).astype(BF16)


def _topk_mask(topk, nt, limit, r_max, r_min, sc_ref):
    _, tq, tk = sc_ref.shape
    kf = float(topk)

    def tiles():
        return [sc_ref[t] for t in range(nt)]

    def row_sum(pred):
        tot = None
        for t, x in enumerate(tiles()):
            c = jnp.sum(pred(x, t).astype(F32), axis=1, keepdims=True)
            tot = c if tot is None else tot + c
        return tot

    def row_max(val):
        best = None
        for t, x in enumerate(tiles()):
            c = jnp.max(val(x, t), axis=1, keepdims=True)
            best = c if best is None else jnp.maximum(best, c)
        return best

    small = limit <= topk
    hi0 = r_max + jnp.maximum(jnp.abs(r_max), 1e-30) * 1e-6

    def bisect(_, carry):
        lo, hi = carry
        mid = 0.5 * (lo + hi)
        ge = row_sum(lambda x, t: x >= mid) >= kf
        return jnp.where(ge, mid, lo), jnp.where(ge, hi, mid)

    lo, hi = lax.fori_loop(0, BISECT_STEPS, bisect, (r_min, hi0))

    def peel_cond(carry):
        return jnp.sum(1.0 - carry[0]) > 0.0

    def peel(carry):
        done, thr, hi_c, n_ge = carry
        v1 = row_max(lambda x, t: jnp.where(x < hi_c, x, -jnp.inf))
        c1 = row_sum(lambda x, t: x >= v1)
        ok = c1 >= kf
        act = done < 0.5
        thr = jnp.where(act & ok, v1, thr)
        n_ge = jnp.where(act & ok, c1, n_ge)
        hi_c = jnp.where(act & (~ok), v1, hi_c)
        return jnp.where(ok, 1.0, done), thr, hi_c, n_ge

    _, thr, _, n_ge = lax.while_loop(peel_cond, peel, (jnp.where(small, 1.0, 0.0), lo, hi, jnp.full_like(lo, kf)))

    def key_index(t):
        return (t * tk + lax.broadcasted_iota(jnp.int32, (tq, tk), 1)).astype(F32)

    contested = jnp.sum(jnp.where((n_ge > kf) & (~small), 1.0, 0.0)) > 0.0
    last = float(nt * tk - 1)

    def tie_cut():
        need = kf - row_sum(lambda x, t: x > thr)

        def tie_search(_, carry):
            jlo, jhi = carry
            mid = jnp.floor(0.5 * (jlo + jhi))
            ge = row_sum(lambda x, t: (x == thr) & (key_index(t) <= mid)) >= need
            return jnp.where(ge, jlo, mid), jnp.where(ge, mid, jhi)

        n_iter = int(np.ceil(np.log2(nt * tk))) + 1
        return lax.fori_loop(0, n_iter, tie_search,
                             (jnp.full((tq, 1), -1.0, F32), jnp.full((tq, 1), last, F32)))[1]

    jcut = lax.cond(contested, tie_cut, lambda: jnp.full((tq, 1), last, F32))
    for t, x in enumerate(tiles()):
        sel = (x > -jnp.inf) & (small | (x > thr) | ((x == thr) & (key_index(t) <= jcut)))
        sc_ref[t] = jnp.where(sel, 0.0, NEG_BIG)


def _dsa(z3, zs3, tq=256, tk=512):
    b, s, _ = z3.shape
    topk = min(TOPK_MAX, s // 4)
    kernel = lambda *refs: _dsa_kernel(topk, tk, *refs)
    qblock = lambda w, cb: pl.BlockSpec((None, tq, w), lambda bi, qi, _c=cb: (bi, qi, _c))
    keys = lambda w, cb: pl.BlockSpec((None, s, w), lambda bi, qi, _c=cb: (bi, 0, _c))
    return pl.pallas_call(
        kernel,
        grid=(b, s // tq),
        in_specs=[qblock(512, O_BQ // 512), qblock(512, O_IQ // 512), keys(128, O_KD // 128),
                  keys(128, O_KD // 128 + 1), keys(256, O_VD // 256), keys(128, O_IK // 128), qblock(128, 0)],
        out_specs=pl.BlockSpec((None, tq, B_HEADS * B_HD), lambda bi, qi: (bi, qi, 0)),
        out_shape=jax.ShapeDtypeStruct((b, s, B_HEADS * B_HD), BF16),
        scratch_shapes=[pltpu.VMEM((s // tk, tq, tk), F32)],
        compiler_params=pltpu.CompilerParams(dimension_semantics=("parallel", "arbitrary"),
                                             vmem_limit_bytes=VMEM_LIMIT),
        name="dsa",
    )(z3, z3, z3, z3, z3, z3, zs3)


def _merge_kernel(x_ref, oa_ref, ob_ref, ga_ref, gb_ref, bg_ref, wa_ref, wb_ref, wo_ref, n2_ref, wr_ref,
                  br_ref, x1_ref, h2_ref, rl_ref):
    pa = _dot(oa_ref[...], wa_ref[...])
    pb = _dot(ob_ref[...], wb_ref[...])
    bgv = bg_ref[...]
    ga = jax.nn.sigmoid(ga_ref[...].astype(F32) + bgv[:, :D_MODEL])
    gb = jax.nn.sigmoid(gb_ref[...].astype(F32) + bgv[:, D_MODEL:])
    merged = ga * pa + gb * pb
    x1 = x_ref[...] + _mm(merged, wo_ref[...])
    x1_ref[...] = x1
    h2 = _rmsnorm_rows(x1, n2_ref[...])
    hb, hl = _split(h2)
    h2_ref[...] = hb
    s1 = _dot(hb, wr_ref[...])
    s2 = _dot(hl, wr_ref[:, :128])
    rl_ref[...] = s1[:, :128] + s1[:, 128:] + s2 + br_ref[...]


def _merge(x2d, oa, ob, z2d, b_gate, wa, wb, wo, n2, wr, br, tm=512):
    n = x2d.shape[0]
    full = lambda shape: pl.BlockSpec(shape, lambda i: (0, 0))
    return pl.pallas_call(
        _merge_kernel,
        grid=(n // tm,),
        in_specs=[
            pl.BlockSpec((tm, D_MODEL), lambda i: (i, 0)),
            pl.BlockSpec((tm, 512), lambda i: (i, 0)),
            pl.BlockSpec((tm, 512), lambda i: (i, 0)),
            pl.BlockSpec((tm, D_MODEL), lambda i: (i, O_GA // D_MODEL)),
            pl.BlockSpec((tm, D_MODEL), lambda i: (i, O_GB // D_MODEL)),
            full((1, 2 * D_MODEL)), full((512, D_MODEL)), full((512, D_MODEL)), full((D_MODEL, D_MODEL)),
            full((1, D_MODEL)), full((D_MODEL, 256)), full((1, 128)),
        ],
        out_specs=[
            pl.BlockSpec((tm, D_MODEL), lambda i: (i, 0)),
            pl.BlockSpec((tm, D_MODEL), lambda i: (i, 0)),
            pl.BlockSpec((tm, 128), lambda i: (i, 0)),
        ],
        out_shape=[jax.ShapeDtypeStruct((n, D_MODEL), F32), jax.ShapeDtypeStruct((n, D_MODEL), BF16),
                   jax.ShapeDtypeStruct((n, 128), F32)],
        compiler_params=pltpu.CompilerParams(dimension_semantics=("parallel",),
                                             vmem_limit_bytes=VMEM_LIMIT),
        name="merge",
    )(x2d, oa, ob, z2d, z2d, b_gate, wa, wb, wo, n2, wr, br)


R_GROUP, R_EXPERT = 0, N_GROUPS


def _routing_weights(rl):
    lane = lax.broadcasted_iota(jnp.int32, rl.shape, 1)
    big = jnp.int32(1 << 20)
    gmask = lane < N_GROUPS
    gl = jnp.where(gmask, rl, -jnp.inf)
    gmax = jnp.max(gl, axis=1, keepdims=True)
    gsel = jnp.min(jnp.where(gmask & (rl == gmax), lane, big), axis=1, keepdims=True)
    ggate = 1.0 / jnp.sum(jnp.where(gmask, jnp.exp(gl - gmax), 0.0), axis=1, keepdims=True)
    e_lo = R_EXPERT + gsel * EXPERTS_PER_GROUP
    emask = (lane >= e_lo) & (lane < e_lo + EXPERTS_PER_GROUP)
    el = jnp.where(emask, rl, -jnp.inf)
    emax = jnp.max(el, axis=1, keepdims=True)
    ee = jnp.where(emask, jnp.exp(el - emax), 0.0)
    ep = jnp.where(emask, ee / jnp.sum(ee, axis=1, keepdims=True), -1.0)
    p1 = jnp.max(ep, axis=1, keepdims=True)
    i1 = jnp.min(jnp.where(ep == p1, lane, big), axis=1, keepdims=True)
    ep2 = jnp.where(lane == i1, -1.0, ep)
    p2 = jnp.max(ep2, axis=1, keepdims=True)
    i2 = jnp.min(jnp.where(ep2 == p2, lane, big), axis=1, keepdims=True)
    tot = p1 + p2
    return (jnp.where(lane == i1, ggate * (p1 / tot), 0.0)
            + jnp.where(lane == i2, ggate * (p2 / tot), 0.0))


def _moe_kernel(x1_ref, h2_ref, rl_ref, w1_ref, w3_ref, w2_ref, fg_ref, o_ref, y_ref, comb_ref):
    step = pl.program_id(1)
    per_step = w1_ref.shape[0]

    @pl.when(step == 0)
    def _():
        comb_ref[...] = _routing_weights(rl_ref[...])
        y_ref[...] = jnp.zeros(y_ref.shape, F32)

    h = h2_ref[...]
    comb = comb_ref[...]
    lane = lax.broadcasted_iota(jnp.int32, comb.shape, 1)
    acts = []
    for j in range(per_step):
        a = _dot(h, w1_ref[j])
        b = _dot(h, w3_ref[j])
        ce = jnp.sum(jnp.where(lane == R_EXPERT + step * per_step + j, comb, 0.0), axis=1, keepdims=True)
        acts.append(((a * jax.nn.sigmoid(a)) * b * ce).astype(BF16))
    y_ref[...] += _dot(jnp.concatenate(acts, axis=1), w2_ref[...])

    @pl.when(step == pl.num_programs(1) - 1)
    def _():
        o_ref[...] = _rmsnorm_rows(x1_ref[...] + y_ref[...], fg_ref[...])


def _moe(x1, h2, rl, w1, w3, w2, fg, tm=1024, per_step=4):
    n = x1.shape[0]
    w2g = w2.reshape(N_EXPERTS // per_step, per_step * D_EXPERT, D_MODEL)
    return pl.pallas_call(
        _moe_kernel,
        grid=(n // tm, N_EXPERTS // per_step),
        in_specs=[
            pl.BlockSpec((tm, D_MODEL), lambda i, e: (i, 0)),
            pl.BlockSpec((tm, D_MODEL), lambda i, e: (i, 0)),
            pl.BlockSpec((tm, 128), lambda i, e: (i, 0)),
            pl.BlockSpec((per_step, D_MODEL, D_EXPERT), lambda i, e: (e, 0, 0)),
            pl.BlockSpec((per_step, D_MODEL, D_EXPERT), lambda i, e: (e, 0, 0)),
            pl.BlockSpec((None, per_step * D_EXPERT, D_MODEL), lambda i, e: (e, 0, 0)),
            pl.BlockSpec((1, D_MODEL), lambda i, e: (0, 0)),
        ],
        out_specs=pl.BlockSpec((tm, D_MODEL), lambda i, e: (i, 0)),
        out_shape=jax.ShapeDtypeStruct((n, D_MODEL), F32),
        scratch_shapes=[pltpu.VMEM((tm, D_MODEL), F32), pltpu.VMEM((tm, 128), F32)],
        compiler_params=pltpu.CompilerParams(dimension_semantics=("parallel", "arbitrary"),
                                             vmem_limit_bytes=VMEM_LIMIT),
        name="moe",
    )(x1, h2, rl, w1, w3, w2g, fg)


_W_OFF = {}
_off = 0
for _name, _n in (("a_q", 512), ("a_k", 512), ("a_v", 512), ("a_z", 512), ("a_beta", 8), ("a_alpha", 8),
                  ("b_q", 512), ("b_k", 128), ("b_v", 128), ("i_q", 512), ("i_k", 64), ("i_w", 8),
                  ("gate_a", 1024), ("gate_b", 1024)):
    _W_OFF[_name] = (_off, _off + _n)
    _off += _n


def _cols(w, name, lo=0, hi=None):
    a, b = _W_OFF[name]
    return w[:, a + lo:(b if hi is None else a + hi)]


def _hi_lo_pair(w_small):
    pad = jnp.zeros((w_small.shape[0], 128 - w_small.shape[1]), F32)
    w = jnp.concatenate([w_small.astype(F32), pad], axis=1)
    hi, lo = _split(w)
    return jnp.concatenate([hi, lo], axis=1)


def _layout_w_in(w):
    wb = w.astype(BF16)
    k0, k1 = _cols(wb, "b_k", 0, 64), _cols(wb, "b_k", 64, 128)
    v0, v1 = _cols(wb, "b_v", 0, 64), _cols(wb, "b_v", 64, 128)
    ik = _cols(wb, "i_k")
    main = jnp.concatenate([
        _cols(wb, "a_q"), _cols(wb, "a_k"), _cols(wb, "a_v"), _cols(wb, "a_z"), _cols(wb, "gate_a"),
        _cols(wb, "gate_b"), _cols(wb, "b_q"), _cols(wb, "i_q"), k0, k0, k1, k1, v0, v0, v1, v1, ik, ik],
        axis=1)
    small = _hi_lo_pair(jnp.concatenate([_cols(w, "a_beta"), _cols(w, "a_alpha"), _cols(w, "i_w")], axis=1))
    return main, small


def kernel(x, positions, norm1_g, w_in, b_gate, conv_w, a_log, dt_bias, a_norm_g, w_proj_a, w_proj_b, w_out,
           norm2_g, w_router_group, b_router_group, w_router_expert, b_router_expert, w_exp_gate, w_exp_up,
           w_exp_down, final_norm_g):
    b, s, d = x.shape
    n = b * s
    depth = w_in.shape[0]
    xc = x.reshape(n, d).astype(F32)
    for l in range(depth):
        w_main, w_small = _layout_w_in(w_in[l])
        z, zs = _in_proj(xc, norm1_g[l][None, :].astype(F32), w_main, w_small, positions)
        z3 = z.reshape(b, s, Z_W)
        zs3 = zs.reshape(b, s, 128)
        o_a = _gdn(z3, zs3, conv_w[l], a_log[l], dt_bias[l], a_norm_g[l])
        o_b = _dsa(z3, zs3)
        wr = _hi_lo_pair(jnp.concatenate([w_router_group[l], w_router_expert[l]], axis=1))
        br = jnp.concatenate([b_router_group[l], b_router_expert[l],
                              jnp.zeros((128 - N_GROUPS - N_EXPERTS,), F32)])[None, :].astype(F32)
        x1, h2, rl = _merge(xc, o_a.reshape(n, -1), o_b.reshape(n, -1), z, b_gate[l][None, :].astype(F32),
                            w_proj_a[l].astype(BF16), w_proj_b[l].astype(BF16), w_out[l].astype(BF16),
                            norm2_g[l][None, :].astype(F32), wr, br)
        last = l == depth - 1
        fg = final_norm_g[None, :].astype(F32) if last else None
        if not last:
            raise NotImplementedError("only the final layer fuses the output norm")
        xc = _moe(x1, h2, rl, w_exp_gate[l].astype(BF16), w_exp_up[l].astype(BF16),
                  w_exp_down[l].astype(BF16), fg)
    return xc.reshape(b, s, d).astype(x.dtype)
```

```python
import numpy as np
import jax
import jax.numpy as jnp
from jax import lax
from jax.experimental import pallas as pl
from jax.experimental.pallas import tpu as pltpu

F32 = jnp.float32
BF16 = jnp.bfloat16

D_MODEL = 1024
CHUNK = 64
EPS = 1e-6
ROPE_THETA = 10000.0
A_HEADS = 8
A_DK = 64
A_DV = 64
CONV_K = 4
B_HEADS = 8
B_KV_HEADS = 2
B_HD = 64
IDX_HEADS = 8
IDX_HD = 64
TOPK_MAX = 256
N_GROUPS = 4
EXPERTS_PER_GROUP = 4
N_EXPERTS = 16
D_EXPERT = 256

O_AQKV, O_AZ, O_GA, O_GB, O_BQ, O_IQ, O_KD, O_VD, O_IK = 0, 1536, 2048, 3072, 4096, 4608, 5120, 5376, 5632
Z_W = 5760
S_BETA, S_ALPHA, S_IW = 0, 8, 16

GROUP_HEADS = 2
BD = GROUP_HEADS * CHUNK
NEG_BIG = -1e30
LOG2E = 1.4426950408889634
BISECT_STEPS = 18
VMEM_LIMIT = 56 * 1024 * 1024


def _split(x):
    hi = x.astype(BF16)
    lo = (x - hi.astype(F32)).astype(BF16)
    return hi, lo


def _dot(a, b):
    return jnp.dot(a, b, preferred_element_type=F32)


def _mm(a, b):
    return _dot(a.astype(BF16), b.astype(BF16))


def _mm_nt(a, b):
    return lax.dot_general(a.astype(BF16), b.astype(BF16), (((1,), (1,)), ((), ())),
                           preferred_element_type=F32)


def _mm_exact_lhs(a_bf16, x):
    xh, xl = _split(x)
    return _dot(a_bf16, xh) + _dot(a_bf16, xl)


def _rmsnorm_rows(x, g):
    return x * lax.rsqrt(jnp.mean(x * x, axis=-1, keepdims=True) + EPS) * g


_Z_CHUNKS = tuple((o, min(512, Z_W - o)) for o in range(0, Z_W, 512))


def _rope(x, cs, sn, first):
    w = x.shape[1]
    rep = w // 128
    if rep > 1:
        cs, sn, first = (jnp.concatenate([a] * rep, axis=1) for a in (cs, sn, first))
    swapped = jnp.where(first, pltpu.roll(x, w - B_HD // 2, 1), pltpu.roll(x, B_HD // 2, 1))
    return x * cs + swapped * sn


def _in_proj_kernel(x_ref, g_ref, w_ref, ws_ref, pos_ref, inv_ref, z_ref, zs_ref):
    h = _rmsnorm_rows(x_ref[...], g_ref[...])
    hb, hl = _split(h)

    ang = pos_ref[...] * inv_ref[...]
    lane = lax.broadcasted_iota(jnp.int32, ang.shape, 1)
    first = (lane & (B_HD - 1)) < (B_HD // 2)
    cs = jnp.cos(ang)
    sn = jnp.sin(ang)
    sn = jnp.where(first, -sn, sn)

    for o, w in _Z_CHUNKS:
        r = _dot(hb, w_ref[:, o:o + w])
        if o == O_BQ:
            r = _rope(r, cs, sn, first) * (B_HD ** -0.5 * LOG2E)
        elif o == O_IQ or o == O_IK:
            r = _rope(r, cs, sn, first)
        elif o == O_KD:
            kw = O_VD - O_KD
            r = jnp.concatenate([_rope(r[:, :kw], cs, sn, first), r[:, kw:]], axis=1)
        z_ref[:, o:o + w] = r.astype(BF16)
    s1 = _dot(hb, ws_ref[...])
    s2 = _dot(hl, ws_ref[:, :128])
    zs_ref[...] = s1[:, :128] + s1[:, 128:] + s2


def _in_proj(x2d, g, w_main, w_small, positions, tm=512):
    n = x2d.shape[0]
    half = B_HD // 2
    inv = ROPE_THETA ** (-jnp.arange(half, dtype=F32) / half)
    inv128 = jnp.tile(inv, 4)[None, :]
    pos = positions.astype(F32).reshape(n, 1)
    return pl.pallas_call(
        _in_proj_kernel,
        grid=(n // tm,),
        in_specs=[
            pl.BlockSpec((tm, D_MODEL), lambda i: (i, 0)),
            pl.BlockSpec((1, D_MODEL), lambda i: (0, 0)),
            pl.BlockSpec((D_MODEL, Z_W), lambda i: (0, 0)),
            pl.BlockSpec((D_MODEL, 256), lambda i: (0, 0)),
            pl.BlockSpec((tm, 1), lambda i: (i, 0)),
            pl.BlockSpec((1, 128), lambda i: (0, 0)),
        ],
        out_specs=[
            pl.BlockSpec((tm, Z_W), lambda i: (i, 0)),
            pl.BlockSpec((tm, 128), lambda i: (i, 0)),
        ],
        out_shape=[jax.ShapeDtypeStruct((n, Z_W), BF16), jax.ShapeDtypeStruct((n, 128), F32)],
        compiler_params=pltpu.CompilerParams(dimension_semantics=("parallel",),
                                             vmem_limit_bytes=VMEM_LIMIT),
        name="in_proj",
    )(x2d, g, w_main, w_small, pos, inv128)


def _gdn_constants():
    r = np.arange(BD)
    same = (r[:, None] // CHUNK) == (r[None, :] // CHUNK)
    incl = same & (r[:, None] >= r[None, :])
    strict = same & (r[:, None] > r[None, :])
    eye = np.eye(BD, dtype=np.float32)
    ll = np.concatenate([incl, same], axis=0).astype(np.float32)
    n_groups = A_HEADS // GROUP_HEADS
    sel = np.zeros((2 * n_groups, BD, 128), np.float32)
    for gi in range(n_groups):
        for h in range(GROUP_HEADS):
            sel[gi * 2 + 0, h * CHUNK:(h + 1) * CHUNK, S_BETA + gi * GROUP_HEADS + h] = 1.0
            sel[gi * 2 + 1, h * CHUNK:(h + 1) * CHUNK, S_ALPHA + gi * GROUP_HEADS + h] = 1.0
    return (jnp.asarray(incl, F32), jnp.asarray(strict, F32), jnp.asarray(same, F32), jnp.asarray(eye),
            jnp.asarray(ll, BF16), jnp.asarray(sel))


def _tile_heads(x):
    return jnp.concatenate([x] * GROUP_HEADS, axis=0)


def _gdn_kernel(zq_ref, zz_ref, zs_ref, cw_ref, av_ref, ag_ref, incl_ref, strict_ref, bdm_ref, eye_ref,
                ll_ref, sel_ref, o_ref, ext_ref, st_ref):
    c = pl.program_id(1)
    nb, t = zq_ref.shape[0], zq_ref.shape[1]

    @pl.when(c == 0)
    def _():
        ext_ref[:, 0:8, :] = jnp.zeros((nb, 8, ext_ref.shape[2]), F32)
        st_ref[...] = jnp.zeros(st_ref.shape, F32)

    hw = A_HEADS * A_DK
    cw = cw_ref[...]
    av = av_ref[...]
    q_all, k_all, v_all, bg = [], [], [], []
    for bi in range(nb):
        ext_ref[bi, 8:8 + t, :] = zq_ref[bi].astype(F32)
        y = cw[0:1, :] * ext_ref[bi, pl.ds(8 - (CONV_K - 1), t), :]
        for j in range(1, CONV_K):
            y = y + cw[j:j + 1, :] * ext_ref[bi, pl.ds(8 - (CONV_K - 1) + j, t), :]
        ext_ref[bi, 0:8, :] = ext_ref[bi, t:t + 8, :]
        y = y * jax.nn.sigmoid(y)
        q_all.append(y[:, :hw])
        k_all.append(y[:, hw:2 * hw])
        v_all.append(y[:, 2 * hw:])

        sm = zs_ref[bi]
        lane = lax.broadcasted_iota(jnp.int32, sm.shape, 1)
        xg = sm + av[1:2, :]
        softplus = jnp.maximum(xg, 0.0) + jnp.log1p(jnp.exp(-jnp.abs(xg)))
        g_all = -jnp.exp(av[0:1, :]) * softplus
        bg.append(jnp.where(lane < S_ALPHA, jax.nn.sigmoid(sm), g_all))

    incl = incl_ref[...]
    strict = strict_ref[...]
    bdm = bdm_ref[...]
    eye = eye_ref[...]
    ll = ll_ref[...]
    incl_b = ll[:BD]

    n_chunks = t // CHUNK
    n_groups = A_HEADS // GROUP_HEADS
    chains = [(bi, ci, gi) for bi in range(nb) for ci in range(n_chunks) for gi in range(n_groups)]

    pre = {}
    for bi, ci, gi in chains:
        r0, c0 = ci * CHUNK, gi * BD
        bg4 = _tile_heads(bg[bi][r0:r0 + CHUNK])
        beta = jnp.sum(bg4 * sel_ref[gi * 2 + 0], axis=1, keepdims=True)
        gcol = jnp.sum(bg4 * sel_ref[gi * 2 + 1], axis=1, keepdims=True)
        gs = _mm_exact_lhs(ll, jnp.broadcast_to(gcol, (BD, 128)))
        g_cum = gs[:BD, :1]
        g_last = gs[BD:, :1]
        diff = _mm_exact_lhs(incl_b, gcol * strict)
        decay = jnp.where(incl > 0.0, jnp.exp(diff), 0.0)
        e_cum = jnp.exp(g_cum)
        kr = _tile_heads(k_all[bi][r0:r0 + CHUNK, c0:c0 + BD]) * bdm
        qr = _tile_heads(q_all[bi][r0:r0 + CHUNK, c0:c0 + BD]) * bdm
        vm = _tile_heads(v_all[bi][r0:r0 + CHUNK, c0:c0 + BD]) * bdm
        km = kr * lax.rsqrt(jnp.sum(kr * kr, axis=1, keepdims=True) + EPS)
        qm = qr * (lax.rsqrt(jnp.sum(qr * qr, axis=1, keepdims=True) + EPS) * (A_DK ** -0.5))
        kkqk = _mm_nt(jnp.concatenate([km, qm], axis=0), km)
        m = -(strict * beta * kkqk[:BD] * decay)
        pre[bi, ci, gi] = dict(beta=beta, g_cum=g_cum, g_last=g_last, e_cum=e_cum, km=km, qm=qm, vm=vm,
                               qk=kkqk[BD:] * decay, m=m, inv=eye + m)

    sq = CHUNK
    while sq > 2:
        for key in chains:
            p = pre[key]
            p["m"] = _mm(p["m"], p["m"])
        for key in chains:
            p = pre[key]
            p["inv"] = p["inv"] + _mm(p["inv"], p["m"])
        sq //= 2

    states = [st_ref[i] for i in range(nb * n_groups)]
    outs = [[] for _ in range(nb)]
    lanes = [(bi, gi) for bi in range(nb) for gi in range(n_groups)]
    for ci in range(n_chunks):
        ps = {k: pre[k[0], ci, k[1]] for k in lanes}
        sidx = {k: k[0] * n_groups + k[1] for k in lanes}
        kq_s = {k: _mm(jnp.concatenate([ps[k]["km"] * ps[k]["e_cum"], ps[k]["qm"] * ps[k]["e_cum"]], axis=0),
                       states[sidx[k]]) for k in lanes}
        v_new = {k: _mm(ps[k]["inv"], ps[k]["beta"] * (ps[k]["vm"] - kq_s[k][:BD])) for k in lanes}
        o_bd = {k: kq_s[k][BD:] + _mm(ps[k]["qk"], v_new[k]) for k in lanes}
        for k in lanes:
            p = ps[k]
            k_dec = p["km"] * jnp.exp(p["g_last"] - p["g_cum"])
            states[sidx[k]] = states[sidx[k]] * jnp.exp(p["g_last"]) + _mm(k_dec.T, v_new[k])
        for bi in range(nb):
            o_groups = []
            for gi in range(n_groups):
                ob = o_bd[bi, gi]
                ob = ob * lax.rsqrt(jnp.sum(ob * ob, axis=1, keepdims=True) * (1.0 / A_DV) + EPS)
                o_groups.append(sum(ob[h * CHUNK:(h + 1) * CHUNK] for h in range(GROUP_HEADS)))
            outs[bi].append(jnp.concatenate(o_groups, axis=1))

    for i, state in enumerate(states):
        st_ref[i] = state
    for bi in range(nb):
        o = outs[bi][0] if n_chunks == 1 else jnp.concatenate(outs[bi], axis=0)
        zz = zz_ref[bi].astype(F32)
        o_ref[bi] = (o * ag_ref[...] * (zz * jax.nn.sigmoid(zz))).astype(BF16)


def _gdn(z3, zs3, conv_w, a_log, dt_bias, a_norm_g, t=2 * CHUNK, nb=2):
    b, s, _ = z3.shape
    consts = _gdn_constants()
    av = jnp.zeros((2, 128), F32)
    av = av.at[0, S_ALPHA:S_ALPHA + A_HEADS].set(a_log.astype(F32))
    av = av.at[1, S_ALPHA:S_ALPHA + A_HEADS].set(dt_bias.astype(F32))
    ag = jnp.tile(a_norm_g.astype(F32), A_HEADS)[None, :]
    conv_cols = 2 * A_HEADS * A_DK + A_HEADS * A_DV

    def const_spec(a):
        nd = a.ndim
        return pl.BlockSpec(a.shape, lambda bi, ci, _n=nd: (0,) * _n)

    small_in = (conv_w.astype(F32), av, ag) + consts
    return pl.pallas_call(
        _gdn_kernel,
        grid=(b // nb, s // t),
        in_specs=[
            pl.BlockSpec((nb, t, conv_cols), lambda bi, ci: (bi, ci, O_AQKV // conv_cols)),
            pl.BlockSpec((nb, t, 512), lambda bi, ci: (bi, ci, O_AZ // 512)),
            pl.BlockSpec((nb, t, 128), lambda bi, ci: (bi, ci, 0)),
        ] + [const_spec(a) for a in small_in],
        out_specs=pl.BlockSpec((nb, t, A_HEADS * A_DV), lambda bi, ci: (bi, ci, 0)),
        out_shape=jax.ShapeDtypeStruct((b, s, A_HEADS * A_DV), BF16),
        scratch_shapes=[pltpu.VMEM((nb, 8 + t, conv_cols), F32),
                        pltpu.VMEM((nb * (A_HEADS // GROUP_HEADS), BD, BD), F32)],
        compiler_params=pltpu.CompilerParams(dimension_semantics=("parallel", "arbitrary"),
                                             vmem_limit_bytes=VMEM_LIMIT),
        name="gdn",
    )(z3, z3, zs3, *small_in)


def _dsa_kernel(topk, tk, q_ref, iq_ref, ka_ref, kb_ref, vd_ref, ik_ref, zs_ref, o_ref, sc_ref):
    qb = pl.program_id(1)
    s = ka_ref.shape[0]
    tq = q_ref.shape[0]
    nkt_max = s // tk
    r0 = qb * tq
    n_tiles = (r0 + tq + tk - 1) // tk

    lane128 = lax.broadcasted_iota(jnp.int32, (tq, 128), 1)
    low_half = lane128 < B_HD
    high_half = lane128 >= B_HD

    def head_slab(x, h):
        slab = x[:, (h // 2) * 128:(h // 2 + 1) * 128]
        return jnp.where(low_half if h % 2 == 0 else high_half, slab, jnp.zeros_like(slab))

    sm = zs_ref[...]
    iw_scale = (IDX_HEADS ** -0.5) * (IDX_HD ** -0.5)
    wcols = [jnp.sum(jnp.where(lane128 == S_IW + h, sm, 0.0), axis=1, keepdims=True) * iw_scale
             for h in range(IDX_HEADS)]
    row = r0 + lax.broadcasted_iota(jnp.int32, (tq, 1), 0)
    limit = ((row >> 6) + 1) << 6

    iq = iq_ref[...]
    iq_heads = [head_slab(iq, h) for h in range(IDX_HEADS)]

    def index_tile(kt, carry):
        r_max, r_min = carry
        off = pl.multiple_of(kt * tk, tk)
        ikt = ik_ref[pl.ds(off, tk), :]
        acc = jnp.zeros((tq, tk), F32)
        for h in range(IDX_HEADS):
            sc = lax.dot_general(iq_heads[h], ikt, (((1,), (1,)), ((), ())), preferred_element_type=F32)
            acc = acc + wcols[h] * jnp.maximum(sc, 0.0)
        valid = off + lax.broadcasted_iota(jnp.int32, (tq, tk), 1) < limit
        masked = jnp.where(valid, acc, -jnp.inf)
        sc_ref[kt] = masked
        r_max = jnp.maximum(r_max, jnp.max(masked, axis=1, keepdims=True))
        r_min = jnp.minimum(r_min, jnp.min(acc, axis=1, keepdims=True))
        return r_max, r_min

    r_max, r_min = lax.fori_loop(0, n_tiles, index_tile,
                                 (jnp.full((tq, 1), -jnp.inf, F32), jnp.full((tq, 1), jnp.inf, F32)))

    all_kept = r0 + tq <= min(topk, tk)

    @pl.when(all_kept)
    def _():
        sc_ref[0] = jnp.where(sc_ref[0] > -jnp.inf, 0.0, NEG_BIG)

    for j in range(nkt_max):
        @pl.when(jnp.logical_and(n_tiles == j + 1, jnp.logical_not(all_kept)))
        def _(j=j):
            _topk_mask(topk, j + 1, limit, r_max, r_min, sc_ref)

    q = q_ref[...]
    heads_per_kv = B_HEADS // B_KV_HEADS
    nr = heads_per_kv * tq
    qrows = [jnp.concatenate([head_slab(q, g * heads_per_kv + j) for j in range(heads_per_kv)], axis=0)
             for g in range(B_KV_HEADS)]
    k_refs = (ka_ref, kb_ref)

    def attend_tile(kt, carry):
        off = pl.multiple_of(kt * tk, tk)
        bias = sc_ref[kt]
        out = []
        for g in range(B_KV_HEADS):
            m_run, l_run, acc = carry[g]
            logits = lax.dot_general(qrows[g], k_refs[g][pl.ds(off, tk), :], (((1,), (1,)), ((), ())),
                                     preferred_element_type=F32)
            logits = (logits.reshape(heads_per_kv, tq, tk) + bias[None]).reshape(nr, tk)
            m_new = jnp.maximum(m_run, jnp.max(logits, axis=1, keepdims=True))
            p = jnp.exp2(logits - m_new)
            alpha = jnp.exp2(m_run - m_new)
            l_new = alpha * l_run + jnp.sum(p, axis=1, keepdims=True)
            acc = alpha * acc + _dot(p.astype(BF16), vd_ref[pl.ds(off, tk), g * 128:(g + 1) * 128])
            out.append((m_new, l_new, acc))
        return tuple(out)

    init = tuple((jnp.full((nr, 1), NEG_BIG, F32), jnp.zeros((nr, 1), F32), jnp.zeros((nr, 128), F32))
                 for _ in range(B_KV_HEADS))
    fin = lax.fori_loop(0, n_tiles, attend_tile, init)
    for g in range(B_KV_HEADS):
        _, l_run, acc = fin[g]
        og = acc / l_run
        for pp in range(heads_per_kv // 2):
            even = og[(2 * pp) * tq:(2 * pp + 1) * tq]
            odd = og[(2 * pp + 1) * tq:(2 * pp + 2) * tq]
            col = (g * (heads_per_kv // 2) + pp) * 128
            o_ref[:, col:col + 128] = jnp.where(low_half, even, odd).astype(BF16)


def _topk_mask(topk, nt, limit, r_max, r_min, sc_ref):
    _, tq, tk = sc_ref.shape
    kf = float(topk)

    def tiles():
        return [sc_ref[t] for t in range(nt)]

    def row_sum(pred):
        tot = None
        for t, x in enumerate(tiles()):
            c = jnp.sum(pred(x, t).astype(F32), axis=1, keepdims=True)
            tot = c if tot is None else tot + c
        return tot

    def row_max(val):
        best = None
        for t, x in enumerate(tiles()):
            c = jnp.max(val(x, t), axis=1, keepdims=True)
            best = c if best is None else jnp.maximum(best, c)
        return best

    small = limit <= topk
    hi0 = r_max + jnp.maximum(jnp.abs(r_max), 1e-30) * 1e-6

    def bisect(_, carry):
        lo, hi = carry
        mid = 0.5 * (lo + hi)
        ge = row_sum(lambda x, t: x >= mid) >= kf
        return jnp.where(ge, mid, lo), jnp.where(ge, hi, mid)

    lo, hi = lax.fori_loop(0, BISECT_STEPS, bisect, (r_min, hi0))

    def peel_cond(carry):
        return jnp.sum(1.0 - carry[0]) > 0.0

    def peel(carry):
        done, thr, hi_c, n_ge = carry
        v1 = row_max(lambda x, t: jnp.where(x < hi_c, x, -jnp.inf))
        c1 = row_sum(lambda x, t: x >= v1)
        ok = c1 >= kf
        act = done < 0.5
        thr = jnp.where(act & ok, v1, thr)
        n_ge = jnp.where(act & ok, c1, n_ge)
        hi_c = jnp.where(act & (~ok), v1, hi_c)
        return jnp.where(ok, 1.0, done), thr, hi_c, n_ge

    _, thr, _, n_ge = lax.while_loop(peel_cond, peel, (jnp.where(small, 1.0, 0.0), lo, hi, jnp.full_like(lo, kf)))

    def key_index(t):
        return (t * tk + lax.broadcasted_iota(jnp.int32, (tq, tk), 1)).astype(F32)

    contested = jnp.sum(jnp.where((n_ge > kf) & (~small), 1.0, 0.0)) > 0.0
    last = float(nt * tk - 1)

    def tie_cut():
        need = kf - row_sum(lambda x, t: x > thr)

        def tie_search(_, carry):
            jlo, jhi = carry
            mid = jnp.floor(0.5 * (jlo + jhi))
            ge = row_sum(lambda x, t: (x == thr) & (key_index(t) <= mid)) >= need
            return jnp.where(ge, jlo, mid), jnp.where(ge, mid, jhi)

        n_iter = int(np.ceil(np.log2(nt * tk))) + 1
        return lax.fori_loop(0, n_iter, tie_search,
                             (jnp.full((tq, 1), -1.0, F32), jnp.full((tq, 1), last, F32)))[1]

    jcut = lax.cond(contested, tie_cut, lambda: jnp.full((tq, 1), last, F32))
    for t, x in enumerate(tiles()):
        sel = (x > -jnp.inf) & (small | (x > thr) | ((x == thr) & (key_index(t) <= jcut)))
        sc_ref[t] = jnp.where(sel, 0.0, NEG_BIG)


def _dsa(z3, zs3, tq=256, tk=512):
    b, s, _ = z3.shape
    topk = min(TOPK_MAX, s // 4)
    kernel = lambda *refs: _dsa_kernel(topk, tk, *refs)
    qblock = lambda w, cb: pl.BlockSpec((None, tq, w), lambda bi, qi, _c=cb: (bi, qi, _c))
    keys = lambda w, cb: pl.BlockSpec((None, s, w), lambda bi, qi, _c=cb: (bi, 0, _c))
    return pl.pallas_call(
        kernel,
        grid=(b, s // tq),
        in_specs=[qblock(512, O_BQ // 512), qblock(512, O_IQ // 512), keys(128, O_KD // 128),
                  keys(128, O_KD // 128 + 1), keys(256, O_VD // 256), keys(128, O_IK // 128), qblock(128, 0)],
        out_specs=pl.BlockSpec((None, tq, B_HEADS * B_HD), lambda bi, qi: (bi, qi, 0)),
        out_shape=jax.ShapeDtypeStruct((b, s, B_HEADS * B_HD), BF16),
        scratch_shapes=[pltpu.VMEM((s // tk, tq, tk), F32)],
        compiler_params=pltpu.CompilerParams(dimension_semantics=("parallel", "arbitrary"),
                                             vmem_limit_bytes=VMEM_LIMIT),
        name="dsa",
    )(z3, z3, z3, z3, z3, z3, zs3)


def _merge_kernel(x_ref, oa_ref, ob_ref, ga_ref, gb_ref, bg_ref, wa_ref, wb_ref, wo_ref, n2_ref, wr_ref,
                  br_ref, x1_ref, h2_ref, rl_ref):
    pa = _dot(oa_ref[...], wa_ref[...])
    pb = _dot(ob_ref[...], wb_ref[...])
    bgv = bg_ref[...]
    ga = jax.nn.sigmoid(ga_ref[...].astype(F32) + bgv[:, :D_MODEL])
    gb = jax.nn.sigmoid(gb_ref[...].astype(F32) + bgv[:, D_MODEL:])
    merged = ga * pa + gb * pb
    x1 = x_ref[...] + _mm(merged, wo_ref[...])
    x1_ref[...] = x1
    h2 = _rmsnorm_rows(x1, n2_ref[...])
    hb, hl = _split(h2)
    h2_ref[...] = hb
    s1 = _dot(hb, wr_ref[...])
    s2 = _dot(hl, wr_ref[:, :128])
    rl_ref[...] = s1[:, :128] + s1[:, 128:] + s2 + br_ref[...]


def _merge(x2d, oa, ob, z2d, b_gate, wa, wb, wo, n2, wr, br, tm=512):
    n = x2d.shape[0]
    full = lambda shape: pl.BlockSpec(shape, lambda i: (0, 0))
    return pl.pallas_call(
        _merge_kernel,
        grid=(n // tm,),
        in_specs=[
            pl.BlockSpec((tm, D_MODEL), lambda i: (i, 0)),
            pl.BlockSpec((tm, 512), lambda i: (i, 0)),
            pl.BlockSpec((tm, 512), lambda i: (i, 0)),
            pl.BlockSpec((tm, D_MODEL), lambda i: (i, O_GA // D_MODEL)),
            pl.BlockSpec((tm, D_MODEL), lambda i: (i, O_GB // D_MODEL)),
            full((1, 2 * D_MODEL)), full((512, D_MODEL)), full((512, D_MODEL)), full((D_MODEL, D_MODEL)),
            full((1, D_MODEL)), full((D_MODEL, 256)), full((1, 128)),
        ],
        out_specs=[
            pl.BlockSpec((tm, D_MODEL), lambda i: (i, 0)),
            pl.BlockSpec((tm, D_MODEL), lambda i: (i, 0)),
            pl.BlockSpec((tm, 128), lambda i: (i, 0)),
        ],
        out_shape=[jax.ShapeDtypeStruct((n, D_MODEL), F32), jax.ShapeDtypeStruct((n, D_MODEL), BF16),
                   jax.ShapeDtypeStruct((n, 128), F32)],
        compiler_params=pltpu.CompilerParams(dimension_semantics=("parallel",),
                                             vmem_limit_bytes=VMEM_LIMIT),
        name="merge",
    )(x2d, oa, ob, z2d, z2d, b_gate, wa, wb, wo, n2, wr, br)


R_GROUP, R_EXPERT = 0, N_GROUPS


def _routing_weights(rl):
    lane = lax.broadcasted_iota(jnp.int32, rl.shape, 1)
    big = jnp.int32(1 << 20)
    gmask = lane < N_GROUPS
    gl = jnp.where(gmask, rl, -jnp.inf)
    gmax = jnp.max(gl, axis=1, keepdims=True)
    gsel = jnp.min(jnp.where(gmask & (rl == gmax), lane, big), axis=1, keepdims=True)
    ggate = 1.0 / jnp.sum(jnp.where(gmask, jnp.exp(gl - gmax), 0.0), axis=1, keepdims=True)
    e_lo = R_EXPERT + gsel * EXPERTS_PER_GROUP
    emask = (lane >= e_lo) & (lane < e_lo + EXPERTS_PER_GROUP)
    el = jnp.where(emask, rl, -jnp.inf)
    emax = jnp.max(el, axis=1, keepdims=True)
    ee = jnp.where(emask, jnp.exp(el - emax), 0.0)
    ep = jnp.where(emask, ee / jnp.sum(ee, axis=1, keepdims=True), -1.0)
    p1 = jnp.max(ep, axis=1, keepdims=True)
    i1 = jnp.min(jnp.where(ep == p1, lane, big), axis=1, keepdims=True)
    ep2 = jnp.where(lane == i1, -1.0, ep)
    p2 = jnp.max(ep2, axis=1, keepdims=True)
    i2 = jnp.min(jnp.where(ep2 == p2, lane, big), axis=1, keepdims=True)
    tot = p1 + p2
    return (jnp.where(lane == i1, ggate * (p1 / tot), 0.0)
            + jnp.where(lane == i2, ggate * (p2 / tot), 0.0))


def _moe_kernel(x1_ref, h2_ref, rl_ref, w1_ref, w3_ref, w2_ref, fg_ref, o_ref, y_ref, comb_ref):
    step = pl.program_id(1)
    per_step = w1_ref.shape[0]

    @pl.when(step == 0)
    def _():
        comb_ref[...] = _routing_weights(rl_ref[...])
        y_ref[...] = jnp.zeros(y_ref.shape, F32)

    h = h2_ref[...]
    comb = comb_ref[...]
    lane = lax.broadcasted_iota(jnp.int32, comb.shape, 1)
    acts = []
    for j in range(per_step):
        a = _dot(h, w1_ref[j])
        b = _dot(h, w3_ref[j])
        ce = jnp.sum(jnp.where(lane == R_EXPERT + step * per_step + j, comb, 0.0), axis=1, keepdims=True)
        acts.append(((a * jax.nn.sigmoid(a)) * b * ce).astype(BF16))
    y_ref[...] += _dot(jnp.concatenate(acts, axis=1), w2_ref[...])

    @pl.when(step == pl.num_programs(1) - 1)
    def _():
        o_ref[...] = _rmsnorm_rows(x1_ref[...] + y_ref[...], fg_ref[...])


def _moe(x1, h2, rl, w1, w3, w2, fg, tm=1024, per_step=4):
    n = x1.shape[0]
    w2g = w2.reshape(N_EXPERTS // per_step, per_step * D_EXPERT, D_MODEL)
    return pl.pallas_call(
        _moe_kernel,
        grid=(n // tm, N_EXPERTS // per_step),
        in_specs=[
            pl.BlockSpec((tm, D_MODEL), lambda i, e: (i, 0)),
            pl.BlockSpec((tm, D_MODEL), lambda i, e: (i, 0)),
            pl.BlockSpec((tm, 128), lambda i, e: (i, 0)),
            pl.BlockSpec((per_step, D_MODEL, D_EXPERT), lambda i, e: (e, 0, 0)),
            pl.BlockSpec((per_step, D_MODEL, D_EXPERT), lambda i, e: (e, 0, 0)),
            pl.BlockSpec((None, per_step * D_EXPERT, D_MODEL), lambda i, e: (e, 0, 0)),
            pl.BlockSpec((1, D_MODEL), lambda i, e: (0, 0)),
        ],
        out_specs=pl.BlockSpec((tm, D_MODEL), lambda i, e: (i, 0)),
        out_shape=jax.ShapeDtypeStruct((n, D_MODEL), F32),
        scratch_shapes=[pltpu.VMEM((tm, D_MODEL), F32), pltpu.VMEM((tm, 128), F32)],
        compiler_params=pltpu.CompilerParams(dimension_semantics=("parallel", "arbitrary"),
                                             vmem_limit_bytes=VMEM_LIMIT),
        name="moe",
    )(x1, h2, rl, w1, w3, w2g, fg)


_W_OFF = {}
_off = 0
for _name, _n in (("a_q", 512), ("a_k", 512), ("a_v", 512), ("a_z", 512), ("a_beta", 8), ("a_alpha", 8),
                  ("b_q", 512), ("b_k", 128), ("b_v", 128), ("i_q", 512), ("i_k", 64), ("i_w", 8),
                  ("gate_a", 1024), ("gate_b", 1024)):
    _W_OFF[_name] = (_off, _off + _n)
    _off += _n


def _cols(w, name, lo=0, hi=None):
    a, b = _W_OFF[name]
    return w[:, a + lo:(b if hi is None else a + hi)]


def _hi_lo_pair(w_small):
    pad = jnp.zeros((w_small.shape[0], 128 - w_small.shape[1]), F32)
    w = jnp.concatenate([w_small.astype(F32), pad], axis=1)
    hi, lo = _split(w)
    return jnp.concatenate([hi, lo], axis=1)


def _layout_w_in(w):
    wb = w.astype(BF16)
    k0, k1 = _cols(wb, "b_k", 0, 64), _cols(wb, "b_k", 64, 128)
    v0, v1 = _cols(wb, "b_v", 0, 64), _cols(wb, "b_v", 64, 128)
    ik = _cols(wb, "i_k")
    main = jnp.concatenate([
        _cols(wb, "a_q"), _cols(wb, "a_k"), _cols(wb, "a_v"), _cols(wb, "a_z"), _cols(wb, "gate_a"),
        _cols(wb, "gate_b"), _cols(wb, "b_q"), _cols(wb, "i_q"), k0, k0, k1, k1, v0, v0, v1, v1, ik, ik],
        axis=1)
    small = _hi_lo_pair(jnp.concatenate([_cols(w, "a_beta"), _cols(w, "a_alpha"), _cols(w, "i_w")], axis=1))
    return main, small


def kernel(x, positions, norm1_g, w_in, b_gate, conv_w, a_log, dt_bias, a_norm_g, w_proj_a, w_proj_b, w_out,
           norm2_g, w_router_group, b_router_group, w_router_expert, b_router_expert, w_exp_gate, w_exp_up,
           w_exp_down, final_norm_g):
    b, s, d = x.shape
    n = b * s
    depth = w_in.shape[0]
    xc = x.reshape(n, d).astype(F32)
    for l in range(depth):
        w_main, w_small = _layout_w_in(w_in[l])
        z, zs = _in_proj(xc, norm1_g[l][None, :].astype(F32), w_main, w_small, positions)
        z3 = z.reshape(b, s, Z_W)
        zs3 = zs.reshape(b, s, 128)
        o_a = _gdn(z3, zs3, conv_w[l], a_log[l], dt_bias[l], a_norm_g[l])
        o_b = _dsa(z3, zs3)
        wr = _hi_lo_pair(jnp.concatenate([w_router_group[l], w_router_expert[l]], axis=1))
        br = jnp.concatenate([b_router_group[l], b_router_expert[l],
                              jnp.zeros((128 - N_GROUPS - N_EXPERTS,), F32)])[None, :].astype(F32)
        x1, h2, rl = _merge(xc, o_a.reshape(n, -1), o_b.reshape(n, -1), z, b_gate[l][None, :].astype(F32),
                            w_proj_a[l].astype(BF16), w_proj_b[l].astype(BF16), w_out[l].astype(BF16),
                            norm2_g[l][None, :].astype(F32), wr, br)
        last = l == depth - 1
        fg = final_norm_g[None, :].astype(F32) if last else None
        if not last:
            raise NotImplementedError("only the final layer fuses the output norm")
        xc = _moe(x1, h2, rl, w_exp_gate[l].astype(BF16), w_exp_up[l].astype(BF16),
                  w_exp_down[l].astype(BF16), fg)
    return xc.reshape(b, s, d).astype(x.dtype)
```

```python
import numpy as np
import jax
import jax.numpy as jnp
from jax import lax
from jax.experimental import pallas as pl
from jax.experimental.pallas import tpu as pltpu

F32 = jnp.float32
BF16 = jnp.bfloat16

D_MODEL = 1024
CHUNK = 64
EPS = 1e-6
ROPE_THETA = 10000.0
A_HEADS = 8
A_DK = 64
A_DV = 64
CONV_K = 4
B_HEADS = 8
B_KV_HEADS = 2
B_HD = 64
IDX_HEADS = 8
IDX_HD = 64
TOPK_MAX = 256
N_GROUPS = 4
EXPERTS_PER_GROUP = 4
N_EXPERTS = 16
D_EXPERT = 256

O_AQKV, O_AZ, O_GA, O_GB, O_BQ, O_IQ, O_KD, O_VD, O_IK = 0, 1536, 2048, 3072, 4096, 4608, 5120, 5376, 5632
Z_W = 5760
S_BETA, S_ALPHA, S_IW = 0, 8, 16

GROUP_HEADS = 2
BD = GROUP_HEADS * CHUNK
NEG_BIG = -1e30
LOG2E = 1.4426950408889634
BISECT_STEPS = 18
VMEM_LIMIT = 56 * 1024 * 1024


def _split(x):
    hi = x.astype(BF16)
    lo = (x - hi.astype(F32)).astype(BF16)
    return hi, lo


def _dot(a, b):
    return jnp.dot(a, b, preferred_element_type=F32)


def _mm(a, b):
    return _dot(a.astype(BF16), b.astype(BF16))


def _mm_nt(a, b):
    return lax.dot_general(a.astype(BF16), b.astype(BF16), (((1,), (1,)), ((), ())),
                           preferred_element_type=F32)


def _mm_exact_lhs(a_bf16, x):
    xh, xl = _split(x)
    return _dot(a_bf16, xh) + _dot(a_bf16, xl)


def _rmsnorm_rows(x, g):
    return x * lax.rsqrt(jnp.mean(x * x, axis=-1, keepdims=True) + EPS) * g


_Z_CHUNKS = tuple((o, min(512, Z_W - o)) for o in range(0, Z_W, 512))


def _rope(x, cs, sn, first):
    w = x.shape[1]
    rep = w // 128
    if rep > 1:
        cs, sn, first = (jnp.concatenate([a] * rep, axis=1) for a in (cs, sn, first))
    swapped = jnp.where(first, pltpu.roll(x, w - B_HD // 2, 1), pltpu.roll(x, B_HD // 2, 1))
    return x * cs + swapped * sn


def _in_proj_kernel(x_ref, g_ref, w_ref, ws_ref, pos_ref, inv_ref, z_ref, zs_ref):
    h = _rmsnorm_rows(x_ref[...], g_ref[...])
    hb, hl = _split(h)

    ang = pos_ref[...] * inv_ref[...]
    lane = lax.broadcasted_iota(jnp.int32, ang.shape, 1)
    first = (lane & (B_HD - 1)) < (B_HD // 2)
    cs = jnp.cos(ang)
    sn = jnp.sin(ang)
    sn = jnp.where(first, -sn, sn)

    for o, w in _Z_CHUNKS:
        r = _dot(hb, w_ref[:, o:o + w])
        if o == O_BQ:
            r = _rope(r, cs, sn, first) * (B_HD ** -0.5 * LOG2E)
        elif o == O_IQ or o == O_IK:
            r = _rope(r, cs, sn, first)
        elif o == O_KD:
            kw = O_VD - O_KD
            r = jnp.concatenate([_rope(r[:, :kw], cs, sn, first), r[:, kw:]], axis=1)
        z_ref[:, o:o + w] = r.astype(BF16)
    s1 = _dot(hb, ws_ref[...])
    s2 = _dot(hl, ws_ref[:, :128])
    zs_ref[...] = s1[:, :128] + s1[:, 128:] + s2


def _in_proj(x2d, g, w_main, w_small, positions, tm=512):
    n = x2d.shape[0]
    half = B_HD // 2
    inv = ROPE_THETA ** (-jnp.arange(half, dtype=F32) / half)
    inv128 = jnp.tile(inv, 4)[None, :]
    pos = positions.astype(F32).reshape(n, 1)
    return pl.pallas_call(
        _in_proj_kernel,
        grid=(n // tm,),
        in_specs=[
            pl.BlockSpec((tm, D_MODEL), lambda i: (i, 0)),
            pl.BlockSpec((1, D_MODEL), lambda i: (0, 0)),
            pl.BlockSpec((D_MODEL, Z_W), lambda i: (0, 0)),
            pl.BlockSpec((D_MODEL, 256), lambda i: (0, 0)),
            pl.BlockSpec((tm, 1), lambda i: (i, 0)),
            pl.BlockSpec((1, 128), lambda i: (0, 0)),
        ],
        out_specs=[
            pl.BlockSpec((tm, Z_W), lambda i: (i, 0)),
            pl.BlockSpec((tm, 128), lambda i: (i, 0)),
        ],
        out_shape=[jax.ShapeDtypeStruct((n, Z_W), BF16), jax.ShapeDtypeStruct((n, 128), F32)],
        compiler_params=pltpu.CompilerParams(dimension_semantics=("parallel",),
                                             vmem_limit_bytes=VMEM_LIMIT),
        name="in_proj",
    )(x2d, g, w_main, w_small, pos, inv128)


def _gdn_constants():
    r = np.arange(BD)
    same = (r[:, None] // CHUNK) == (r[None, :] // CHUNK)
    incl = same & (r[:, None] >= r[None, :])
    strict = same & (r[:, None] > r[None, :])
    eye = np.eye(BD, dtype=np.float32)
    ll = np.concatenate([incl, same], axis=0).astype(np.float32)
    n_groups = A_HEADS // GROUP_HEADS
    sel = np.zeros((2 * n_groups, BD, 128), np.float32)
    for gi in range(n_groups):
        for h in range(GROUP_HEADS):
            sel[gi * 2 + 0, h * CHUNK:(h + 1) * CHUNK, S_BETA + gi * GROUP_HEADS + h] = 1.0
            sel[gi * 2 + 1, h * CHUNK:(h + 1) * CHUNK, S_ALPHA + gi * GROUP_HEADS + h] = 1.0
    return (jnp.asarray(incl, F32), jnp.asarray(strict, F32), jnp.asarray(same, F32), jnp.asarray(eye),
            jnp.asarray(ll, BF16), jnp.asarray(sel))


def _tile_heads(x):
    return jnp.concatenate([x] * GROUP_HEADS, axis=0)


def _gdn_kernel(zq_ref, zz_ref, zs_ref, cw_ref, av_ref, ag_ref, incl_ref, strict_ref, bdm_ref, eye_ref,
                ll_ref, sel_ref, o_ref, ext_ref, st_ref):
    c = pl.program_id(1)
    nb, t = zq_ref.shape[0], zq_ref.shape[1]

    @pl.when(c == 0)
    def _():
        ext_ref[:, 0:8, :] = jnp.zeros((nb, 8, ext_ref.shape[2]), F32)
        st_ref[...] = jnp.zeros(st_ref.shape, F32)

    hw = A_HEADS * A_DK
    cw = cw_ref[...]
    av = av_ref[...]
    q_all, k_all, v_all, bg = [], [], [], []
    for bi in range(nb):
        ext_ref[bi, 8:8 + t, :] = zq_ref[bi].astype(F32)
        y = cw[0:1, :] * ext_ref[bi, pl.ds(8 - (CONV_K - 1), t), :]
        for j in range(1, CONV_K):
            y = y + cw[j:j + 1, :] * ext_ref[bi, pl.ds(8 - (CONV_K - 1) + j, t), :]
        ext_ref[bi, 0:8, :] = ext_ref[bi, t:t + 8, :]
        y = y * jax.nn.sigmoid(y)
        q_all.append(y[:, :hw])
        k_all.append(y[:, hw:2 * hw])
        v_all.append(y[:, 2 * hw:])

        sm = zs_ref[bi]
        lane = lax.broadcasted_iota(jnp.int32, sm.shape, 1)
        xg = sm + av[1:2, :]
        softplus = jnp.maximum(xg, 0.0) + jnp.log1p(jnp.exp(-jnp.abs(xg)))
        g_all = -jnp.exp(av[0:1, :]) * softplus
        bg.append(jnp.where(lane < S_ALPHA, jax.nn.sigmoid(sm), g_all))

    incl = incl_ref[...]
    strict = strict_ref[...]
    bdm = bdm_ref[...]
    eye = eye_ref[...]
    ll = ll_ref[...]
    incl_b = ll[:BD]

    n_chunks = t // CHUNK
    n_groups = A_HEADS // GROUP_HEADS
    chains = [(bi, ci, gi) for bi in range(nb) for ci in range(n_chunks) for gi in range(n_groups)]

    pre = {}
    for bi, ci, gi in chains:
        r0, c0 = ci * CHUNK, gi * BD
        bg4 = _tile_heads(bg[bi][r0:r0 + CHUNK])
        beta = jnp.sum(bg4 * sel_ref[gi * 2 + 0], axis=1, keepdims=True)
        gcol = jnp.sum(bg4 * sel_ref[gi * 2 + 1], axis=1, keepdims=True)
        gs = _mm_exact_lhs(ll, jnp.broadcast_to(gcol, (BD, 128)))
        g_cum = gs[:BD, :1]
        g_last = gs[BD:, :1]
        diff = _mm_exact_lhs(incl_b, gcol * strict)
        decay = jnp.where(incl > 0.0, jnp.exp(diff), 0.0)
        e_cum = jnp.exp(g_cum)
        kr = _tile_heads(k_all[bi][r0:r0 + CHUNK, c0:c0 + BD]) * bdm
        qr = _tile_heads(q_all[bi][r0:r0 + CHUNK, c0:c0 + BD]) * bdm
        vm = _tile_heads(v_all[bi][r0:r0 + CHUNK, c0:c0 + BD]) * bdm
        km = kr * lax.rsqrt(jnp.sum(kr * kr, axis=1, keepdims=True) + EPS)
        qm = qr * (lax.rsqrt(jnp.sum(qr * qr, axis=1, keepdims=True) + EPS) * (A_DK ** -0.5))
        kkqk = _mm_nt(jnp.concatenate([km, qm], axis=0), km)
        m = -(strict * beta * kkqk[:BD] * decay)
        pre[bi, ci, gi] = dict(beta=beta, g_cum=g_cum, g_last=g_last, e_cum=e_cum, km=km, qm=qm, vm=vm,
                               qk=kkqk[BD:] * decay, m=m, inv=eye + m)

    sq = CHUNK
    while sq > 2:
        for key in chains:
            p = pre[key]
            p["m"] = _mm(p["m"], p["m"])
        for key in chains:
            p = pre[key]
            p["inv"] = p["inv"] + _mm(p["inv"], p["m"])
        sq //= 2

    states = [st_ref[i] for i in range(nb * n_groups)]
    outs = [[] for _ in range(nb)]
    lanes = [(bi, gi) for bi in range(nb) for gi in range(n_groups)]
    for ci in range(n_chunks):
        ps = {k: pre[k[0], ci, k[1]] for k in lanes}
        sidx = {k: k[0] * n_groups + k[1] for k in lanes}
        kq_s = {k: _mm(jnp.concatenate([ps[k]["km"] * ps[k]["e_cum"], ps[k]["qm"] * ps[k]["e_cum"]], axis=0),
                       states[sidx[k]]) for k in lanes}
        v_new = {k: _mm(ps[k]["inv"], ps[k]["beta"] * (ps[k]["vm"] - kq_s[k][:BD])) for k in lanes}
        o_bd = {k: kq_s[k][BD:] + _mm(ps[k]["qk"], v_new[k]) for k in lanes}
        for k in lanes:
            p = ps[k]
            k_dec = p["km"] * jnp.exp(p["g_last"] - p["g_cum"])
            states[sidx[k]] = states[sidx[k]] * jnp.exp(p["g_last"]) + _mm(k_dec.T, v_new[k])
        for bi in range(nb):
            o_groups = []
            for gi in range(n_groups):
                ob = o_bd[bi, gi]
                ob = ob * lax.rsqrt(jnp.sum(ob * ob, axis=1, keepdims=True) * (1.0 / A_DV) + EPS)
                o_groups.append(sum(ob[h * CHUNK:(h + 1) * CHUNK] for h in range(GROUP_HEADS)))
            outs[bi].append(jnp.concatenate(o_groups, axis=1))

    for i, state in enumerate(states):
        st_ref[i] = state
    for bi in range(nb):
        o = outs[bi][0] if n_chunks == 1 else jnp.concatenate(outs[bi], axis=0)
        zz = zz_ref[bi].astype(F32)
        o_ref[bi] = (o * ag_ref[...] * (zz * jax.nn.sigmoid(zz))).astype(BF16)


def _gdn(z3, zs3, conv_w, a_log, dt_bias, a_norm_g, t=2 * CHUNK, nb=2):
    b, s, _ = z3.shape
    consts = _gdn_constants()
    av = jnp.zeros((2, 128), F32)
    av = av.at[0, S_ALPHA:S_ALPHA + A_HEADS].set(a_log.astype(F32))
    av = av.at[1, S_ALPHA:S_ALPHA + A_HEADS].set(dt_bias.astype(F32))
    ag = jnp.tile(a_norm_g.astype(F32), A_HEADS)[None, :]
    conv_cols = 2 * A_HEADS * A_DK + A_HEADS * A_DV

    def const_spec(a):
        nd = a.ndim
        return pl.BlockSpec(a.shape, lambda bi, ci, _n=nd: (0,) * _n)

    small_in = (conv_w.astype(F32), av, ag) + consts
    return pl.pallas_call(
        _gdn_kernel,
        grid=(b // nb, s // t),
        in_specs=[
            pl.BlockSpec((nb, t, conv_cols), lambda bi, ci: (bi, ci, O_AQKV // conv_cols)),
            pl.BlockSpec((nb, t, 512), lambda bi, ci: (bi, ci, O_AZ // 512)),
            pl.BlockSpec((nb, t, 128), lambda bi, ci: (bi, ci, 0)),
        ] + [const_spec(a) for a in small_in],
        out_specs=pl.BlockSpec((nb, t, A_HEADS * A_DV), lambda bi, ci: (bi, ci, 0)),
        out_shape=jax.ShapeDtypeStruct((b, s, A_HEADS * A_DV), BF16),
        scratch_shapes=[pltpu.VMEM((nb, 8 + t, conv_cols), F32),
                        pltpu.VMEM((nb * (A_HEADS // GROUP_HEADS), BD, BD), F32)],
        compiler_params=pltpu.CompilerParams(dimension_semantics=("parallel", "arbitrary"),
                                             vmem_limit_bytes=VMEM_LIMIT),
        name="gdn",
    )(z3, z3, zs3, *small_in)


def _dsa_kernel(topk, tk, q_ref, iq_ref, ka_ref, kb_ref, vd_ref, ik_ref, zs_ref, o_ref, sc_ref, st_ref):
    qb = pl.program_id(1)
    s = ka_ref.shape[0]
    tq = q_ref.shape[0]
    nkt_max = s // tk
    r0 = qb * tq
    n_tiles = (r0 + tq + tk - 1) // tk

    lane128 = lax.broadcasted_iota(jnp.int32, (tq, 128), 1)
    low_half = lane128 < B_HD
    high_half = lane128 >= B_HD

    def head_slab(x, h):
        slab = x[:, (h // 2) * 128:(h // 2 + 1) * 128]
        return jnp.where(low_half if h % 2 == 0 else high_half, slab, jnp.zeros_like(slab))

    sm = zs_ref[...]
    iw_scale = (IDX_HEADS ** -0.5) * (IDX_HD ** -0.5)
    wcols = [jnp.sum(jnp.where(lane128 == S_IW + h, sm, 0.0), axis=1, keepdims=True) * iw_scale
             for h in range(IDX_HEADS)]
    row = r0 + lax.broadcasted_iota(jnp.int32, (tq, 1), 0)
    limit = ((row >> 6) + 1) << 6

    iq = iq_ref[...]
    iq_heads = [head_slab(iq, h) for h in range(IDX_HEADS)]

    def index_tile(kt, carry):
        r_max, r_min = carry
        off = pl.multiple_of(kt * tk, tk)
        ikt = ik_ref[pl.ds(off, tk), :]
        acc = jnp.zeros((tq, tk), F32)
        for h in range(IDX_HEADS):
            sc = lax.dot_general(iq_heads[h], ikt, (((1,), (1,)), ((), ())), preferred_element_type=F32)
            acc = acc + wcols[h] * jnp.maximum(sc, 0.0)
        valid = off + lax.broadcasted_iota(jnp.int32, (tq, tk), 1) < limit
        masked = jnp.where(valid, acc, -jnp.inf)
        sc_ref[kt] = masked
        masked_t = masked.T
        st_ref[kt] = masked_t
        r_max = jnp.maximum(r_max, jnp.max(masked_t, axis=0, keepdims=True))
        r_min = jnp.minimum(r_min, jnp.min(acc.T, axis=0, keepdims=True))
        return r_max, r_min

    r_max, r_min = lax.fori_loop(0, n_tiles, index_tile,
                                 (jnp.full((1, tq), -jnp.inf, F32), jnp.full((1, tq), jnp.inf, F32)))
    qrow = r0 + lax.broadcasted_iota(jnp.int32, (1, tq), 1)
    small = (((qrow >> 6) + 1) << 6) <= topk

    all_kept = r0 + tq <= min(topk, tk)

    @pl.when(all_kept)
    def _():
        sc_ref[0] = jnp.where(sc_ref[0] > -jnp.inf, 0.0, NEG_BIG)

    for j in range(nkt_max):
        @pl.when(jnp.logical_and(n_tiles == j + 1, jnp.logical_not(all_kept)))
        def _(j=j):
            _topk_mask(topk, j + 1, small, r_max, r_min, sc_ref, st_ref)

    q = q_ref[...]
    heads_per_kv = B_HEADS // B_KV_HEADS
    nr = heads_per_kv * tq
    qrows = [jnp.concatenate([head_slab(q, g * heads_per_kv + j) for j in range(heads_per_kv)], axis=0)
             for g in range(B_KV_HEADS)]
    k_refs = (ka_ref, kb_ref)

    def attend_tile(kt, carry):
        off = pl.multiple_of(kt * tk, tk)
        bias = sc_ref[kt]
        out = []
        for g in range(B_KV_HEADS):
            m_run, l_run, acc = carry[g]
            logits = lax.dot_general(qrows[g], k_refs[g][pl.ds(off, tk), :], (((1,), (1,)), ((), ())),
                                     preferred_element_type=F32)
            logits = (logits.reshape(heads_per_kv, tq, tk) + bias[None]).reshape(nr, tk)
            m_new = jnp.maximum(m_run, jnp.max(logits, axis=1, keepdims=True))
            p = jnp.exp2(logits - m_new)
            alpha = jnp.exp2(m_run - m_new)
            l_new = alpha * l_run + jnp.sum(p, axis=1, keepdims=True)
            acc = alpha * acc + _dot(p.astype(BF16), vd_ref[pl.ds(off, tk), g * 128:(g + 1) * 128])
            out.append((m_new, l_new, acc))
        return tuple(out)

    init = tuple((jnp.full((nr, 1), NEG_BIG, F32), jnp.zeros((nr, 1), F32), jnp.zeros((nr, 128), F32))
                 for _ in range(B_KV_HEADS))
    fin = lax.fori_loop(0, n_tiles, attend_tile, init)
    for g in range(B_KV_HEADS):
        _, l_run, acc = fin[g]
        og = acc / l_run
        for pp in range(heads_per_kv // 2):
            even = og[(2 * pp) * tq:(2 * pp + 1) * tq]
            odd = og[(2 * pp + 1) * tq:(2 * pp + 2) * tq]
            col = (g * (heads_per_kv // 2) + pp) * 128
            o_ref[:, col:col + 128] = jnp.where(low_half, even, odd).astype(BF16)


def _topk_mask(topk, nt, small, r_max, r_min, sc_ref, st_ref):
    _, tq, tk = sc_ref.shape
    kf = float(topk)

    def tiles():
        return [st_ref[t] for t in range(nt)]

    def q_sum(pred):
        tot = None
        for t, x in enumerate(tiles()):
            c = jnp.sum(pred(x, t).astype(F32), axis=0, keepdims=True)
            tot = c if tot is None else tot + c
        return tot

    def q_max(val):
        best = None
        for t, x in enumerate(tiles()):
            c = jnp.max(val(x, t), axis=0, keepdims=True)
            best = c if best is None else jnp.maximum(best, c)
        return best

    hi0 = r_max + jnp.maximum(jnp.abs(r_max), 1e-30) * 1e-6

    def bisect(_, carry):
        lo, hi = carry
        mid = 0.5 * (lo + hi)
        ge = q_sum(lambda x, t: x >= mid) >= kf
        return jnp.where(ge, mid, lo), jnp.where(ge, hi, mid)

    lo, hi = lax.fori_loop(0, BISECT_STEPS, bisect, (r_min, hi0))

    def peel_cond(carry):
        return jnp.sum(1.0 - carry[0]) > 0.0

    def peel(carry):
        done, thr, hi_c, n_ge = carry
        v1 = q_max(lambda x, t: jnp.where(x < hi_c, x, -jnp.inf))
        c1 = q_sum(lambda x, t: x >= v1)
        ok = c1 >= kf
        act = done < 0.5
        thr = jnp.where(act & ok, v1, thr)
        n_ge = jnp.where(act & ok, c1, n_ge)
        hi_c = jnp.where(act & (~ok), v1, hi_c)
        return jnp.where(ok, 1.0, done), thr, hi_c, n_ge

    _, thr, _, n_ge = lax.while_loop(peel_cond, peel, (jnp.where(small, 1.0, 0.0), lo, hi, jnp.full_like(lo, kf)))

    contested = jnp.sum(jnp.where((n_ge > kf) & (~small), 1.0, 0.0)) > 0.0
    last = float(nt * tk - 1)

    def tie_cut():
        need = kf - q_sum(lambda x, t: x > thr)

        def key_index_t(t):
            return (t * tk + lax.broadcasted_iota(jnp.int32, (tk, tq), 0)).astype(F32)

        def tie_search(_, carry):
            jlo, jhi = carry
            mid = jnp.floor(0.5 * (jlo + jhi))
            ge = q_sum(lambda x, t: (x == thr) & (key_index_t(t) <= mid)) >= need
            return jnp.where(ge, jlo, mid), jnp.where(ge, mid, jhi)

        n_iter = int(np.ceil(np.log2(nt * tk))) + 1
        return lax.fori_loop(0, n_iter, tie_search,
                             (jnp.full((1, tq), -1.0, F32), jnp.full((1, tq), last, F32)))[1]

    jcut = lax.cond(contested, tie_cut, lambda: jnp.full((1, tq), last, F32))

    def as_column(row):
        return jnp.broadcast_to(row, (128, tq)).T[:, :1]

    thr_c = as_column(thr)
    jcut_c = as_column(jcut)
    keep_all = as_column(jnp.where(small, 1.0, 0.0)) > 0.5
    for t in range(nt):
        x = sc_ref[t]
        kidx = (t * tk + lax.broadcasted_iota(jnp.int32, (tq, tk), 1)).astype(F32)
        sel = (x > -jnp.inf) & (keep_all | (x > thr_c) | ((x == thr_c) & (kidx <= jcut_c)))
        sc_ref[t] = jnp.where(sel, 0.0, NEG_BIG)


def _dsa(z3, zs3, tq=256, tk=512):
    b, s, _ = z3.shape
    topk = min(TOPK_MAX, s // 4)
    kernel = lambda *refs: _dsa_kernel(topk, tk, *refs)
    qblock = lambda w, cb: pl.BlockSpec((None, tq, w), lambda bi, qi, _c=cb: (bi, qi, _c))
    keys = lambda w, cb: pl.BlockSpec((None, s, w), lambda bi, qi, _c=cb: (bi, 0, _c))
    return pl.pallas_call(
        kernel,
        grid=(b, s // tq),
        in_specs=[qblock(512, O_BQ // 512), qblock(512, O_IQ // 512), keys(128, O_KD // 128),
                  keys(128, O_KD // 128 + 1), keys(256, O_VD // 256), keys(128, O_IK // 128), qblock(128, 0)],
        out_specs=pl.BlockSpec((None, tq, B_HEADS * B_HD), lambda bi, qi: (bi, qi, 0)),
        out_shape=jax.ShapeDtypeStruct((b, s, B_HEADS * B_HD), BF16),
        scratch_shapes=[pltpu.VMEM((s // tk, tq, tk), F32), pltpu.VMEM((s // tk, tk, tq), F32)],
        compiler_params=pltpu.CompilerParams(dimension_semantics=("parallel", "arbitrary"),
                                             vmem_limit_bytes=VMEM_LIMIT),
        name="dsa",
    )(z3, z3, z3, z3, z3, z3, zs3)


def _merge_kernel(x_ref, oa_ref, ob_ref, ga_ref, gb_ref, bg_ref, wa_ref, wb_ref, wo_ref, n2_ref, wr_ref,
                  br_ref, x1_ref, h2_ref, rl_ref):
    pa = _dot(oa_ref[...], wa_ref[...])
    pb = _dot(ob_ref[...], wb_ref[...])
    bgv = bg_ref[...]
    ga = jax.nn.sigmoid(ga_ref[...].astype(F32) + bgv[:, :D_MODEL])
    gb = jax.nn.sigmoid(gb_ref[...].astype(F32) + bgv[:, D_MODEL:])
    merged = ga * pa + gb * pb
    x1 = x_ref[...] + _mm(merged, wo_ref[...])
    x1_ref[...] = x1
    h2 = _rmsnorm_rows(x1, n2_ref[...])
    hb, hl = _split(h2)
    h2_ref[...] = hb
    s1 = _dot(hb, wr_ref[...])
    s2 = _dot(hl, wr_ref[:, :128])
    rl_ref[...] = s1[:, :128] + s1[:, 128:] + s2 + br_ref[...]


def _merge(x2d, oa, ob, z2d, b_gate, wa, wb, wo, n2, wr, br, tm=512):
    n = x2d.shape[0]
    full = lambda shape: pl.BlockSpec(shape, lambda i: (0, 0))
    return pl.pallas_call(
        _merge_kernel,
        grid=(n // tm,),
        in_specs=[
            pl.BlockSpec((tm, D_MODEL), lambda i: (i, 0)),
            pl.BlockSpec((tm, 512), lambda i: (i, 0)),
            pl.BlockSpec((tm, 512), lambda i: (i, 0)),
            pl.BlockSpec((tm, D_MODEL), lambda i: (i, O_GA // D_MODEL)),
            pl.BlockSpec((tm, D_MODEL), lambda i: (i, O_GB // D_MODEL)),
            full((1, 2 * D_MODEL)), full((512, D_MODEL)), full((512, D_MODEL)), full((D_MODEL, D_MODEL)),
            full((1, D_MODEL)), full((D_MODEL, 256)), full((1, 128)),
        ],
        out_specs=[
            pl.BlockSpec((tm, D_MODEL), lambda i: (i, 0)),
            pl.BlockSpec((tm, D_MODEL), lambda i: (i, 0)),
            pl.BlockSpec((tm, 128), lambda i: (i, 0)),
        ],
        out_shape=[jax.ShapeDtypeStruct((n, D_MODEL), F32), jax.ShapeDtypeStruct((n, D_MODEL), BF16),
                   jax.ShapeDtypeStruct((n, 128), F32)],
        compiler_params=pltpu.CompilerParams(dimension_semantics=("parallel",),
                                             vmem_limit_bytes=VMEM_LIMIT),
        name="merge",
    )(x2d, oa, ob, z2d, z2d, b_gate, wa, wb, wo, n2, wr, br)


R_GROUP, R_EXPERT = 0, N_GROUPS


def _routing_weights(rl):
    lane = lax.broadcasted_iota(jnp.int32, rl.shape, 1)
    big = jnp.int32(1 << 20)
    gmask = lane < N_GROUPS
    gl = jnp.where(gmask, rl, -jnp.inf)
    gmax = jnp.max(gl, axis=1, keepdims=True)
    gsel = jnp.min(jnp.where(gmask & (rl == gmax), lane, big), axis=1, keepdims=True)
    ggate = 1.0 / jnp.sum(jnp.where(gmask, jnp.exp(gl - gmax), 0.0), axis=1, keepdims=True)
    e_lo = R_EXPERT + gsel * EXPERTS_PER_GROUP
    emask = (lane >= e_lo) & (lane < e_lo + EXPERTS_PER_GROUP)
    el = jnp.where(emask, rl, -jnp.inf)
    emax = jnp.max(el, axis=1, keepdims=True)
    ee = jnp.where(emask, jnp.exp(el - emax), 0.0)
    ep = jnp.where(emask, ee / jnp.sum(ee, axis=1, keepdims=True), -1.0)
    p1 = jnp.max(ep, axis=1, keepdims=True)
    i1 = jnp.min(jnp.where(ep == p1, lane, big), axis=1, keepdims=True)
    ep2 = jnp.where(lane == i1, -1.0, ep)
    p2 = jnp.max(ep2, axis=1, keepdims=True)
    i2 = jnp.min(jnp.where(ep2 == p2, lane, big), axis=1, keepdims=True)
    tot = p1 + p2
    return (jnp.where(lane == i1, ggate * (p1 / tot), 0.0)
            + jnp.where(lane == i2, ggate * (p2 / tot), 0.0))


def _moe_kernel(x1_ref, h2_ref, rl_ref, w1_ref, w3_ref, w2_ref, fg_ref, o_ref, y_ref, comb_ref):
    step = pl.program_id(1)
    per_step = w1_ref.shape[0]

    @pl.when(step == 0)
    def _():
        comb_ref[...] = _routing_weights(rl_ref[...])
        y_ref[...] = jnp.zeros(y_ref.shape, F32)

    h = h2_ref[...]
    comb = comb_ref[...]
    lane = lax.broadcasted_iota(jnp.int32, comb.shape, 1)
    acts = []
    for j in range(per_step):
        a = _dot(h, w1_ref[j])
        b = _dot(h, w3_ref[j])
        ce = jnp.sum(jnp.where(lane == R_EXPERT + step * per_step + j, comb, 0.0), axis=1, keepdims=True)
        acts.append(((a * jax.nn.sigmoid(a)) * b * ce).astype(BF16))
    y_ref[...] += _dot(jnp.concatenate(acts, axis=1), w2_ref[...])

    @pl.when(step == pl.num_programs(1) - 1)
    def _():
        o_ref[...] = _rmsnorm_rows(x1_ref[...] + y_ref[...], fg_ref[...])


def _moe(x1, h2, rl, w1, w3, w2, fg, tm=1024, per_step=4):
    n = x1.shape[0]
    w2g = w2.reshape(N_EXPERTS // per_step, per_step * D_EXPERT, D_MODEL)
    return pl.pallas_call(
        _moe_kernel,
        grid=(n // tm, N_EXPERTS // per_step),
        in_specs=[
            pl.BlockSpec((tm, D_MODEL), lambda i, e: (i, 0)),
            pl.BlockSpec((tm, D_MODEL), lambda i, e: (i, 0)),
            pl.BlockSpec((tm, 128), lambda i, e: (i, 0)),
            pl.BlockSpec((per_step, D_MODEL, D_EXPERT), lambda i, e: (e, 0, 0)),
            pl.BlockSpec((per_step, D_MODEL, D_EXPERT), lambda i, e: (e, 0, 0)),
            pl.BlockSpec((None, per_step * D_EXPERT, D_MODEL), lambda i, e: (e, 0, 0)),
            pl.BlockSpec((1, D_MODEL), lambda i, e: (0, 0)),
        ],
        out_specs=pl.BlockSpec((tm, D_MODEL), lambda i, e: (i, 0)),
        out_shape=jax.ShapeDtypeStruct((n, D_MODEL), F32),
        scratch_shapes=[pltpu.VMEM((tm, D_MODEL), F32), pltpu.VMEM((tm, 128), F32)],
        compiler_params=pltpu.CompilerParams(dimension_semantics=("parallel", "arbitrary"),
                                             vmem_limit_bytes=VMEM_LIMIT),
        name="moe",
    )(x1, h2, rl, w1, w3, w2g, fg)


_W_OFF = {}
_off = 0
for _name, _n in (("a_q", 512), ("a_k", 512), ("a_v", 512), ("a_z", 512), ("a_beta", 8), ("a_alpha", 8),
                  ("b_q", 512), ("b_k", 128), ("b_v", 128), ("i_q", 512), ("i_k", 64), ("i_w", 8),
                  ("gate_a", 1024), ("gate_b", 1024)):
    _W_OFF[_name] = (_off, _off + _n)
    _off += _n


def _cols(w, name, lo=0, hi=None):
    a, b = _W_OFF[name]
    return w[:, a + lo:(b if hi is None else a + hi)]


def _hi_lo_pair(w_small):
    pad = jnp.zeros((w_small.shape[0], 128 - w_small.shape[1]), F32)
    w = jnp.concatenate([w_small.astype(F32), pad], axis=1)
    hi, lo = _split(w)
    return jnp.concatenate([hi, lo], axis=1)


def _layout_w_in(w):
    wb = w.astype(BF16)
    k0, k1 = _cols(wb, "b_k", 0, 64), _cols(wb, "b_k", 64, 128)
    v0, v1 = _cols(wb, "b_v", 0, 64), _cols(wb, "b_v", 64, 128)
    ik = _cols(wb, "i_k")
    main = jnp.concatenate([
        _cols(wb, "a_q"), _cols(wb, "a_k"), _cols(wb, "a_v"), _cols(wb, "a_z"), _cols(wb, "gate_a"),
        _cols(wb, "gate_b"), _cols(wb, "b_q"), _cols(wb, "i_q"), k0, k0, k1, k1, v0, v0, v1, v1, ik, ik],
        axis=1)
    small = _hi_lo_pair(jnp.concatenate([_cols(w, "a_beta"), _cols(w, "a_alpha"), _cols(w, "i_w")], axis=1))
    return main, small


def kernel(x, positions, norm1_g, w_in, b_gate, conv_w, a_log, dt_bias, a_norm_g, w_proj_a, w_proj_b, w_out,
           norm2_g, w_router_group, b_router_group, w_router_expert, b_router_expert, w_exp_gate, w_exp_up,
           w_exp_down, final_norm_g):
    b, s, d = x.shape
    n = b * s
    depth = w_in.shape[0]
    xc = x.reshape(n, d).astype(F32)
    for l in range(depth):
        w_main, w_small = _layout_w_in(w_in[l])
        z, zs = _in_proj(xc, norm1_g[l][None, :].astype(F32), w_main, w_small, positions)
        z3 = z.reshape(b, s, Z_W)
        zs3 = zs.reshape(b, s, 128)
        o_a = _gdn(z3, zs3, conv_w[l], a_log[l], dt_bias[l], a_norm_g[l])
        o_b = _dsa(z3, zs3)
        wr = _hi_lo_pair(jnp.concatenate([w_router_group[l], w_router_expert[l]], axis=1))
        br = jnp.concatenate([b_router_group[l], b_router_expert[l],
                              jnp.zeros((128 - N_GROUPS - N_EXPERTS,), F32)])[None, :].astype(F32)
        x1, h2, rl = _merge(xc, o_a.reshape(n, -1), o_b.reshape(n, -1), z, b_gate[l][None, :].astype(F32),
                            w_proj_a[l].astype(BF16), w_proj_b[l].astype(BF16), w_out[l].astype(BF16),
                            norm2_g[l][None, :].astype(F32), wr, br)
        last = l == depth - 1
        fg = final_norm_g[None, :].astype(F32) if last else None
        if not last:
            raise NotImplementedError("only the final layer fuses the output norm")
        xc = _moe(x1, h2, rl, w_exp_gate[l].astype(BF16), w_exp_up[l].astype(BF16),
                  w_exp_down[l].astype(BF16), fg)
    return xc.reshape(b, s, d).astype(x.dtype)
```

```python
import numpy as np
import jax
import jax.numpy as jnp
from jax import lax
from jax.experimental import pallas as pl
from jax.experimental.pallas import tpu as pltpu

F32 = jnp.float32
BF16 = jnp.bfloat16

D_MODEL = 1024
CHUNK = 64
EPS = 1e-6
ROPE_THETA = 10000.0
A_HEADS = 8
A_DK = 64
A_DV = 64
CONV_K = 4
B_HEADS = 8
B_KV_HEADS = 2
B_HD = 64
IDX_HEADS = 8
IDX_HD = 64
TOPK_MAX = 256
N_GROUPS = 4
EXPERTS_PER_GROUP = 4
N_EXPERTS = 16
D_EXPERT = 256

O_AQKV, O_AZ, O_GA, O_GB, O_BQ, O_IQ, O_KD, O_VD, O_IK = 0, 1536, 2048, 3072, 4096, 4608, 5120, 5376, 5632
Z_W = 5760
S_BETA, S_ALPHA, S_IW = 0, 8, 16

GROUP_HEADS = 2
BD = GROUP_HEADS * CHUNK
NEG_BIG = -1e30
LOG2E = 1.4426950408889634
BISECT_STEPS = 18
VMEM_LIMIT = 56 * 1024 * 1024


def _split(x):
    hi = x.astype(BF16)
    lo = (x - hi.astype(F32)).astype(BF16)
    return hi, lo


def _dot(a, b):
    return jnp.dot(a, b, preferred_element_type=F32)


def _mm(a, b):
    return _dot(a.astype(BF16), b.astype(BF16))


def _mm_nt(a, b):
    return lax.dot_general(a.astype(BF16), b.astype(BF16), (((1,), (1,)), ((), ())),
                           preferred_element_type=F32)


def _mm_exact_lhs(a_bf16, x):
    xh, xl = _split(x)
    return _dot(a_bf16, xh) + _dot(a_bf16, xl)


def _rmsnorm_rows(x, g):
    return x * lax.rsqrt(jnp.mean(x * x, axis=-1, keepdims=True) + EPS) * g


_Z_CHUNKS = tuple((o, min(512, Z_W - o)) for o in range(0, Z_W, 512))


def _rope(x, cs, sn, first):
    w = x.shape[1]
    rep = w // 128
    if rep > 1:
        cs, sn, first = (jnp.concatenate([a] * rep, axis=1) for a in (cs, sn, first))
    swapped = jnp.where(first, pltpu.roll(x, w - B_HD // 2, 1), pltpu.roll(x, B_HD // 2, 1))
    return x * cs + swapped * sn


def _in_proj_kernel(x_ref, g_ref, w_ref, ws_ref, pos_ref, inv_ref, z_ref, zs_ref):
    h = _rmsnorm_rows(x_ref[...], g_ref[...])
    hb, hl = _split(h)

    ang = pos_ref[...] * inv_ref[...]
    lane = lax.broadcasted_iota(jnp.int32, ang.shape, 1)
    first = (lane & (B_HD - 1)) < (B_HD // 2)
    cs = jnp.cos(ang)
    sn = jnp.sin(ang)
    sn = jnp.where(first, -sn, sn)

    for o, w in _Z_CHUNKS:
        r = _dot(hb, w_ref[:, o:o + w])
        if o == O_BQ:
            r = _rope(r, cs, sn, first) * (B_HD ** -0.5 * LOG2E)
        elif o == O_IQ or o == O_IK:
            r = _rope(r, cs, sn, first)
        elif o == O_KD:
            kw = O_VD - O_KD
            r = jnp.concatenate([_rope(r[:, :kw], cs, sn, first), r[:, kw:]], axis=1)
        z_ref[:, o:o + w] = r.astype(BF16)
    s1 = _dot(hb, ws_ref[...])
    s2 = _dot(hl, ws_ref[:, :128])
    zs_ref[...] = s1[:, :128] + s1[:, 128:] + s2


def _in_proj(x2d, g, w_main, w_small, positions, tm=512):
    n = x2d.shape[0]
    half = B_HD // 2
    inv = ROPE_THETA ** (-jnp.arange(half, dtype=F32) / half)
    inv128 = jnp.tile(inv, 4)[None, :]
    pos = positions.astype(F32).reshape(n, 1)
    return pl.pallas_call(
        _in_proj_kernel,
        grid=(n // tm,),
        in_specs=[
            pl.BlockSpec((tm, D_MODEL), lambda i: (i, 0)),
            pl.BlockSpec((1, D_MODEL), lambda i: (0, 0)),
            pl.BlockSpec((D_MODEL, Z_W), lambda i: (0, 0)),
            pl.BlockSpec((D_MODEL, 256), lambda i: (0, 0)),
            pl.BlockSpec((tm, 1), lambda i: (i, 0)),
            pl.BlockSpec((1, 128), lambda i: (0, 0)),
        ],
        out_specs=[
            pl.BlockSpec((tm, Z_W), lambda i: (i, 0)),
            pl.BlockSpec((tm, 128), lambda i: (i, 0)),
        ],
        out_shape=[jax.ShapeDtypeStruct((n, Z_W), BF16), jax.ShapeDtypeStruct((n, 128), F32)],
        compiler_params=pltpu.CompilerParams(dimension_semantics=("parallel",),
                                             vmem_limit_bytes=VMEM_LIMIT),
        name="in_proj",
    )(x2d, g, w_main, w_small, pos, inv128)


def _gdn_constants():
    r = np.arange(BD)
    same = (r[:, None] // CHUNK) == (r[None, :] // CHUNK)
    incl = same & (r[:, None] >= r[None, :])
    strict = same & (r[:, None] > r[None, :])
    eye = np.eye(BD, dtype=np.float32)
    ll = np.concatenate([incl, same], axis=0).astype(np.float32)
    n_groups = A_HEADS // GROUP_HEADS
    sel = np.zeros((2 * n_groups, BD, 128), np.float32)
    for gi in range(n_groups):
        for h in range(GROUP_HEADS):
            sel[gi * 2 + 0, h * CHUNK:(h + 1) * CHUNK, S_BETA + gi * GROUP_HEADS + h] = 1.0
            sel[gi * 2 + 1, h * CHUNK:(h + 1) * CHUNK, S_ALPHA + gi * GROUP_HEADS + h] = 1.0
    return (jnp.asarray(incl, F32), jnp.asarray(strict, F32), jnp.asarray(same, F32), jnp.asarray(eye),
            jnp.asarray(ll, BF16), jnp.asarray(sel))


def _tile_heads(x):
    return jnp.concatenate([x] * GROUP_HEADS, axis=0)


def _gdn_kernel(zq_ref, zz_ref, zs_ref, cw_ref, av_ref, ag_ref, incl_ref, strict_ref, bdm_ref, eye_ref,
                ll_ref, sel_ref, o_ref, ext_ref, st_ref):
    c = pl.program_id(1)
    nb, t = zq_ref.shape[0], zq_ref.shape[1]

    @pl.when(c == 0)
    def _():
        ext_ref[:, 0:8, :] = jnp.zeros((nb, 8, ext_ref.shape[2]), F32)
        st_ref[...] = jnp.zeros(st_ref.shape, F32)

    hw = A_HEADS * A_DK
    cw = cw_ref[...]
    av = av_ref[...]
    q_all, k_all, v_all, bg = [], [], [], []
    for bi in range(nb):
        ext_ref[bi, 8:8 + t, :] = zq_ref[bi].astype(F32)
        y = cw[0:1, :] * ext_ref[bi, pl.ds(8 - (CONV_K - 1), t), :]
        for j in range(1, CONV_K):
            y = y + cw[j:j + 1, :] * ext_ref[bi, pl.ds(8 - (CONV_K - 1) + j, t), :]
        ext_ref[bi, 0:8, :] = ext_ref[bi, t:t + 8, :]
        y = y * jax.nn.sigmoid(y)
        q_all.append(y[:, :hw])
        k_all.append(y[:, hw:2 * hw])
        v_all.append(y[:, 2 * hw:])

        sm = zs_ref[bi]
        lane = lax.broadcasted_iota(jnp.int32, sm.shape, 1)
        xg = sm + av[1:2, :]
        softplus = jnp.maximum(xg, 0.0) + jnp.log1p(jnp.exp(-jnp.abs(xg)))
        g_all = -jnp.exp(av[0:1, :]) * softplus
        bg.append(jnp.where(lane < S_ALPHA, jax.nn.sigmoid(sm), g_all))

    incl = incl_ref[...]
    strict = strict_ref[...]
    bdm = bdm_ref[...]
    eye = eye_ref[...]
    ll = ll_ref[...]
    incl_b = ll[:BD]

    n_chunks = t // CHUNK
    n_groups = A_HEADS // GROUP_HEADS
    chains = [(bi, ci, gi) for bi in range(nb) for ci in range(n_chunks) for gi in range(n_groups)]

    pre = {}
    for bi, ci, gi in chains:
        r0, c0 = ci * CHUNK, gi * BD
        bg4 = _tile_heads(bg[bi][r0:r0 + CHUNK])
        beta = jnp.sum(bg4 * sel_ref[gi * 2 + 0], axis=1, keepdims=True)
        gcol = jnp.sum(bg4 * sel_ref[gi * 2 + 1], axis=1, keepdims=True)
        gs = _mm_exact_lhs(ll, jnp.broadcast_to(gcol, (BD, 128)))
        g_cum = gs[:BD, :1]
        g_last = gs[BD:, :1]
        diff = _mm_exact_lhs(incl_b, gcol * strict)
        decay = jnp.where(incl > 0.0, jnp.exp(diff), 0.0)
        e_cum = jnp.exp(g_cum)
        kr = _tile_heads(k_all[bi][r0:r0 + CHUNK, c0:c0 + BD]) * bdm
        qr = _tile_heads(q_all[bi][r0:r0 + CHUNK, c0:c0 + BD]) * bdm
        vm = _tile_heads(v_all[bi][r0:r0 + CHUNK, c0:c0 + BD]) * bdm
        km = kr * lax.rsqrt(jnp.sum(kr * kr, axis=1, keepdims=True) + EPS)
        qm = qr * (lax.rsqrt(jnp.sum(qr * qr, axis=1, keepdims=True) + EPS) * (A_DK ** -0.5))
        kkqk = _mm_nt(jnp.concatenate([km, qm], axis=0), km)
        m = -(strict * beta * kkqk[:BD] * decay)
        pre[bi, ci, gi] = dict(beta=beta, g_cum=g_cum, g_last=g_last, e_cum=e_cum, km=km, qm=qm, vm=vm,
                               qk=kkqk[BD:] * decay, m=m, inv=eye + m)

    sq = CHUNK
    while sq > 2:
        for key in chains:
            p = pre[key]
            p["m"] = _mm(p["m"], p["m"])
        for key in chains:
            p = pre[key]
            p["inv"] = p["inv"] + _mm(p["inv"], p["m"])
        sq //= 2

    states = [st_ref[i] for i in range(nb * n_groups)]
    outs = [[] for _ in range(nb)]
    lanes = [(bi, gi) for bi in range(nb) for gi in range(n_groups)]
    for ci in range(n_chunks):
        ps = {k: pre[k[0], ci, k[1]] for k in lanes}
        sidx = {k: k[0] * n_groups + k[1] for k in lanes}
        kq_s = {k: _mm(jnp.concatenate([ps[k]["km"] * ps[k]["e_cum"], ps[k]["qm"] * ps[k]["e_cum"]], axis=0),
                       states[sidx[k]]) for k in lanes}
        v_new = {k: _mm(ps[k]["inv"], ps[k]["beta"] * (ps[k]["vm"] - kq_s[k][:BD])) for k in lanes}
        o_bd = {k: kq_s[k][BD:] + _mm(ps[k]["qk"], v_new[k]) for k in lanes}
        for k in lanes:
            p = ps[k]
            k_dec = p["km"] * jnp.exp(p["g_last"] - p["g_cum"])
            states[sidx[k]] = states[sidx[k]] * jnp.exp(p["g_last"]) + _mm(k_dec.T, v_new[k])
        for bi in range(nb):
            o_groups = []
            for gi in range(n_groups):
                ob = o_bd[bi, gi]
                ob = ob * lax.rsqrt(jnp.sum(ob * ob, axis=1, keepdims=True) * (1.0 / A_DV) + EPS)
                o_groups.append(sum(ob[h * CHUNK:(h + 1) * CHUNK] for h in range(GROUP_HEADS)))
            outs[bi].append(jnp.concatenate(o_groups, axis=1))

    for i, state in enumerate(states):
        st_ref[i] = state
    for bi in range(nb):
        o = outs[bi][0] if n_chunks == 1 else jnp.concatenate(outs[bi], axis=0)
        zz = zz_ref[bi].astype(F32)
        o_ref[bi] = (o * ag_ref[...] * (zz * jax.nn.sigmoid(zz))).astype(BF16)


def _gdn(z3, zs3, conv_w, a_log, dt_bias, a_norm_g, t=2 * CHUNK, nb=2):
    b, s, _ = z3.shape
    consts = _gdn_constants()
    av = jnp.zeros((2, 128), F32)
    av = av.at[0, S_ALPHA:S_ALPHA + A_HEADS].set(a_log.astype(F32))
    av = av.at[1, S_ALPHA:S_ALPHA + A_HEADS].set(dt_bias.astype(F32))
    ag = jnp.tile(a_norm_g.astype(F32), A_HEADS)[None, :]
    conv_cols = 2 * A_HEADS * A_DK + A_HEADS * A_DV

    def const_spec(a):
        nd = a.ndim
        return pl.BlockSpec(a.shape, lambda bi, ci, _n=nd: (0,) * _n)

    small_in = (conv_w.astype(F32), av, ag) + consts
    return pl.pallas_call(
        _gdn_kernel,
        grid=(b // nb, s // t),
        in_specs=[
            pl.BlockSpec((nb, t, conv_cols), lambda bi, ci: (bi, ci, O_AQKV // conv_cols)),
            pl.BlockSpec((nb, t, 512), lambda bi, ci: (bi, ci, O_AZ // 512)),
            pl.BlockSpec((nb, t, 128), lambda bi, ci: (bi, ci, 0)),
        ] + [const_spec(a) for a in small_in],
        out_specs=pl.BlockSpec((nb, t, A_HEADS * A_DV), lambda bi, ci: (bi, ci, 0)),
        out_shape=jax.ShapeDtypeStruct((b, s, A_HEADS * A_DV), BF16),
        scratch_shapes=[pltpu.VMEM((nb, 8 + t, conv_cols), F32),
                        pltpu.VMEM((nb * (A_HEADS // GROUP_HEADS), BD, BD), F32)],
        compiler_params=pltpu.CompilerParams(dimension_semantics=("parallel", "arbitrary"),
                                             vmem_limit_bytes=VMEM_LIMIT),
        name="gdn",
    )(z3, z3, zs3, *small_in)


def _dsa_kernel(topk, tk, q_ref, iq_ref, ka_ref, kb_ref, vd_ref, ik_ref, zs_ref, o_ref, st_ref, vt_ref):
    qb = pl.program_id(1)
    s = ka_ref.shape[0]
    tq = q_ref.shape[0]
    nkt_max = s // tk
    r0 = qb * tq
    n_tiles = (r0 + tq + tk - 1) // tk
    heads_per_kv = B_HEADS // B_KV_HEADS

    @pl.when(qb == 0)
    def _():
        for t in range(nkt_max):
            for g in range(B_KV_HEADS):
                v_tile = vd_ref[t * tk:(t + 1) * tk, g * 128:(g + 1) * 128]
                vt_ref[t, g] = v_tile.astype(F32).T.astype(BF16)

    lane128 = lax.broadcasted_iota(jnp.int32, (tq, 128), 1)
    low_half = lane128 < B_HD
    high_half = lane128 >= B_HD

    def head_slab(x, h):
        slab = x[:, (h // 2) * 128:(h // 2 + 1) * 128]
        return jnp.where(low_half if h % 2 == 0 else high_half, slab, jnp.zeros_like(slab))

    zs_t = zs_ref[...].T
    iw_scale = (IDX_HEADS ** -0.5) * (IDX_HD ** -0.5)
    wrows = [zs_t[S_IW + h:S_IW + h + 1, :] * iw_scale for h in range(IDX_HEADS)]
    qrow = r0 + lax.broadcasted_iota(jnp.int32, (1, tq), 1)
    limit = ((qrow >> 6) + 1) << 6
    small = limit <= topk

    iq = iq_ref[...]
    iq_heads = [head_slab(iq, h) for h in range(IDX_HEADS)]

    def index_tile(kt, carry):
        r_max, r_min = carry
        off = pl.multiple_of(kt * tk, tk)
        ikt = ik_ref[pl.ds(off, tk), :]
        acc = jnp.zeros((tk, tq), F32)
        for h in range(IDX_HEADS):
            sc = lax.dot_general(ikt, iq_heads[h], (((1,), (1,)), ((), ())), preferred_element_type=F32)
            acc = acc + wrows[h] * jnp.maximum(sc, 0.0)
        valid = off + lax.broadcasted_iota(jnp.int32, (tk, tq), 0) < limit
        masked = jnp.where(valid, acc, -jnp.inf)
        st_ref[kt] = masked
        r_max = jnp.maximum(r_max, jnp.max(masked, axis=0, keepdims=True))
        r_min = jnp.minimum(r_min, jnp.min(acc, axis=0, keepdims=True))
        return r_max, r_min

    r_max, r_min = lax.fori_loop(0, n_tiles, index_tile,
                                 (jnp.full((1, tq), -jnp.inf, F32), jnp.full((1, tq), jnp.inf, F32)))

    all_kept = r0 + tq <= min(topk, tk)

    @pl.when(all_kept)
    def _():
        st_ref[0] = jnp.where(st_ref[0] > -jnp.inf, 0.0, NEG_BIG)

    for j in range(nkt_max):
        @pl.when(jnp.logical_and(n_tiles == j + 1, jnp.logical_not(all_kept)))
        def _(j=j):
            _topk_mask(topk, j + 1, small, r_max, r_min, st_ref)

    q = q_ref[...]
    nr = heads_per_kv * tq
    qcols = [jnp.concatenate([head_slab(q, g * heads_per_kv + j) for j in range(heads_per_kv)], axis=0)
             for g in range(B_KV_HEADS)]
    k_refs = (ka_ref, kb_ref)

    def attend_tile(kt, carry):
        off = pl.multiple_of(kt * tk, tk)
        bias = st_ref[kt]
        bias4 = jnp.concatenate([bias] * heads_per_kv, axis=1)
        out = []
        for g in range(B_KV_HEADS):
            m_run, l_run, acc = carry[g]
            logits = lax.dot_general(k_refs[g][pl.ds(off, tk), :], qcols[g], (((1,), (1,)), ((), ())),
                                     preferred_element_type=F32) + bias4
            m_new = jnp.maximum(m_run, jnp.max(logits, axis=0, keepdims=True))
            p = jnp.exp2(logits - m_new)
            alpha = jnp.exp2(m_run - m_new)
            l_new = alpha * l_run + jnp.sum(p, axis=0, keepdims=True)
            acc = alpha * acc + _dot(vt_ref[kt, g], p.astype(BF16))
            out.append((m_new, l_new, acc))
        return tuple(out)

    init = tuple((jnp.full((1, nr), NEG_BIG, F32), jnp.zeros((1, nr), F32), jnp.zeros((128, nr), F32))
                 for _ in range(B_KV_HEADS))
    fin = lax.fori_loop(0, n_tiles, attend_tile, init)
    for g in range(B_KV_HEADS):
        _, l_run, acc = fin[g]
        og = acc / l_run
        heads = [og[:, j * tq:(j + 1) * tq].T for j in range(heads_per_kv)]
        for pp in range(heads_per_kv // 2):
            col = (g * (heads_per_kv // 2) + pp) * 128
            o_ref[:, col:col + 128] = jnp.where(low_half, heads[2 * pp], heads[2 * pp + 1]).astype(BF16)


def _topk_mask(topk, nt, small, r_max, r_min, st_ref):
    _, tk, tq = st_ref.shape
    kf = float(topk)

    def tiles():
        return [st_ref[t] for t in range(nt)]

    def q_sum(pred):
        tot = None
        for t, x in enumerate(tiles()):
            c = jnp.sum(pred(x, t).astype(F32), axis=0, keepdims=True)
            tot = c if tot is None else tot + c
        return tot

    def q_max(val):
        best = None
        for t, x in enumerate(tiles()):
            c = jnp.max(val(x, t), axis=0, keepdims=True)
            best = c if best is None else jnp.maximum(best, c)
        return best

    hi0 = r_max + jnp.maximum(jnp.abs(r_max), 1e-30) * 1e-6

    def bisect(_, carry):
        lo, hi = carry
        mid = 0.5 * (lo + hi)
        ge = q_sum(lambda x, t: x >= mid) >= kf
        return jnp.where(ge, mid, lo), jnp.where(ge, hi, mid)

    lo, hi = lax.fori_loop(0, BISECT_STEPS, bisect, (r_min, hi0))

    def peel_cond(carry):
        return jnp.sum(1.0 - carry[0]) > 0.0

    def peel(carry):
        done, thr, hi_c, n_ge = carry
        v1 = q_max(lambda x, t: jnp.where(x < hi_c, x, -jnp.inf))
        c1 = q_sum(lambda x, t: x >= v1)
        ok = c1 >= kf
        act = done < 0.5
        thr = jnp.where(act & ok, v1, thr)
        n_ge = jnp.where(act & ok, c1, n_ge)
        hi_c = jnp.where(act & (~ok), v1, hi_c)
        return jnp.where(ok, 1.0, done), thr, hi_c, n_ge

    _, thr, _, n_ge = lax.while_loop(peel_cond, peel, (jnp.where(small, 1.0, 0.0), lo, hi, jnp.full_like(lo, kf)))

    def key_index(t):
        return (t * tk + lax.broadcasted_iota(jnp.int32, (tk, tq), 0)).astype(F32)

    contested = jnp.sum(jnp.where((n_ge > kf) & (~small), 1.0, 0.0)) > 0.0
    last = float(nt * tk - 1)

    def tie_cut():
        need = kf - q_sum(lambda x, t: x > thr)

        def tie_search(_, carry):
            jlo, jhi = carry
            mid = jnp.floor(0.5 * (jlo + jhi))
            ge = q_sum(lambda x, t: (x == thr) & (key_index(t) <= mid)) >= need
            return jnp.where(ge, jlo, mid), jnp.where(ge, mid, jhi)

        n_iter = int(np.ceil(np.log2(nt * tk))) + 1
        return lax.fori_loop(0, n_iter, tie_search,
                             (jnp.full((1, tq), -1.0, F32), jnp.full((1, tq), last, F32)))[1]

    jcut = lax.cond(contested, tie_cut, lambda: jnp.full((1, tq), last, F32))
    for t, x in enumerate(tiles()):
        sel = (x > -jnp.inf) & (small | (x > thr) | ((x == thr) & (key_index(t) <= jcut)))
        st_ref[t] = jnp.where(sel, 0.0, NEG_BIG)


def _dsa(z3, zs3, tq=256, tk=512):
    b, s, _ = z3.shape
    topk = min(TOPK_MAX, s // 4)
    kernel = lambda *refs: _dsa_kernel(topk, tk, *refs)
    qblock = lambda w, cb: pl.BlockSpec((None, tq, w), lambda bi, qi, _c=cb: (bi, qi, _c))
    keys = lambda w, cb: pl.BlockSpec((None, s, w), lambda bi, qi, _c=cb: (bi, 0, _c))
    return pl.pallas_call(
        kernel,
        grid=(b, s // tq),
        in_specs=[qblock(512, O_BQ // 512), qblock(512, O_IQ // 512), keys(128, O_KD // 128),
                  keys(128, O_KD // 128 + 1), keys(256, O_VD // 256), keys(128, O_IK // 128), qblock(128, 0)],
        out_specs=pl.BlockSpec((None, tq, B_HEADS * B_HD), lambda bi, qi: (bi, qi, 0)),
        out_shape=jax.ShapeDtypeStruct((b, s, B_HEADS * B_HD), BF16),
        scratch_shapes=[pltpu.VMEM((s // tk, tk, tq), F32),
                        pltpu.VMEM((s // tk, B_KV_HEADS, 128, tk), BF16)],
        compiler_params=pltpu.CompilerParams(dimension_semantics=("parallel", "arbitrary"),
                                             vmem_limit_bytes=VMEM_LIMIT),
        name="dsa",
    )(z3, z3, z3, z3, z3, z3, zs3)


def _merge_kernel(x_ref, oa_ref, ob_ref, ga_ref, gb_ref, bg_ref, wa_ref, wb_ref, wo_ref, n2_ref, wr_ref,
                  br_ref, x1_ref, h2_ref, rl_ref):
    pa = _dot(oa_ref[...], wa_ref[...])
    pb = _dot(ob_ref[...], wb_ref[...])
    bgv = bg_ref[...]
    ga = jax.nn.sigmoid(ga_ref[...].astype(F32) + bgv[:, :D_MODEL])
    gb = jax.nn.sigmoid(gb_ref[...].astype(F32) + bgv[:, D_MODEL:])
    merged = ga * pa + gb * pb
    x1 = x_ref[...] + _mm(merged, wo_ref[...])
    x1_ref[...] = x1
    h2 = _rmsnorm_rows(x1, n2_ref[...])
    hb, hl = _split(h2)
    h2_ref[...] = hb
    s1 = _dot(hb, wr_ref[...])
    s2 = _dot(hl, wr_ref[:, :128])
    rl_ref[...] = s1[:, :128] + s1[:, 128:] + s2 + br_ref[...]


def _merge(x2d, oa, ob, z2d, b_gate, wa, wb, wo, n2, wr, br, tm=512):
    n = x2d.shape[0]
    full = lambda shape: pl.BlockSpec(shape, lambda i: (0, 0))
    return pl.pallas_call(
        _merge_kernel,
        grid=(n // tm,),
        in_specs=[
            pl.BlockSpec((tm, D_MODEL), lambda i: (i, 0)),
            pl.BlockSpec((tm, 512), lambda i: (i, 0)),
            pl.BlockSpec((tm, 512), lambda i: (i, 0)),
            pl.BlockSpec((tm, D_MODEL), lambda i: (i, O_GA // D_MODEL)),
            pl.BlockSpec((tm, D_MODEL), lambda i: (i, O_GB // D_MODEL)),
            full((1, 2 * D_MODEL)), full((512, D_MODEL)), full((512, D_MODEL)), full((D_MODEL, D_MODEL)),
            full((1, D_MODEL)), full((D_MODEL, 256)), full((1, 128)),
        ],
        out_specs=[
            pl.BlockSpec((tm, D_MODEL), lambda i: (i, 0)),
            pl.BlockSpec((tm, D_MODEL), lambda i: (i, 0)),
            pl.BlockSpec((tm, 128), lambda i: (i, 0)),
        ],
        out_shape=[jax.ShapeDtypeStruct((n, D_MODEL), F32), jax.ShapeDtypeStruct((n, D_MODEL), BF16),
                   jax.ShapeDtypeStruct((n, 128), F32)],
        compiler_params=pltpu.CompilerParams(dimension_semantics=("parallel",),
                                             vmem_limit_bytes=VMEM_LIMIT),
        name="merge",
    )(x2d, oa, ob, z2d, z2d, b_gate, wa, wb, wo, n2, wr, br)


R_GROUP, R_EXPERT = 0, N_GROUPS


def _routing_weights(rl):
    lane = lax.broadcasted_iota(jnp.int32, rl.shape, 1)
    big = jnp.int32(1 << 20)
    gmask = lane < N_GROUPS
    gl = jnp.where(gmask, rl, -jnp.inf)
    gmax = jnp.max(gl, axis=1, keepdims=True)
    gsel = jnp.min(jnp.where(gmask & (rl == gmax), lane, big), axis=1, keepdims=True)
    ggate = 1.0 / jnp.sum(jnp.where(gmask, jnp.exp(gl - gmax), 0.0), axis=1, keepdims=True)
    e_lo = R_EXPERT + gsel * EXPERTS_PER_GROUP
    emask = (lane >= e_lo) & (lane < e_lo + EXPERTS_PER_GROUP)
    el = jnp.where(emask, rl, -jnp.inf)
    emax = jnp.max(el, axis=1, keepdims=True)
    ee = jnp.where(emask, jnp.exp(el - emax), 0.0)
    ep = jnp.where(emask, ee / jnp.sum(ee, axis=1, keepdims=True), -1.0)
    p1 = jnp.max(ep, axis=1, keepdims=True)
    i1 = jnp.min(jnp.where(ep == p1, lane, big), axis=1, keepdims=True)
    ep2 = jnp.where(lane == i1, -1.0, ep)
    p2 = jnp.max(ep2, axis=1, keepdims=True)
    i2 = jnp.min(jnp.where(ep2 == p2, lane, big), axis=1, keepdims=True)
    tot = p1 + p2
    return (jnp.where(lane == i1, ggate * (p1 / tot), 0.0)
            + jnp.where(lane == i2, ggate * (p2 / tot), 0.0))


def _moe_kernel(x1_ref, h2_ref, rl_ref, w1_ref, w3_ref, w2_ref, fg_ref, o_ref, y_ref, comb_ref):
    step = pl.program_id(1)
    per_step = w1_ref.shape[0]

    @pl.when(step == 0)
    def _():
        comb_ref[...] = _routing_weights(rl_ref[...])
        y_ref[...] = jnp.zeros(y_ref.shape, F32)

    h = h2_ref[...]
    comb = comb_ref[...]
    lane = lax.broadcasted_iota(jnp.int32, comb.shape, 1)
    acts = []
    for j in range(per_step):
        a = _dot(h, w1_ref[j])
        b = _dot(h, w3_ref[j])
        ce = jnp.sum(jnp.where(lane == R_EXPERT + step * per_step + j, comb, 0.0), axis=1, keepdims=True)
        acts.append(((a * jax.nn.sigmoid(a)) * b * ce).astype(BF16))
    y_ref[...] += _dot(jnp.concatenate(acts, axis=1), w2_ref[...])

    @pl.when(step == pl.num_programs(1) - 1)
    def _():
        o_ref[...] = _rmsnorm_rows(x1_ref[...] + y_ref[...], fg_ref[...])


def _moe(x1, h2, rl, w1, w3, w2, fg, tm=1024, per_step=4):
    n = x1.shape[0]
    w2g = w2.reshape(N_EXPERTS // per_step, per_step * D_EXPERT, D_MODEL)
    return pl.pallas_call(
        _moe_kernel,
        grid=(n // tm, N_EXPERTS // per_step),
        in_specs=[
            pl.BlockSpec((tm, D_MODEL), lambda i, e: (i, 0)),
            pl.BlockSpec((tm, D_MODEL), lambda i, e: (i, 0)),
            pl.BlockSpec((tm, 128), lambda i, e: (i, 0)),
            pl.BlockSpec((per_step, D_MODEL, D_EXPERT), lambda i, e: (e, 0, 0)),
            pl.BlockSpec((per_step, D_MODEL, D_EXPERT), lambda i, e: (e, 0, 0)),
            pl.BlockSpec((None, per_step * D_EXPERT, D_MODEL), lambda i, e: (e, 0, 0)),
            pl.BlockSpec((1, D_MODEL), lambda i, e: (0, 0)),
        ],
        out_specs=pl.BlockSpec((tm, D_MODEL), lambda i, e: (i, 0)),
        out_shape=jax.ShapeDtypeStruct((n, D_MODEL), F32),
        scratch_shapes=[pltpu.VMEM((tm, D_MODEL), F32), pltpu.VMEM((tm, 128), F32)],
        compiler_params=pltpu.CompilerParams(dimension_semantics=("parallel", "arbitrary"),
                                             vmem_limit_bytes=VMEM_LIMIT),
        name="moe",
    )(x1, h2, rl, w1, w3, w2g, fg)


_W_OFF = {}
_off = 0
for _name, _n in (("a_q", 512), ("a_k", 512), ("a_v", 512), ("a_z", 512), ("a_beta", 8), ("a_alpha", 8),
                  ("b_q", 512), ("b_k", 128), ("b_v", 128), ("i_q", 512), ("i_k", 64), ("i_w", 8),
                  ("gate_a", 1024), ("gate_b", 1024)):
    _W_OFF[_name] = (_off, _off + _n)
    _off += _n


def _cols(w, name, lo=0, hi=None):
    a, b = _W_OFF[name]
    return w[:, a + lo:(b if hi is None else a + hi)]


def _hi_lo_pair(w_small):
    pad = jnp.zeros((w_small.shape[0], 128 - w_small.shape[1]), F32)
    w = jnp.concatenate([w_small.astype(F32), pad], axis=1)
    hi, lo = _split(w)
    return jnp.concatenate([hi, lo], axis=1)


def _layout_w_in(w):
    wb = w.astype(BF16)
    k0, k1 = _cols(wb, "b_k", 0, 64), _cols(wb, "b_k", 64, 128)
    v0, v1 = _cols(wb, "b_v", 0, 64), _cols(wb, "b_v", 64, 128)
    ik = _cols(wb, "i_k")
    main = jnp.concatenate([
        _cols(wb, "a_q"), _cols(wb, "a_k"), _cols(wb, "a_v"), _cols(wb, "a_z"), _cols(wb, "gate_a"),
        _cols(wb, "gate_b"), _cols(wb, "b_q"), _cols(wb, "i_q"), k0, k0, k1, k1, v0, v0, v1, v1, ik, ik],
        axis=1)
    small = _hi_lo_pair(jnp.concatenate([_cols(w, "a_beta"), _cols(w, "a_alpha"), _cols(w, "i_w")], axis=1))
    return main, small


def kernel(x, positions, norm1_g, w_in, b_gate, conv_w, a_log, dt_bias, a_norm_g, w_proj_a, w_proj_b, w_out,
           norm2_g, w_router_group, b_router_group, w_router_expert, b_router_expert, w_exp_gate, w_exp_up,
           w_exp_down, final_norm_g):
    b, s, d = x.shape
    n = b * s
    depth = w_in.shape[0]
    xc = x.reshape(n, d).astype(F32)
    for l in range(depth):
        w_main, w_small = _layout_w_in(w_in[l])
        z, zs = _in_proj(xc, norm1_g[l][None, :].astype(F32), w_main, w_small, positions)
        z3 = z.reshape(b, s, Z_W)
        zs3 = zs.reshape(b, s, 128)
        o_a = _gdn(z3, zs3, conv_w[l], a_log[l], dt_bias[l], a_norm_g[l])
        o_b = _dsa(z3, zs3)
        wr = _hi_lo_pair(jnp.concatenate([w_router_group[l], w_router_expert[l]], axis=1))
        br = jnp.concatenate([b_router_group[l], b_router_expert[l],
                              jnp.zeros((128 - N_GROUPS - N_EXPERTS,), F32)])[None, :].astype(F32)
        x1, h2, rl = _merge(xc, o_a.reshape(n, -1), o_b.reshape(n, -1), z, b_gate[l][None, :].astype(F32),
                            w_proj_a[l].astype(BF16), w_proj_b[l].astype(BF16), w_out[l].astype(BF16),
                            norm2_g[l][None, :].astype(F32), wr, br)
        last = l == depth - 1
        fg = final_norm_g[None, :].astype(F32) if last else None
        if not last:
            raise NotImplementedError("only the final layer fuses the output norm")
        xc = _moe(x1, h2, rl, w_exp_gate[l].astype(BF16), w_exp_up[l].astype(BF16),
                  w_exp_down[l].astype(BF16), fg)
    return xc.reshape(b, s, d).astype(x.dtype)
```

```python
import numpy as np
import jax
import jax.numpy as jnp
from jax import lax
from jax.experimental import pallas as pl
from jax.experimental.pallas import tpu as pltpu

F32 = jnp.float32
BF16 = jnp.bfloat16

D_MODEL = 1024
CHUNK = 64
EPS = 1e-6
ROPE_THETA = 10000.0
A_HEADS = 8
A_DK = 64
A_DV = 64
CONV_K = 4
B_HEADS = 8
B_KV_HEADS = 2
B_HD = 64
IDX_HEADS = 8
IDX_HD = 64
TOPK_MAX = 256
N_GROUPS = 4
EXPERTS_PER_GROUP = 4
N_EXPERTS = 16
D_EXPERT = 256

O_AQKV, O_AZ, O_GA, O_GB, O_BQ, O_IQ, O_KD, O_VD, O_IK = 0, 1536, 2048, 3072, 4096, 4608, 5120, 5376, 5632
Z_W = 5760
S_BETA, S_ALPHA, S_IW = 0, 8, 16

GROUP_HEADS = 2
BD = GROUP_HEADS * CHUNK
NEG_BIG = -1e30
LOG2E = 1.4426950408889634
BISECT_STEPS = 18
VMEM_LIMIT = 56 * 1024 * 1024


def _split(x):
    hi = x.astype(BF16)
    lo = (x - hi.astype(F32)).astype(BF16)
    return hi, lo


def _dot(a, b):
    return jnp.dot(a, b, preferred_element_type=F32)


def _mm(a, b):
    return _dot(a.astype(BF16), b.astype(BF16))


def _mm_nt(a, b):
    return lax.dot_general(a.astype(BF16), b.astype(BF16), (((1,), (1,)), ((), ())),
                           preferred_element_type=F32)


def _mm_exact_lhs(a_bf16, x):
    xh, xl = _split(x)
    return _dot(a_bf16, xh) + _dot(a_bf16, xl)


def _rmsnorm_rows(x, g):
    return x * lax.rsqrt(jnp.mean(x * x, axis=-1, keepdims=True) + EPS) * g


_Z_CHUNKS = tuple((o, min(512, Z_W - o)) for o in range(0, Z_W, 512))


def _rope(x, cs, sn, first):
    w = x.shape[1]
    rep = w // 128
    if rep > 1:
        cs, sn, first = (jnp.concatenate([a] * rep, axis=1) for a in (cs, sn, first))
    swapped = jnp.where(first, pltpu.roll(x, w - B_HD // 2, 1), pltpu.roll(x, B_HD // 2, 1))
    return x * cs + swapped * sn


def _in_proj_kernel(x_ref, g_ref, w_ref, ws_ref, pos_ref, inv_ref, z_ref, zs_ref):
    h = _rmsnorm_rows(x_ref[...], g_ref[...])
    hb, hl = _split(h)

    ang = pos_ref[...] * inv_ref[...]
    lane = lax.broadcasted_iota(jnp.int32, ang.shape, 1)
    first = (lane & (B_HD - 1)) < (B_HD // 2)
    cs = jnp.cos(ang)
    sn = jnp.sin(ang)
    sn = jnp.where(first, -sn, sn)

    for o, w in _Z_CHUNKS:
        r = _dot(hb, w_ref[:, o:o + w])
        if o == O_BQ:
            r = _rope(r, cs, sn, first) * (B_HD ** -0.5 * LOG2E)
        elif o == O_IQ or o == O_IK:
            r = _rope(r, cs, sn, first)
        elif o == O_KD:
            kw = O_VD - O_KD
            r = jnp.concatenate([_rope(r[:, :kw], cs, sn, first), r[:, kw:]], axis=1)
        z_ref[:, o:o + w] = r.astype(BF16)
    s1 = _dot(hb, ws_ref[...])
    s2 = _dot(hl, ws_ref[:, :128])
    zs_ref[...] = s1[:, :128] + s1[:, 128:] + s2


def _in_proj(x2d, g, w_main, w_small, positions, tm=512):
    n = x2d.shape[0]
    half = B_HD // 2
    inv = ROPE_THETA ** (-jnp.arange(half, dtype=F32) / half)
    inv128 = jnp.tile(inv, 4)[None, :]
    pos = positions.astype(F32).reshape(n, 1)
    return pl.pallas_call(
        _in_proj_kernel,
        grid=(n // tm,),
        in_specs=[
            pl.BlockSpec((tm, D_MODEL), lambda i: (i, 0)),
            pl.BlockSpec((1, D_MODEL), lambda i: (0, 0)),
            pl.BlockSpec((D_MODEL, Z_W), lambda i: (0, 0)),
            pl.BlockSpec((D_MODEL, 256), lambda i: (0, 0)),
            pl.BlockSpec((tm, 1), lambda i: (i, 0)),
            pl.BlockSpec((1, 128), lambda i: (0, 0)),
        ],
        out_specs=[
            pl.BlockSpec((tm, Z_W), lambda i: (i, 0)),
            pl.BlockSpec((tm, 128), lambda i: (i, 0)),
        ],
        out_shape=[jax.ShapeDtypeStruct((n, Z_W), BF16), jax.ShapeDtypeStruct((n, 128), F32)],
        compiler_params=pltpu.CompilerParams(dimension_semantics=("parallel",),
                                             vmem_limit_bytes=VMEM_LIMIT),
        name="in_proj",
    )(x2d, g, w_main, w_small, pos, inv128)


def _gdn_constants():
    r = np.arange(BD)
    same = (r[:, None] // CHUNK) == (r[None, :] // CHUNK)
    incl = same & (r[:, None] >= r[None, :])
    strict = same & (r[:, None] > r[None, :])
    eye = np.eye(BD, dtype=np.float32)
    ll = np.concatenate([incl, same], axis=0).astype(np.float32)
    n_groups = A_HEADS // GROUP_HEADS
    sel = np.zeros((2 * n_groups, BD, 128), np.float32)
    for gi in range(n_groups):
        for h in range(GROUP_HEADS):
            sel[gi * 2 + 0, h * CHUNK:(h + 1) * CHUNK, S_BETA + gi * GROUP_HEADS + h] = 1.0
            sel[gi * 2 + 1, h * CHUNK:(h + 1) * CHUNK, S_ALPHA + gi * GROUP_HEADS + h] = 1.0
    return (jnp.asarray(incl, F32), jnp.asarray(strict, F32), jnp.asarray(same, F32), jnp.asarray(eye),
            jnp.asarray(ll, BF16), jnp.asarray(sel))


def _tile_heads(x):
    return jnp.concatenate([x] * GROUP_HEADS, axis=0)


def _gdn_kernel(zq_ref, zz_ref, zs_ref, cw_ref, av_ref, ag_ref, incl_ref, strict_ref, bdm_ref, eye_ref,
                ll_ref, sel_ref, o_ref, ext_ref, st_ref):
    c = pl.program_id(1)
    nb, t = zq_ref.shape[0], zq_ref.shape[1]

    @pl.when(c == 0)
    def _():
        ext_ref[:, 0:8, :] = jnp.zeros((nb, 8, ext_ref.shape[2]), F32)
        st_ref[...] = jnp.zeros(st_ref.shape, F32)

    hw = A_HEADS * A_DK
    cw = cw_ref[...]
    av = av_ref[...]
    q_all, k_all, v_all, bg = [], [], [], []
    for bi in range(nb):
        ext_ref[bi, 8:8 + t, :] = zq_ref[bi].astype(F32)
        y = cw[0:1, :] * ext_ref[bi, pl.ds(8 - (CONV_K - 1), t), :]
        for j in range(1, CONV_K):
            y = y + cw[j:j + 1, :] * ext_ref[bi, pl.ds(8 - (CONV_K - 1) + j, t), :]
        ext_ref[bi, 0:8, :] = ext_ref[bi, t:t + 8, :]
        y = y * jax.nn.sigmoid(y)
        q_all.append(y[:, :hw])
        k_all.append(y[:, hw:2 * hw])
        v_all.append(y[:, 2 * hw:])

        sm = zs_ref[bi]
        lane = lax.broadcasted_iota(jnp.int32, sm.shape, 1)
        xg = sm + av[1:2, :]
        softplus = jnp.maximum(xg, 0.0) + jnp.log1p(jnp.exp(-jnp.abs(xg)))
        g_all = -jnp.exp(av[0:1, :]) * softplus
        bg.append(jnp.where(lane < S_ALPHA, jax.nn.sigmoid(sm), g_all))

    incl = incl_ref[...]
    strict = strict_ref[...]
    bdm = bdm_ref[...]
    eye = eye_ref[...]
    ll = ll_ref[...]
    incl_b = ll[:BD]

    n_chunks = t // CHUNK
    n_groups = A_HEADS // GROUP_HEADS
    chains = [(bi, ci, gi) for bi in range(nb) for ci in range(n_chunks) for gi in range(n_groups)]

    pre = {}
    for bi, ci, gi in chains:
        r0, c0 = ci * CHUNK, gi * BD
        bg4 = _tile_heads(bg[bi][r0:r0 + CHUNK])
        beta = jnp.sum(bg4 * sel_ref[gi * 2 + 0], axis=1, keepdims=True)
        gcol = jnp.sum(bg4 * sel_ref[gi * 2 + 1], axis=1, keepdims=True)
        gs = _mm_exact_lhs(ll, jnp.broadcast_to(gcol, (BD, 128)))
        g_cum = gs[:BD, :1]
        g_last = gs[BD:, :1]
        diff = _mm_exact_lhs(incl_b, gcol * strict)
        decay = jnp.where(incl > 0.0, jnp.exp(diff), 0.0)
        e_cum = jnp.exp(g_cum)
        kr = _tile_heads(k_all[bi][r0:r0 + CHUNK, c0:c0 + BD]) * bdm
        qr = _tile_heads(q_all[bi][r0:r0 + CHUNK, c0:c0 + BD]) * bdm
        vm = _tile_heads(v_all[bi][r0:r0 + CHUNK, c0:c0 + BD]) * bdm
        km = kr * lax.rsqrt(jnp.sum(kr * kr, axis=1, keepdims=True) + EPS)
        qm = qr * (lax.rsqrt(jnp.sum(qr * qr, axis=1, keepdims=True) + EPS) * (A_DK ** -0.5))
        kkqk = _mm_nt(jnp.concatenate([km, qm], axis=0), km)
        m = -(strict * beta * kkqk[:BD] * decay)
        pre[bi, ci, gi] = dict(beta=beta, g_cum=g_cum, g_last=g_last, e_cum=e_cum, km=km, qm=qm, vm=vm,
                               qk=kkqk[BD:] * decay, m=m, inv=eye + m)

    sq = CHUNK
    while sq > 2:
        for key in chains:
            p = pre[key]
            p["m"] = _mm(p["m"], p["m"])
        for key in chains:
            p = pre[key]
            p["inv"] = p["inv"] + _mm(p["inv"], p["m"])
        sq //= 2

    states = [st_ref[i] for i in range(nb * n_groups)]
    outs = [[] for _ in range(nb)]
    lanes = [(bi, gi) for bi in range(nb) for gi in range(n_groups)]
    for ci in range(n_chunks):
        ps = {k: pre[k[0], ci, k[1]] for k in lanes}
        sidx = {k: k[0] * n_groups + k[1] for k in lanes}
        kq_s = {k: _mm(jnp.concatenate([ps[k]["km"] * ps[k]["e_cum"], ps[k]["qm"] * ps[k]["e_cum"]], axis=0),
                       states[sidx[k]]) for k in lanes}
        v_new = {k: _mm(ps[k]["inv"], ps[k]["beta"] * (ps[k]["vm"] - kq_s[k][:BD])) for k in lanes}
        o_bd = {k: kq_s[k][BD:] + _mm(ps[k]["qk"], v_new[k]) for k in lanes}
        for k in lanes:
            p = ps[k]
            k_dec = p["km"] * jnp.exp(p["g_last"] - p["g_cum"])
            states[sidx[k]] = states[sidx[k]] * jnp.exp(p["g_last"]) + _mm(k_dec.T, v_new[k])
        for bi in range(nb):
            o_groups = []
            for gi in range(n_groups):
                ob = o_bd[bi, gi]
                ob = ob * lax.rsqrt(jnp.sum(ob * ob, axis=1, keepdims=True) * (1.0 / A_DV) + EPS)
                o_groups.append(sum(ob[h * CHUNK:(h + 1) * CHUNK] for h in range(GROUP_HEADS)))
            outs[bi].append(jnp.concatenate(o_groups, axis=1))

    for i, state in enumerate(states):
        st_ref[i] = state
    for bi in range(nb):
        o = outs[bi][0] if n_chunks == 1 else jnp.concatenate(outs[bi], axis=0)
        zz = zz_ref[bi].astype(F32)
        o_ref[bi] = (o * ag_ref[...] * (zz * jax.nn.sigmoid(zz))).astype(BF16)


def _gdn(z3, zs3, conv_w, a_log, dt_bias, a_norm_g, t=2 * CHUNK, nb=2):
    b, s, _ = z3.shape
    consts = _gdn_constants()
    av = jnp.zeros((2, 128), F32)
    av = av.at[0, S_ALPHA:S_ALPHA + A_HEADS].set(a_log.astype(F32))
    av = av.at[1, S_ALPHA:S_ALPHA + A_HEADS].set(dt_bias.astype(F32))
    ag = jnp.tile(a_norm_g.astype(F32), A_HEADS)[None, :]
    conv_cols = 2 * A_HEADS * A_DK + A_HEADS * A_DV

    def const_spec(a):
        nd = a.ndim
        return pl.BlockSpec(a.shape, lambda bi, ci, _n=nd: (0,) * _n)

    small_in = (conv_w.astype(F32), av, ag) + consts
    return pl.pallas_call(
        _gdn_kernel,
        grid=(b // nb, s // t),
        in_specs=[
            pl.BlockSpec((nb, t, conv_cols), lambda bi, ci: (bi, ci, O_AQKV // conv_cols)),
            pl.BlockSpec((nb, t, 512), lambda bi, ci: (bi, ci, O_AZ // 512)),
            pl.BlockSpec((nb, t, 128), lambda bi, ci: (bi, ci, 0)),
        ] + [const_spec(a) for a in small_in],
        out_specs=pl.BlockSpec((nb, t, A_HEADS * A_DV), lambda bi, ci: (bi, ci, 0)),
        out_shape=jax.ShapeDtypeStruct((b, s, A_HEADS * A_DV), BF16),
        scratch_shapes=[pltpu.VMEM((nb, 8 + t, conv_cols), F32),
                        pltpu.VMEM((nb * (A_HEADS // GROUP_HEADS), BD, BD), F32)],
        compiler_params=pltpu.CompilerParams(dimension_semantics=("parallel", "arbitrary"),
                                             vmem_limit_bytes=VMEM_LIMIT),
        name="gdn",
    )(z3, z3, zs3, *small_in)


def _dsa_kernel(topk, tk, q_ref, iq_ref, ka_ref, kb_ref, vd_ref, ik_ref, zs_ref, o_ref, st_ref, vt_ref):
    qb = pl.program_id(1)
    s = ka_ref.shape[0]
    tq = q_ref.shape[0]
    nkt_max = s // tk
    r0 = qb * tq
    n_tiles = (r0 + tq + tk - 1) // tk
    heads_per_kv = B_HEADS // B_KV_HEADS

    @pl.when(qb == 0)
    def _():
        for t in range(nkt_max):
            for g in range(B_KV_HEADS):
                v_tile = vd_ref[t * tk:(t + 1) * tk, g * 128:(g + 1) * 128]
                vt_ref[t, g] = v_tile.astype(F32).T.astype(BF16)

    lane128 = lax.broadcasted_iota(jnp.int32, (tq, 128), 1)
    low_half = lane128 < B_HD
    high_half = lane128 >= B_HD

    def head_slab(x, h):
        slab = x[:, (h // 2) * 128:(h // 2 + 1) * 128]
        return jnp.where(low_half if h % 2 == 0 else high_half, slab, jnp.zeros_like(slab))

    zs_t = zs_ref[...].T
    iw_scale = (IDX_HEADS ** -0.5) * (IDX_HD ** -0.5)
    wrows = [zs_t[S_IW + h:S_IW + h + 1, :] * iw_scale for h in range(IDX_HEADS)]
    qrow = r0 + lax.broadcasted_iota(jnp.int32, (1, tq), 1)
    limit = ((qrow >> 6) + 1) << 6
    small = limit <= topk

    iq = iq_ref[...]
    iq_heads = [head_slab(iq, h) for h in range(IDX_HEADS)]

    def index_tile(kt, carry):
        r_max, r_min = carry
        off = pl.multiple_of(kt * tk, tk)
        ikt = ik_ref[pl.ds(off, tk), :]
        acc = jnp.zeros((tk, tq), F32)
        for h in range(IDX_HEADS):
            sc = lax.dot_general(ikt, iq_heads[h], (((1,), (1,)), ((), ())), preferred_element_type=F32)
            acc = acc + wrows[h] * jnp.maximum(sc, 0.0)
        valid = off + lax.broadcasted_iota(jnp.int32, (tk, tq), 0) < limit
        masked = jnp.where(valid, acc, -jnp.inf)
        st_ref[kt] = masked
        r_max = jnp.maximum(r_max, jnp.max(masked, axis=0, keepdims=True))
        r_min = jnp.minimum(r_min, jnp.min(acc, axis=0, keepdims=True))
        return r_max, r_min

    r_max, r_min = lax.fori_loop(0, n_tiles, index_tile,
                                 (jnp.full((1, tq), -jnp.inf, F32), jnp.full((1, tq), jnp.inf, F32)))

    all_kept = r0 + tq <= min(topk, tk)

    @pl.when(all_kept)
    def _():
        st_ref[0] = jnp.where(st_ref[0] > -jnp.inf, 0.0, NEG_BIG)

    for j in range(nkt_max):
        @pl.when(jnp.logical_and(n_tiles == j + 1, jnp.logical_not(all_kept)))
        def _(j=j):
            _topk_mask(topk, j + 1, small, r_max, r_min, st_ref)

    q = q_ref[...]
    nr = heads_per_kv * tq
    qcols = [jnp.concatenate([head_slab(q, g * heads_per_kv + j) for j in range(heads_per_kv)], axis=0)
             for g in range(B_KV_HEADS)]
    k_refs = (ka_ref, kb_ref)

    def attend_tile(kt, carry):
        off = pl.multiple_of(kt * tk, tk)
        bias = st_ref[kt]
        bias4 = jnp.concatenate([bias] * heads_per_kv, axis=1)
        groups = range(B_KV_HEADS)
        logits = [lax.dot_general(k_refs[g][pl.ds(off, tk), :], qcols[g], (((1,), (1,)), ((), ())),
                                  preferred_element_type=F32) + bias4 for g in groups]
        m_new = [jnp.maximum(carry[g][0], jnp.max(logits[g], axis=0, keepdims=True)) for g in groups]
        p = [jnp.exp2(logits[g] - m_new[g]) for g in groups]
        alpha = [jnp.exp2(carry[g][0] - m_new[g]) for g in groups]
        l_new = [alpha[g] * carry[g][1] + jnp.sum(p[g], axis=0, keepdims=True) for g in groups]
        pv = [_dot(vt_ref[kt, g], p[g].astype(BF16)) for g in groups]
        return tuple((m_new[g], l_new[g], alpha[g] * carry[g][2] + pv[g]) for g in groups)

    init = tuple((jnp.full((1, nr), NEG_BIG, F32), jnp.zeros((1, nr), F32), jnp.zeros((128, nr), F32))
                 for _ in range(B_KV_HEADS))
    fin = lax.fori_loop(0, n_tiles, attend_tile, init)
    for g in range(B_KV_HEADS):
        _, l_run, acc = fin[g]
        og = acc / l_run
        heads = [og[:, j * tq:(j + 1) * tq].T for j in range(heads_per_kv)]
        for pp in range(heads_per_kv // 2):
            col = (g * (heads_per_kv // 2) + pp) * 128
            o_ref[:, col:col + 128] = jnp.where(low_half, heads[2 * pp], heads[2 * pp + 1]).astype(BF16)


def _topk_mask(topk, nt, small, r_max, r_min, st_ref):
    _, tk, tq = st_ref.shape
    kf = float(topk)

    def tiles():
        return [st_ref[t] for t in range(nt)]

    def q_sum(pred):
        tot = None
        for t, x in enumerate(tiles()):
            c = jnp.sum(pred(x, t).astype(F32), axis=0, keepdims=True)
            tot = c if tot is None else tot + c
        return tot

    def q_max(val):
        best = None
        for t, x in enumerate(tiles()):
            c = jnp.max(val(x, t), axis=0, keepdims=True)
            best = c if best is None else jnp.maximum(best, c)
        return best

    hi0 = r_max + jnp.maximum(jnp.abs(r_max), 1e-30) * 1e-6

    def bisect(_, carry):
        lo, hi = carry
        mid = 0.5 * (lo + hi)
        ge = q_sum(lambda x, t: x >= mid) >= kf
        return jnp.where(ge, mid, lo), jnp.where(ge, hi, mid)

    lo, hi = lax.fori_loop(0, BISECT_STEPS, bisect, (r_min, hi0))

    def peel_cond(carry):
        return jnp.sum(1.0 - carry[0]) > 0.0

    def peel(carry):
        done, thr, hi_c, n_ge = carry
        v1 = q_max(lambda x, t: jnp.where(x < hi_c, x, -jnp.inf))
        c1 = q_sum(lambda x, t: x >= v1)
        ok = c1 >= kf
        act = done < 0.5
        thr = jnp.where(act & ok, v1, thr)
        n_ge = jnp.where(act & ok, c1, n_ge)
        hi_c = jnp.where(act & (~ok), v1, hi_c)
        return jnp.where(ok, 1.0, done), thr, hi_c, n_ge

    _, thr, _, n_ge = lax.while_loop(peel_cond, peel, (jnp.where(small, 1.0, 0.0), lo, hi, jnp.full_like(lo, kf)))

    def key_index(t):
        return (t * tk + lax.broadcasted_iota(jnp.int32, (tk, tq), 0)).astype(F32)

    contested = jnp.sum(jnp.where((n_ge > kf) & (~small), 1.0, 0.0)) > 0.0
    last = float(nt * tk - 1)

    def tie_cut():
        need = kf - q_sum(lambda x, t: x > thr)

        def tie_search(_, carry):
            jlo, jhi = carry
            mid = jnp.floor(0.5 * (jlo + jhi))
            ge = q_sum(lambda x, t: (x == thr) & (key_index(t) <= mid)) >= need
            return jnp.where(ge, jlo, mid), jnp.where(ge, mid, jhi)

        n_iter = int(np.ceil(np.log2(nt * tk))) + 1
        return lax.fori_loop(0, n_iter, tie_search,
                             (jnp.full((1, tq), -1.0, F32), jnp.full((1, tq), last, F32)))[1]

    jcut = lax.cond(contested, tie_cut, lambda: jnp.full((1, tq), last, F32))
    for t, x in enumerate(tiles()):
        sel = (x > -jnp.inf) & (small | (x > thr) | ((x == thr) & (key_index(t) <= jcut)))
        st_ref[t] = jnp.where(sel, 0.0, NEG_BIG)


def _dsa(z3, zs3, tq=256, tk=512):
    b, s, _ = z3.shape
    topk = min(TOPK_MAX, s // 4)
    kernel = lambda *refs: _dsa_kernel(topk, tk, *refs)
    qblock = lambda w, cb: pl.BlockSpec((None, tq, w), lambda bi, qi, _c=cb: (bi, qi, _c))
    keys = lambda w, cb: pl.BlockSpec((None, s, w), lambda bi, qi, _c=cb: (bi, 0, _c))
    return pl.pallas_call(
        kernel,
        grid=(b, s // tq),
        in_specs=[qblock(512, O_BQ // 512), qblock(512, O_IQ // 512), keys(128, O_KD // 128),
                  keys(128, O_KD // 128 + 1), keys(256, O_VD // 256), keys(128, O_IK // 128), qblock(128, 0)],
        out_specs=pl.BlockSpec((None, tq, B_HEADS * B_HD), lambda bi, qi: (bi, qi, 0)),
        out_shape=jax.ShapeDtypeStruct((b, s, B_HEADS * B_HD), BF16),
        scratch_shapes=[pltpu.VMEM((s // tk, tk, tq), F32),
                        pltpu.VMEM((s // tk, B_KV_HEADS, 128, tk), BF16)],
        compiler_params=pltpu.CompilerParams(dimension_semantics=("parallel", "arbitrary"),
                                             vmem_limit_bytes=VMEM_LIMIT),
        name="dsa",
    )(z3, z3, z3, z3, z3, z3, zs3)


def _merge_kernel(x_ref, oa_ref, ob_ref, ga_ref, gb_ref, bg_ref, wa_ref, wb_ref, wo_ref, n2_ref, wr_ref,
                  br_ref, x1_ref, h2_ref, rl_ref):
    pa = _dot(oa_ref[...], wa_ref[...])
    pb = _dot(ob_ref[...], wb_ref[...])
    bgv = bg_ref[...]
    ga = jax.nn.sigmoid(ga_ref[...].astype(F32) + bgv[:, :D_MODEL])
    gb = jax.nn.sigmoid(gb_ref[...].astype(F32) + bgv[:, D_MODEL:])
    merged = ga * pa + gb * pb
    x1 = x_ref[...] + _mm(merged, wo_ref[...])
    x1_ref[...] = x1
    h2 = _rmsnorm_rows(x1, n2_ref[...])
    hb, hl = _split(h2)
    h2_ref[...] = hb
    s1 = _dot(hb, wr_ref[...])
    s2 = _dot(hl, wr_ref[:, :128])
    rl_ref[...] = s1[:, :128] + s1[:, 128:] + s2 + br_ref[...]


def _merge(x2d, oa, ob, z2d, b_gate, wa, wb, wo, n2, wr, br, tm=512):
    n = x2d.shape[0]
    full = lambda shape: pl.BlockSpec(shape, lambda i: (0, 0))
    return pl.pallas_call(
        _merge_kernel,
        grid=(n // tm,),
        in_specs=[
            pl.BlockSpec((tm, D_MODEL), lambda i: (i, 0)),
            pl.BlockSpec((tm, 512), lambda i: (i, 0)),
            pl.BlockSpec((tm, 512), lambda i: (i, 0)),
            pl.BlockSpec((tm, D_MODEL), lambda i: (i, O_GA // D_MODEL)),
            pl.BlockSpec((tm, D_MODEL), lambda i: (i, O_GB // D_MODEL)),
            full((1, 2 * D_MODEL)), full((512, D_MODEL)), full((512, D_MODEL)), full((D_MODEL, D_MODEL)),
            full((1, D_MODEL)), full((D_MODEL, 256)), full((1, 128)),
        ],
        out_specs=[
            pl.BlockSpec((tm, D_MODEL), lambda i: (i, 0)),
            pl.BlockSpec((tm, D_MODEL), lambda i: (i, 0)),
            pl.BlockSpec((tm, 128), lambda i: (i, 0)),
        ],
        out_shape=[jax.ShapeDtypeStruct((n, D_MODEL), F32), jax.ShapeDtypeStruct((n, D_MODEL), BF16),
                   jax.ShapeDtypeStruct((n, 128), F32)],
        compiler_params=pltpu.CompilerParams(dimension_semantics=("parallel",),
                                             vmem_limit_bytes=VMEM_LIMIT),
        name="merge",
    )(x2d, oa, ob, z2d, z2d, b_gate, wa, wb, wo, n2, wr, br)


R_GROUP, R_EXPERT = 0, N_GROUPS


def _routing_weights(rl):
    lane = lax.broadcasted_iota(jnp.int32, rl.shape, 1)
    big = jnp.int32(1 << 20)
    gmask = lane < N_GROUPS
    gl = jnp.where(gmask, rl, -jnp.inf)
    gmax = jnp.max(gl, axis=1, keepdims=True)
    gsel = jnp.min(jnp.where(gmask & (rl == gmax), lane, big), axis=1, keepdims=True)
    ggate = 1.0 / jnp.sum(jnp.where(gmask, jnp.exp(gl - gmax), 0.0), axis=1, keepdims=True)
    e_lo = R_EXPERT + gsel * EXPERTS_PER_GROUP
    emask = (lane >= e_lo) & (lane < e_lo + EXPERTS_PER_GROUP)
    el = jnp.where(emask, rl, -jnp.inf)
    emax = jnp.max(el, axis=1, keepdims=True)
    ee = jnp.where(emask, jnp.exp(el - emax), 0.0)
    ep = jnp.where(emask, ee / jnp.sum(ee, axis=1, keepdims=True), -1.0)
    p1 = jnp.max(ep, axis=1, keepdims=True)
    i1 = jnp.min(jnp.where(ep == p1, lane, big), axis=1, keepdims=True)
    ep2 = jnp.where(lane == i1, -1.0, ep)
    p2 = jnp.max(ep2, axis=1, keepdims=True)
    i2 = jnp.min(jnp.where(ep2 == p2, lane, big), axis=1, keepdims=True)
    tot = p1 + p2
    return (jnp.where(lane == i1, ggate * (p1 / tot), 0.0)
            + jnp.where(lane == i2, ggate * (p2 / tot), 0.0))


def _moe_kernel(x1_ref, h2_ref, rl_ref, w1_ref, w3_ref, w2_ref, fg_ref, o_ref, y_ref, comb_ref):
    step = pl.program_id(1)
    per_step = w1_ref.shape[0]

    @pl.when(step == 0)
    def _():
        comb_ref[...] = _routing_weights(rl_ref[...])
        y_ref[...] = jnp.zeros(y_ref.shape, F32)

    h = h2_ref[...]
    comb = comb_ref[...]
    lane = lax.broadcasted_iota(jnp.int32, comb.shape, 1)
    acts = []
    for j in range(per_step):
        a = _dot(h, w1_ref[j])
        b = _dot(h, w3_ref[j])
        ce = jnp.sum(jnp.where(lane == R_EXPERT + step * per_step + j, comb, 0.0), axis=1, keepdims=True)
        acts.append(((a * jax.nn.sigmoid(a)) * b * ce).astype(BF16))
    y_ref[...] += _dot(jnp.concatenate(acts, axis=1), w2_ref[...])

    @pl.when(step == pl.num_programs(1) - 1)
    def _():
        o_ref[...] = _rmsnorm_rows(x1_ref[...] + y_ref[...], fg_ref[...])


def _moe(x1, h2, rl, w1, w3, w2, fg, tm=1024, per_step=4):
    n = x1.shape[0]
    w2g = w2.reshape(N_EXPERTS // per_step, per_step * D_EXPERT, D_MODEL)
    return pl.pallas_call(
        _moe_kernel,
        grid=(n // tm, N_EXPERTS // per_step),
        in_specs=[
            pl.BlockSpec((tm, D_MODEL), lambda i, e: (i, 0)),
            pl.BlockSpec((tm, D_MODEL), lambda i, e: (i, 0)),
            pl.BlockSpec((tm, 128), lambda i, e: (i, 0)),
            pl.BlockSpec((per_step, D_MODEL, D_EXPERT), lambda i, e: (e, 0, 0)),
            pl.BlockSpec((per_step, D_MODEL, D_EXPERT), lambda i, e: (e, 0, 0)),
            pl.BlockSpec((None, per_step * D_EXPERT, D_MODEL), lambda i, e: (e, 0, 0)),
            pl.BlockSpec((1, D_MODEL), lambda i, e: (0, 0)),
        ],
        out_specs=pl.BlockSpec((tm, D_MODEL), lambda i, e: (i, 0)),
        out_shape=jax.ShapeDtypeStruct((n, D_MODEL), F32),
        scratch_shapes=[pltpu.VMEM((tm, D_MODEL), F32), pltpu.VMEM((tm, 128), F32)],
        compiler_params=pltpu.CompilerParams(dimension_semantics=("parallel", "arbitrary"),
                                             vmem_limit_bytes=VMEM_LIMIT),
        name="moe",
    )(x1, h2, rl, w1, w3, w2g, fg)


_W_OFF = {}
_off = 0
for _name, _n in (("a_q", 512), ("a_k", 512), ("a_v", 512), ("a_z", 512), ("a_beta", 8), ("a_alpha", 8),
                  ("b_q", 512), ("b_k", 128), ("b_v", 128), ("i_q", 512), ("i_k", 64), ("i_w", 8),
                  ("gate_a", 1024), ("gate_b", 1024)):
    _W_OFF[_name] = (_off, _off + _n)
    _off += _n


def _cols(w, name, lo=0, hi=None):
    a, b = _W_OFF[name]
    return w[:, a + lo:(b if hi is None else a + hi)]


def _hi_lo_pair(w_small):
    pad = jnp.zeros((w_small.shape[0], 128 - w_small.shape[1]), F32)
    w = jnp.concatenate([w_small.astype(F32), pad], axis=1)
    hi, lo = _split(w)
    return jnp.concatenate([hi, lo], axis=1)


def _layout_w_in(w):
    wb = w.astype(BF16)
    k0, k1 = _cols(wb, "b_k", 0, 64), _cols(wb, "b_k", 64, 128)
    v0, v1 = _cols(wb, "b_v", 0, 64), _cols(wb, "b_v", 64, 128)
    ik = _cols(wb, "i_k")
    main = jnp.concatenate([
        _cols(wb, "a_q"), _cols(wb, "a_k"), _cols(wb, "a_v"), _cols(wb, "a_z"), _cols(wb, "gate_a"),
        _cols(wb, "gate_b"), _cols(wb, "b_q"), _cols(wb, "i_q"), k0, k0, k1, k1, v0, v0, v1, v1, ik, ik],
        axis=1)
    small = _hi_lo_pair(jnp.concatenate([_cols(w, "a_beta"), _cols(w, "a_alpha"), _cols(w, "i_w")], axis=1))
    return main, small


def kernel(x, positions, norm1_g, w_in, b_gate, conv_w, a_log, dt_bias, a_norm_g, w_proj_a, w_proj_b, w_out,
           norm2_g, w_router_group, b_router_group, w_router_expert, b_router_expert, w_exp_gate, w_exp_up,
           w_exp_down, final_norm_g):
    b, s, d = x.shape
    n = b * s
    depth = w_in.shape[0]
    xc = x.reshape(n, d).astype(F32)
    for l in range(depth):
        w_main, w_small = _layout_w_in(w_in[l])
        z, zs = _in_proj(xc, norm1_g[l][None, :].astype(F32), w_main, w_small, positions)
        z3 = z.reshape(b, s, Z_W)
        zs3 = zs.reshape(b, s, 128)
        o_a = _gdn(z3, zs3, conv_w[l], a_log[l], dt_bias[l], a_norm_g[l])
        o_b = _dsa(z3, zs3)
        wr = _hi_lo_pair(jnp.concatenate([w_router_group[l], w_router_expert[l]], axis=1))
        br = jnp.concatenate([b_router_group[l], b_router_expert[l],
                              jnp.zeros((128 - N_GROUPS - N_EXPERTS,), F32)])[None, :].astype(F32)
        x1, h2, rl = _merge(xc, o_a.reshape(n, -1), o_b.reshape(n, -1), z, b_gate[l][None, :].astype(F32),
                            w_proj_a[l].astype(BF16), w_proj_b[l].astype(BF16), w_out[l].astype(BF16),
                            norm2_g[l][None, :].astype(F32), wr, br)
        last = l == depth - 1
        fg = final_norm_g[None, :].astype(F32) if last else None
        if not last:
            raise NotImplementedError("only the final layer fuses the output norm")
        xc = _moe(x1, h2, rl, w_exp_gate[l].astype(BF16), w_exp_up[l].astype(BF16),
                  w_exp_down[l].astype(BF16), fg)
    return xc.reshape(b, s, d).astype(x.dtype)
```

```python
import numpy as np
import jax
import jax.numpy as jnp
from jax import lax
from jax.experimental import pallas as pl
from jax.experimental.pallas import tpu as pltpu

F32 = jnp.float32
BF16 = jnp.bfloat16

D_MODEL = 1024
CHUNK = 64
EPS = 1e-6
ROPE_THETA = 10000.0
A_HEADS = 8
A_DK = 64
A_DV = 64
CONV_K = 4
B_HEADS = 8
B_KV_HEADS = 2
B_HD = 64
IDX_HEADS = 8
IDX_HD = 64
TOPK_MAX = 256
N_GROUPS = 4
EXPERTS_PER_GROUP = 4
N_EXPERTS = 16
D_EXPERT = 256

O_AQKV, O_AZ, O_GA, O_GB, O_BQ, O_IQ, O_KD, O_VD, O_IK = 0, 1536, 2048, 3072, 4096, 4608, 5120, 5376, 5632
Z_W = 5760
S_BETA, S_ALPHA, S_IW = 0, 8, 16

GROUP_HEADS = 2
BD = GROUP_HEADS * CHUNK
NEG_BIG = -1e30
LOG2E = 1.4426950408889634
BISECT_STEPS = 18
VMEM_LIMIT = 56 * 1024 * 1024


def _split(x):
    hi = x.astype(BF16)
    lo = (x - hi.astype(F32)).astype(BF16)
    return hi, lo


def _dot(a, b):
    return jnp.dot(a, b, preferred_element_type=F32)


def _mm(a, b):
    return _dot(a.astype(BF16), b.astype(BF16))


def _mm_nt(a, b):
    return lax.dot_general(a.astype(BF16), b.astype(BF16), (((1,), (1,)), ((), ())),
                           preferred_element_type=F32)


def _mm_exact_lhs(a_bf16, x):
    xh, xl = _split(x)
    return _dot(a_bf16, xh) + _dot(a_bf16, xl)


def _rmsnorm_rows(x, g):
    return x * lax.rsqrt(jnp.mean(x * x, axis=-1, keepdims=True) + EPS) * g


_Z_CHUNKS = tuple((o, min(512, Z_W - o)) for o in range(0, Z_W, 512))


def _rope(x, cs, sn, first):
    w = x.shape[1]
    rep = w // 128
    if rep > 1:
        cs, sn, first = (jnp.concatenate([a] * rep, axis=1) for a in (cs, sn, first))
    swapped = jnp.where(first, pltpu.roll(x, w - B_HD // 2, 1), pltpu.roll(x, B_HD // 2, 1))
    return x * cs + swapped * sn


def _in_proj_kernel(x_ref, g_ref, w_ref, ws_ref, pos_ref, inv_ref, z_ref, zs_ref):
    h = _rmsnorm_rows(x_ref[...], g_ref[...])
    hb, hl = _split(h)

    ang = pos_ref[...] * inv_ref[...]
    lane = lax.broadcasted_iota(jnp.int32, ang.shape, 1)
    first = (lane & (B_HD - 1)) < (B_HD // 2)
    cs = jnp.cos(ang)
    sn = jnp.sin(ang)
    sn = jnp.where(first, -sn, sn)

    for o, w in _Z_CHUNKS:
        r = _dot(hb, w_ref[:, o:o + w])
        if o == O_BQ:
            r = _rope(r, cs, sn, first) * (B_HD ** -0.5 * LOG2E)
        elif o == O_IQ or o == O_IK:
            r = _rope(r, cs, sn, first)
        elif o == O_KD:
            kw = O_VD - O_KD
            r = jnp.concatenate([_rope(r[:, :kw], cs, sn, first), r[:, kw:]], axis=1)
        z_ref[:, o:o + w] = r.astype(BF16)
    s1 = _dot(hb, ws_ref[...])
    s2 = _dot(hl, ws_ref[:, :128])
    zs_ref[...] = s1[:, :128] + s1[:, 128:] + s2


def _in_proj(x2d, g, w_main, w_small, positions, tm=512):
    n = x2d.shape[0]
    half = B_HD // 2
    inv = ROPE_THETA ** (-jnp.arange(half, dtype=F32) / half)
    inv128 = jnp.tile(inv, 4)[None, :]
    pos = positions.astype(F32).reshape(n, 1)
    return pl.pallas_call(
        _in_proj_kernel,
        grid=(n // tm,),
        in_specs=[
            pl.BlockSpec((tm, D_MODEL), lambda i: (i, 0)),
            pl.BlockSpec((1, D_MODEL), lambda i: (0, 0)),
            pl.BlockSpec((D_MODEL, Z_W), lambda i: (0, 0)),
            pl.BlockSpec((D_MODEL, 256), lambda i: (0, 0)),
            pl.BlockSpec((tm, 1), lambda i: (i, 0)),
            pl.BlockSpec((1, 128), lambda i: (0, 0)),
        ],
        out_specs=[
            pl.BlockSpec((tm, Z_W), lambda i: (i, 0)),
            pl.BlockSpec((tm, 128), lambda i: (i, 0)),
        ],
        out_shape=[jax.ShapeDtypeStruct((n, Z_W), BF16), jax.ShapeDtypeStruct((n, 128), F32)],
        compiler_params=pltpu.CompilerParams(dimension_semantics=("parallel",),
                                             vmem_limit_bytes=VMEM_LIMIT),
        name="in_proj",
    )(x2d, g, w_main, w_small, pos, inv128)


def _gdn_constants():
    r = np.arange(BD)
    same = (r[:, None] // CHUNK) == (r[None, :] // CHUNK)
    incl = same & (r[:, None] >= r[None, :])
    strict = same & (r[:, None] > r[None, :])
    eye = np.eye(BD, dtype=np.float32)
    ll = np.concatenate([incl, same], axis=0).astype(np.float32)
    n_groups = A_HEADS // GROUP_HEADS
    sel = np.zeros((2 * n_groups, BD, 128), np.float32)
    for gi in range(n_groups):
        for h in range(GROUP_HEADS):
            sel[gi * 2 + 0, h * CHUNK:(h + 1) * CHUNK, S_BETA + gi * GROUP_HEADS + h] = 1.0
            sel[gi * 2 + 1, h * CHUNK:(h + 1) * CHUNK, S_ALPHA + gi * GROUP_HEADS + h] = 1.0
    return (jnp.asarray(incl, F32), jnp.asarray(strict, F32), jnp.asarray(same, F32), jnp.asarray(eye),
            jnp.asarray(ll, BF16), jnp.asarray(sel))


def _tile_heads(x):
    return jnp.concatenate([x] * GROUP_HEADS, axis=0)


def _gdn_kernel(zq_ref, zz_ref, zs_ref, cw_ref, av_ref, ag_ref, incl_ref, strict_ref, bdm_ref, eye_ref,
                ll_ref, sel_ref, o_ref, ext_ref, st_ref):
    c = pl.program_id(1)
    nb, t = zq_ref.shape[0], zq_ref.shape[1]

    @pl.when(c == 0)
    def _():
        ext_ref[:, 0:8, :] = jnp.zeros((nb, 8, ext_ref.shape[2]), F32)
        st_ref[...] = jnp.zeros(st_ref.shape, F32)

    hw = A_HEADS * A_DK
    cw = cw_ref[...]
    av = av_ref[...]
    q_all, k_all, v_all, bg = [], [], [], []
    for bi in range(nb):
        ext_ref[bi, 8:8 + t, :] = zq_ref[bi].astype(F32)
        y = cw[0:1, :] * ext_ref[bi, pl.ds(8 - (CONV_K - 1), t), :]
        for j in range(1, CONV_K):
            y = y + cw[j:j + 1, :] * ext_ref[bi, pl.ds(8 - (CONV_K - 1) + j, t), :]
        ext_ref[bi, 0:8, :] = ext_ref[bi, t:t + 8, :]
        y = y * jax.nn.sigmoid(y)
        q_all.append(y[:, :hw])
        k_all.append(y[:, hw:2 * hw])
        v_all.append(y[:, 2 * hw:])

        sm = zs_ref[bi]
        lane = lax.broadcasted_iota(jnp.int32, sm.shape, 1)
        xg = sm + av[1:2, :]
        softplus = jnp.maximum(xg, 0.0) + jnp.log1p(jnp.exp(-jnp.abs(xg)))
        g_all = -jnp.exp(av[0:1, :]) * softplus
        bg.append(jnp.where(lane < S_ALPHA, jax.nn.sigmoid(sm), g_all))

    incl = incl_ref[...]
    strict = strict_ref[...]
    bdm = bdm_ref[...]
    eye = eye_ref[...]
    ll = ll_ref[...]
    incl_b = ll[:BD]

    n_chunks = t // CHUNK
    n_groups = A_HEADS // GROUP_HEADS
    chains = [(bi, ci, gi) for bi in range(nb) for ci in range(n_chunks) for gi in range(n_groups)]

    pre = {}
    for bi, ci, gi in chains:
        r0, c0 = ci * CHUNK, gi * BD
        bg4 = _tile_heads(bg[bi][r0:r0 + CHUNK])
        beta = jnp.sum(bg4 * sel_ref[gi * 2 + 0], axis=1, keepdims=True)
        gcol = jnp.sum(bg4 * sel_ref[gi * 2 + 1], axis=1, keepdims=True)
        gs = _mm_exact_lhs(ll, jnp.broadcast_to(gcol, (BD, 128)))
        g_cum = gs[:BD, :1]
        g_last = gs[BD:, :1]
        diff = _mm_exact_lhs(incl_b, gcol * strict)
        decay = jnp.where(incl > 0.0, jnp.exp(diff), 0.0)
        e_cum = jnp.exp(g_cum)
        kr = _tile_heads(k_all[bi][r0:r0 + CHUNK, c0:c0 + BD]) * bdm
        qr = _tile_heads(q_all[bi][r0:r0 + CHUNK, c0:c0 + BD]) * bdm
        vm = _tile_heads(v_all[bi][r0:r0 + CHUNK, c0:c0 + BD]) * bdm
        km = kr * lax.rsqrt(jnp.sum(kr * kr, axis=1, keepdims=True) + EPS)
        qm = qr * (lax.rsqrt(jnp.sum(qr * qr, axis=1, keepdims=True) + EPS) * (A_DK ** -0.5))
        kkqk = _mm_nt(jnp.concatenate([km, qm], axis=0), km)
        m = -(strict * beta * kkqk[:BD] * decay)
        pre[bi, ci, gi] = dict(beta=beta, g_cum=g_cum, g_last=g_last, e_cum=e_cum, km=km, qm=qm, vm=vm,
                               qk=kkqk[BD:] * decay, m=m, inv=eye + m)

    sq = CHUNK
    while sq > 2:
        for key in chains:
            p = pre[key]
            p["m"] = _mm(p["m"], p["m"])
        for key in chains:
            p = pre[key]
            p["inv"] = p["inv"] + _mm(p["inv"], p["m"])
        sq //= 2

    states = [st_ref[i] for i in range(nb * n_groups)]
    outs = [[] for _ in range(nb)]
    lanes = [(bi, gi) for bi in range(nb) for gi in range(n_groups)]
    for ci in range(n_chunks):
        ps = {k: pre[k[0], ci, k[1]] for k in lanes}
        sidx = {k: k[0] * n_groups + k[1] for k in lanes}
        kq_s = {k: _mm(jnp.concatenate([ps[k]["km"] * ps[k]["e_cum"], ps[k]["qm"] * ps[k]["e_cum"]], axis=0),
                       states[sidx[k]]) for k in lanes}
        v_new = {k: _mm(ps[k]["inv"], ps[k]["beta"] * (ps[k]["vm"] - kq_s[k][:BD])) for k in lanes}
        o_bd = {k: kq_s[k][BD:] + _mm(ps[k]["qk"], v_new[k]) for k in lanes}
        for k in lanes:
            p = ps[k]
            k_dec = p["km"] * jnp.exp(p["g_last"] - p["g_cum"])
            states[sidx[k]] = states[sidx[k]] * jnp.exp(p["g_last"]) + _mm(k_dec.T, v_new[k])
        for bi in range(nb):
            o_groups = []
            for gi in range(n_groups):
                ob = o_bd[bi, gi]
                ob = ob * lax.rsqrt(jnp.sum(ob * ob, axis=1, keepdims=True) * (1.0 / A_DV) + EPS)
                o_groups.append(sum(ob[h * CHUNK:(h + 1) * CHUNK] for h in range(GROUP_HEADS)))
            outs[bi].append(jnp.concatenate(o_groups, axis=1))

    for i, state in enumerate(states):
        st_ref[i] = state
    for bi in range(nb):
        o = outs[bi][0] if n_chunks == 1 else jnp.concatenate(outs[bi], axis=0)
        zz = zz_ref[bi].astype(F32)
        o_ref[bi] = (o * ag_ref[...] * (zz * jax.nn.sigmoid(zz))).astype(BF16)


def _gdn(z3, zs3, conv_w, a_log, dt_bias, a_norm_g, t=2 * CHUNK, nb=2):
    b, s, _ = z3.shape
    consts = _gdn_constants()
    av = jnp.zeros((2, 128), F32)
    av = av.at[0, S_ALPHA:S_ALPHA + A_HEADS].set(a_log.astype(F32))
    av = av.at[1, S_ALPHA:S_ALPHA + A_HEADS].set(dt_bias.astype(F32))
    ag = jnp.tile(a_norm_g.astype(F32), A_HEADS)[None, :]
    conv_cols = 2 * A_HEADS * A_DK + A_HEADS * A_DV

    def const_spec(a):
        nd = a.ndim
        return pl.BlockSpec(a.shape, lambda bi, ci, _n=nd: (0,) * _n)

    small_in = (conv_w.astype(F32), av, ag) + consts
    return pl.pallas_call(
        _gdn_kernel,
        grid=(b // nb, s // t),
        in_specs=[
            pl.BlockSpec((nb, t, conv_cols), lambda bi, ci: (bi, ci, O_AQKV // conv_cols)),
            pl.BlockSpec((nb, t, 512), lambda bi, ci: (bi, ci, O_AZ // 512)),
            pl.BlockSpec((nb, t, 128), lambda bi, ci: (bi, ci, 0)),
        ] + [const_spec(a) for a in small_in],
        out_specs=pl.BlockSpec((nb, t, A_HEADS * A_DV), lambda bi, ci: (bi, ci, 0)),
        out_shape=jax.ShapeDtypeStruct((b, s, A_HEADS * A_DV), BF16),
        scratch_shapes=[pltpu.VMEM((nb, 8 + t, conv_cols), F32),
                        pltpu.VMEM((nb * (A_HEADS // GROUP_HEADS), BD, BD), F32)],
        compiler_params=pltpu.CompilerParams(dimension_semantics=("parallel", "arbitrary"),
                                             vmem_limit_bytes=VMEM_LIMIT),
        name="gdn",
    )(z3, z3, zs3, *small_in)


def _dsa_kernel(topk, tk, q_ref, iq_ref, ka_ref, kb_ref, vd_ref, ik_ref, zs_ref, o_ref, st_ref, vt_ref):
    qb = pl.program_id(1)
    s = ka_ref.shape[0]
    tq = q_ref.shape[0]
    nkt_max = s // tk
    r0 = qb * tq
    n_tiles = (r0 + tq + tk - 1) // tk
    heads_per_kv = B_HEADS // B_KV_HEADS

    @pl.when(qb == 0)
    def _():
        for t in range(nkt_max):
            for g in range(B_KV_HEADS):
                v_tile = vd_ref[t * tk:(t + 1) * tk, g * 128:(g + 1) * 128]
                vt_ref[t, g] = v_tile.astype(F32).T.astype(BF16)

    lane128 = lax.broadcasted_iota(jnp.int32, (tq, 128), 1)
    low_half = lane128 < B_HD
    high_half = lane128 >= B_HD

    def head_slab(x, h):
        slab = x[:, (h // 2) * 128:(h // 2 + 1) * 128]
        return jnp.where(low_half if h % 2 == 0 else high_half, slab, jnp.zeros_like(slab))

    zs_t = zs_ref[...].T
    iw_scale = (IDX_HEADS ** -0.5) * (IDX_HD ** -0.5)
    wrows = [zs_t[S_IW + h:S_IW + h + 1, :] * iw_scale for h in range(IDX_HEADS)]
    qrow = r0 + lax.broadcasted_iota(jnp.int32, (1, tq), 1)
    limit = ((qrow >> 6) + 1) << 6
    small = limit <= topk

    iq = iq_ref[...]
    iq_heads = [head_slab(iq, h) for h in range(IDX_HEADS)]

    def index_tile(kt, carry):
        r_max, r_min = carry
        off = pl.multiple_of(kt * tk, tk)
        ikt = ik_ref[pl.ds(off, tk), :]
        acc = jnp.zeros((tk, tq), F32)
        for h in range(IDX_HEADS):
            sc = lax.dot_general(ikt, iq_heads[h], (((1,), (1,)), ((), ())), preferred_element_type=F32)
            acc = acc + wrows[h] * jnp.maximum(sc, 0.0)
        valid = off + lax.broadcasted_iota(jnp.int32, (tk, tq), 0) < limit
        masked = jnp.where(valid, acc, -jnp.inf)
        st_ref[kt] = masked
        r_max = jnp.maximum(r_max, jnp.max(masked, axis=0, keepdims=True))
        r_min = jnp.minimum(r_min, jnp.min(acc, axis=0, keepdims=True))
        return r_max, r_min

    r_max, r_min = lax.fori_loop(0, n_tiles, index_tile,
                                 (jnp.full((1, tq), -jnp.inf, F32), jnp.full((1, tq), jnp.inf, F32)))

    all_kept = r0 + tq <= min(topk, tk)

    @pl.when(all_kept)
    def _():
        st_ref[0] = jnp.where(st_ref[0] > -jnp.inf, 0.0, NEG_BIG)

    for j in range(nkt_max):
        @pl.when(jnp.logical_and(n_tiles == j + 1, jnp.logical_not(all_kept)))
        def _(j=j):
            _topk_mask(topk, j + 1, small, r_max, r_min, st_ref)

    q = q_ref[...]
    nr = heads_per_kv * tq
    qcols = [jnp.concatenate([head_slab(q, g * heads_per_kv + j) for j in range(heads_per_kv)], axis=0)
             for g in range(B_KV_HEADS)]
    k_refs = (ka_ref, kb_ref)

    def attend_tile(kt, carry):
        off = pl.multiple_of(kt * tk, tk)
        bias = st_ref[kt]
        bias4 = jnp.concatenate([bias] * heads_per_kv, axis=1)
        groups = range(B_KV_HEADS)
        logits = [lax.dot_general(k_refs[g][pl.ds(off, tk), :], qcols[g], (((1,), (1,)), ((), ())),
                                  preferred_element_type=F32) + bias4 for g in groups]
        m_new = [jnp.maximum(carry[g][0], jnp.max(logits[g], axis=0, keepdims=True)) for g in groups]
        p = [jnp.exp2(logits[g] - m_new[g]) for g in groups]
        alpha = [jnp.exp2(carry[g][0] - m_new[g]) for g in groups]
        l_new = [alpha[g] * carry[g][1] + jnp.sum(p[g], axis=0, keepdims=True) for g in groups]
        pv = [_dot(vt_ref[kt, g], p[g].astype(BF16)) for g in groups]
        return tuple((m_new[g], l_new[g], alpha[g] * carry[g][2] + pv[g]) for g in groups)

    init = tuple((jnp.full((1, nr), NEG_BIG, F32), jnp.zeros((1, nr), F32), jnp.zeros((128, nr), F32))
                 for _ in range(B_KV_HEADS))
    fin = lax.fori_loop(0, n_tiles, attend_tile, init)
    for g in range(B_KV_HEADS):
        _, l_run, acc = fin[g]
        og = acc / l_run
        heads = [og[:, j * tq:(j + 1) * tq].T for j in range(heads_per_kv)]
        for pp in range(heads_per_kv // 2):
            col = (g * (heads_per_kv // 2) + pp) * 128
            o_ref[:, col:col + 128] = jnp.where(low_half, heads[2 * pp], heads[2 * pp + 1]).astype(BF16)


def _topk_mask(topk, nt, small, r_max, r_min, st_ref):
    _, tk, tq = st_ref.shape
    kf = float(topk)

    def tiles():
        return [st_ref[t] for t in range(nt)]

    def q_sum(pred):
        tot = None
        for t, x in enumerate(tiles()):
            c = jnp.sum(pred(x, t).astype(F32), axis=0, keepdims=True)
            tot = c if tot is None else tot + c
        return tot

    def q_max(val):
        best = None
        for t, x in enumerate(tiles()):
            c = jnp.max(val(x, t), axis=0, keepdims=True)
            best = c if best is None else jnp.maximum(best, c)
        return best

    hi0 = r_max + jnp.maximum(jnp.abs(r_max), 1e-30) * 1e-6

    def bisect(_, carry):
        lo, hi = carry
        mid = 0.5 * (lo + hi)
        ge = q_sum(lambda x, t: x >= mid) >= kf
        return jnp.where(ge, mid, lo), jnp.where(ge, hi, mid)

    lo, hi = lax.fori_loop(0, BISECT_STEPS, bisect, (r_min, hi0))

    def peel_cond(carry):
        return jnp.sum(1.0 - carry[0]) > 0.0

    def peel(carry):
        done, thr, hi_c, n_ge = carry
        v1 = q_max(lambda x, t: jnp.where(x < hi_c, x, -jnp.inf))
        c1 = q_sum(lambda x, t: x >= v1)
        ok = c1 >= kf
        act = done < 0.5
        thr = jnp.where(act & ok, v1, thr)
        n_ge = jnp.where(act & ok, c1, n_ge)
        hi_c = jnp.where(act & (~ok), v1, hi_c)
        return jnp.where(ok, 1.0, done), thr, hi_c, n_ge

    _, thr, _, n_ge = lax.while_loop(peel_cond, peel, (jnp.where(small, 1.0, 0.0), lo, hi, jnp.full_like(lo, kf)))

    def key_index(t):
        return (t * tk + lax.broadcasted_iota(jnp.int32, (tk, tq), 0)).astype(F32)

    contested = jnp.sum(jnp.where((n_ge > kf) & (~small), 1.0, 0.0)) > 0.0
    last = float(nt * tk - 1)

    def tie_cut():
        need = kf - q_sum(lambda x, t: x > thr)

        def tie_search(_, carry):
            jlo, jhi = carry
            mid = jnp.floor(0.5 * (jlo + jhi))
            ge = q_sum(lambda x, t: (x == thr) & (key_index(t) <= mid)) >= need
            return jnp.where(ge, jlo, mid), jnp.where(ge, mid, jhi)

        n_iter = int(np.ceil(np.log2(nt * tk))) + 1
        return lax.fori_loop(0, n_iter, tie_search,
                             (jnp.full((1, tq), -1.0, F32), jnp.full((1, tq), last, F32)))[1]

    jcut = lax.cond(contested, tie_cut, lambda: jnp.full((1, tq), last, F32))
    for t, x in enumerate(tiles()):
        sel = (x > -jnp.inf) & (small | (x > thr) | ((x == thr) & (key_index(t) <= jcut)))
        st_ref[t] = jnp.where(sel, 0.0, NEG_BIG)


def _dsa(z3, zs3, tq=256, tk=512):
    b, s, _ = z3.shape
    topk = min(TOPK_MAX, s // 4)
    kernel = lambda *refs: _dsa_kernel(topk, tk, *refs)
    qblock = lambda w, cb: pl.BlockSpec((None, tq, w), lambda bi, qi, _c=cb: (bi, qi, _c))
    keys = lambda w, cb: pl.BlockSpec((None, s, w), lambda bi, qi, _c=cb: (bi, 0, _c))
    return pl.pallas_call(
        kernel,
        grid=(b, s // tq),
        in_specs=[qblock(512, O_BQ // 512), qblock(512, O_IQ // 512), keys(128, O_KD // 128),
                  keys(128, O_KD // 128 + 1), keys(256, O_VD // 256), keys(128, O_IK // 128), qblock(128, 0)],
        out_specs=pl.BlockSpec((None, tq, B_HEADS * B_HD), lambda bi, qi: (bi, qi, 0)),
        out_shape=jax.ShapeDtypeStruct((b, s, B_HEADS * B_HD), BF16),
        scratch_shapes=[pltpu.VMEM((s // tk, tk, tq), F32),
                        pltpu.VMEM((s // tk, B_KV_HEADS, 128, tk), BF16)],
        compiler_params=pltpu.CompilerParams(dimension_semantics=("parallel", "arbitrary"),
                                             vmem_limit_bytes=VMEM_LIMIT),
        name="dsa",
    )(z3, z3, z3, z3, z3, z3, zs3)


def _merge_kernel(x_ref, oa_ref, ob_ref, ga_ref, gb_ref, bg_ref, wa_ref, wb_ref, wo_ref, n2_ref, wr_ref,
                  br_ref, x1_ref, h2_ref, rl_ref):
    pa = _dot(oa_ref[...], wa_ref[...])
    pb = _dot(ob_ref[...], wb_ref[...])
    bgv = bg_ref[...]
    ga = jax.nn.sigmoid(ga_ref[...].astype(F32) + bgv[:, :D_MODEL])
    gb = jax.nn.sigmoid(gb_ref[...].astype(F32) + bgv[:, D_MODEL:])
    merged = ga * pa + gb * pb
    x1 = x_ref[...] + _mm(merged, wo_ref[...])
    x1_ref[...] = x1
    h2 = _rmsnorm_rows(x1, n2_ref[...])
    hb, hl = _split(h2)
    h2_ref[...] = hb
    s1 = _dot(hb, wr_ref[...])
    s2 = _dot(hl, wr_ref[:, :128])
    rl_ref[...] = s1[:, :128] + s1[:, 128:] + s2 + br_ref[...]


def _merge(x2d, oa, ob, z2d, b_gate, wa, wb, wo, n2, wr, br, tm=512):
    n = x2d.shape[0]
    full = lambda shape: pl.BlockSpec(shape, lambda i: (0, 0))
    return pl.pallas_call(
        _merge_kernel,
        grid=(n // tm,),
        in_specs=[
            pl.BlockSpec((tm, D_MODEL), lambda i: (i, 0)),
            pl.BlockSpec((tm, 512), lambda i: (i, 0)),
            pl.BlockSpec((tm, 512), lambda i: (i, 0)),
            pl.BlockSpec((tm, D_MODEL), lambda i: (i, O_GA // D_MODEL)),
            pl.BlockSpec((tm, D_MODEL), lambda i: (i, O_GB // D_MODEL)),
            full((1, 2 * D_MODEL)), full((512, D_MODEL)), full((512, D_MODEL)), full((D_MODEL, D_MODEL)),
            full((1, D_MODEL)), full((D_MODEL, 256)), full((1, 128)),
        ],
        out_specs=[
            pl.BlockSpec((tm, D_MODEL), lambda i: (i, 0)),
            pl.BlockSpec((tm, D_MODEL), lambda i: (i, 0)),
            pl.BlockSpec((tm, 128), lambda i: (i, 0)),
        ],
        out_shape=[jax.ShapeDtypeStruct((n, D_MODEL), F32), jax.ShapeDtypeStruct((n, D_MODEL), BF16),
                   jax.ShapeDtypeStruct((n, 128), F32)],
        compiler_params=pltpu.CompilerParams(dimension_semantics=("parallel",),
                                             vmem_limit_bytes=VMEM_LIMIT),
        name="merge",
    )(x2d, oa, ob, z2d, z2d, b_gate, wa, wb, wo, n2, wr, br)


R_GROUP, R_EXPERT = 0, 8


def _routing_weights(rl):
    t = rl.T
    n_tok = t.shape[1]
    gl = t[R_GROUP:R_GROUP + N_GROUPS]
    gidx = lax.broadcasted_iota(jnp.int32, gl.shape, 0)
    gmax = jnp.max(gl, axis=0, keepdims=True)
    gsel = jnp.min(jnp.where(gl == gmax, gidx, N_GROUPS), axis=0, keepdims=True)
    ggate = 1.0 / jnp.sum(jnp.exp(gl - gmax), axis=0, keepdims=True)
    el = t[R_EXPERT:R_EXPERT + N_EXPERTS]
    eidx = lax.broadcasted_iota(jnp.int32, el.shape, 0)
    e_lo = gsel * EXPERTS_PER_GROUP
    emask = (eidx >= e_lo) & (eidx < e_lo + EXPERTS_PER_GROUP)
    el = jnp.where(emask, el, -jnp.inf)
    emax = jnp.max(el, axis=0, keepdims=True)
    ee = jnp.where(emask, jnp.exp(el - emax), 0.0)
    ep = jnp.where(emask, ee / jnp.sum(ee, axis=0, keepdims=True), -1.0)
    p1 = jnp.max(ep, axis=0, keepdims=True)
    i1 = jnp.min(jnp.where(ep == p1, eidx, N_EXPERTS), axis=0, keepdims=True)
    ep2 = jnp.where(eidx == i1, -1.0, ep)
    p2 = jnp.max(ep2, axis=0, keepdims=True)
    i2 = jnp.min(jnp.where(ep2 == p2, eidx, N_EXPERTS), axis=0, keepdims=True)
    tot = p1 + p2
    comb_t = (jnp.where(eidx == i1, ggate * (p1 / tot), 0.0)
              + jnp.where(eidx == i2, ggate * (p2 / tot), 0.0))
    full = jnp.concatenate([jnp.zeros((R_EXPERT, n_tok), F32), comb_t,
                            jnp.zeros((128 - R_EXPERT - N_EXPERTS, n_tok), F32)], axis=0)
    return full.T


def _moe_kernel(x1_ref, h2_ref, rl_ref, w1_ref, w3_ref, w2_ref, fg_ref, o_ref, y_ref, comb_ref):
    step = pl.program_id(1)
    per_step = w1_ref.shape[0]

    @pl.when(step == 0)
    def _():
        comb_ref[...] = _routing_weights(rl_ref[...])
        y_ref[...] = jnp.zeros(y_ref.shape, F32)

    h = h2_ref[...]
    comb = comb_ref[...]
    lane = lax.broadcasted_iota(jnp.int32, comb.shape, 1)
    acts = []
    for j in range(per_step):
        a = _dot(h, w1_ref[j])
        b = _dot(h, w3_ref[j])
        ce = jnp.sum(jnp.where(lane == R_EXPERT + step * per_step + j, comb, 0.0), axis=1, keepdims=True)
        acts.append(((a * jax.nn.sigmoid(a)) * b * ce).astype(BF16))
    y_ref[...] += _dot(jnp.concatenate(acts, axis=1), w2_ref[...])

    @pl.when(step == pl.num_programs(1) - 1)
    def _():
        o_ref[...] = _rmsnorm_rows(x1_ref[...] + y_ref[...], fg_ref[...])


def _moe(x1, h2, rl, w1, w3, w2, fg, tm=1024, per_step=4):
    n = x1.shape[0]
    w2g = w2.reshape(N_EXPERTS // per_step, per_step * D_EXPERT, D_MODEL)
    return pl.pallas_call(
        _moe_kernel,
        grid=(n // tm, N_EXPERTS // per_step),
        in_specs=[
            pl.BlockSpec((tm, D_MODEL), lambda i, e: (i, 0)),
            pl.BlockSpec((tm, D_MODEL), lambda i, e: (i, 0)),
            pl.BlockSpec((tm, 128), lambda i, e: (i, 0)),
            pl.BlockSpec((per_step, D_MODEL, D_EXPERT), lambda i, e: (e, 0, 0)),
            pl.BlockSpec((per_step, D_MODEL, D_EXPERT), lambda i, e: (e, 0, 0)),
            pl.BlockSpec((None, per_step * D_EXPERT, D_MODEL), lambda i, e: (e, 0, 0)),
            pl.BlockSpec((1, D_MODEL), lambda i, e: (0, 0)),
        ],
        out_specs=pl.BlockSpec((tm, D_MODEL), lambda i, e: (i, 0)),
        out_shape=jax.ShapeDtypeStruct((n, D_MODEL), F32),
        scratch_shapes=[pltpu.VMEM((tm, D_MODEL), F32), pltpu.VMEM((tm, 128), F32)],
        compiler_params=pltpu.CompilerParams(dimension_semantics=("parallel", "arbitrary"),
                                             vmem_limit_bytes=VMEM_LIMIT),
        name="moe",
    )(x1, h2, rl, w1, w3, w2g, fg)


_W_OFF = {}
_off = 0
for _name, _n in (("a_q", 512), ("a_k", 512), ("a_v", 512), ("a_z", 512), ("a_beta", 8), ("a_alpha", 8),
                  ("b_q", 512), ("b_k", 128), ("b_v", 128), ("i_q", 512), ("i_k", 64), ("i_w", 8),
                  ("gate_a", 1024), ("gate_b", 1024)):
    _W_OFF[_name] = (_off, _off + _n)
    _off += _n


def _cols(w, name, lo=0, hi=None):
    a, b = _W_OFF[name]
    return w[:, a + lo:(b if hi is None else a + hi)]


def _hi_lo_pair(w_small):
    pad = jnp.zeros((w_small.shape[0], 128 - w_small.shape[1]), F32)
    w = jnp.concatenate([w_small.astype(F32), pad], axis=1)
    hi, lo = _split(w)
    return jnp.concatenate([hi, lo], axis=1)


def _layout_w_in(w):
    wb = w.astype(BF16)
    k0, k1 = _cols(wb, "b_k", 0, 64), _cols(wb, "b_k", 64, 128)
    v0, v1 = _cols(wb, "b_v", 0, 64), _cols(wb, "b_v", 64, 128)
    ik = _cols(wb, "i_k")
    main = jnp.concatenate([
        _cols(wb, "a_q"), _cols(wb, "a_k"), _cols(wb, "a_v"), _cols(wb, "a_z"), _cols(wb, "gate_a"),
        _cols(wb, "gate_b"), _cols(wb, "b_q"), _cols(wb, "i_q"), k0, k0, k1, k1, v0, v0, v1, v1, ik, ik],
        axis=1)
    small = _hi_lo_pair(jnp.concatenate([_cols(w, "a_beta"), _cols(w, "a_alpha"), _cols(w, "i_w")], axis=1))
    return main, small


def kernel(x, positions, norm1_g, w_in, b_gate, conv_w, a_log, dt_bias, a_norm_g, w_proj_a, w_proj_b, w_out,
           norm2_g, w_router_group, b_router_group, w_router_expert, b_router_expert, w_exp_gate, w_exp_up,
           w_exp_down, final_norm_g):
    b, s, d = x.shape
    n = b * s
    depth = w_in.shape[0]
    xc = x.reshape(n, d).astype(F32)
    for l in range(depth):
        w_main, w_small = _layout_w_in(w_in[l])
        z, zs = _in_proj(xc, norm1_g[l][None, :].astype(F32), w_main, w_small, positions)
        z3 = z.reshape(b, s, Z_W)
        zs3 = zs.reshape(b, s, 128)
        o_a = _gdn(z3, zs3, conv_w[l], a_log[l], dt_bias[l], a_norm_g[l])
        o_b = _dsa(z3, zs3)
        gap = R_EXPERT - N_GROUPS
        wr = _hi_lo_pair(jnp.concatenate([w_router_group[l], jnp.zeros((d, gap), F32), w_router_expert[l]],
                                         axis=1))
        br = jnp.concatenate([b_router_group[l], jnp.zeros((gap,), F32), b_router_expert[l],
                              jnp.zeros((128 - R_EXPERT - N_EXPERTS,), F32)])[None, :].astype(F32)
        x1, h2, rl = _merge(xc, o_a.reshape(n, -1), o_b.reshape(n, -1), z, b_gate[l][None, :].astype(F32),
                            w_proj_a[l].astype(BF16), w_proj_b[l].astype(BF16), w_out[l].astype(BF16),
                            norm2_g[l][None, :].astype(F32), wr, br)
        last = l == depth - 1
        fg = final_norm_g[None, :].astype(F32) if last else None
        if not last:
            raise NotImplementedError("only the final layer fuses the output norm")
        xc = _moe(x1, h2, rl, w_exp_gate[l].astype(BF16), w_exp_up[l].astype(BF16),
                  w_exp_down[l].astype(BF16), fg)
    return xc.reshape(b, s, d).astype(x.dtype)
```

```python
import numpy as np
import jax
import jax.numpy as jnp
from jax import lax
from jax.experimental import pallas as pl
from jax.experimental.pallas import tpu as pltpu

F32 = jnp.float32
BF16 = jnp.bfloat16

D_MODEL = 1024
CHUNK = 64
EPS = 1e-6
ROPE_THETA = 10000.0
A_HEADS = 8
A_DK = 64
A_DV = 64
CONV_K = 4
B_HEADS = 8
B_KV_HEADS = 2
B_HD = 64
IDX_HEADS = 8
IDX_HD = 64
TOPK_MAX = 256
N_GROUPS = 4
EXPERTS_PER_GROUP = 4
N_EXPERTS = 16
D_EXPERT = 256

O_AQKV, O_AZ, O_GA, O_GB, O_BQ, O_IQ, O_KD, O_VD, O_IK = 0, 1536, 2048, 3072, 4096, 4608, 5120, 5376, 5632
Z_W = 5760
S_BETA, S_ALPHA, S_IW = 0, 8, 16

GROUP_HEADS = 2
BD = GROUP_HEADS * CHUNK
NEG_BIG = -1e30
LOG2E = 1.4426950408889634
BISECT_STEPS = 18
VMEM_LIMIT = 56 * 1024 * 1024


def _split(x):
    hi = x.astype(BF16)
    lo = (x - hi.astype(F32)).astype(BF16)
    return hi, lo


def _dot(a, b):
    return jnp.dot(a, b, preferred_element_type=F32)


def _mm(a, b):
    return _dot(a.astype(BF16), b.astype(BF16))


def _mm_nt(a, b):
    return lax.dot_general(a.astype(BF16), b.astype(BF16), (((1,), (1,)), ((), ())),
                           preferred_element_type=F32)


def _mm_exact_lhs(a_bf16, x):
    xh, xl = _split(x)
    return _dot(a_bf16, xh) + _dot(a_bf16, xl)


def _rmsnorm_rows(x, g):
    return x * lax.rsqrt(jnp.mean(x * x, axis=-1, keepdims=True) + EPS) * g


_Z_CHUNKS = tuple((o, min(512, Z_W - o)) for o in range(0, Z_W, 512))


def _rope(x, cs, sn, first):
    w = x.shape[1]
    rep = w // 128
    if rep > 1:
        cs, sn, first = (jnp.concatenate([a] * rep, axis=1) for a in (cs, sn, first))
    swapped = jnp.where(first, pltpu.roll(x, w - B_HD // 2, 1), pltpu.roll(x, B_HD // 2, 1))
    return x * cs + swapped * sn


def _in_proj_kernel(x_ref, g_ref, w_ref, ws_ref, pos_ref, inv_ref, z_ref, zs_ref):
    h = _rmsnorm_rows(x_ref[...], g_ref[...])
    hb, hl = _split(h)

    ang = pos_ref[...] * inv_ref[...]
    lane = lax.broadcasted_iota(jnp.int32, ang.shape, 1)
    first = (lane & (B_HD - 1)) < (B_HD // 2)
    cs = jnp.cos(ang)
    sn = jnp.sin(ang)
    sn = jnp.where(first, -sn, sn)

    for o, w in _Z_CHUNKS:
        r = _dot(hb, w_ref[:, o:o + w])
        if o == O_BQ:
            r = _rope(r, cs, sn, first) * (B_HD ** -0.5 * LOG2E)
        elif o == O_IQ or o == O_IK:
            r = _rope(r, cs, sn, first)
        elif o == O_KD:
            kw = O_VD - O_KD
            r = jnp.concatenate([_rope(r[:, :kw], cs, sn, first), r[:, kw:]], axis=1)
        z_ref[:, o:o + w] = r.astype(BF16)
    s1 = _dot(hb, ws_ref[...])
    s2 = _dot(hl, ws_ref[:, :128])
    zs_ref[...] = s1[:, :128] + s1[:, 128:] + s2


def _in_proj(x2d, g, w_main, w_small, positions, tm=512):
    n = x2d.shape[0]
    half = B_HD // 2
    inv = ROPE_THETA ** (-jnp.arange(half, dtype=F32) / half)
    inv128 = jnp.tile(inv, 4)[None, :]
    pos = positions.astype(F32).reshape(n, 1)
    return pl.pallas_call(
        _in_proj_kernel,
        grid=(n // tm,),
        in_specs=[
            pl.BlockSpec((tm, D_MODEL), lambda i: (i, 0)),
            pl.BlockSpec((1, D_MODEL), lambda i: (0, 0)),
            pl.BlockSpec((D_MODEL, Z_W), lambda i: (0, 0)),
            pl.BlockSpec((D_MODEL, 256), lambda i: (0, 0)),
            pl.BlockSpec((tm, 1), lambda i: (i, 0)),
            pl.BlockSpec((1, 128), lambda i: (0, 0)),
        ],
        out_specs=[
            pl.BlockSpec((tm, Z_W), lambda i: (i, 0)),
            pl.BlockSpec((tm, 128), lambda i: (i, 0)),
        ],
        out_shape=[jax.ShapeDtypeStruct((n, Z_W), BF16), jax.ShapeDtypeStruct((n, 128), F32)],
        compiler_params=pltpu.CompilerParams(dimension_semantics=("parallel",),
                                             vmem_limit_bytes=VMEM_LIMIT),
        name="in_proj",
    )(x2d, g, w_main, w_small, pos, inv128)


def _gdn_constants():
    r = np.arange(BD)
    same = (r[:, None] // CHUNK) == (r[None, :] // CHUNK)
    incl = same & (r[:, None] >= r[None, :])
    strict = same & (r[:, None] > r[None, :])
    eye = np.eye(BD, dtype=np.float32)
    ll = np.concatenate([incl, same], axis=0).astype(np.float32)
    n_groups = A_HEADS // GROUP_HEADS
    sel = np.zeros((2 * n_groups, BD, 128), np.float32)
    for gi in range(n_groups):
        for h in range(GROUP_HEADS):
            sel[gi * 2 + 0, h * CHUNK:(h + 1) * CHUNK, S_BETA + gi * GROUP_HEADS + h] = 1.0
            sel[gi * 2 + 1, h * CHUNK:(h + 1) * CHUNK, S_ALPHA + gi * GROUP_HEADS + h] = 1.0
    return (jnp.asarray(incl, F32), jnp.asarray(strict, F32), jnp.asarray(same, F32), jnp.asarray(eye),
            jnp.asarray(ll, BF16), jnp.asarray(sel))


def _tile_heads(x):
    return jnp.concatenate([x] * GROUP_HEADS, axis=0)


def _gdn_kernel(zq_ref, zz_ref, zs_ref, cw_ref, av_ref, ag_ref, incl_ref, strict_ref, bdm_ref, eye_ref,
                ll_ref, sel_ref, o_ref, ext_ref, st_ref):
    c = pl.program_id(1)
    nb, t = zq_ref.shape[0], zq_ref.shape[1]

    @pl.when(c == 0)
    def _():
        ext_ref[:, 0:8, :] = jnp.zeros((nb, 8, ext_ref.shape[2]), F32)
        st_ref[...] = jnp.zeros(st_ref.shape, F32)

    hw = A_HEADS * A_DK
    cw = cw_ref[...]
    av = av_ref[...]
    q_all, k_all, v_all, bg = [], [], [], []
    for bi in range(nb):
        ext_ref[bi, 8:8 + t, :] = zq_ref[bi].astype(F32)
        y = cw[0:1, :] * ext_ref[bi, pl.ds(8 - (CONV_K - 1), t), :]
        for j in range(1, CONV_K):
            y = y + cw[j:j + 1, :] * ext_ref[bi, pl.ds(8 - (CONV_K - 1) + j, t), :]
        ext_ref[bi, 0:8, :] = ext_ref[bi, t:t + 8, :]
        y = y * jax.nn.sigmoid(y)
        q_all.append(y[:, :hw])
        k_all.append(y[:, hw:2 * hw])
        v_all.append(y[:, 2 * hw:])

        sm = zs_ref[bi]
        lane = lax.broadcasted_iota(jnp.int32, sm.shape, 1)
        xg = sm + av[1:2, :]
        softplus = jnp.maximum(xg, 0.0) + jnp.log1p(jnp.exp(-jnp.abs(xg)))
        g_all = -jnp.exp(av[0:1, :]) * softplus
        bg.append(jnp.where(lane < S_ALPHA, jax.nn.sigmoid(sm), g_all))

    incl = incl_ref[...]
    strict = strict_ref[...]
    bdm = bdm_ref[...]
    eye = eye_ref[...]
    ll = ll_ref[...]
    incl_b = ll[:BD]

    n_chunks = t // CHUNK
    n_groups = A_HEADS // GROUP_HEADS
    chains = [(bi, ci, gi) for bi in range(nb) for ci in range(n_chunks) for gi in range(n_groups)]

    pre = {}
    for bi, ci, gi in chains:
        r0, c0 = ci * CHUNK, gi * BD
        bg4 = _tile_heads(bg[bi][r0:r0 + CHUNK])
        beta = jnp.sum(bg4 * sel_ref[gi * 2 + 0], axis=1, keepdims=True)
        gcol = jnp.sum(bg4 * sel_ref[gi * 2 + 1], axis=1, keepdims=True)
        gs = _mm_exact_lhs(ll, jnp.broadcast_to(gcol, (BD, 128)))
        g_cum = gs[:BD, :1]
        g_last = gs[BD:, :1]
        diff = _mm_exact_lhs(incl_b, gcol * strict)
        decay = jnp.where(incl > 0.0, jnp.exp(diff), 0.0)
        e_cum = jnp.exp(g_cum)
        kr = _tile_heads(k_all[bi][r0:r0 + CHUNK, c0:c0 + BD]) * bdm
        qr = _tile_heads(q_all[bi][r0:r0 + CHUNK, c0:c0 + BD]) * bdm
        vm = _tile_heads(v_all[bi][r0:r0 + CHUNK, c0:c0 + BD]) * bdm
        km = kr * lax.rsqrt(jnp.sum(kr * kr, axis=1, keepdims=True) + EPS)
        qm = qr * (lax.rsqrt(jnp.sum(qr * qr, axis=1, keepdims=True) + EPS) * (A_DK ** -0.5))
        kkqk = _mm_nt(jnp.concatenate([km, qm], axis=0), km)
        m = -(strict * beta * kkqk[:BD] * decay)
        pre[bi, ci, gi] = dict(beta=beta, g_cum=g_cum, g_last=g_last, e_cum=e_cum, km=km, qm=qm, vm=vm,
                               qk=kkqk[BD:] * decay, m=m, inv=eye + m)

    sq = CHUNK
    while sq > 2:
        for key in chains:
            p = pre[key]
            p["m"] = _mm(p["m"], p["m"])
        for key in chains:
            p = pre[key]
            p["inv"] = p["inv"] + _mm(p["inv"], p["m"])
        sq //= 2

    states = [st_ref[i] for i in range(nb * n_groups)]
    outs = [[] for _ in range(nb)]
    lanes = [(bi, gi) for bi in range(nb) for gi in range(n_groups)]
    for ci in range(n_chunks):
        ps = {k: pre[k[0], ci, k[1]] for k in lanes}
        sidx = {k: k[0] * n_groups + k[1] for k in lanes}
        kq_s = {k: _mm(jnp.concatenate([ps[k]["km"] * ps[k]["e_cum"], ps[k]["qm"] * ps[k]["e_cum"]], axis=0),
                       states[sidx[k]]) for k in lanes}
        v_new = {k: _mm(ps[k]["inv"], ps[k]["beta"] * (ps[k]["vm"] - kq_s[k][:BD])) for k in lanes}
        o_bd = {k: kq_s[k][BD:] + _mm(ps[k]["qk"], v_new[k]) for k in lanes}
        for k in lanes:
            p = ps[k]
            k_dec = p["km"] * jnp.exp(p["g_last"] - p["g_cum"])
            states[sidx[k]] = states[sidx[k]] * jnp.exp(p["g_last"]) + _mm(k_dec.T, v_new[k])
        for bi in range(nb):
            o_groups = []
            for gi in range(n_groups):
                ob = o_bd[bi, gi]
                ob = ob * lax.rsqrt(jnp.sum(ob * ob, axis=1, keepdims=True) * (1.0 / A_DV) + EPS)
                o_groups.append(sum(ob[h * CHUNK:(h + 1) * CHUNK] for h in range(GROUP_HEADS)))
            outs[bi].append(jnp.concatenate(o_groups, axis=1))

    for i, state in enumerate(states):
        st_ref[i] = state
    for bi in range(nb):
        o = outs[bi][0] if n_chunks == 1 else jnp.concatenate(outs[bi], axis=0)
        zz = zz_ref[bi].astype(F32)
        o_ref[bi] = (o * ag_ref[...] * (zz * jax.nn.sigmoid(zz))).astype(BF16)


def _gdn(z3, zs3, conv_w, a_log, dt_bias, a_norm_g, t=2 * CHUNK, nb=2):
    b, s, _ = z3.shape
    consts = _gdn_constants()
    av = jnp.zeros((2, 128), F32)
    av = av.at[0, S_ALPHA:S_ALPHA + A_HEADS].set(a_log.astype(F32))
    av = av.at[1, S_ALPHA:S_ALPHA + A_HEADS].set(dt_bias.astype(F32))
    ag = jnp.tile(a_norm_g.astype(F32), A_HEADS)[None, :]
    conv_cols = 2 * A_HEADS * A_DK + A_HEADS * A_DV

    def const_spec(a):
        nd = a.ndim
        return pl.BlockSpec(a.shape, lambda bi, ci, _n=nd: (0,) * _n)

    small_in = (conv_w.astype(F32), av, ag) + consts
    return pl.pallas_call(
        _gdn_kernel,
        grid=(b // nb, s // t),
        in_specs=[
            pl.BlockSpec((nb, t, conv_cols), lambda bi, ci: (bi, ci, O_AQKV // conv_cols)),
            pl.BlockSpec((nb, t, 512), lambda bi, ci: (bi, ci, O_AZ // 512)),
            pl.BlockSpec((nb, t, 128), lambda bi, ci: (bi, ci, 0)),
        ] + [const_spec(a) for a in small_in],
        out_specs=pl.BlockSpec((nb, t, A_HEADS * A_DV), lambda bi, ci: (bi, ci, 0)),
        out_shape=jax.ShapeDtypeStruct((b, s, A_HEADS * A_DV), BF16),
        scratch_shapes=[pltpu.VMEM((nb, 8 + t, conv_cols), F32),
                        pltpu.VMEM((nb * (A_HEADS // GROUP_HEADS), BD, BD), F32)],
        compiler_params=pltpu.CompilerParams(dimension_semantics=("parallel", "arbitrary"),
                                             vmem_limit_bytes=VMEM_LIMIT),
        name="gdn",
    )(z3, z3, zs3, *small_in)


def _dsa_kernel(topk, tk, q_ref, iq_ref, ka_ref, kb_ref, vd_ref, ik_ref, zs_ref, o_ref, st_ref, vt_ref):
    qb = pl.program_id(1)
    s = ka_ref.shape[0]
    tq = q_ref.shape[0]
    nkt_max = s // tk
    r0 = qb * tq
    n_tiles = (r0 + tq + tk - 1) // tk
    heads_per_kv = B_HEADS // B_KV_HEADS

    @pl.when(qb == 0)
    def _():
        for t in range(nkt_max):
            for g in range(B_KV_HEADS):
                v_tile = vd_ref[t * tk:(t + 1) * tk, g * 128:(g + 1) * 128]
                vt_ref[t, g] = v_tile.astype(F32).T.astype(BF16)

    lane128 = lax.broadcasted_iota(jnp.int32, (tq, 128), 1)
    low_half = lane128 < B_HD
    high_half = lane128 >= B_HD

    def head_slab(x, h):
        slab = x[:, (h // 2) * 128:(h // 2 + 1) * 128]
        return jnp.where(low_half if h % 2 == 0 else high_half, slab, jnp.zeros_like(slab))

    zs_t = zs_ref[...].T
    iw_scale = (IDX_HEADS ** -0.5) * (IDX_HD ** -0.5)
    wrows = [zs_t[S_IW + h:S_IW + h + 1, :] * iw_scale for h in range(IDX_HEADS)]
    qrow = r0 + lax.broadcasted_iota(jnp.int32, (1, tq), 1)
    limit = ((qrow >> 6) + 1) << 6
    small = limit <= topk

    iq = iq_ref[...]
    iq_heads = [head_slab(iq, h) for h in range(IDX_HEADS)]

    def index_tile(kt, carry):
        r_max, r_min = carry
        off = pl.multiple_of(kt * tk, tk)
        ikt = ik_ref[pl.ds(off, tk), :]
        acc = jnp.zeros((tk, tq), F32)
        for h in range(IDX_HEADS):
            sc = lax.dot_general(ikt, iq_heads[h], (((1,), (1,)), ((), ())), preferred_element_type=F32)
            acc = acc + wrows[h] * jnp.maximum(sc, 0.0)
        valid = off + lax.broadcasted_iota(jnp.int32, (tk, tq), 0) < limit
        masked = jnp.where(valid, acc, -jnp.inf)
        st_ref[kt] = masked
        r_max = jnp.maximum(r_max, jnp.max(masked, axis=0, keepdims=True))
        r_min = jnp.minimum(r_min, jnp.min(acc, axis=0, keepdims=True))
        return r_max, r_min

    r_max, r_min = lax.fori_loop(0, n_tiles, index_tile,
                                 (jnp.full((1, tq), -jnp.inf, F32), jnp.full((1, tq), jnp.inf, F32)))

    all_kept = r0 + tq <= min(topk, tk)

    @pl.when(all_kept)
    def _():
        st_ref[0] = jnp.where(st_ref[0] > -jnp.inf, 0.0, NEG_BIG)

    for j in range(nkt_max):
        @pl.when(jnp.logical_and(n_tiles == j + 1, jnp.logical_not(all_kept)))
        def _(j=j):
            _topk_mask(topk, j + 1, small, r_max, r_min, st_ref)

    q = q_ref[...]
    nr = heads_per_kv * tq
    qcols = [jnp.concatenate([head_slab(q, g * heads_per_kv + j) for j in range(heads_per_kv)], axis=0)
             for g in range(B_KV_HEADS)]
    k_refs = (ka_ref, kb_ref)

    def attend_tile(kt, carry):
        off = pl.multiple_of(kt * tk, tk)
        bias = st_ref[kt]
        bias4 = jnp.concatenate([bias] * heads_per_kv, axis=1)
        groups = range(B_KV_HEADS)
        logits = [lax.dot_general(k_refs[g][pl.ds(off, tk), :], qcols[g], (((1,), (1,)), ((), ())),
                                  preferred_element_type=F32) + bias4 for g in groups]
        m_new = [jnp.maximum(carry[g][0], jnp.max(logits[g], axis=0, keepdims=True)) for g in groups]
        p = [jnp.exp2(logits[g] - m_new[g]) for g in groups]
        alpha = [jnp.exp2(carry[g][0] - m_new[g]) for g in groups]
        l_new = [alpha[g] * carry[g][1] + jnp.sum(p[g], axis=0, keepdims=True) for g in groups]
        pv = [_dot(vt_ref[kt, g], p[g].astype(BF16)) for g in groups]
        return tuple((m_new[g], l_new[g], alpha[g] * carry[g][2] + pv[g]) for g in groups)

    init = tuple((jnp.full((1, nr), NEG_BIG, F32), jnp.zeros((1, nr), F32), jnp.zeros((128, nr), F32))
                 for _ in range(B_KV_HEADS))
    fin = lax.fori_loop(0, n_tiles, attend_tile, init)
    for g in range(B_KV_HEADS):
        _, l_run, acc = fin[g]
        og = acc / l_run
        heads = [og[:, j * tq:(j + 1) * tq].T for j in range(heads_per_kv)]
        for pp in range(heads_per_kv // 2):
            col = (g * (heads_per_kv // 2) + pp) * 128
            o_ref[:, col:col + 128] = jnp.where(low_half, heads[2 * pp], heads[2 * pp + 1]).astype(BF16)


def _topk_mask(topk, nt, small, r_max, r_min, st_ref):
    _, tk, tq = st_ref.shape
    kf = float(topk)

    def tiles():
        return [st_ref[t] for t in range(nt)]

    def q_sum(pred):
        tot = None
        for t, x in enumerate(tiles()):
            c = jnp.sum(pred(x, t).astype(F32), axis=0, keepdims=True)
            tot = c if tot is None else tot + c
        return tot

    def q_max(val):
        best = None
        for t, x in enumerate(tiles()):
            c = jnp.max(val(x, t), axis=0, keepdims=True)
            best = c if best is None else jnp.maximum(best, c)
        return best

    hi0 = r_max + jnp.maximum(jnp.abs(r_max), 1e-30) * 1e-6

    def bisect(_, carry):
        lo, hi = carry
        mid = 0.5 * (lo + hi)
        ge = q_sum(lambda x, t: x >= mid) >= kf
        return jnp.where(ge, mid, lo), jnp.where(ge, hi, mid)

    lo, hi = lax.fori_loop(0, BISECT_STEPS, bisect, (r_min, hi0))

    def peel_cond(carry):
        return jnp.sum(1.0 - carry[0]) > 0.0

    def peel(carry):
        done, thr, hi_c, n_ge = carry
        v1 = q_max(lambda x, t: jnp.where(x < hi_c, x, -jnp.inf))
        c1 = q_sum(lambda x, t: x >= v1)
        ok = c1 >= kf
        act = done < 0.5
        thr = jnp.where(act & ok, v1, thr)
        n_ge = jnp.where(act & ok, c1, n_ge)
        hi_c = jnp.where(act & (~ok), v1, hi_c)
        return jnp.where(ok, 1.0, done), thr, hi_c, n_ge

    _, thr, _, n_ge = lax.while_loop(peel_cond, peel, (jnp.where(small, 1.0, 0.0), lo, hi, jnp.full_like(lo, kf)))

    def key_index(t):
        return (t * tk + lax.broadcasted_iota(jnp.int32, (tk, tq), 0)).astype(F32)

    contested = jnp.sum(jnp.where((n_ge > kf) & (~small), 1.0, 0.0)) > 0.0
    last = float(nt * tk - 1)

    def tie_cut():
        need = kf - q_sum(lambda x, t: x > thr)

        def tie_search(_, carry):
            jlo, jhi = carry
            mid = jnp.floor(0.5 * (jlo + jhi))
            ge = q_sum(lambda x, t: (x == thr) & (key_index(t) <= mid)) >= need
            return jnp.where(ge, jlo, mid), jnp.where(ge, mid, jhi)

        n_iter = int(np.ceil(np.log2(nt * tk))) + 1
        return lax.fori_loop(0, n_iter, tie_search,
                             (jnp.full((1, tq), -1.0, F32), jnp.full((1, tq), last, F32)))[1]

    jcut = lax.cond(contested, tie_cut, lambda: jnp.full((1, tq), last, F32))
    for t, x in enumerate(tiles()):
        sel = (x > -jnp.inf) & (small | (x > thr) | ((x == thr) & (key_index(t) <= jcut)))
        st_ref[t] = jnp.where(sel, 0.0, NEG_BIG)


def _dsa(z3, zs3, tq=512, tk=512):
    b, s, _ = z3.shape
    topk = min(TOPK_MAX, s // 4)
    kernel = lambda *refs: _dsa_kernel(topk, tk, *refs)
    qblock = lambda w, cb: pl.BlockSpec((None, tq, w), lambda bi, qi, _c=cb: (bi, qi, _c))
    keys = lambda w, cb: pl.BlockSpec((None, s, w), lambda bi, qi, _c=cb: (bi, 0, _c))
    return pl.pallas_call(
        kernel,
        grid=(b, s // tq),
        in_specs=[qblock(512, O_BQ // 512), qblock(512, O_IQ // 512), keys(128, O_KD // 128),
                  keys(128, O_KD // 128 + 1), keys(256, O_VD // 256), keys(128, O_IK // 128), qblock(128, 0)],
        out_specs=pl.BlockSpec((None, tq, B_HEADS * B_HD), lambda bi, qi: (bi, qi, 0)),
        out_shape=jax.ShapeDtypeStruct((b, s, B_HEADS * B_HD), BF16),
        scratch_shapes=[pltpu.VMEM((s // tk, tk, tq), F32),
                        pltpu.VMEM((s // tk, B_KV_HEADS, 128, tk), BF16)],
        compiler_params=pltpu.CompilerParams(dimension_semantics=("parallel", "arbitrary"),
                                             vmem_limit_bytes=VMEM_LIMIT),
        name="dsa",
    )(z3, z3, z3, z3, z3, z3, zs3)


def _merge_kernel(x_ref, oa_ref, ob_ref, ga_ref, gb_ref, bg_ref, wa_ref, wb_ref, wo_ref, n2_ref, wr_ref,
                  br_ref, x1_ref, h2_ref, rl_ref):
    pa = _dot(oa_ref[...], wa_ref[...])
    pb = _dot(ob_ref[...], wb_ref[...])
    bgv = bg_ref[...]
    ga = jax.nn.sigmoid(ga_ref[...].astype(F32) + bgv[:, :D_MODEL])
    gb = jax.nn.sigmoid(gb_ref[...].astype(F32) + bgv[:, D_MODEL:])
    merged = ga * pa + gb * pb
    x1 = x_ref[...] + _mm(merged, wo_ref[...])
    x1_ref[...] = x1
    h2 = _rmsnorm_rows(x1, n2_ref[...])
    hb, hl = _split(h2)
    h2_ref[...] = hb
    s1 = _dot(hb, wr_ref[...])
    s2 = _dot(hl, wr_ref[:, :128])
    rl_ref[...] = s1[:, :128] + s1[:, 128:] + s2 + br_ref[...]


def _merge(x2d, oa, ob, z2d, b_gate, wa, wb, wo, n2, wr, br, tm=512):
    n = x2d.shape[0]
    full = lambda shape: pl.BlockSpec(shape, lambda i: (0, 0))
    return pl.pallas_call(
        _merge_kernel,
        grid=(n // tm,),
        in_specs=[
            pl.BlockSpec((tm, D_MODEL), lambda i: (i, 0)),
            pl.BlockSpec((tm, 512), lambda i: (i, 0)),
            pl.BlockSpec((tm, 512), lambda i: (i, 0)),
            pl.BlockSpec((tm, D_MODEL), lambda i: (i, O_GA // D_MODEL)),
            pl.BlockSpec((tm, D_MODEL), lambda i: (i, O_GB // D_MODEL)),
            full((1, 2 * D_MODEL)), full((512, D_MODEL)), full((512, D_MODEL)), full((D_MODEL, D_MODEL)),
            full((1, D_MODEL)), full((D_MODEL, 256)), full((1, 128)),
        ],
        out_specs=[
            pl.BlockSpec((tm, D_MODEL), lambda i: (i, 0)),
            pl.BlockSpec((tm, D_MODEL), lambda i: (i, 0)),
            pl.BlockSpec((tm, 128), lambda i: (i, 0)),
        ],
        out_shape=[jax.ShapeDtypeStruct((n, D_MODEL), F32), jax.ShapeDtypeStruct((n, D_MODEL), BF16),
                   jax.ShapeDtypeStruct((n, 128), F32)],
        compiler_params=pltpu.CompilerParams(dimension_semantics=("parallel",),
                                             vmem_limit_bytes=VMEM_LIMIT),
        name="merge",
    )(x2d, oa, ob, z2d, z2d, b_gate, wa, wb, wo, n2, wr, br)


R_GROUP, R_EXPERT = 0, 8


def _routing_weights(rl):
    t = rl.T
    n_tok = t.shape[1]
    gl = t[R_GROUP:R_GROUP + N_GROUPS]
    gidx = lax.broadcasted_iota(jnp.int32, gl.shape, 0)
    gmax = jnp.max(gl, axis=0, keepdims=True)
    gsel = jnp.min(jnp.where(gl == gmax, gidx, N_GROUPS), axis=0, keepdims=True)
    ggate = 1.0 / jnp.sum(jnp.exp(gl - gmax), axis=0, keepdims=True)
    el = t[R_EXPERT:R_EXPERT + N_EXPERTS]
    eidx = lax.broadcasted_iota(jnp.int32, el.shape, 0)
    e_lo = gsel * EXPERTS_PER_GROUP
    emask = (eidx >= e_lo) & (eidx < e_lo + EXPERTS_PER_GROUP)
    el = jnp.where(emask, el, -jnp.inf)
    emax = jnp.max(el, axis=0, keepdims=True)
    ee = jnp.where(emask, jnp.exp(el - emax), 0.0)
    ep = jnp.where(emask, ee / jnp.sum(ee, axis=0, keepdims=True), -1.0)
    p1 = jnp.max(ep, axis=0, keepdims=True)
    i1 = jnp.min(jnp.where(ep == p1, eidx, N_EXPERTS), axis=0, keepdims=True)
    ep2 = jnp.where(eidx == i1, -1.0, ep)
    p2 = jnp.max(ep2, axis=0, keepdims=True)
    i2 = jnp.min(jnp.where(ep2 == p2, eidx, N_EXPERTS), axis=0, keepdims=True)
    tot = p1 + p2
    comb_t = (jnp.where(eidx == i1, ggate * (p1 / tot), 0.0)
              + jnp.where(eidx == i2, ggate * (p2 / tot), 0.0))
    full = jnp.concatenate([jnp.zeros((R_EXPERT, n_tok), F32), comb_t,
                            jnp.zeros((128 - R_EXPERT - N_EXPERTS, n_tok), F32)], axis=0)
    return full.T


def _moe_kernel(x1_ref, h2_ref, rl_ref, w1_ref, w3_ref, w2_ref, fg_ref, o_ref, y_ref, comb_ref):
    step = pl.program_id(1)
    per_step = w1_ref.shape[0]

    @pl.when(step == 0)
    def _():
        comb_ref[...] = _routing_weights(rl_ref[...])
        y_ref[...] = jnp.zeros(y_ref.shape, F32)

    h = h2_ref[...]
    comb = comb_ref[...]
    lane = lax.broadcasted_iota(jnp.int32, comb.shape, 1)
    acts = []
    for j in range(per_step):
        a = _dot(h, w1_ref[j])
        b = _dot(h, w3_ref[j])
        ce = jnp.sum(jnp.where(lane == R_EXPERT + step * per_step + j, comb, 0.0), axis=1, keepdims=True)
        acts.append(((a * jax.nn.sigmoid(a)) * b * ce).astype(BF16))
    y_ref[...] += _dot(jnp.concatenate(acts, axis=1), w2_ref[...])

    @pl.when(step == pl.num_programs(1) - 1)
    def _():
        o_ref[...] = _rmsnorm_rows(x1_ref[...] + y_ref[...], fg_ref[...])


def _moe(x1, h2, rl, w1, w3, w2, fg, tm=1024, per_step=4):
    n = x1.shape[0]
    w2g = w2.reshape(N_EXPERTS // per_step, per_step * D_EXPERT, D_MODEL)
    return pl.pallas_call(
        _moe_kernel,
        grid=(n // tm, N_EXPERTS // per_step),
        in_specs=[
            pl.BlockSpec((tm, D_MODEL), lambda i, e: (i, 0)),
            pl.BlockSpec((tm, D_MODEL), lambda i, e: (i, 0)),
            pl.BlockSpec((tm, 128), lambda i, e: (i, 0)),
            pl.BlockSpec((per_step, D_MODEL, D_EXPERT), lambda i, e: (e, 0, 0)),
            pl.BlockSpec((per_step, D_MODEL, D_EXPERT), lambda i, e: (e, 0, 0)),
            pl.BlockSpec((None, per_step * D_EXPERT, D_MODEL), lambda i, e: (e, 0, 0)),
            pl.BlockSpec((1, D_MODEL), lambda i, e: (0, 0)),
        ],
        out_specs=pl.BlockSpec((tm, D_MODEL), lambda i, e: (i, 0)),
        out_shape=jax.ShapeDtypeStruct((n, D_MODEL), F32),
        scratch_shapes=[pltpu.VMEM((tm, D_MODEL), F32), pltpu.VMEM((tm, 128), F32)],
        compiler_params=pltpu.CompilerParams(dimension_semantics=("parallel", "arbitrary"),
                                             vmem_limit_bytes=VMEM_LIMIT),
        name="moe",
    )(x1, h2, rl, w1, w3, w2g, fg)


_W_OFF = {}
_off = 0
for _name, _n in (("a_q", 512), ("a_k", 512), ("a_v", 512), ("a_z", 512), ("a_beta", 8), ("a_alpha", 8),
                  ("b_q", 512), ("b_k", 128), ("b_v", 128), ("i_q", 512), ("i_k", 64), ("i_w", 8),
                  ("gate_a", 1024), ("gate_b", 1024)):
    _W_OFF[_name] = (_off, _off + _n)
    _off += _n


def _cols(w, name, lo=0, hi=None):
    a, b = _W_OFF[name]
    return w[:, a + lo:(b if hi is None else a + hi)]


def _hi_lo_pair(w_small):
    pad = jnp.zeros((w_small.shape[0], 128 - w_small.shape[1]), F32)
    w = jnp.concatenate([w_small.astype(F32), pad], axis=1)
    hi, lo = _split(w)
    return jnp.concatenate([hi, lo], axis=1)


def _layout_w_in(w):
    wb = w.astype(BF16)
    k0, k1 = _cols(wb, "b_k", 0, 64), _cols(wb, "b_k", 64, 128)
    v0, v1 = _cols(wb, "b_v", 0, 64), _cols(wb, "b_v", 64, 128)
    ik = _cols(wb, "i_k")
    main = jnp.concatenate([
        _cols(wb, "a_q"), _cols(wb, "a_k"), _cols(wb, "a_v"), _cols(wb, "a_z"), _cols(wb, "gate_a"),
        _cols(wb, "gate_b"), _cols(wb, "b_q"), _cols(wb, "i_q"), k0, k0, k1, k1, v0, v0, v1, v1, ik, ik],
        axis=1)
    small = _hi_lo_pair(jnp.concatenate([_cols(w, "a_beta"), _cols(w, "a_alpha"), _cols(w, "i_w")], axis=1))
    return main, small


def kernel(x, positions, norm1_g, w_in, b_gate, conv_w, a_log, dt_bias, a_norm_g, w_proj_a, w_proj_b, w_out,
           norm2_g, w_router_group, b_router_group, w_router_expert, b_router_expert, w_exp_gate, w_exp_up,
           w_exp_down, final_norm_g):
    b, s, d = x.shape
    n = b * s
    depth = w_in.shape[0]
    xc = x.reshape(n, d).astype(F32)
    for l in range(depth):
        w_main, w_small = _layout_w_in(w_in[l])
        z, zs = _in_proj(xc, norm1_g[l][None, :].astype(F32), w_main, w_small, positions)
        z3 = z.reshape(b, s, Z_W)
        zs3 = zs.reshape(b, s, 128)
        o_a = _gdn(z3, zs3, conv_w[l], a_log[l], dt_bias[l], a_norm_g[l])
        o_b = _dsa(z3, zs3)
        gap = R_EXPERT - N_GROUPS
        wr = _hi_lo_pair(jnp.concatenate([w_router_group[l], jnp.zeros((d, gap), F32), w_router_expert[l]],
                                         axis=1))
        br = jnp.concatenate([b_router_group[l], jnp.zeros((gap,), F32), b_router_expert[l],
                              jnp.zeros((128 - R_EXPERT - N_EXPERTS,), F32)])[None, :].astype(F32)
        x1, h2, rl = _merge(xc, o_a.reshape(n, -1), o_b.reshape(n, -1), z, b_gate[l][None, :].astype(F32),
                            w_proj_a[l].astype(BF16), w_proj_b[l].astype(BF16), w_out[l].astype(BF16),
                            norm2_g[l][None, :].astype(F32), wr, br)
        last = l == depth - 1
        fg = final_norm_g[None, :].astype(F32) if last else None
        if not last:
            raise NotImplementedError("only the final layer fuses the output norm")
        xc = _moe(x1, h2, rl, w_exp_gate[l].astype(BF16), w_exp_up[l].astype(BF16),
                  w_exp_down[l].astype(BF16), fg)
    return xc.reshape(b, s, d).astype(x.dtype)
```

```python
import numpy as np
import jax
import jax.numpy as jnp
from jax import lax
from jax.experimental import pallas as pl
from jax.experimental.pallas import tpu as pltpu

F32 = jnp.float32
BF16 = jnp.bfloat16

D_MODEL = 1024
CHUNK = 64
EPS = 1e-6
ROPE_THETA = 10000.0
A_HEADS = 8
A_DK = 64
A_DV = 64
CONV_K = 4
B_HEADS = 8
B_KV_HEADS = 2
B_HD = 64
IDX_HEADS = 8
IDX_HD = 64
TOPK_MAX = 256
N_GROUPS = 4
EXPERTS_PER_GROUP = 4
N_EXPERTS = 16
D_EXPERT = 256

O_AQKV, O_AZ, O_GA, O_GB, O_BQ, O_IQ, O_KD, O_VD, O_IK = 0, 1536, 2048, 3072, 4096, 4608, 5120, 5376, 5632
Z_W = 5760
S_BETA, S_ALPHA, S_IW = 0, 8, 16

GROUP_HEADS = 2
BD = GROUP_HEADS * CHUNK
NEG_BIG = -1e30
LOG2E = 1.4426950408889634
BISECT_STEPS = 18
VMEM_LIMIT = 56 * 1024 * 1024


def _split(x):
    hi = x.astype(BF16)
    lo = (x - hi.astype(F32)).astype(BF16)
    return hi, lo


def _dot(a, b):
    return jnp.dot(a, b, preferred_element_type=F32)


def _mm(a, b):
    return _dot(a.astype(BF16), b.astype(BF16))


def _mm_nt(a, b):
    return lax.dot_general(a.astype(BF16), b.astype(BF16), (((1,), (1,)), ((), ())),
                           preferred_element_type=F32)


def _mm_exact_lhs(a_bf16, x):
    xh, xl = _split(x)
    return _dot(a_bf16, xh) + _dot(a_bf16, xl)


def _rmsnorm_rows(x, g):
    return x * lax.rsqrt(jnp.mean(x * x, axis=-1, keepdims=True) + EPS) * g


_Z_CHUNKS = tuple((o, min(512, Z_W - o)) for o in range(0, Z_W, 512))


def _rope(x, cs, sn, first):
    w = x.shape[1]
    rep = w // 128
    if rep > 1:
        cs, sn, first = (jnp.concatenate([a] * rep, axis=1) for a in (cs, sn, first))
    swapped = jnp.where(first, pltpu.roll(x, w - B_HD // 2, 1), pltpu.roll(x, B_HD // 2, 1))
    return x * cs + swapped * sn


def _in_proj_kernel(x_ref, g_ref, w_ref, ws_ref, pos_ref, inv_ref, z_ref, zs_ref):
    h = _rmsnorm_rows(x_ref[...], g_ref[...])
    hb, hl = _split(h)

    ang = pos_ref[...] * inv_ref[...]
    lane = lax.broadcasted_iota(jnp.int32, ang.shape, 1)
    first = (lane & (B_HD - 1)) < (B_HD // 2)
    cs = jnp.cos(ang)
    sn = jnp.sin(ang)
    sn = jnp.where(first, -sn, sn)

    for o, w in _Z_CHUNKS:
        r = _dot(hb, w_ref[:, o:o + w])
        if o == O_BQ:
            r = _rope(r, cs, sn, first) * (B_HD ** -0.5 * LOG2E)
        elif o == O_IQ or o == O_IK:
            r = _rope(r, cs, sn, first)
        elif o == O_KD:
            kw = O_VD - O_KD
            r = jnp.concatenate([_rope(r[:, :kw], cs, sn, first), r[:, kw:]], axis=1)
        z_ref[:, o:o + w] = r.astype(BF16)
    s1 = _dot(hb, ws_ref[...])
    s2 = _dot(hl, ws_ref[:, :128])
    zs_ref[...] = s1[:, :128] + s1[:, 128:] + s2


def _in_proj(x2d, g, w_main, w_small, positions, tm=512):
    n = x2d.shape[0]
    assert n % tm == 0 and x2d.shape[1] == D_MODEL and w_main.shape == (D_MODEL, Z_W)
    half = B_HD // 2
    inv = ROPE_THETA ** (-jnp.arange(half, dtype=F32) / half)
    inv128 = jnp.tile(inv, 4)[None, :]
    pos = positions.astype(F32).reshape(n, 1)
    return pl.pallas_call(
        _in_proj_kernel,
        grid=(n // tm,),
        in_specs=[
            pl.BlockSpec((tm, D_MODEL), lambda i: (i, 0)),
            pl.BlockSpec((1, D_MODEL), lambda i: (0, 0)),
            pl.BlockSpec((D_MODEL, Z_W), lambda i: (0, 0)),
            pl.BlockSpec((D_MODEL, 256), lambda i: (0, 0)),
            pl.BlockSpec((tm, 1), lambda i: (i, 0)),
            pl.BlockSpec((1, 128), lambda i: (0, 0)),
        ],
        out_specs=[
            pl.BlockSpec((tm, Z_W), lambda i: (i, 0)),
            pl.BlockSpec((tm, 128), lambda i: (i, 0)),
        ],
        out_shape=[jax.ShapeDtypeStruct((n, Z_W), BF16), jax.ShapeDtypeStruct((n, 128), F32)],
        compiler_params=pltpu.CompilerParams(dimension_semantics=("parallel",),
                                             vmem_limit_bytes=VMEM_LIMIT),
        name="in_proj",
    )(x2d, g, w_main, w_small, pos, inv128)


def _gdn_constants():
    r = np.arange(BD)
    same = (r[:, None] // CHUNK) == (r[None, :] // CHUNK)
    incl = same & (r[:, None] >= r[None, :])
    strict = same & (r[:, None] > r[None, :])
    eye = np.eye(BD, dtype=np.float32)
    ll = np.concatenate([incl, same], axis=0).astype(np.float32)
    n_groups = A_HEADS // GROUP_HEADS
    sel = np.zeros((2 * n_groups, BD, 128), np.float32)
    for gi in range(n_groups):
        for h in range(GROUP_HEADS):
            sel[gi * 2 + 0, h * CHUNK:(h + 1) * CHUNK, S_BETA + gi * GROUP_HEADS + h] = 1.0
            sel[gi * 2 + 1, h * CHUNK:(h + 1) * CHUNK, S_ALPHA + gi * GROUP_HEADS + h] = 1.0
    return (jnp.asarray(incl, F32), jnp.asarray(strict, F32), jnp.asarray(same, F32), jnp.asarray(eye),
            jnp.asarray(ll, BF16), jnp.asarray(sel))


def _tile_heads(x):
    return jnp.concatenate([x] * GROUP_HEADS, axis=0)


def _gdn_kernel(zq_ref, zz_ref, zs_ref, cw_ref, av_ref, ag_ref, incl_ref, strict_ref, bdm_ref, eye_ref,
                ll_ref, sel_ref, o_ref, ext_ref, st_ref):
    c = pl.program_id(1)
    nb, t = zq_ref.shape[0], zq_ref.shape[1]

    @pl.when(c == 0)
    def _():
        ext_ref[:, 0:8, :] = jnp.zeros((nb, 8, ext_ref.shape[2]), F32)
        st_ref[...] = jnp.zeros(st_ref.shape, F32)

    hw = A_HEADS * A_DK
    cw = cw_ref[...]
    av = av_ref[...]
    q_all, k_all, v_all, bg = [], [], [], []
    for bi in range(nb):
        ext_ref[bi, 8:8 + t, :] = zq_ref[bi].astype(F32)
        y = cw[0:1, :] * ext_ref[bi, pl.ds(8 - (CONV_K - 1), t), :]
        for j in range(1, CONV_K):
            y = y + cw[j:j + 1, :] * ext_ref[bi, pl.ds(8 - (CONV_K - 1) + j, t), :]
        ext_ref[bi, 0:8, :] = ext_ref[bi, t:t + 8, :]
        y = y * jax.nn.sigmoid(y)
        q_all.append(y[:, :hw])
        k_all.append(y[:, hw:2 * hw])
        v_all.append(y[:, 2 * hw:])

        sm = zs_ref[bi]
        lane = lax.broadcasted_iota(jnp.int32, sm.shape, 1)
        xg = sm + av[1:2, :]
        softplus = jnp.maximum(xg, 0.0) + jnp.log1p(jnp.exp(-jnp.abs(xg)))
        g_all = -jnp.exp(av[0:1, :]) * softplus
        bg.append(jnp.where(lane < S_ALPHA, jax.nn.sigmoid(sm), g_all))

    incl = incl_ref[...]
    strict = strict_ref[...]
    bdm = bdm_ref[...]
    eye = eye_ref[...]
    ll = ll_ref[...]
    incl_b = ll[:BD]

    n_chunks = t // CHUNK
    n_groups = A_HEADS // GROUP_HEADS
    chains = [(bi, ci, gi) for bi in range(nb) for ci in range(n_chunks) for gi in range(n_groups)]

    pre = {}
    for bi, ci, gi in chains:
        r0, c0 = ci * CHUNK, gi * BD
        bg4 = _tile_heads(bg[bi][r0:r0 + CHUNK])
        beta = jnp.sum(bg4 * sel_ref[gi * 2 + 0], axis=1, keepdims=True)
        gcol = jnp.sum(bg4 * sel_ref[gi * 2 + 1], axis=1, keepdims=True)
        gs = _mm_exact_lhs(ll, jnp.broadcast_to(gcol, (BD, 128)))
        g_cum = gs[:BD, :1]
        g_last = gs[BD:, :1]
        diff = _mm_exact_lhs(incl_b, gcol * strict)
        decay = jnp.where(incl > 0.0, jnp.exp(diff), 0.0)
        e_cum = jnp.exp(g_cum)
        kr = _tile_heads(k_all[bi][r0:r0 + CHUNK, c0:c0 + BD]) * bdm
        qr = _tile_heads(q_all[bi][r0:r0 + CHUNK, c0:c0 + BD]) * bdm
        vm = _tile_heads(v_all[bi][r0:r0 + CHUNK, c0:c0 + BD]) * bdm
        km = kr * lax.rsqrt(jnp.sum(kr * kr, axis=1, keepdims=True) + EPS)
        qm = qr * (lax.rsqrt(jnp.sum(qr * qr, axis=1, keepdims=True) + EPS) * (A_DK ** -0.5))
        kkqk = _mm_nt(jnp.concatenate([km, qm], axis=0), km)
        m = -(strict * beta * kkqk[:BD] * decay)
        pre[bi, ci, gi] = dict(beta=beta, g_cum=g_cum, g_last=g_last, e_cum=e_cum, km=km, qm=qm, vm=vm,
                               qk=kkqk[BD:] * decay, m=m, inv=eye + m)

    sq = CHUNK
    while sq > 2:
        for key in chains:
            p = pre[key]
            p["m"] = _mm(p["m"], p["m"])
        for key in chains:
            p = pre[key]
            p["inv"] = p["inv"] + _mm(p["inv"], p["m"])
        sq //= 2

    states = [st_ref[i] for i in range(nb * n_groups)]
    outs = [[] for _ in range(nb)]
    lanes = [(bi, gi) for bi in range(nb) for gi in range(n_groups)]
    for ci in range(n_chunks):
        ps = {k: pre[k[0], ci, k[1]] for k in lanes}
        sidx = {k: k[0] * n_groups + k[1] for k in lanes}
        kq_s = {k: _mm(jnp.concatenate([ps[k]["km"] * ps[k]["e_cum"], ps[k]["qm"] * ps[k]["e_cum"]], axis=0),
                       states[sidx[k]]) for k in lanes}
        v_new = {k: _mm(ps[k]["inv"], ps[k]["beta"] * (ps[k]["vm"] - kq_s[k][:BD])) for k in lanes}
        o_bd = {k: kq_s[k][BD:] + _mm(ps[k]["qk"], v_new[k]) for k in lanes}
        for k in lanes:
            p = ps[k]
            k_dec = p["km"] * jnp.exp(p["g_last"] - p["g_cum"])
            states[sidx[k]] = states[sidx[k]] * jnp.exp(p["g_last"]) + _mm(k_dec.T, v_new[k])
        for bi in range(nb):
            o_groups = []
            for gi in range(n_groups):
                ob = o_bd[bi, gi]
                ob = ob * lax.rsqrt(jnp.sum(ob * ob, axis=1, keepdims=True) * (1.0 / A_DV) + EPS)
                o_groups.append(sum(ob[h * CHUNK:(h + 1) * CHUNK] for h in range(GROUP_HEADS)))
            outs[bi].append(jnp.concatenate(o_groups, axis=1))

    for i, state in enumerate(states):
        st_ref[i] = state
    for bi in range(nb):
        o = outs[bi][0] if n_chunks == 1 else jnp.concatenate(outs[bi], axis=0)
        zz = zz_ref[bi].astype(F32)
        o_ref[bi] = (o * ag_ref[...] * (zz * jax.nn.sigmoid(zz))).astype(BF16)


def _gdn(z3, zs3, conv_w, a_log, dt_bias, a_norm_g, t=2 * CHUNK, nb=2):
    b, s, _ = z3.shape
    assert b % nb == 0 and s % t == 0 and t % CHUNK == 0 and t >= 8
    consts = _gdn_constants()
    av = jnp.zeros((2, 128), F32)
    av = av.at[0, S_ALPHA:S_ALPHA + A_HEADS].set(a_log.astype(F32))
    av = av.at[1, S_ALPHA:S_ALPHA + A_HEADS].set(dt_bias.astype(F32))
    ag = jnp.tile(a_norm_g.astype(F32), A_HEADS)[None, :]
    conv_cols = 2 * A_HEADS * A_DK + A_HEADS * A_DV

    def const_spec(a):
        nd = a.ndim
        return pl.BlockSpec(a.shape, lambda bi, ci, _n=nd: (0,) * _n)

    small_in = (conv_w.astype(F32), av, ag) + consts
    return pl.pallas_call(
        _gdn_kernel,
        grid=(b // nb, s // t),
        in_specs=[
            pl.BlockSpec((nb, t, conv_cols), lambda bi, ci: (bi, ci, O_AQKV // conv_cols)),
            pl.BlockSpec((nb, t, 512), lambda bi, ci: (bi, ci, O_AZ // 512)),
            pl.BlockSpec((nb, t, 128), lambda bi, ci: (bi, ci, 0)),
        ] + [const_spec(a) for a in small_in],
        out_specs=pl.BlockSpec((nb, t, A_HEADS * A_DV), lambda bi, ci: (bi, ci, 0)),
        out_shape=jax.ShapeDtypeStruct((b, s, A_HEADS * A_DV), BF16),
        scratch_shapes=[pltpu.VMEM((nb, 8 + t, conv_cols), F32),
                        pltpu.VMEM((nb * (A_HEADS // GROUP_HEADS), BD, BD), F32)],
        compiler_params=pltpu.CompilerParams(dimension_semantics=("parallel", "arbitrary"),
                                             vmem_limit_bytes=VMEM_LIMIT),
        name="gdn",
    )(z3, z3, zs3, *small_in)


def _dsa_kernel(topk, tk, q_ref, iq_ref, ka_ref, kb_ref, vd_ref, ik_ref, zs_ref, o_ref, st_ref, vt_ref):
    qb = pl.program_id(1)
    s = ka_ref.shape[0]
    tq = q_ref.shape[0]
    nkt_max = s // tk
    r0 = qb * tq
    n_tiles = (r0 + tq + tk - 1) // tk
    heads_per_kv = B_HEADS // B_KV_HEADS

    @pl.when(qb == 0)
    def _():
        for t in range(nkt_max):
            for g in range(B_KV_HEADS):
                v_tile = vd_ref[t * tk:(t + 1) * tk, g * 128:(g + 1) * 128]
                vt_ref[t, g] = v_tile.astype(F32).T.astype(BF16)

    lane128 = lax.broadcasted_iota(jnp.int32, (tq, 128), 1)
    low_half = lane128 < B_HD
    high_half = lane128 >= B_HD

    def head_slab(x, h):
        slab = x[:, (h // 2) * 128:(h // 2 + 1) * 128]
        return jnp.where(low_half if h % 2 == 0 else high_half, slab, jnp.zeros_like(slab))

    zs_t = zs_ref[...].T
    iw_scale = (IDX_HEADS ** -0.5) * (IDX_HD ** -0.5)
    wrows = [zs_t[S_IW + h:S_IW + h + 1, :] * iw_scale for h in range(IDX_HEADS)]
    qrow = r0 + lax.broadcasted_iota(jnp.int32, (1, tq), 1)
    limit = ((qrow >> 6) + 1) << 6
    small = limit <= topk

    iq = iq_ref[...]
    iq_heads = [head_slab(iq, h) for h in range(IDX_HEADS)]

    def index_tile(kt, carry):
        r_max, r_min = carry
        off = pl.multiple_of(kt * tk, tk)
        ikt = ik_ref[pl.ds(off, tk), :]
        acc = jnp.zeros((tk, tq), F32)
        for h in range(IDX_HEADS):
            sc = lax.dot_general(ikt, iq_heads[h], (((1,), (1,)), ((), ())), preferred_element_type=F32)
            acc = acc + wrows[h] * jnp.maximum(sc, 0.0)
        valid = off + lax.broadcasted_iota(jnp.int32, (tk, tq), 0) < limit
        masked = jnp.where(valid, acc, -jnp.inf)
        st_ref[kt] = masked
        r_max = jnp.maximum(r_max, jnp.max(masked, axis=0, keepdims=True))
        r_min = jnp.minimum(r_min, jnp.min(acc, axis=0, keepdims=True))
        return r_max, r_min

    r_max, r_min = lax.fori_loop(0, n_tiles, index_tile,
                                 (jnp.full((1, tq), -jnp.inf, F32), jnp.full((1, tq), jnp.inf, F32)))

    all_kept = r0 + tq <= min(topk, tk)

    @pl.when(all_kept)
    def _():
        st_ref[0] = jnp.where(st_ref[0] > -jnp.inf, 0.0, NEG_BIG)

    for j in range(nkt_max):
        @pl.when(jnp.logical_and(n_tiles == j + 1, jnp.logical_not(all_kept)))
        def _(j=j):
            _topk_mask(topk, j + 1, small, r_max, r_min, st_ref)

    q = q_ref[...]
    nr = heads_per_kv * tq
    qcols = [jnp.concatenate([head_slab(q, g * heads_per_kv + j) for j in range(heads_per_kv)], axis=0)
             for g in range(B_KV_HEADS)]
    k_refs = (ka_ref, kb_ref)

    def attend_tile(kt, carry):
        off = pl.multiple_of(kt * tk, tk)
        bias = st_ref[kt]
        bias4 = jnp.concatenate([bias] * heads_per_kv, axis=1)
        groups = range(B_KV_HEADS)
        logits = [lax.dot_general(k_refs[g][pl.ds(off, tk), :], qcols[g], (((1,), (1,)), ((), ())),
                                  preferred_element_type=F32) + bias4 for g in groups]
        m_new = [jnp.maximum(carry[g][0], jnp.max(logits[g], axis=0, keepdims=True)) for g in groups]
        p = [jnp.exp2(logits[g] - m_new[g]) for g in groups]
        alpha = [jnp.exp2(carry[g][0] - m_new[g]) for g in groups]
        l_new = [alpha[g] * carry[g][1] + jnp.sum(p[g], axis=0, keepdims=True) for g in groups]
        pv = [_dot(vt_ref[kt, g], p[g].astype(BF16)) for g in groups]
        return tuple((m_new[g], l_new[g], alpha[g] * carry[g][2] + pv[g]) for g in groups)

    init = tuple((jnp.full((1, nr), NEG_BIG, F32), jnp.zeros((1, nr), F32), jnp.zeros((128, nr), F32))
                 for _ in range(B_KV_HEADS))
    fin = lax.fori_loop(0, n_tiles, attend_tile, init)
    for g in range(B_KV_HEADS):
        _, l_run, acc = fin[g]
        og = acc / l_run
        heads = [og[:, j * tq:(j + 1) * tq].T for j in range(heads_per_kv)]
        for pp in range(heads_per_kv // 2):
            col = (g * (heads_per_kv // 2) + pp) * 128
            o_ref[:, col:col + 128] = jnp.where(low_half, heads[2 * pp], heads[2 * pp + 1]).astype(BF16)


def _topk_mask(topk, nt, small, r_max, r_min, st_ref):
    _, tk, tq = st_ref.shape
    kf = float(topk)

    def tiles():
        return [st_ref[t] for t in range(nt)]

    def q_sum(pred):
        tot = None
        for t, x in enumerate(tiles()):
            c = jnp.sum(pred(x, t).astype(F32), axis=0, keepdims=True)
            tot = c if tot is None else tot + c
        return tot

    def q_max(val):
        best = None
        for t, x in enumerate(tiles()):
            c = jnp.max(val(x, t), axis=0, keepdims=True)
            best = c if best is None else jnp.maximum(best, c)
        return best

    hi0 = r_max + jnp.maximum(jnp.abs(r_max), 1e-30) * 1e-6

    def bisect(_, carry):
        lo, hi = carry
        mid = 0.5 * (lo + hi)
        ge = q_sum(lambda x, t: x >= mid) >= kf
        return jnp.where(ge, mid, lo), jnp.where(ge, hi, mid)

    lo, hi = lax.fori_loop(0, BISECT_STEPS, bisect, (r_min, hi0))

    def peel_cond(carry):
        return jnp.sum(1.0 - carry[0]) > 0.0

    def peel(carry):
        done, thr, hi_c, n_ge = carry
        v1 = q_max(lambda x, t: jnp.where(x < hi_c, x, -jnp.inf))
        c1 = q_sum(lambda x, t: x >= v1)
        ok = c1 >= kf
        act = done < 0.5
        thr = jnp.where(act & ok, v1, thr)
        n_ge = jnp.where(act & ok, c1, n_ge)
        hi_c = jnp.where(act & (~ok), v1, hi_c)
        return jnp.where(ok, 1.0, done), thr, hi_c, n_ge

    _, thr, _, n_ge = lax.while_loop(peel_cond, peel, (jnp.where(small, 1.0, 0.0), lo, hi, jnp.full_like(lo, kf)))

    def key_index(t):
        return (t * tk + lax.broadcasted_iota(jnp.int32, (tk, tq), 0)).astype(F32)

    contested = jnp.sum(jnp.where((n_ge > kf) & (~small), 1.0, 0.0)) > 0.0
    last = float(nt * tk - 1)

    def tie_cut():
        need = kf - q_sum(lambda x, t: x > thr)

        def tie_search(_, carry):
            jlo, jhi = carry
            mid = jnp.floor(0.5 * (jlo + jhi))
            ge = q_sum(lambda x, t: (x == thr) & (key_index(t) <= mid)) >= need
            return jnp.where(ge, jlo, mid), jnp.where(ge, mid, jhi)

        n_iter = int(np.ceil(np.log2(nt * tk))) + 1
        return lax.fori_loop(0, n_iter, tie_search,
                             (jnp.full((1, tq), -1.0, F32), jnp.full((1, tq), last, F32)))[1]

    jcut = lax.cond(contested, tie_cut, lambda: jnp.full((1, tq), last, F32))
    for t, x in enumerate(tiles()):
        sel = (x > -jnp.inf) & (small | (x > thr) | ((x == thr) & (key_index(t) <= jcut)))
        st_ref[t] = jnp.where(sel, 0.0, NEG_BIG)


def _dsa(z3, zs3, tq=512, tk=512):
    b, s, _ = z3.shape
    assert s % tq == 0 and s % tk == 0 and tq % CHUNK == 0 and tk % CHUNK == 0
    topk = min(TOPK_MAX, s // 4)
    kernel = lambda *refs: _dsa_kernel(topk, tk, *refs)
    qblock = lambda w, cb: pl.BlockSpec((None, tq, w), lambda bi, qi, _c=cb: (bi, qi, _c))
    keys = lambda w, cb: pl.BlockSpec((None, s, w), lambda bi, qi, _c=cb: (bi, 0, _c))
    return pl.pallas_call(
        kernel,
        grid=(b, s // tq),
        in_specs=[qblock(512, O_BQ // 512), qblock(512, O_IQ // 512), keys(128, O_KD // 128),
                  keys(128, O_KD // 128 + 1), keys(256, O_VD // 256), keys(128, O_IK // 128), qblock(128, 0)],
        out_specs=pl.BlockSpec((None, tq, B_HEADS * B_HD), lambda bi, qi: (bi, qi, 0)),
        out_shape=jax.ShapeDtypeStruct((b, s, B_HEADS * B_HD), BF16),
        scratch_shapes=[pltpu.VMEM((s // tk, tk, tq), F32),
                        pltpu.VMEM((s // tk, B_KV_HEADS, 128, tk), BF16)],
        compiler_params=pltpu.CompilerParams(dimension_semantics=("parallel", "arbitrary"),
                                             vmem_limit_bytes=VMEM_LIMIT),
        name="dsa",
    )(z3, z3, z3, z3, z3, z3, zs3)


def _merge_kernel(x_ref, oa_ref, ob_ref, ga_ref, gb_ref, bg_ref, wa_ref, wb_ref, wo_ref, n2_ref, wr_ref,
                  br_ref, x1_ref, h2_ref, rl_ref):
    pa = _dot(oa_ref[...], wa_ref[...])
    pb = _dot(ob_ref[...], wb_ref[...])
    bgv = bg_ref[...]
    ga = jax.nn.sigmoid(ga_ref[...].astype(F32) + bgv[:, :D_MODEL])
    gb = jax.nn.sigmoid(gb_ref[...].astype(F32) + bgv[:, D_MODEL:])
    merged = ga * pa + gb * pb
    x1 = x_ref[...] + _mm(merged, wo_ref[...])
    x1_ref[...] = x1
    h2 = _rmsnorm_rows(x1, n2_ref[...])
    hb, hl = _split(h2)
    h2_ref[...] = hb
    s1 = _dot(hb, wr_ref[...])
    s2 = _dot(hl, wr_ref[:, :128])
    rl_ref[...] = s1[:, :128] + s1[:, 128:] + s2 + br_ref[...]


def _merge(x2d, oa, ob, z2d, b_gate, wa, wb, wo, n2, wr, br, tm=512):
    n = x2d.shape[0]
    assert n % tm == 0 and oa.shape == ob.shape == (n, 512)
    full =lambda shape: pl.BlockSpec(shape, lambda i: (0, 0))
    return pl.pallas_call(
        _merge_kernel,
        grid=(n // tm,),
        in_specs=[
            pl.BlockSpec((tm, D_MODEL), lambda i: (i, 0)),
            pl.BlockSpec((tm, 512), lambda i: (i, 0)),
            pl.BlockSpec((tm, 512), lambda i: (i, 0)),
            pl.BlockSpec((tm, D_MODEL), lambda i: (i, O_GA // D_MODEL)),
            pl.BlockSpec((tm, D_MODEL), lambda i: (i, O_GB // D_MODEL)),
            full((1, 2 * D_MODEL)), full((512, D_MODEL)), full((512, D_MODEL)), full((D_MODEL, D_MODEL)),
            full((1, D_MODEL)), full((D_MODEL, 256)), full((1, 128)),
        ],
        out_specs=[
            pl.BlockSpec((tm, D_MODEL), lambda i: (i, 0)),
            pl.BlockSpec((tm, D_MODEL), lambda i: (i, 0)),
            pl.BlockSpec((tm, 128), lambda i: (i, 0)),
        ],
        out_shape=[jax.ShapeDtypeStruct((n, D_MODEL), F32), jax.ShapeDtypeStruct((n, D_MODEL), BF16),
                   jax.ShapeDtypeStruct((n, 128), F32)],
        compiler_params=pltpu.CompilerParams(dimension_semantics=("parallel",),
                                             vmem_limit_bytes=VMEM_LIMIT),
        name="merge",
    )(x2d, oa, ob, z2d, z2d, b_gate, wa, wb, wo, n2, wr, br)


R_GROUP, R_EXPERT = 0, 8


def _routing_weights(rl):
    t = rl.T
    n_tok = t.shape[1]
    gl = t[R_GROUP:R_GROUP + N_GROUPS]
    gidx = lax.broadcasted_iota(jnp.int32, gl.shape, 0)
    gmax = jnp.max(gl, axis=0, keepdims=True)
    gsel = jnp.min(jnp.where(gl == gmax, gidx, N_GROUPS), axis=0, keepdims=True)
    ggate = 1.0 / jnp.sum(jnp.exp(gl - gmax), axis=0, keepdims=True)
    el = t[R_EXPERT:R_EXPERT + N_EXPERTS]
    eidx = lax.broadcasted_iota(jnp.int32, el.shape, 0)
    e_lo = gsel * EXPERTS_PER_GROUP
    emask = (eidx >= e_lo) & (eidx < e_lo + EXPERTS_PER_GROUP)
    el = jnp.where(emask, el, -jnp.inf)
    emax = jnp.max(el, axis=0, keepdims=True)
    ee = jnp.where(emask, jnp.exp(el - emax), 0.0)
    ep = jnp.where(emask, ee / jnp.sum(ee, axis=0, keepdims=True), -1.0)
    p1 = jnp.max(ep, axis=0, keepdims=True)
    i1 = jnp.min(jnp.where(ep == p1, eidx, N_EXPERTS), axis=0, keepdims=True)
    ep2 = jnp.where(eidx == i1, -1.0, ep)
    p2 = jnp.max(ep2, axis=0, keepdims=True)
    i2 = jnp.min(jnp.where(ep2 == p2, eidx, N_EXPERTS), axis=0, keepdims=True)
    tot = p1 + p2
    comb_t = (jnp.where(eidx == i1, ggate * (p1 / tot), 0.0)
              + jnp.where(eidx == i2, ggate * (p2 / tot), 0.0))
    full = jnp.concatenate([jnp.zeros((R_EXPERT, n_tok), F32), comb_t,
                            jnp.zeros((128 - R_EXPERT - N_EXPERTS, n_tok), F32)], axis=0)
    return full.T


def _moe_kernel(x1_ref, h2_ref, rl_ref, w1_ref, w3_ref, w2_ref, fg_ref, o_ref, y_ref, comb_ref):
    step = pl.program_id(1)
    per_step = w1_ref.shape[0]

    @pl.when(step == 0)
    def _():
        comb_ref[...] = _routing_weights(rl_ref[...])
        y_ref[...] = jnp.zeros(y_ref.shape, F32)

    h = h2_ref[...]
    comb = comb_ref[...]
    lane = lax.broadcasted_iota(jnp.int32, comb.shape, 1)
    acts = []
    for j in range(per_step):
        a = _dot(h, w1_ref[j])
        b = _dot(h, w3_ref[j])
        ce = jnp.sum(jnp.where(lane == R_EXPERT + step * per_step + j, comb, 0.0), axis=1, keepdims=True)
        acts.append(((a * jax.nn.sigmoid(a)) * b * ce).astype(BF16))
    y_ref[...] += _dot(jnp.concatenate(acts, axis=1), w2_ref[...])

    @pl.when(step == pl.num_programs(1) - 1)
    def _():
        o_ref[...] = _rmsnorm_rows(x1_ref[...] + y_ref[...], fg_ref[...])


def _moe(x1, h2, rl, w1, w3, w2, fg, tm=1024, per_step=4):
    n = x1.shape[0]
    assert n % tm == 0 and N_EXPERTS % per_step == 0 and w1.shape == (N_EXPERTS, D_MODEL, D_EXPERT)
    w2g =w2.reshape(N_EXPERTS // per_step, per_step * D_EXPERT, D_MODEL)
    return pl.pallas_call(
        _moe_kernel,
        grid=(n // tm, N_EXPERTS // per_step),
        in_specs=[
            pl.BlockSpec((tm, D_MODEL), lambda i, e: (i, 0)),
            pl.BlockSpec((tm, D_MODEL), lambda i, e: (i, 0)),
            pl.BlockSpec((tm, 128), lambda i, e: (i, 0)),
            pl.BlockSpec((per_step, D_MODEL, D_EXPERT), lambda i, e: (e, 0, 0)),
            pl.BlockSpec((per_step, D_MODEL, D_EXPERT), lambda i, e: (e, 0, 0)),
            pl.BlockSpec((None, per_step * D_EXPERT, D_MODEL), lambda i, e: (e, 0, 0)),
            pl.BlockSpec((1, D_MODEL), lambda i, e: (0, 0)),
        ],
        out_specs=pl.BlockSpec((tm, D_MODEL), lambda i, e: (i, 0)),
        out_shape=jax.ShapeDtypeStruct((n, D_MODEL), F32),
        scratch_shapes=[pltpu.VMEM((tm, D_MODEL), F32), pltpu.VMEM((tm, 128), F32)],
        compiler_params=pltpu.CompilerParams(dimension_semantics=("parallel", "arbitrary"),
                                             vmem_limit_bytes=VMEM_LIMIT),
        name="moe",
    )(x1, h2, rl, w1, w3, w2g, fg)


_W_OFF = {}
_off = 0
for _name, _n in (("a_q", 512), ("a_k", 512), ("a_v", 512), ("a_z", 512), ("a_beta", 8), ("a_alpha", 8),
                  ("b_q", 512), ("b_k", 128), ("b_v", 128), ("i_q", 512), ("i_k", 64), ("i_w", 8),
                  ("gate_a", 1024), ("gate_b", 1024)):
    _W_OFF[_name] = (_off, _off + _n)
    _off += _n


def _cols(w, name, lo=0, hi=None):
    a, b = _W_OFF[name]
    return w[:, a + lo:(b if hi is None else a + hi)]


def _hi_lo_pair(w_small):
    pad = jnp.zeros((w_small.shape[0], 128 - w_small.shape[1]), F32)
    w = jnp.concatenate([w_small.astype(F32), pad], axis=1)
    hi, lo = _split(w)
    return jnp.concatenate([hi, lo], axis=1)


def _layout_w_in(w):
    wb = w.astype(BF16)
    k0, k1 = _cols(wb, "b_k", 0, 64), _cols(wb, "b_k", 64, 128)
    v0, v1 = _cols(wb, "b_v", 0, 64), _cols(wb, "b_v", 64, 128)
    ik = _cols(wb, "i_k")
    main = jnp.concatenate([
        _cols(wb, "a_q"), _cols(wb, "a_k"), _cols(wb, "a_v"), _cols(wb, "a_z"), _cols(wb, "gate_a"),
        _cols(wb, "gate_b"), _cols(wb, "b_q"), _cols(wb, "i_q"), k0, k0, k1, k1, v0, v0, v1, v1, ik, ik],
        axis=1)
    small = _hi_lo_pair(jnp.concatenate([_cols(w, "a_beta"), _cols(w, "a_alpha"), _cols(w, "i_w")], axis=1))
    return main, small


def kernel(x, positions, norm1_g, w_in, b_gate, conv_w, a_log, dt_bias, a_norm_g, w_proj_a, w_proj_b, w_out,
           norm2_g, w_router_group, b_router_group, w_router_expert, b_router_expert, w_exp_gate, w_exp_up,
           w_exp_down, final_norm_g):
    b, s, d = x.shape
    n = b * s
    assert w_in.shape[0] == 1 and d == D_MODEL, "one layer: the MoE kernel also applies the final norm"
    xc = x.reshape(n, d).astype(F32)
    for l in range(w_in.shape[0]):
        w_main, w_small = _layout_w_in(w_in[l])
        z, zs = _in_proj(xc, norm1_g[l][None, :].astype(F32), w_main, w_small, positions)
        z3 = z.reshape(b, s, Z_W)
        zs3 = zs.reshape(b, s, 128)
        o_a = _gdn(z3, zs3, conv_w[l], a_log[l], dt_bias[l], a_norm_g[l])
        o_b = _dsa(z3, zs3)
        gap = R_EXPERT - N_GROUPS
        wr = _hi_lo_pair(jnp.concatenate([w_router_group[l], jnp.zeros((d, gap), F32), w_router_expert[l]],
                                         axis=1))
        br = jnp.concatenate([b_router_group[l], jnp.zeros((gap,), F32), b_router_expert[l],
                              jnp.zeros((128 - R_EXPERT - N_EXPERTS,), F32)])[None, :].astype(F32)
        x1, h2, rl = _merge(xc, o_a.reshape(n, -1), o_b.reshape(n, -1), z, b_gate[l][None, :].astype(F32),
                            w_proj_a[l].astype(BF16), w_proj_b[l].astype(BF16), w_out[l].astype(BF16),
                            norm2_g[l][None, :].astype(F32), wr, br)
        xc = _moe(x1, h2, rl, w_exp_gate[l].astype(BF16), w_exp_up[l].astype(BF16),
                  w_exp_down[l].astype(BF16), final_norm_g[None, :].astype(F32))
    return xc.reshape(b, s, d).astype(x.dtype)
```

```python
import numpy as np
import jax
import jax.numpy as jnp
from jax import lax
from jax.experimental import pallas as pl
from jax.experimental.pallas import tpu as pltpu

F32 = jnp.float32
BF16 = jnp.bfloat16

D_MODEL = 1024
CHUNK = 64
EPS = 1e-6
ROPE_THETA = 10000.0
A_HEADS = 8
A_DK = 64
A_DV = 64
CONV_K = 4
B_HEADS = 8
B_KV_HEADS = 2
B_HD = 64
IDX_HEADS = 8
IDX_HD = 64
TOPK_MAX = 256
N_GROUPS = 4
EXPERTS_PER_GROUP = 4
N_EXPERTS = 16
D_EXPERT = 256

O_AQKV, O_AZ, O_GA, O_GB, O_BQ, O_IQ, O_KD, O_VD, O_IK = 0, 1536, 2048, 3072, 4096, 4608, 5120, 5376, 5632
Z_W = 5760
S_BETA, S_ALPHA, S_IW = 0, 8, 16

GROUP_HEADS = 2
BD = GROUP_HEADS * CHUNK
NEG_BIG = -1e30
LOG2E = 1.4426950408889634
BISECT_STEPS = 18
VMEM_LIMIT = 56 * 1024 * 1024


def _split(x):
    hi = x.astype(BF16)
    lo = (x - hi.astype(F32)).astype(BF16)
    return hi, lo


def _dot(a, b):
    return jnp.dot(a, b, preferred_element_type=F32)


def _mm(a, b):
    return _dot(a.astype(BF16), b.astype(BF16))


def _mm_nt(a, b):
    return lax.dot_general(a.astype(BF16), b.astype(BF16), (((1,), (1,)), ((), ())),
                           preferred_element_type=F32)


def _mm_exact_lhs(a_bf16, x):
    xh, xl = _split(x)
    return _dot(a_bf16, xh) + _dot(a_bf16, xl)


def _rmsnorm_rows(x, g):
    return x * lax.rsqrt(jnp.mean(x * x, axis=-1, keepdims=True) + EPS) * g


_Z_CHUNKS = tuple((o, min(512, Z_W - o)) for o in range(0, Z_W, 512))


def _rope(x, cs, sn, first):
    w = x.shape[1]
    rep = w // 128
    if rep > 1:
        cs, sn, first = (jnp.concatenate([a] * rep, axis=1) for a in (cs, sn, first))
    swapped = jnp.where(first, pltpu.roll(x, w - B_HD // 2, 1), pltpu.roll(x, B_HD // 2, 1))
    return x * cs + swapped * sn


def _in_proj_kernel(x_ref, g_ref, w_ref, ws_ref, pos_ref, inv_ref, z_ref, zs_ref):
    h = _rmsnorm_rows(x_ref[...], g_ref[...])
    hb, hl = _split(h)

    ang = pos_ref[...] * inv_ref[...]
    lane = lax.broadcasted_iota(jnp.int32, ang.shape, 1)
    first = (lane & (B_HD - 1)) < (B_HD // 2)
    cs = jnp.cos(ang)
    sn = jnp.sin(ang)
    sn = jnp.where(first, -sn, sn)

    for o, w in _Z_CHUNKS:
        r = _dot(hb, w_ref[:, o:o + w])
        if o == O_BQ:
            r = _rope(r, cs, sn, first) * (B_HD ** -0.5 * LOG2E)
        elif o == O_IQ or o == O_IK:
            r = _rope(r, cs, sn, first)
        elif o == O_KD:
            kw = O_VD - O_KD
            r = jnp.concatenate([_rope(r[:, :kw], cs, sn, first), r[:, kw:]], axis=1)
        z_ref[:, o:o + w] = r.astype(BF16)
    s1 = _dot(hb, ws_ref[...])
    s2 = _dot(hl, ws_ref[:, :128])
    zs_ref[...] = s1[:, :128] + s1[:, 128:] + s2


def _in_proj(x2d, g, w_main, w_small, positions, tm=512):
    n = x2d.shape[0]
    assert n % tm == 0 and x2d.shape[1] == D_MODEL and w_main.shape == (D_MODEL, Z_W)
    half = B_HD // 2
    inv = ROPE_THETA ** (-jnp.arange(half, dtype=F32) / half)
    inv128 = jnp.tile(inv, 4)[None, :]
    pos = positions.astype(F32).reshape(n, 1)
    return pl.pallas_call(
        _in_proj_kernel,
        grid=(n // tm,),
        in_specs=[
            pl.BlockSpec((tm, D_MODEL), lambda i: (i, 0)),
            pl.BlockSpec((1, D_MODEL), lambda i: (0, 0)),
            pl.BlockSpec((D_MODEL, Z_W), lambda i: (0, 0)),
            pl.BlockSpec((D_MODEL, 256), lambda i: (0, 0)),
            pl.BlockSpec((tm, 1), lambda i: (i, 0)),
            pl.BlockSpec((1, 128), lambda i: (0, 0)),
        ],
        out_specs=[
            pl.BlockSpec((tm, Z_W), lambda i: (i, 0)),
            pl.BlockSpec((tm, 128), lambda i: (i, 0)),
        ],
        out_shape=[jax.ShapeDtypeStruct((n, Z_W), BF16), jax.ShapeDtypeStruct((n, 128), F32)],
        compiler_params=pltpu.CompilerParams(dimension_semantics=("parallel",),
                                             vmem_limit_bytes=VMEM_LIMIT),
        name="in_proj",
    )(x2d, g, w_main, w_small, pos, inv128)


def _gdn_constants():
    r = np.arange(BD)
    same = (r[:, None] // CHUNK) == (r[None, :] // CHUNK)
    incl = same & (r[:, None] >= r[None, :])
    strict = same & (r[:, None] > r[None, :])
    eye = np.eye(BD, dtype=np.float32)
    ll = np.concatenate([incl, same], axis=0).astype(np.float32)
    n_groups = A_HEADS // GROUP_HEADS
    sel = np.zeros((2 * n_groups, BD, 128), np.float32)
    for gi in range(n_groups):
        for h in range(GROUP_HEADS):
            sel[gi * 2 + 0, h * CHUNK:(h + 1) * CHUNK, S_BETA + gi * GROUP_HEADS + h] = 1.0
            sel[gi * 2 + 1, h * CHUNK:(h + 1) * CHUNK, S_ALPHA + gi * GROUP_HEADS + h] = 1.0
    return (jnp.asarray(incl, F32), jnp.asarray(strict, F32), jnp.asarray(same, F32), jnp.asarray(eye),
            jnp.asarray(ll, BF16), jnp.asarray(sel))


def _tile_heads(x):
    return jnp.concatenate([x] * GROUP_HEADS, axis=0)


def _gdn_kernel(zq_ref, zz_ref, zs_ref, cw_ref, av_ref, ag_ref, incl_ref, strict_ref, bdm_ref, eye_ref,
                ll_ref, sel_ref, o_ref, ext_ref, st_ref):
    c = pl.program_id(1)
    nb, t = zq_ref.shape[0], zq_ref.shape[1]

    @pl.when(c == 0)
    def _():
        ext_ref[:, 0:8, :] = jnp.zeros((nb, 8, ext_ref.shape[2]), F32)
        st_ref[...] = jnp.zeros(st_ref.shape, F32)

    hw = A_HEADS * A_DK
    cw = cw_ref[...]
    av = av_ref[...]
    q_all, k_all, v_all, bg = [], [], [], []
    for bi in range(nb):
        ext_ref[bi, 8:8 + t, :] = zq_ref[bi].astype(F32)
        y = cw[0:1, :] * ext_ref[bi, pl.ds(8 - (CONV_K - 1), t), :]
        for j in range(1, CONV_K):
            y = y + cw[j:j + 1, :] * ext_ref[bi, pl.ds(8 - (CONV_K - 1) + j, t), :]
        ext_ref[bi, 0:8, :] = ext_ref[bi, t:t + 8, :]
        y = y * jax.nn.sigmoid(y)
        q_all.append(y[:, :hw])
        k_all.append(y[:, hw:2 * hw])
        v_all.append(y[:, 2 * hw:])

        sm = zs_ref[bi]
        lane = lax.broadcasted_iota(jnp.int32, sm.shape, 1)
        xg = sm + av[1:2, :]
        softplus = jnp.maximum(xg, 0.0) + jnp.log1p(jnp.exp(-jnp.abs(xg)))
        g_all = -jnp.exp(av[0:1, :]) * softplus
        bg.append(jnp.where(lane < S_ALPHA, jax.nn.sigmoid(sm), g_all))

    incl = incl_ref[...]
    strict = strict_ref[...]
    bdm = bdm_ref[...]
    eye = eye_ref[...]
    ll = ll_ref[...]
    incl_b = ll[:BD]

    n_chunks = t // CHUNK
    n_groups = A_HEADS // GROUP_HEADS
    chains = [(bi, ci, gi) for bi in range(nb) for ci in range(n_chunks) for gi in range(n_groups)]

    pre = {}
    for bi, ci, gi in chains:
        r0, c0 = ci * CHUNK, gi * BD
        bg4 = _tile_heads(bg[bi][r0:r0 + CHUNK])
        beta = jnp.sum(bg4 * sel_ref[gi * 2 + 0], axis=1, keepdims=True)
        gcol = jnp.sum(bg4 * sel_ref[gi * 2 + 1], axis=1, keepdims=True)
        gs = _mm_exact_lhs(ll, jnp.broadcast_to(gcol, (BD, 128)))
        g_cum = gs[:BD, :1]
        g_last = gs[BD:, :1]
        diff = _mm_exact_lhs(incl_b, gcol * strict)
        decay = jnp.where(incl > 0.0, jnp.exp(diff), 0.0)
        e_cum = jnp.exp(g_cum)
        kr = _tile_heads(k_all[bi][r0:r0 + CHUNK, c0:c0 + BD]) * bdm
        qr = _tile_heads(q_all[bi][r0:r0 + CHUNK, c0:c0 + BD]) * bdm
        vm = _tile_heads(v_all[bi][r0:r0 + CHUNK, c0:c0 + BD]) * bdm
        km = kr * lax.rsqrt(jnp.sum(kr * kr, axis=1, keepdims=True) + EPS)
        qm = qr * (lax.rsqrt(jnp.sum(qr * qr, axis=1, keepdims=True) + EPS) * (A_DK ** -0.5))
        kkqk = _mm_nt(jnp.concatenate([km, qm], axis=0), km)
        m = -(strict * beta * kkqk[:BD] * decay)
        pre[bi, ci, gi] = dict(beta=beta, g_cum=g_cum, g_last=g_last, e_cum=e_cum, km=km, qm=qm, vm=vm,
                               qk=kkqk[BD:] * decay, m=m, inv=eye + m)

    sq = CHUNK
    while sq > 2:
        for key in chains:
            p = pre[key]
            p["m"] = _mm(p["m"], p["m"])
        for key in chains:
            p = pre[key]
            p["inv"] = p["inv"] + _mm(p["inv"], p["m"])
        sq //= 2

    states = [st_ref[i] for i in range(nb * n_groups)]
    outs = [[] for _ in range(nb)]
    lanes = [(bi, gi) for bi in range(nb) for gi in range(n_groups)]
    for ci in range(n_chunks):
        ps = {k: pre[k[0], ci, k[1]] for k in lanes}
        sidx = {k: k[0] * n_groups + k[1] for k in lanes}
        kq_s = {k: _mm(jnp.concatenate([ps[k]["km"] * ps[k]["e_cum"], ps[k]["qm"] * ps[k]["e_cum"]], axis=0),
                       states[sidx[k]]) for k in lanes}
        v_new = {k: _mm(ps[k]["inv"], ps[k]["beta"] * (ps[k]["vm"] - kq_s[k][:BD])) for k in lanes}
        o_bd = {k: kq_s[k][BD:] + _mm(ps[k]["qk"], v_new[k]) for k in lanes}
        for k in lanes:
            p = ps[k]
            k_dec = p["km"] * jnp.exp(p["g_last"] - p["g_cum"])
            states[sidx[k]] = states[sidx[k]] * jnp.exp(p["g_last"]) + _mm(k_dec.T, v_new[k])
        for bi in range(nb):
            o_groups = []
            for gi in range(n_groups):
                ob = o_bd[bi, gi]
                ob = ob * lax.rsqrt(jnp.sum(ob * ob, axis=1, keepdims=True) * (1.0 / A_DV) + EPS)
                o_groups.append(sum(ob[h * CHUNK:(h + 1) * CHUNK] for h in range(GROUP_HEADS)))
            outs[bi].append(jnp.concatenate(o_groups, axis=1))

    for i, state in enumerate(states):
        st_ref[i] = state
    for bi in range(nb):
        o = outs[bi][0] if n_chunks == 1 else jnp.concatenate(outs[bi], axis=0)
        zz = zz_ref[bi].astype(F32)
        o_ref[bi] = (o * ag_ref[...] * (zz * jax.nn.sigmoid(zz))).astype(BF16)


def _gdn(z3, zs3, conv_w, a_log, dt_bias, a_norm_g, t=2 * CHUNK, nb=2):
    b, s, _ = z3.shape
    assert b % nb == 0 and s % t == 0 and t % CHUNK == 0 and t >= 8
    consts = _gdn_constants()
    av = jnp.zeros((2, 128), F32)
    av = av.at[0, S_ALPHA:S_ALPHA + A_HEADS].set(a_log.astype(F32))
    av = av.at[1, S_ALPHA:S_ALPHA + A_HEADS].set(dt_bias.astype(F32))
    ag = jnp.tile(a_norm_g.astype(F32), A_HEADS)[None, :]
    conv_cols = 2 * A_HEADS * A_DK + A_HEADS * A_DV

    def const_spec(a):
        nd = a.ndim
        return pl.BlockSpec(a.shape, lambda bi, ci, _n=nd: (0,) * _n)

    small_in = (conv_w.astype(F32), av, ag) + consts
    return pl.pallas_call(
        _gdn_kernel,
        grid=(b // nb, s // t),
        in_specs=[
            pl.BlockSpec((nb, t, conv_cols), lambda bi, ci: (bi, ci, O_AQKV // conv_cols)),
            pl.BlockSpec((nb, t, 512), lambda bi, ci: (bi, ci, O_AZ // 512)),
            pl.BlockSpec((nb, t, 128), lambda bi, ci: (bi, ci, 0)),
        ] + [const_spec(a) for a in small_in],
        out_specs=pl.BlockSpec((nb, t, A_HEADS * A_DV), lambda bi, ci: (bi, ci, 0)),
        out_shape=jax.ShapeDtypeStruct((b, s, A_HEADS * A_DV), BF16),
        scratch_shapes=[pltpu.VMEM((nb, 8 + t, conv_cols), F32),
                        pltpu.VMEM((nb * (A_HEADS // GROUP_HEADS), BD, BD), F32)],
        compiler_params=pltpu.CompilerParams(dimension_semantics=("parallel", "arbitrary"),
                                             vmem_limit_bytes=VMEM_LIMIT),
        name="gdn",
    )(z3, z3, zs3, *small_in)


def _dsa_kernel(topk, tk, q_ref, iq_ref, ka_ref, kb_ref, vd_ref, ik_ref, zs_ref, o_ref, st_ref, vt_ref):
    qb = pl.program_id(1)
    s = ka_ref.shape[0]
    tq = q_ref.shape[0]
    nkt_max = s // tk
    r0 = qb * tq
    n_tiles = (r0 + tq + tk - 1) // tk
    heads_per_kv = B_HEADS // B_KV_HEADS

    @pl.when(qb == 0)
    def _():
        for t in range(nkt_max):
            for g in range(B_KV_HEADS):
                v_tile = vd_ref[t * tk:(t + 1) * tk, g * 128:(g + 1) * 128]
                vt_ref[t, g] = v_tile.astype(F32).T.astype(BF16)

    lane128 = lax.broadcasted_iota(jnp.int32, (tq, 128), 1)
    low_half = lane128 < B_HD
    high_half = lane128 >= B_HD

    def head_slab(x, h):
        slab = x[:, (h // 2) * 128:(h // 2 + 1) * 128]
        return jnp.where(low_half if h % 2 == 0 else high_half, slab, jnp.zeros_like(slab))

    zs_t = zs_ref[...].T
    iw_scale = (IDX_HEADS ** -0.5) * (IDX_HD ** -0.5)
    wrows = [zs_t[S_IW + h:S_IW + h + 1, :] * iw_scale for h in range(IDX_HEADS)]
    qrow = r0 + lax.broadcasted_iota(jnp.int32, (1, tq), 1)
    limit = ((qrow >> 6) + 1) << 6
    small = limit <= topk

    iq = iq_ref[...]
    iq_heads = [head_slab(iq, h) for h in range(IDX_HEADS)]

    def index_tile(kt, carry):
        r_max, r_min = carry
        off = pl.multiple_of(kt * tk, tk)
        ikt = ik_ref[pl.ds(off, tk), :]
        acc = jnp.zeros((tk, tq), F32)
        for h in range(IDX_HEADS):
            sc = lax.dot_general(ikt, iq_heads[h], (((1,), (1,)), ((), ())), preferred_element_type=F32)
            acc = acc + wrows[h] * jnp.maximum(sc, 0.0)
        valid = off + lax.broadcasted_iota(jnp.int32, (tk, tq), 0) < limit
        masked = jnp.where(valid, acc, -jnp.inf)
        st_ref[kt] = masked
        r_max = jnp.maximum(r_max, jnp.max(masked, axis=0, keepdims=True))
        r_min = jnp.minimum(r_min, jnp.min(acc, axis=0, keepdims=True))
        return r_max, r_min

    r_max, r_min = lax.fori_loop(0, n_tiles, index_tile,
                                 (jnp.full((1, tq), -jnp.inf, F32), jnp.full((1, tq), jnp.inf, F32)))

    all_kept = r0 + tq <= min(topk, tk)

    @pl.when(all_kept)
    def _():
        st_ref[0] = jnp.where(st_ref[0] > -jnp.inf, 0.0, NEG_BIG)

    for j in range(nkt_max):
        @pl.when(jnp.logical_and(n_tiles == j + 1, jnp.logical_not(all_kept)))
        def _(j=j):
            _topk_mask(topk, j + 1, small, r_max, r_min, st_ref)

    q = q_ref[...]
    nr = heads_per_kv * tq
    qcols = [jnp.concatenate([head_slab(q, g * heads_per_kv + j) for j in range(heads_per_kv)], axis=0)
             for g in range(B_KV_HEADS)]
    k_refs = (ka_ref, kb_ref)

    def attend_tile(kt, carry):
        off = pl.multiple_of(kt * tk, tk)
        bias = st_ref[kt]
        bias4 = jnp.concatenate([bias] * heads_per_kv, axis=1)
        groups = range(B_KV_HEADS)
        logits = [lax.dot_general(k_refs[g][pl.ds(off, tk), :], qcols[g], (((1,), (1,)), ((), ())),
                                  preferred_element_type=F32) + bias4 for g in groups]
        m_new = [jnp.maximum(carry[g][0], jnp.max(logits[g], axis=0, keepdims=True)) for g in groups]
        p = [jnp.exp2(logits[g] - m_new[g]) for g in groups]
        alpha = [jnp.exp2(carry[g][0] - m_new[g]) for g in groups]
        l_new = [alpha[g] * carry[g][1] + jnp.sum(p[g], axis=0, keepdims=True) for g in groups]
        pv = [_dot(vt_ref[kt, g], p[g].astype(BF16)) for g in groups]
        return tuple((m_new[g], l_new[g], alpha[g] * carry[g][2] + pv[g]) for g in groups)

    init = tuple((jnp.full((1, nr), NEG_BIG, F32), jnp.zeros((1, nr), F32), jnp.zeros((128, nr), F32))
                 for _ in range(B_KV_HEADS))
    fin = lax.fori_loop(0, n_tiles, attend_tile, init)
    for g in range(B_KV_HEADS):
        _, l_run, acc = fin[g]
        og = acc / l_run
        heads = [og[:, j * tq:(j + 1) * tq].T for j in range(heads_per_kv)]
        for pp in range(heads_per_kv // 2):
            col = (g * (heads_per_kv // 2) + pp) * 128
            o_ref[:, col:col + 128] = jnp.where(low_half, heads[2 * pp], heads[2 * pp + 1]).astype(BF16)


def _topk_mask(topk, nt, small, r_max, r_min, st_ref):
    _, tk, tq = st_ref.shape
    kf = float(topk)

    def tiles():
        return [st_ref[t] for t in range(nt)]

    def q_sum(pred):
        tot = None
        for t, x in enumerate(tiles()):
            c = jnp.sum(pred(x, t).astype(F32), axis=0, keepdims=True)
            tot = c if tot is None else tot + c
        return tot

    def q_max(val):
        best = None
        for t, x in enumerate(tiles()):
            c = jnp.max(val(x, t), axis=0, keepdims=True)
            best = c if best is None else jnp.maximum(best, c)
        return best

    hi0 = r_max + jnp.maximum(jnp.abs(r_max), 1e-30) * 1e-6

    def bisect(_, carry):
        lo, hi = carry
        mid = 0.5 * (lo + hi)
        ge = q_sum(lambda x, t: x >= mid) >= kf
        return jnp.where(ge, mid, lo), jnp.where(ge, hi, mid)

    lo, hi = lax.fori_loop(0, BISECT_STEPS, bisect, (r_min, hi0))

    def peel_cond(carry):
        return jnp.sum(1.0 - carry[0]) > 0.0

    def peel(carry):
        done, thr, hi_c, n_ge = carry
        v1 = q_max(lambda x, t: jnp.where(x < hi_c, x, -jnp.inf))
        c1 = q_sum(lambda x, t: x >= v1)
        ok = c1 >= kf
        act = done < 0.5
        thr = jnp.where(act & ok, v1, thr)
        n_ge = jnp.where(act & ok, c1, n_ge)
        hi_c = jnp.where(act & (~ok), v1, hi_c)
        return jnp.where(ok, 1.0, done), thr, hi_c, n_ge

    _, thr, _, n_ge = lax.while_loop(peel_cond, peel, (jnp.where(small, 1.0, 0.0), lo, hi, jnp.full_like(lo, kf)))

    def key_index(t):
        return (t * tk + lax.broadcasted_iota(jnp.int32, (tk, tq), 0)).astype(F32)

    contested = jnp.sum(jnp.where((n_ge > kf) & (~small), 1.0, 0.0)) > 0.0
    last = float(nt * tk - 1)

    def tie_cut():
        need = kf - q_sum(lambda x, t: x > thr)

        def tie_search(_, carry):
            jlo, jhi = carry
            mid = jnp.floor(0.5 * (jlo + jhi))
            ge = q_sum(lambda x, t: (x == thr) & (key_index(t) <= mid)) >= need
            return jnp.where(ge, jlo, mid), jnp.where(ge, mid, jhi)

        n_iter = int(np.ceil(np.log2(nt * tk))) + 1
        return lax.fori_loop(0, n_iter, tie_search,
                             (jnp.full((1, tq), -1.0, F32), jnp.full((1, tq), last, F32)))[1]

    jcut = lax.cond(contested, tie_cut, lambda: jnp.full((1, tq), last, F32))
    for t, x in enumerate(tiles()):
        sel = (x > -jnp.inf) & (small | (x > thr) | ((x == thr) & (key_index(t) <= jcut)))
        st_ref[t] = jnp.where(sel, 0.0, NEG_BIG)


def _dsa(z3, zs3, tq=512, tk=512):
    b, s, _ = z3.shape
    assert s % tq == 0 and s % tk == 0 and tq % CHUNK == 0 and tk % CHUNK == 0
    topk = min(TOPK_MAX, s // 4)
    kernel = lambda *refs: _dsa_kernel(topk, tk, *refs)
    qblock = lambda w, cb: pl.BlockSpec((None, tq, w), lambda bi, qi, _c=cb: (bi, qi, _c))
    keys = lambda w, cb: pl.BlockSpec((None, s, w), lambda bi, qi, _c=cb: (bi, 0, _c))
    return pl.pallas_call(
        kernel,
        grid=(b, s // tq),
        in_specs=[qblock(512, O_BQ // 512), qblock(512, O_IQ // 512), keys(128, O_KD // 128),
                  keys(128, O_KD // 128 + 1), keys(256, O_VD // 256), keys(128, O_IK // 128), qblock(128, 0)],
        out_specs=pl.BlockSpec((None, tq, B_HEADS * B_HD), lambda bi, qi: (bi, qi, 0)),
        out_shape=jax.ShapeDtypeStruct((b, s, B_HEADS * B_HD), BF16),
        scratch_shapes=[pltpu.VMEM((s // tk, tk, tq), F32),
                        pltpu.VMEM((s // tk, B_KV_HEADS, 128, tk), BF16)],
        compiler_params=pltpu.CompilerParams(dimension_semantics=("parallel", "arbitrary"),
                                             vmem_limit_bytes=VMEM_LIMIT),
        name="dsa",
    )(z3, z3, z3, z3, z3, z3, zs3)


def _merge_kernel(x_ref, oa_ref, ob_ref, ga_ref, gb_ref, bg_ref, wa_ref, wb_ref, wo_ref, n2_ref, wr_ref,
                  br_ref, x1_ref, h2_ref, rl_ref):
    pa = _dot(oa_ref[...], wa_ref[...])
    pb = _dot(ob_ref[...], wb_ref[...])
    bgv = bg_ref[...]
    ga = jax.nn.sigmoid(ga_ref[...].astype(F32) + bgv[:, :D_MODEL])
    gb = jax.nn.sigmoid(gb_ref[...].astype(F32) + bgv[:, D_MODEL:])
    merged = ga * pa + gb * pb
    x1 = x_ref[...] + _mm(merged, wo_ref[...])
    x1_ref[...] = x1
    h2 = _rmsnorm_rows(x1, n2_ref[...])
    hb, hl = _split(h2)
    h2_ref[...] = hb
    s1 = _dot(hb, wr_ref[...])
    s2 = _dot(hl, wr_ref[:, :128])
    rl_ref[...] = s1[:, :128] + s1[:, 128:] + s2 + br_ref[...]


def _merge(x2d, oa, ob, z2d, b_gate, wa, wb, wo, n2, wr, br, tm=512):
    n = x2d.shape[0]
    assert n % tm == 0 and oa.shape == ob.shape == (n, 512)
    full =lambda shape: pl.BlockSpec(shape, lambda i: (0, 0))
    return pl.pallas_call(
        _merge_kernel,
        grid=(n // tm,),
        in_specs=[
            pl.BlockSpec((tm, D_MODEL), lambda i: (i, 0)),
            pl.BlockSpec((tm, 512), lambda i: (i, 0)),
            pl.BlockSpec((tm, 512), lambda i: (i, 0)),
            pl.BlockSpec((tm, D_MODEL), lambda i: (i, O_GA // D_MODEL)),
            pl.BlockSpec((tm, D_MODEL), lambda i: (i, O_GB // D_MODEL)),
            full((1, 2 * D_MODEL)), full((512, D_MODEL)), full((512, D_MODEL)), full((D_MODEL, D_MODEL)),
            full((1, D_MODEL)), full((D_MODEL, 256)), full((1, 128)),
        ],
        out_specs=[
            pl.BlockSpec((tm, D_MODEL), lambda i: (i, 0)),
            pl.BlockSpec((tm, D_MODEL), lambda i: (i, 0)),
            pl.BlockSpec((tm, 128), lambda i: (i, 0)),
        ],
        out_shape=[jax.ShapeDtypeStruct((n, D_MODEL), F32), jax.ShapeDtypeStruct((n, D_MODEL), BF16),
                   jax.ShapeDtypeStruct((n, 128), F32)],
        compiler_params=pltpu.CompilerParams(dimension_semantics=("parallel",),
                                             vmem_limit_bytes=VMEM_LIMIT),
        name="merge",
    )(x2d, oa, ob, z2d, z2d, b_gate, wa, wb, wo, n2, wr, br)


R_GROUP, R_EXPERT = 0, 8


def _routing_weights(rl):
    t = rl.T
    n_tok = t.shape[1]
    gl = t[R_GROUP:R_GROUP + N_GROUPS]
    gidx = lax.broadcasted_iota(jnp.int32, gl.shape, 0)
    gmax = jnp.max(gl, axis=0, keepdims=True)
    gsel = jnp.min(jnp.where(gl == gmax, gidx, N_GROUPS), axis=0, keepdims=True)
    ggate = 1.0 / jnp.sum(jnp.exp(gl - gmax), axis=0, keepdims=True)
    el = t[R_EXPERT:R_EXPERT + N_EXPERTS]
    eidx = lax.broadcasted_iota(jnp.int32, el.shape, 0)
    e_lo = gsel * EXPERTS_PER_GROUP
    emask = (eidx >= e_lo) & (eidx < e_lo + EXPERTS_PER_GROUP)
    el = jnp.where(emask, el, -jnp.inf)
    emax = jnp.max(el, axis=0, keepdims=True)
    ee = jnp.where(emask, jnp.exp(el - emax), 0.0)
    ep = jnp.where(emask, ee / jnp.sum(ee, axis=0, keepdims=True), -1.0)
    p1 = jnp.max(ep, axis=0, keepdims=True)
    i1 = jnp.min(jnp.where(ep == p1, eidx, N_EXPERTS), axis=0, keepdims=True)
    ep2 = jnp.where(eidx == i1, -1.0, ep)
    p2 = jnp.max(ep2, axis=0, keepdims=True)
    i2 = jnp.min(jnp.where(ep2 == p2, eidx, N_EXPERTS), axis=0, keepdims=True)
    tot = p1 + p2
    comb_t = (jnp.where(eidx == i1, ggate * (p1 / tot), 0.0)
              + jnp.where(eidx == i2, ggate * (p2 / tot), 0.0))
    full = jnp.concatenate([jnp.zeros((R_EXPERT, n_tok), F32), comb_t,
                            jnp.zeros((128 - R_EXPERT - N_EXPERTS, n_tok), F32)], axis=0)
    return full.T


def _moe_kernel(x1_ref, h2_ref, rl_ref, w1_ref, w3_ref, w2_ref, fg_ref, o_ref, y_ref, comb_ref):
    step = pl.program_id(1)
    per_step = w1_ref.shape[0]

    @pl.when(step == 0)
    def _():
        comb_ref[...] = _routing_weights(rl_ref[...])
        y_ref[...] = jnp.zeros(y_ref.shape, F32)

    h = h2_ref[...]
    comb = comb_ref[...]
    lane = lax.broadcasted_iota(jnp.int32, comb.shape, 1)
    acts = []
    for j in range(per_step):
        a = _dot(h, w1_ref[j])
        b = _dot(h, w3_ref[j])
        ce = jnp.sum(jnp.where(lane == R_EXPERT + step * per_step + j, comb, 0.0), axis=1, keepdims=True)
        acts.append(((a * jax.nn.sigmoid(a)) * b * ce).astype(BF16))
    y_ref[...] += _dot(jnp.concatenate(acts, axis=1), w2_ref[...])

    @pl.when(step == pl.num_programs(1) - 1)
    def _():
        o_ref[...] = _rmsnorm_rows(x1_ref[...] + y_ref[...], fg_ref[...])


def _moe(x1, h2, rl, w1, w3, w2, fg, tm=1024, per_step=4):
    n = x1.shape[0]
    assert n % tm == 0 and N_EXPERTS % per_step == 0 and w1.shape == (N_EXPERTS, D_MODEL, D_EXPERT)
    w2g =w2.reshape(N_EXPERTS // per_step, per_step * D_EXPERT, D_MODEL)
    return pl.pallas_call(
        _moe_kernel,
        grid=(n // tm, N_EXPERTS // per_step),
        in_specs=[
            pl.BlockSpec((tm, D_MODEL), lambda i, e: (i, 0)),
            pl.BlockSpec((tm, D_MODEL), lambda i, e: (i, 0)),
            pl.BlockSpec((tm, 128), lambda i, e: (i, 0)),
            pl.BlockSpec((per_step, D_MODEL, D_EXPERT), lambda i, e: (e, 0, 0)),
            pl.BlockSpec((per_step, D_MODEL, D_EXPERT), lambda i, e: (e, 0, 0)),
            pl.BlockSpec((None, per_step * D_EXPERT, D_MODEL), lambda i, e: (e, 0, 0)),
            pl.BlockSpec((1, D_MODEL), lambda i, e: (0, 0)),
        ],
        out_specs=pl.BlockSpec((tm, D_MODEL), lambda i, e: (i, 0)),
        out_shape=jax.ShapeDtypeStruct((n, D_MODEL), F32),
        scratch_shapes=[pltpu.VMEM((tm, D_MODEL), F32), pltpu.VMEM((tm, 128), F32)],
        compiler_params=pltpu.CompilerParams(dimension_semantics=("parallel", "arbitrary"),
                                             vmem_limit_bytes=VMEM_LIMIT),
        name="moe",
    )(x1, h2, rl, w1, w3, w2g, fg)


_W_OFF = {}
_off = 0
for _name, _n in (("a_q", 512), ("a_k", 512), ("a_v", 512), ("a_z", 512), ("a_beta", 8), ("a_alpha", 8),
                  ("b_q", 512), ("b_k", 128), ("b_v", 128), ("i_q", 512), ("i_k", 64), ("i_w", 8),
                  ("gate_a", 1024), ("gate_b", 1024)):
    _W_OFF[_name] = (_off, _off + _n)
    _off += _n


def _cols(w, name, lo=0, hi=None):
    a, b = _W_OFF[name]
    return w[:, a + lo:(b if hi is None else a + hi)]


def _hi_lo_pair(w_small):
    pad = jnp.zeros((w_small.shape[0], 128 - w_small.shape[1]), F32)
    w = jnp.concatenate([w_small.astype(F32), pad], axis=1)
    hi, lo = _split(w)
    return jnp.concatenate([hi, lo], axis=1)


def _layout_kernel(w_ref, main_ref):
    def piece(name, lo=0, hi=None):
        a, b = _W_OFF[name]
        return w_ref[:, a + lo:(b if hi is None else a + hi)]

    k0, k1 = piece("b_k", 0, 64), piece("b_k", 64, 128)
    v0, v1 = piece("b_v", 0, 64), piece("b_v", 64, 128)
    ik = piece("i_k")
    pieces = [piece("a_q"), piece("a_k"), piece("a_v"), piece("a_z"), piece("gate_a"), piece("gate_b"),
              piece("b_q"), piece("i_q"), k0, k0, k1, k1, v0, v0, v1, v1, ik, ik]
    off = 0
    for p in pieces:
        main_ref[:, off:off + p.shape[1]] = p.astype(BF16)
        off += p.shape[1]


def _layout_w_in(w, tr=128):
    rows, cols = w.shape
    assert rows % tr == 0
    main = pl.pallas_call(
        _layout_kernel,
        grid=(rows // tr,),
        in_specs=[pl.BlockSpec((tr, cols), lambda i: (i, 0))],
        out_specs=pl.BlockSpec((tr, Z_W), lambda i: (i, 0)),
        out_shape=jax.ShapeDtypeStruct((rows, Z_W), BF16),
        compiler_params=pltpu.CompilerParams(dimension_semantics=("parallel",), vmem_limit_bytes=VMEM_LIMIT),
        name="w_layout",
    )(w)
    small = _hi_lo_pair(jnp.concatenate([_cols(w, "a_beta"), _cols(w, "a_alpha"), _cols(w, "i_w")], axis=1))
    return main, small


def kernel(x, positions, norm1_g, w_in, b_gate, conv_w, a_log, dt_bias, a_norm_g, w_proj_a, w_proj_b, w_out,
           norm2_g, w_router_group, b_router_group, w_router_expert, b_router_expert, w_exp_gate, w_exp_up,
           w_exp_down, final_norm_g):
    b, s, d = x.shape
    n = b * s
    assert w_in.shape[0] == 1 and d == D_MODEL, "one layer: the MoE kernel also applies the final norm"
    xc = x.reshape(n, d).astype(F32)
    for l in range(w_in.shape[0]):
        w_main, w_small = _layout_w_in(w_in[l])
        z, zs = _in_proj(xc, norm1_g[l][None, :].astype(F32), w_main, w_small, positions)
        z3 = z.reshape(b, s, Z_W)
        zs3 = zs.reshape(b, s, 128)
        o_a = _gdn(z3, zs3, conv_w[l], a_log[l], dt_bias[l], a_norm_g[l])
        o_b = _dsa(z3, zs3)
        gap = R_EXPERT - N_GROUPS
        wr = _hi_lo_pair(jnp.concatenate([w_router_group[l], jnp.zeros((d, gap), F32), w_router_expert[l]],
                                         axis=1))
        br = jnp.concatenate([b_router_group[l], jnp.zeros((gap,), F32), b_router_expert[l],
                              jnp.zeros((128 - R_EXPERT - N_EXPERTS,), F32)])[None, :].astype(F32)
        x1, h2, rl = _merge(xc, o_a.reshape(n, -1), o_b.reshape(n, -1), z, b_gate[l][None, :].astype(F32),
                            w_proj_a[l].astype(BF16), w_proj_b[l].astype(BF16), w_out[l].astype(BF16),
                            norm2_g[l][None, :].astype(F32), wr, br)
        xc = _moe(x1, h2, rl, w_exp_gate[l].astype(BF16), w_exp_up[l].astype(BF16),
                  w_exp_down[l].astype(BF16), final_norm_g[None, :].astype(F32))
    return xc.reshape(b, s, d).astype(x.dtype)
```

```python
import numpy as np
import jax
import jax.numpy as jnp
from jax import lax
from jax.experimental import pallas as pl
from jax.experimental.pallas import tpu as pltpu

F32 = jnp.float32
BF16 = jnp.bfloat16

D_MODEL = 1024
CHUNK = 64
EPS = 1e-6
ROPE_THETA = 10000.0
A_HEADS = 8
A_DK = 64
A_DV = 64
CONV_K = 4
B_HEADS = 8
B_KV_HEADS = 2
B_HD = 64
IDX_HEADS = 8
IDX_HD = 64
TOPK_MAX = 256
N_GROUPS = 4
EXPERTS_PER_GROUP = 4
N_EXPERTS = 16
D_EXPERT = 256

O_AQKV, O_AZ, O_GA, O_GB, O_BQ, O_IQ, O_KD, O_VD, O_IK = 0, 1536, 2048, 3072, 4096, 4608, 5120, 5376, 5632
Z_W = 5760
Z_MAIN = O_KD
S_BETA, S_ALPHA, S_IW = 0, 8, 16

GROUP_HEADS = 2
BD = GROUP_HEADS * CHUNK
NEG_BIG = -1e30
LOG2E = 1.4426950408889634
BISECT_STEPS = 18
VMEM_LIMIT = 56 * 1024 * 1024


def _split(x):
    hi = x.astype(BF16)
    lo = (x - hi.astype(F32)).astype(BF16)
    return hi, lo


def _dot(a, b):
    return jnp.dot(a, b, preferred_element_type=F32)


def _mm(a, b):
    return _dot(a.astype(BF16), b.astype(BF16))


def _mm_nt(a, b):
    return lax.dot_general(a.astype(BF16), b.astype(BF16), (((1,), (1,)), ((), ())),
                           preferred_element_type=F32)


def _mm_exact_lhs(a_bf16, x):
    xh, xl = _split(x)
    return _dot(a_bf16, xh) + _dot(a_bf16, xl)


def _rmsnorm_rows(x, g):
    return x * lax.rsqrt(jnp.mean(x * x, axis=-1, keepdims=True) + EPS) * g


_Z_CHUNKS = tuple((o, min(512, Z_W - o)) for o in range(0, Z_W, 512))


def _rope(x, cs, sn, first):
    w = x.shape[1]
    rep = w // 128
    if rep > 1:
        cs, sn, first = (jnp.concatenate([a] * rep, axis=1) for a in (cs, sn, first))
    swapped = jnp.where(first, pltpu.roll(x, w - B_HD // 2, 1), pltpu.roll(x, B_HD // 2, 1))
    return x * cs + swapped * sn


def _in_proj_kernel(x_ref, g_ref, w_ref, wt_ref, ws_ref, pos_ref, inv_ref, z_ref, zs_ref):
    h = _rmsnorm_rows(x_ref[...], g_ref[...])
    hb, hl = _split(h)

    ang = pos_ref[...] * inv_ref[...]
    lane = lax.broadcasted_iota(jnp.int32, ang.shape, 1)
    first = (lane & (B_HD - 1)) < (B_HD // 2)
    cs = jnp.cos(ang)
    sn = jnp.sin(ang)
    sn = jnp.where(first, -sn, sn)

    for o, w in _Z_CHUNKS:
        r = _dot(hb, w_ref[:, o:o + w] if o < Z_MAIN else wt_ref[:, o - Z_MAIN:o - Z_MAIN + w])
        if o == O_BQ:
            r = _rope(r, cs, sn, first) * (B_HD ** -0.5 * LOG2E)
        elif o == O_IQ or o == O_IK:
            r = _rope(r, cs, sn, first)
        elif o == O_KD:
            kw = O_VD - O_KD
            r = jnp.concatenate([_rope(r[:, :kw], cs, sn, first), r[:, kw:]], axis=1)
        z_ref[:, o:o + w] = r.astype(BF16)
    s1 = _dot(hb, ws_ref[...])
    s2 = _dot(hl, ws_ref[:, :128])
    zs_ref[...] = s1[:, :128] + s1[:, 128:] + s2


def _in_proj(x2d, g, w_main, w_tail, w_small, positions, tm=512):
    n = x2d.shape[0]
    assert n % tm == 0 and x2d.shape[1] == D_MODEL
    assert w_main.shape == (D_MODEL, Z_MAIN) and w_tail.shape == (D_MODEL, Z_W - Z_MAIN)
    half = B_HD // 2
    inv = ROPE_THETA ** (-jnp.arange(half, dtype=F32) / half)
    inv128 = jnp.tile(inv, 4)[None, :]
    pos = positions.astype(F32).reshape(n, 1)
    return pl.pallas_call(
        _in_proj_kernel,
        grid=(n // tm,),
        in_specs=[
            pl.BlockSpec((tm, D_MODEL), lambda i: (i, 0)),
            pl.BlockSpec((1, D_MODEL), lambda i: (0, 0)),
            pl.BlockSpec((D_MODEL, Z_MAIN), lambda i: (0, 0)),
            pl.BlockSpec((D_MODEL, Z_W - Z_MAIN), lambda i: (0, 0)),
            pl.BlockSpec((D_MODEL, 256), lambda i: (0, 0)),
            pl.BlockSpec((tm, 1), lambda i: (i, 0)),
            pl.BlockSpec((1, 128), lambda i: (0, 0)),
        ],
        out_specs=[
            pl.BlockSpec((tm, Z_W), lambda i: (i, 0)),
            pl.BlockSpec((tm, 128), lambda i: (i, 0)),
        ],
        out_shape=[jax.ShapeDtypeStruct((n, Z_W), BF16), jax.ShapeDtypeStruct((n, 128), F32)],
        compiler_params=pltpu.CompilerParams(dimension_semantics=("parallel",),
                                             vmem_limit_bytes=VMEM_LIMIT),
        name="in_proj",
    )(x2d, g, w_main, w_tail, w_small, pos, inv128)


def _gdn_constants():
    r = np.arange(BD)
    same = (r[:, None] // CHUNK) == (r[None, :] // CHUNK)
    incl = same & (r[:, None] >= r[None, :])
    strict = same & (r[:, None] > r[None, :])
    eye = np.eye(BD, dtype=np.float32)
    ll = np.concatenate([incl, same], axis=0).astype(np.float32)
    n_groups = A_HEADS // GROUP_HEADS
    sel = np.zeros((2 * n_groups, BD, 128), np.float32)
    for gi in range(n_groups):
        for h in range(GROUP_HEADS):
            sel[gi * 2 + 0, h * CHUNK:(h + 1) * CHUNK, S_BETA + gi * GROUP_HEADS + h] = 1.0
            sel[gi * 2 + 1, h * CHUNK:(h + 1) * CHUNK, S_ALPHA + gi * GROUP_HEADS + h] = 1.0
    return (jnp.asarray(incl, F32), jnp.asarray(strict, F32), jnp.asarray(same, F32), jnp.asarray(eye),
            jnp.asarray(ll, BF16), jnp.asarray(sel))


def _tile_heads(x):
    return jnp.concatenate([x] * GROUP_HEADS, axis=0)


def _gdn_kernel(zq_ref, zz_ref, zs_ref, cw_ref, av_ref, ag_ref, incl_ref, strict_ref, bdm_ref, eye_ref,
                ll_ref, sel_ref, o_ref, ext_ref, st_ref):
    c = pl.program_id(1)
    nb, t = zq_ref.shape[0], zq_ref.shape[1]

    @pl.when(c == 0)
    def _():
        ext_ref[:, 0:8, :] = jnp.zeros((nb, 8, ext_ref.shape[2]), F32)
        st_ref[...] = jnp.zeros(st_ref.shape, F32)

    hw = A_HEADS * A_DK
    cw = cw_ref[...]
    av = av_ref[...]
    q_all, k_all, v_all, bg = [], [], [], []
    for bi in range(nb):
        ext_ref[bi, 8:8 + t, :] = zq_ref[bi].astype(F32)
        y = cw[0:1, :] * ext_ref[bi, pl.ds(8 - (CONV_K - 1), t), :]
        for j in range(1, CONV_K):
            y = y + cw[j:j + 1, :] * ext_ref[bi, pl.ds(8 - (CONV_K - 1) + j, t), :]
        ext_ref[bi, 0:8, :] = ext_ref[bi, t:t + 8, :]
        y = y * jax.nn.sigmoid(y)
        q_all.append(y[:, :hw])
        k_all.append(y[:, hw:2 * hw])
        v_all.append(y[:, 2 * hw:])

        sm = zs_ref[bi]
        lane = lax.broadcasted_iota(jnp.int32, sm.shape, 1)
        xg = sm + av[1:2, :]
        softplus = jnp.maximum(xg, 0.0) + jnp.log1p(jnp.exp(-jnp.abs(xg)))
        g_all = -jnp.exp(av[0:1, :]) * softplus
        bg.append(jnp.where(lane < S_ALPHA, jax.nn.sigmoid(sm), g_all))

    incl = incl_ref[...]
    strict = strict_ref[...]
    bdm = bdm_ref[...]
    eye = eye_ref[...]
    ll = ll_ref[...]
    incl_b = ll[:BD]

    n_chunks = t // CHUNK
    n_groups = A_HEADS // GROUP_HEADS
    chains = [(bi, ci, gi) for bi in range(nb) for ci in range(n_chunks) for gi in range(n_groups)]

    pre = {}
    for bi, ci, gi in chains:
        r0, c0 = ci * CHUNK, gi * BD
        bg4 = _tile_heads(bg[bi][r0:r0 + CHUNK])
        beta = jnp.sum(bg4 * sel_ref[gi * 2 + 0], axis=1, keepdims=True)
        gcol = jnp.sum(bg4 * sel_ref[gi * 2 + 1], axis=1, keepdims=True)
        gs = _mm_exact_lhs(ll, jnp.broadcast_to(gcol, (BD, 128)))
        g_cum = gs[:BD, :1]
        g_last = gs[BD:, :1]
        diff = _mm_exact_lhs(incl_b, gcol * strict)
        decay = jnp.where(incl > 0.0, jnp.exp(diff), 0.0)
        e_cum = jnp.exp(g_cum)
        kr = _tile_heads(k_all[bi][r0:r0 + CHUNK, c0:c0 + BD]) * bdm
        qr = _tile_heads(q_all[bi][r0:r0 + CHUNK, c0:c0 + BD]) * bdm
        vm = _tile_heads(v_all[bi][r0:r0 + CHUNK, c0:c0 + BD]) * bdm
        km = kr * lax.rsqrt(jnp.sum(kr * kr, axis=1, keepdims=True) + EPS)
        qm = qr * (lax.rsqrt(jnp.sum(qr * qr, axis=1, keepdims=True) + EPS) * (A_DK ** -0.5))
        kkqk = _mm_nt(jnp.concatenate([km, qm], axis=0), km)
        m = -(strict * beta * kkqk[:BD] * decay)
        pre[bi, ci, gi] = dict(beta=beta, g_cum=g_cum, g_last=g_last, e_cum=e_cum, km=km, qm=qm, vm=vm,
                               qk=kkqk[BD:] * decay, m=m, inv=eye + m)

    sq = CHUNK
    while sq > 2:
        for key in chains:
            p = pre[key]
            p["m"] = _mm(p["m"], p["m"])
        for key in chains:
            p = pre[key]
            p["inv"] = p["inv"] + _mm(p["inv"], p["m"])
        sq //= 2

    states = [st_ref[i] for i in range(nb * n_groups)]
    outs = [[] for _ in range(nb)]
    lanes = [(bi, gi) for bi in range(nb) for gi in range(n_groups)]
    for ci in range(n_chunks):
        ps = {k: pre[k[0], ci, k[1]] for k in lanes}
        sidx = {k: k[0] * n_groups + k[1] for k in lanes}
        kq_s = {k: _mm(jnp.concatenate([ps[k]["km"] * ps[k]["e_cum"], ps[k]["qm"] * ps[k]["e_cum"]], axis=0),
                       states[sidx[k]]) for k in lanes}
        v_new = {k: _mm(ps[k]["inv"], ps[k]["beta"] * (ps[k]["vm"] - kq_s[k][:BD])) for k in lanes}
        o_bd = {k: kq_s[k][BD:] + _mm(ps[k]["qk"], v_new[k]) for k in lanes}
        for k in lanes:
            p = ps[k]
            k_dec = p["km"] * jnp.exp(p["g_last"] - p["g_cum"])
            states[sidx[k]] = states[sidx[k]] * jnp.exp(p["g_last"]) + _mm(k_dec.T, v_new[k])
        for bi in range(nb):
            o_groups = []
            for gi in range(n_groups):
                ob = o_bd[bi, gi]
                ob = ob * lax.rsqrt(jnp.sum(ob * ob, axis=1, keepdims=True) * (1.0 / A_DV) + EPS)
                o_groups.append(sum(ob[h * CHUNK:(h + 1) * CHUNK] for h in range(GROUP_HEADS)))
            outs[bi].append(jnp.concatenate(o_groups, axis=1))

    for i, state in enumerate(states):
        st_ref[i] = state
    for bi in range(nb):
        o = outs[bi][0] if n_chunks == 1 else jnp.concatenate(outs[bi], axis=0)
        zz = zz_ref[bi].astype(F32)
        o_ref[bi] = (o * ag_ref[...] * (zz * jax.nn.sigmoid(zz))).astype(BF16)


def _gdn(z3, zs3, conv_w, a_log, dt_bias, a_norm_g, t=2 * CHUNK, nb=2):
    b, s, _ = z3.shape
    assert b % nb == 0 and s % t == 0 and t % CHUNK == 0 and t >= 8
    consts = _gdn_constants()
    av = jnp.zeros((2, 128), F32)
    av = av.at[0, S_ALPHA:S_ALPHA + A_HEADS].set(a_log.astype(F32))
    av = av.at[1, S_ALPHA:S_ALPHA + A_HEADS].set(dt_bias.astype(F32))
    ag = jnp.tile(a_norm_g.astype(F32), A_HEADS)[None, :]
    conv_cols = 2 * A_HEADS * A_DK + A_HEADS * A_DV

    def const_spec(a):
        nd = a.ndim
        return pl.BlockSpec(a.shape, lambda bi, ci, _n=nd: (0,) * _n)

    small_in = (conv_w.astype(F32), av, ag) + consts
    return pl.pallas_call(
        _gdn_kernel,
        grid=(b // nb, s // t),
        in_specs=[
            pl.BlockSpec((nb, t, conv_cols), lambda bi, ci: (bi, ci, O_AQKV // conv_cols)),
            pl.BlockSpec((nb, t, 512), lambda bi, ci: (bi, ci, O_AZ // 512)),
            pl.BlockSpec((nb, t, 128), lambda bi, ci: (bi, ci, 0)),
        ] + [const_spec(a) for a in small_in],
        out_specs=pl.BlockSpec((nb, t, A_HEADS * A_DV), lambda bi, ci: (bi, ci, 0)),
        out_shape=jax.ShapeDtypeStruct((b, s, A_HEADS * A_DV), BF16),
        scratch_shapes=[pltpu.VMEM((nb, 8 + t, conv_cols), F32),
                        pltpu.VMEM((nb * (A_HEADS // GROUP_HEADS), BD, BD), F32)],
        compiler_params=pltpu.CompilerParams(dimension_semantics=("parallel", "arbitrary"),
                                             vmem_limit_bytes=VMEM_LIMIT),
        name="gdn",
    )(z3, z3, zs3, *small_in)


def _dsa_kernel(topk, tk, q_ref, iq_ref, ka_ref, kb_ref, vd_ref, ik_ref, zs_ref, o_ref, st_ref, vt_ref):
    qb = pl.program_id(1)
    s = ka_ref.shape[0]
    tq = q_ref.shape[0]
    nkt_max = s // tk
    r0 = qb * tq
    n_tiles = (r0 + tq + tk - 1) // tk
    heads_per_kv = B_HEADS // B_KV_HEADS

    @pl.when(qb == 0)
    def _():
        for t in range(nkt_max):
            for g in range(B_KV_HEADS):
                v_tile = vd_ref[t * tk:(t + 1) * tk, g * 128:(g + 1) * 128]
                vt_ref[t, g] = v_tile.astype(F32).T.astype(BF16)

    lane128 = lax.broadcasted_iota(jnp.int32, (tq, 128), 1)
    low_half = lane128 < B_HD
    high_half = lane128 >= B_HD

    def head_slab(x, h):
        slab = x[:, (h // 2) * 128:(h // 2 + 1) * 128]
        return jnp.where(low_half if h % 2 == 0 else high_half, slab, jnp.zeros_like(slab))

    zs_t = zs_ref[...].T
    iw_scale = (IDX_HEADS ** -0.5) * (IDX_HD ** -0.5)
    wrows = [zs_t[S_IW + h:S_IW + h + 1, :] * iw_scale for h in range(IDX_HEADS)]
    qrow = r0 + lax.broadcasted_iota(jnp.int32, (1, tq), 1)
    limit = ((qrow >> 6) + 1) << 6
    small = limit <= topk

    iq = iq_ref[...]
    iq_heads = [head_slab(iq, h) for h in range(IDX_HEADS)]

    def index_tile(kt, carry):
        r_max, r_min = carry
        off = pl.multiple_of(kt * tk, tk)
        ikt = ik_ref[pl.ds(off, tk), :]
        acc = jnp.zeros((tk, tq), F32)
        for h in range(IDX_HEADS):
            sc = lax.dot_general(ikt, iq_heads[h], (((1,), (1,)), ((), ())), preferred_element_type=F32)
            acc = acc + wrows[h] * jnp.maximum(sc, 0.0)
        valid = off + lax.broadcasted_iota(jnp.int32, (tk, tq), 0) < limit
        masked = jnp.where(valid, acc, -jnp.inf)
        st_ref[kt] = masked
        r_max = jnp.maximum(r_max, jnp.max(masked, axis=0, keepdims=True))
        r_min = jnp.minimum(r_min, jnp.min(acc, axis=0, keepdims=True))
        return r_max, r_min

    r_max, r_min = lax.fori_loop(0, n_tiles, index_tile,
                                 (jnp.full((1, tq), -jnp.inf, F32), jnp.full((1, tq), jnp.inf, F32)))

    all_kept = r0 + tq <= min(topk, tk)

    @pl.when(all_kept)
    def _():
        st_ref[0] = jnp.where(st_ref[0] > -jnp.inf, 0.0, NEG_BIG)

    for j in range(nkt_max):
        @pl.when(jnp.logical_and(n_tiles == j + 1, jnp.logical_not(all_kept)))
        def _(j=j):
            _topk_mask(topk, j + 1, small, r_max, r_min, st_ref)

    q = q_ref[...]
    nr = heads_per_kv * tq
    qcols = [jnp.concatenate([head_slab(q, g * heads_per_kv + j) for j in range(heads_per_kv)], axis=0)
             for g in range(B_KV_HEADS)]
    k_refs = (ka_ref, kb_ref)

    def attend_tile(kt, carry):
        off = pl.multiple_of(kt * tk, tk)
        bias = st_ref[kt]
        bias4 = jnp.concatenate([bias] * heads_per_kv, axis=1)
        groups = range(B_KV_HEADS)
        logits = [lax.dot_general(k_refs[g][pl.ds(off, tk), :], qcols[g], (((1,), (1,)), ((), ())),
                                  preferred_element_type=F32) + bias4 for g in groups]
        m_new = [jnp.maximum(carry[g][0], jnp.max(logits[g], axis=0, keepdims=True)) for g in groups]
        p = [jnp.exp2(logits[g] - m_new[g]) for g in groups]
        alpha = [jnp.exp2(carry[g][0] - m_new[g]) for g in groups]
        l_new = [alpha[g] * carry[g][1] + jnp.sum(p[g], axis=0, keepdims=True) for g in groups]
        pv = [_dot(vt_ref[kt, g], p[g].astype(BF16)) for g in groups]
        return tuple((m_new[g], l_new[g], alpha[g] * carry[g][2] + pv[g]) for g in groups)

    init = tuple((jnp.full((1, nr), NEG_BIG, F32), jnp.zeros((1, nr), F32), jnp.zeros((128, nr), F32))
                 for _ in range(B_KV_HEADS))
    fin = lax.fori_loop(0, n_tiles, attend_tile, init)
    for g in range(B_KV_HEADS):
        _, l_run, acc = fin[g]
        og = acc / l_run
        heads = [og[:, j * tq:(j + 1) * tq].T for j in range(heads_per_kv)]
        for pp in range(heads_per_kv // 2):
            col = (g * (heads_per_kv // 2) + pp) * 128
            o_ref[:, col:col + 128] = jnp.where(low_half, heads[2 * pp], heads[2 * pp + 1]).astype(BF16)


def _topk_mask(topk, nt, small, r_max, r_min, st_ref):
    _, tk, tq = st_ref.shape
    kf = float(topk)

    def tiles():
        return [st_ref[t] for t in range(nt)]

    def q_sum(pred):
        tot = None
        for t, x in enumerate(tiles()):
            c = jnp.sum(pred(x, t).astype(F32), axis=0, keepdims=True)
            tot = c if tot is None else tot + c
        return tot

    def q_max(val):
        best = None
        for t, x in enumerate(tiles()):
            c = jnp.max(val(x, t), axis=0, keepdims=True)
            best = c if best is None else jnp.maximum(best, c)
        return best

    hi0 = r_max + jnp.maximum(jnp.abs(r_max), 1e-30) * 1e-6

    def bisect(_, carry):
        lo, hi = carry
        mid = 0.5 * (lo + hi)
        ge = q_sum(lambda x, t: x >= mid) >= kf
        return jnp.where(ge, mid, lo), jnp.where(ge, hi, mid)

    lo, hi = lax.fori_loop(0, BISECT_STEPS, bisect, (r_min, hi0))

    def peel_cond(carry):
        return jnp.sum(1.0 - carry[0]) > 0.0

    def peel(carry):
        done, thr, hi_c, n_ge = carry
        v1 = q_max(lambda x, t: jnp.where(x < hi_c, x, -jnp.inf))
        c1 = q_sum(lambda x, t: x >= v1)
        ok = c1 >= kf
        act = done < 0.5
        thr = jnp.where(act & ok, v1, thr)
        n_ge = jnp.where(act & ok, c1, n_ge)
        hi_c = jnp.where(act & (~ok), v1, hi_c)
        return jnp.where(ok, 1.0, done), thr, hi_c, n_ge

    _, thr, _, n_ge = lax.while_loop(peel_cond, peel, (jnp.where(small, 1.0, 0.0), lo, hi, jnp.full_like(lo, kf)))

    def key_index(t):
        return (t * tk + lax.broadcasted_iota(jnp.int32, (tk, tq), 0)).astype(F32)

    contested = jnp.sum(jnp.where((n_ge > kf) & (~small), 1.0, 0.0)) > 0.0
    last = float(nt * tk - 1)

    def tie_cut():
        need = kf - q_sum(lambda x, t: x > thr)

        def tie_search(_, carry):
            jlo, jhi = carry
            mid = jnp.floor(0.5 * (jlo + jhi))
            ge = q_sum(lambda x, t: (x == thr) & (key_index(t) <= mid)) >= need
            return jnp.where(ge, jlo, mid), jnp.where(ge, mid, jhi)

        n_iter = int(np.ceil(np.log2(nt * tk))) + 1
        return lax.fori_loop(0, n_iter, tie_search,
                             (jnp.full((1, tq), -1.0, F32), jnp.full((1, tq), last, F32)))[1]

    jcut = lax.cond(contested, tie_cut, lambda: jnp.full((1, tq), last, F32))
    for t, x in enumerate(tiles()):
        sel = (x > -jnp.inf) & (small | (x > thr) | ((x == thr) & (key_index(t) <= jcut)))
        st_ref[t] = jnp.where(sel, 0.0, NEG_BIG)


def _dsa(z3, zs3, tq=512, tk=512):
    b, s, _ = z3.shape
    assert s % tq == 0 and s % tk == 0 and tq % CHUNK == 0 and tk % CHUNK == 0
    topk = min(TOPK_MAX, s // 4)
    kernel = lambda *refs: _dsa_kernel(topk, tk, *refs)
    qblock = lambda w, cb: pl.BlockSpec((None, tq, w), lambda bi, qi, _c=cb: (bi, qi, _c))
    keys = lambda w, cb: pl.BlockSpec((None, s, w), lambda bi, qi, _c=cb: (bi, 0, _c))
    return pl.pallas_call(
        kernel,
        grid=(b, s // tq),
        in_specs=[qblock(512, O_BQ // 512), qblock(512, O_IQ // 512), keys(128, O_KD // 128),
                  keys(128, O_KD // 128 + 1), keys(256, O_VD // 256), keys(128, O_IK // 128), qblock(128, 0)],
        out_specs=pl.BlockSpec((None, tq, B_HEADS * B_HD), lambda bi, qi: (bi, qi, 0)),
        out_shape=jax.ShapeDtypeStruct((b, s, B_HEADS * B_HD), BF16),
        scratch_shapes=[pltpu.VMEM((s // tk, tk, tq), F32),
                        pltpu.VMEM((s // tk, B_KV_HEADS, 128, tk), BF16)],
        compiler_params=pltpu.CompilerParams(dimension_semantics=("parallel", "arbitrary"),
                                             vmem_limit_bytes=VMEM_LIMIT),
        name="dsa",
    )(z3, z3, z3, z3, z3, z3, zs3)


def _merge_kernel(x_ref, oa_ref, ob_ref, ga_ref, gb_ref, bg_ref, wa_ref, wb_ref, wo_ref, n2_ref, wr_ref,
                  br_ref, x1_ref, h2_ref, rl_ref):
    pa = _dot(oa_ref[...], wa_ref[...])
    pb = _dot(ob_ref[...], wb_ref[...])
    bgv = bg_ref[...]
    ga = jax.nn.sigmoid(ga_ref[...].astype(F32) + bgv[:, :D_MODEL])
    gb = jax.nn.sigmoid(gb_ref[...].astype(F32) + bgv[:, D_MODEL:])
    merged = ga * pa + gb * pb
    x1 = x_ref[...] + _mm(merged, wo_ref[...])
    x1_ref[...] = x1
    h2 = _rmsnorm_rows(x1, n2_ref[...])
    hb, hl = _split(h2)
    h2_ref[...] = hb
    s1 = _dot(hb, wr_ref[...])
    s2 = _dot(hl, wr_ref[:, :128])
    rl_ref[...] = s1[:, :128] + s1[:, 128:] + s2 + br_ref[...]


def _merge(x2d, oa, ob, z2d, b_gate, wa, wb, wo, n2, wr, br, tm=512):
    n = x2d.shape[0]
    assert n % tm == 0 and oa.shape == ob.shape == (n, 512)
    full =lambda shape: pl.BlockSpec(shape, lambda i: (0, 0))
    return pl.pallas_call(
        _merge_kernel,
        grid=(n // tm,),
        in_specs=[
            pl.BlockSpec((tm, D_MODEL), lambda i: (i, 0)),
            pl.BlockSpec((tm, 512), lambda i: (i, 0)),
            pl.BlockSpec((tm, 512), lambda i: (i, 0)),
            pl.BlockSpec((tm, D_MODEL), lambda i: (i, O_GA // D_MODEL)),
            pl.BlockSpec((tm, D_MODEL), lambda i: (i, O_GB // D_MODEL)),
            full((1, 2 * D_MODEL)), full((512, D_MODEL)), full((512, D_MODEL)), full((D_MODEL, D_MODEL)),
            full((1, D_MODEL)), full((D_MODEL, 256)), full((1, 128)),
        ],
        out_specs=[
            pl.BlockSpec((tm, D_MODEL), lambda i: (i, 0)),
            pl.BlockSpec((tm, D_MODEL), lambda i: (i, 0)),
            pl.BlockSpec((tm, 128), lambda i: (i, 0)),
        ],
        out_shape=[jax.ShapeDtypeStruct((n, D_MODEL), F32), jax.ShapeDtypeStruct((n, D_MODEL), BF16),
                   jax.ShapeDtypeStruct((n, 128), F32)],
        compiler_params=pltpu.CompilerParams(dimension_semantics=("parallel",),
                                             vmem_limit_bytes=VMEM_LIMIT),
        name="merge",
    )(x2d, oa, ob, z2d, z2d, b_gate, wa, wb, wo, n2, wr, br)


R_GROUP, R_EXPERT = 0, 8


def _routing_weights(rl):
    t = rl.T
    n_tok = t.shape[1]
    gl = t[R_GROUP:R_GROUP + N_GROUPS]
    gidx = lax.broadcasted_iota(jnp.int32, gl.shape, 0)
    gmax = jnp.max(gl, axis=0, keepdims=True)
    gsel = jnp.min(jnp.where(gl == gmax, gidx, N_GROUPS), axis=0, keepdims=True)
    ggate = 1.0 / jnp.sum(jnp.exp(gl - gmax), axis=0, keepdims=True)
    el = t[R_EXPERT:R_EXPERT + N_EXPERTS]
    eidx = lax.broadcasted_iota(jnp.int32, el.shape, 0)
    e_lo = gsel * EXPERTS_PER_GROUP
    emask = (eidx >= e_lo) & (eidx < e_lo + EXPERTS_PER_GROUP)
    el = jnp.where(emask, el, -jnp.inf)
    emax = jnp.max(el, axis=0, keepdims=True)
    ee = jnp.where(emask, jnp.exp(el - emax), 0.0)
    ep = jnp.where(emask, ee / jnp.sum(ee, axis=0, keepdims=True), -1.0)
    p1 = jnp.max(ep, axis=0, keepdims=True)
    i1 = jnp.min(jnp.where(ep == p1, eidx, N_EXPERTS), axis=0, keepdims=True)
    ep2 = jnp.where(eidx == i1, -1.0, ep)
    p2 = jnp.max(ep2, axis=0, keepdims=True)
    i2 = jnp.min(jnp.where(ep2 == p2, eidx, N_EXPERTS), axis=0, keepdims=True)
    tot = p1 + p2
    comb_t = (jnp.where(eidx == i1, ggate * (p1 / tot), 0.0)
              + jnp.where(eidx == i2, ggate * (p2 / tot), 0.0))
    full = jnp.concatenate([jnp.zeros((R_EXPERT, n_tok), F32), comb_t,
                            jnp.zeros((128 - R_EXPERT - N_EXPERTS, n_tok), F32)], axis=0)
    return full.T


def _moe_kernel(x1_ref, h2_ref, rl_ref, w1_ref, w3_ref, w2_ref, fg_ref, o_ref, y_ref, comb_ref):
    step = pl.program_id(1)
    per_step = w1_ref.shape[0]

    @pl.when(step == 0)
    def _():
        comb_ref[...] = _routing_weights(rl_ref[...])
        y_ref[...] = jnp.zeros(y_ref.shape, F32)

    h = h2_ref[...]
    comb = comb_ref[...]
    lane = lax.broadcasted_iota(jnp.int32, comb.shape, 1)
    acts = []
    for j in range(per_step):
        a = _dot(h, w1_ref[j])
        b = _dot(h, w3_ref[j])
        ce = jnp.sum(jnp.where(lane == R_EXPERT + step * per_step + j, comb, 0.0), axis=1, keepdims=True)
        acts.append(((a * jax.nn.sigmoid(a)) * b * ce).astype(BF16))
    y_ref[...] += _dot(jnp.concatenate(acts, axis=1), w2_ref[...])

    @pl.when(step == pl.num_programs(1) - 1)
    def _():
        o_ref[...] = _rmsnorm_rows(x1_ref[...] + y_ref[...], fg_ref[...])


def _moe(x1, h2, rl, w1, w3, w2, fg, tm=1024, per_step=4):
    n = x1.shape[0]
    assert n % tm == 0 and N_EXPERTS % per_step == 0 and w1.shape == (N_EXPERTS, D_MODEL, D_EXPERT)
    w2g =w2.reshape(N_EXPERTS // per_step, per_step * D_EXPERT, D_MODEL)
    return pl.pallas_call(
        _moe_kernel,
        grid=(n // tm, N_EXPERTS // per_step),
        in_specs=[
            pl.BlockSpec((tm, D_MODEL), lambda i, e: (i, 0)),
            pl.BlockSpec((tm, D_MODEL), lambda i, e: (i, 0)),
            pl.BlockSpec((tm, 128), lambda i, e: (i, 0)),
            pl.BlockSpec((per_step, D_MODEL, D_EXPERT), lambda i, e: (e, 0, 0)),
            pl.BlockSpec((per_step, D_MODEL, D_EXPERT), lambda i, e: (e, 0, 0)),
            pl.BlockSpec((None, per_step * D_EXPERT, D_MODEL), lambda i, e: (e, 0, 0)),
            pl.BlockSpec((1, D_MODEL), lambda i, e: (0, 0)),
        ],
        out_specs=pl.BlockSpec((tm, D_MODEL), lambda i, e: (i, 0)),
        out_shape=jax.ShapeDtypeStruct((n, D_MODEL), F32),
        scratch_shapes=[pltpu.VMEM((tm, D_MODEL), F32), pltpu.VMEM((tm, 128), F32)],
        compiler_params=pltpu.CompilerParams(dimension_semantics=("parallel", "arbitrary"),
                                             vmem_limit_bytes=VMEM_LIMIT),
        name="moe",
    )(x1, h2, rl, w1, w3, w2g, fg)


_W_OFF = {}
_off = 0
for _name, _n in (("a_q", 512), ("a_k", 512), ("a_v", 512), ("a_z", 512), ("a_beta", 8), ("a_alpha", 8),
                  ("b_q", 512), ("b_k", 128), ("b_v", 128), ("i_q", 512), ("i_k", 64), ("i_w", 8),
                  ("gate_a", 1024), ("gate_b", 1024)):
    _W_OFF[_name] = (_off, _off + _n)
    _off += _n


def _cols(w, name, lo=0, hi=None):
    a, b = _W_OFF[name]
    return w[:, a + lo:(b if hi is None else a + hi)]


def _hi_lo_pair(w_small):
    pad = jnp.zeros((w_small.shape[0], 128 - w_small.shape[1]), F32)
    w = jnp.concatenate([w_small.astype(F32), pad], axis=1)
    hi, lo = _split(w)
    return jnp.concatenate([hi, lo], axis=1)


def _transpose_cast_kernel(off_ref, wt_ref, out_ref):
    del off_ref
    blk = wt_ref[...]
    reps = out_ref.shape[1] // blk.shape[0]
    if reps > 1:
        blk = jnp.concatenate([blk] * reps, axis=0)
    out_ref[...] = blk.T.astype(BF16)


def _gather_columns(wt, offsets, rows, out_cols):
    d = wt.shape[1]
    grid_spec = pltpu.PrefetchScalarGridSpec(
        num_scalar_prefetch=1,
        grid=(len(offsets),),
        in_specs=[pl.BlockSpec((pl.Element(rows), pl.Element(d)), lambda u, off: (pl.multiple_of(off[u], 8), 0))],
        out_specs=pl.BlockSpec((d, out_cols), lambda u, off: (0, u)),
    )
    return pl.pallas_call(
        _transpose_cast_kernel,
        grid_spec=grid_spec,
        out_shape=jax.ShapeDtypeStruct((d, out_cols * len(offsets)), BF16),
        compiler_params=pltpu.CompilerParams(dimension_semantics=("arbitrary",), vmem_limit_bytes=VMEM_LIMIT),
        name="w_layout",
    )(jnp.asarray(offsets, jnp.int32), wt)


def _layout_w_in(w):
    wt = jnp.swapaxes(w, 0, 1)
    start = lambda name: _W_OFF[name][0]
    assert all(start(nm) % 8 == 0 for nm in ("a_q", "gate_a", "b_q", "i_q", "b_k", "b_v", "i_k"))
    main_off = ([start("a_q") + 512 * i for i in range(4)] + [start("gate_a") + 512 * i for i in range(4)]
                + [start("b_q"), start("i_q")])
    tail_off = [start("b_k"), start("b_k") + 64, start("b_v"), start("b_v") + 64, start("i_k")]
    main = _gather_columns(wt, main_off, 512, 512)
    tail = _gather_columns(wt, tail_off, 64, 128)
    small = _hi_lo_pair(jnp.concatenate([_cols(w, "a_beta"), _cols(w, "a_alpha"), _cols(w, "i_w")], axis=1))
    return main, tail, small


def kernel(x, positions, norm1_g, w_in, b_gate, conv_w, a_log, dt_bias, a_norm_g, w_proj_a, w_proj_b, w_out,
           norm2_g, w_router_group, b_router_group, w_router_expert, b_router_expert, w_exp_gate, w_exp_up,
           w_exp_down, final_norm_g):
    b, s, d = x.shape
    n = b * s
    assert w_in.shape[0] == 1 and d == D_MODEL, "one layer: the MoE kernel also applies the final norm"
    xc = x.reshape(n, d).astype(F32)
    for l in range(w_in.shape[0]):
        w_main, w_tail, w_small = _layout_w_in(w_in[l])
        z, zs = _in_proj(xc, norm1_g[l][None, :].astype(F32), w_main, w_tail, w_small, positions)
        z3 = z.reshape(b, s, Z_W)
        zs3 = zs.reshape(b, s, 128)
        o_a = _gdn(z3, zs3, conv_w[l], a_log[l], dt_bias[l], a_norm_g[l])
        o_b = _dsa(z3, zs3)
        gap = R_EXPERT - N_GROUPS
        wr = _hi_lo_pair(jnp.concatenate([w_router_group[l], jnp.zeros((d, gap), F32), w_router_expert[l]],
                                         axis=1))
        br = jnp.concatenate([b_router_group[l], jnp.zeros((gap,), F32), b_router_expert[l],
                              jnp.zeros((128 - R_EXPERT - N_EXPERTS,), F32)])[None, :].astype(F32)
        x1, h2, rl = _merge(xc, o_a.reshape(n, -1), o_b.reshape(n, -1), z, b_gate[l][None, :].astype(F32),
                            w_proj_a[l].astype(BF16), w_proj_b[l].astype(BF16), w_out[l].astype(BF16),
                            norm2_g[l][None, :].astype(F32), wr, br)
        xc = _moe(x1, h2, rl, w_exp_gate[l].astype(BF16), w_exp_up[l].astype(BF16),
                  w_exp_down[l].astype(BF16), final_norm_g[None, :].astype(F32))
    return xc.reshape(b, s, d).astype(x.dtype)
```

```python
import numpy as np
import jax
import jax.numpy as jnp
from jax import lax
from jax.experimental import pallas as pl
from jax.experimental.pallas import tpu as pltpu

F32 = jnp.float32
BF16 = jnp.bfloat16

D_MODEL = 1024
CHUNK = 64
EPS = 1e-6
ROPE_THETA = 10000.0
A_HEADS = 8
A_DK = 64
A_DV = 64
CONV_K = 4
B_HEADS = 8
B_KV_HEADS = 2
B_HD = 64
IDX_HEADS = 8
IDX_HD = 64
TOPK_MAX = 256
N_GROUPS = 4
EXPERTS_PER_GROUP = 4
N_EXPERTS = 16
D_EXPERT = 256

O_AQKV, O_AZ, O_GA, O_GB, O_BQ, O_IQ, O_KD, O_VD, O_IK = 0, 1536, 2048, 3072, 4096, 4608, 5120, 5376, 5632
Z_W = 5760
Z_MAIN = O_KD
S_BETA, S_ALPHA, S_IW = 0, 8, 16

GROUP_HEADS = 2
BD = GROUP_HEADS * CHUNK
NEG_BIG = -1e30
LOG2E = 1.4426950408889634
BISECT_STEPS = 18
VMEM_LIMIT = 56 * 1024 * 1024


def _split(x):
    hi = x.astype(BF16)
    lo = (x - hi.astype(F32)).astype(BF16)
    return hi, lo


def _dot(a, b):
    return jnp.dot(a, b, preferred_element_type=F32)


def _mm(a, b):
    return _dot(a.astype(BF16), b.astype(BF16))


def _mm_nt(a, b):
    return lax.dot_general(a.astype(BF16), b.astype(BF16), (((1,), (1,)), ((), ())),
                           preferred_element_type=F32)


def _mm_exact_lhs(a_bf16, x):
    xh, xl = _split(x)
    return _dot(a_bf16, xh) + _dot(a_bf16, xl)


def _rmsnorm_rows(x, g):
    return x * lax.rsqrt(jnp.mean(x * x, axis=-1, keepdims=True) + EPS) * g


_Z_CHUNKS = tuple((o, min(512, Z_W - o)) for o in range(0, Z_W, 512))


def _rope(x, cs, sn, first):
    w = x.shape[1]
    rep = w // 128
    if rep > 1:
        cs, sn, first = (jnp.concatenate([a] * rep, axis=1) for a in (cs, sn, first))
    swapped = jnp.where(first, pltpu.roll(x, w - B_HD // 2, 1), pltpu.roll(x, B_HD // 2, 1))
    return x * cs + swapped * sn


def _in_proj_kernel(x_ref, g_ref, w_ref, wt_ref, ws_ref, pos_ref, inv_ref, z_ref, zs_ref):
    h = _rmsnorm_rows(x_ref[...], g_ref[...])
    hb, hl = _split(h)

    ang = pos_ref[...] * inv_ref[...]
    lane = lax.broadcasted_iota(jnp.int32, ang.shape, 1)
    first = (lane & (B_HD - 1)) < (B_HD // 2)
    cs = jnp.cos(ang)
    sn = jnp.sin(ang)
    sn = jnp.where(first, -sn, sn)

    for o, w in _Z_CHUNKS:
        r = _dot(hb, w_ref[:, o:o + w] if o < Z_MAIN else wt_ref[:, o - Z_MAIN:o - Z_MAIN + w])
        if o == O_BQ:
            r = _rope(r, cs, sn, first) * (B_HD ** -0.5 * LOG2E)
        elif o == O_IQ or o == O_IK:
            r = _rope(r, cs, sn, first)
        elif o == O_KD:
            kw = O_VD - O_KD
            r = jnp.concatenate([_rope(r[:, :kw], cs, sn, first), r[:, kw:]], axis=1)
        z_ref[:, o:o + w] = r.astype(BF16)
    s1 = _dot(hb, ws_ref[...])
    s2 = _dot(hl, ws_ref[:, :128])
    zs_ref[...] = s1[:, :128] + s1[:, 128:] + s2


def _in_proj(x2d, g, w_main, w_tail, w_small, positions, tm=512):
    n = x2d.shape[0]
    assert n % tm == 0 and x2d.shape[1] == D_MODEL
    assert w_main.shape == (D_MODEL, Z_MAIN) and w_tail.shape == (D_MODEL, Z_W - Z_MAIN)
    half = B_HD // 2
    inv = ROPE_THETA ** (-jnp.arange(half, dtype=F32) / half)
    inv128 = jnp.tile(inv, 4)[None, :]
    pos = positions.astype(F32).reshape(n, 1)
    return pl.pallas_call(
        _in_proj_kernel,
        grid=(n // tm,),
        in_specs=[
            pl.BlockSpec((tm, D_MODEL), lambda i: (i, 0)),
            pl.BlockSpec((1, D_MODEL), lambda i: (0, 0)),
            pl.BlockSpec((D_MODEL, Z_MAIN), lambda i: (0, 0)),
            pl.BlockSpec((D_MODEL, Z_W - Z_MAIN), lambda i: (0, 0)),
            pl.BlockSpec((D_MODEL, 256), lambda i: (0, 0)),
            pl.BlockSpec((tm, 1), lambda i: (i, 0)),
            pl.BlockSpec((1, 128), lambda i: (0, 0)),
        ],
        out_specs=[
            pl.BlockSpec((tm, Z_W), lambda i: (i, 0)),
            pl.BlockSpec((tm, 128), lambda i: (i, 0)),
        ],
        out_shape=[jax.ShapeDtypeStruct((n, Z_W), BF16), jax.ShapeDtypeStruct((n, 128), F32)],
        compiler_params=pltpu.CompilerParams(dimension_semantics=("parallel",),
                                             vmem_limit_bytes=VMEM_LIMIT),
        name="in_proj",
    )(x2d, g, w_main, w_tail, w_small, pos, inv128)


def _gdn_constants():
    r = np.arange(BD)
    same = (r[:, None] // CHUNK) == (r[None, :] // CHUNK)
    incl = same & (r[:, None] >= r[None, :])
    strict = same & (r[:, None] > r[None, :])
    eye = np.eye(BD, dtype=np.float32)
    ll = np.concatenate([incl, same], axis=0).astype(np.float32)
    n_groups = A_HEADS // GROUP_HEADS
    sel = np.zeros((2 * n_groups, BD, 128), np.float32)
    for gi in range(n_groups):
        for h in range(GROUP_HEADS):
            sel[gi * 2 + 0, h * CHUNK:(h + 1) * CHUNK, S_BETA + gi * GROUP_HEADS + h] = 1.0
            sel[gi * 2 + 1, h * CHUNK:(h + 1) * CHUNK, S_ALPHA + gi * GROUP_HEADS + h] = 1.0
    return (jnp.asarray(incl, F32), jnp.asarray(strict, F32), jnp.asarray(same, F32), jnp.asarray(eye),
            jnp.asarray(ll, BF16), jnp.asarray(sel))


def _tile_heads(x):
    return jnp.concatenate([x] * GROUP_HEADS, axis=0)


def _gdn_kernel(zq_ref, zz_ref, zs_ref, cw_ref, av_ref, ag_ref, incl_ref, strict_ref, bdm_ref, eye_ref,
                ll_ref, sel_ref, o_ref, ext_ref, st_ref):
    c = pl.program_id(1)
    nb, t = zq_ref.shape[0], zq_ref.shape[1]

    @pl.when(c == 0)
    def _():
        ext_ref[:, 0:8, :] = jnp.zeros((nb, 8, ext_ref.shape[2]), F32)
        st_ref[...] = jnp.zeros(st_ref.shape, F32)

    hw = A_HEADS * A_DK
    cw = cw_ref[...]
    av = av_ref[...]
    q_all, k_all, v_all, bg = [], [], [], []
    for bi in range(nb):
        ext_ref[bi, 8:8 + t, :] = zq_ref[bi].astype(F32)
        y = cw[0:1, :] * ext_ref[bi, pl.ds(8 - (CONV_K - 1), t), :]
        for j in range(1, CONV_K):
            y = y + cw[j:j + 1, :] * ext_ref[bi, pl.ds(8 - (CONV_K - 1) + j, t), :]
        ext_ref[bi, 0:8, :] = ext_ref[bi, t:t + 8, :]
        y = y * jax.nn.sigmoid(y)
        q_all.append(y[:, :hw])
        k_all.append(y[:, hw:2 * hw])
        v_all.append(y[:, 2 * hw:])

        sm = zs_ref[bi]
        lane = lax.broadcasted_iota(jnp.int32, sm.shape, 1)
        xg = sm + av[1:2, :]
        softplus = jnp.maximum(xg, 0.0) + jnp.log1p(jnp.exp(-jnp.abs(xg)))
        g_all = -jnp.exp(av[0:1, :]) * softplus
        bg.append(jnp.where(lane < S_ALPHA, jax.nn.sigmoid(sm), g_all))

    incl = incl_ref[...]
    strict = strict_ref[...]
    bdm = bdm_ref[...]
    eye = eye_ref[...]
    ll = ll_ref[...]
    incl_b = ll[:BD]

    n_chunks = t // CHUNK
    n_groups = A_HEADS // GROUP_HEADS
    chains = [(bi, ci, gi) for bi in range(nb) for ci in range(n_chunks) for gi in range(n_groups)]

    pre = {}
    for bi, ci, gi in chains:
        r0, c0 = ci * CHUNK, gi * BD
        bg4 = _tile_heads(bg[bi][r0:r0 + CHUNK])
        beta = jnp.sum(bg4 * sel_ref[gi * 2 + 0], axis=1, keepdims=True)
        gcol = jnp.sum(bg4 * sel_ref[gi * 2 + 1], axis=1, keepdims=True)
        gs = _mm_exact_lhs(ll, jnp.broadcast_to(gcol, (BD, 128)))
        g_cum = gs[:BD, :1]
        g_last = gs[BD:, :1]
        diff = _mm_exact_lhs(incl_b, gcol * strict)
        decay = jnp.where(incl > 0.0, jnp.exp(diff), 0.0)
        e_cum = jnp.exp(g_cum)
        kr = _tile_heads(k_all[bi][r0:r0 + CHUNK, c0:c0 + BD]) * bdm
        qr = _tile_heads(q_all[bi][r0:r0 + CHUNK, c0:c0 + BD]) * bdm
        vm = _tile_heads(v_all[bi][r0:r0 + CHUNK, c0:c0 + BD]) * bdm
        km = kr * lax.rsqrt(jnp.sum(kr * kr, axis=1, keepdims=True) + EPS)
        qm = qr * (lax.rsqrt(jnp.sum(qr * qr, axis=1, keepdims=True) + EPS) * (A_DK ** -0.5))
        kkqk = _mm_nt(jnp.concatenate([km, qm], axis=0), km)
        m = -(strict * beta * kkqk[:BD] * decay)
        pre[bi, ci, gi] = dict(beta=beta, g_cum=g_cum, g_last=g_last, e_cum=e_cum, km=km, qm=qm, vm=vm,
                               qk=kkqk[BD:] * decay, m=m, inv=eye + m)

    sq = CHUNK
    while sq > 2:
        for key in chains:
            p = pre[key]
            p["m"] = _mm(p["m"], p["m"])
        for key in chains:
            p = pre[key]
            p["inv"] = p["inv"] + _mm(p["inv"], p["m"])
        sq //= 2

    states = [st_ref[i] for i in range(nb * n_groups)]
    outs = [[] for _ in range(nb)]
    lanes = [(bi, gi) for bi in range(nb) for gi in range(n_groups)]
    for ci in range(n_chunks):
        ps = {k: pre[k[0], ci, k[1]] for k in lanes}
        sidx = {k: k[0] * n_groups + k[1] for k in lanes}
        kq_s = {k: _mm(jnp.concatenate([ps[k]["km"] * ps[k]["e_cum"], ps[k]["qm"] * ps[k]["e_cum"]], axis=0),
                       states[sidx[k]]) for k in lanes}
        v_new = {k: _mm(ps[k]["inv"], ps[k]["beta"] * (ps[k]["vm"] - kq_s[k][:BD])) for k in lanes}
        o_bd = {k: kq_s[k][BD:] + _mm(ps[k]["qk"], v_new[k]) for k in lanes}
        for k in lanes:
            p = ps[k]
            k_dec = p["km"] * jnp.exp(p["g_last"] - p["g_cum"])
            states[sidx[k]] = states[sidx[k]] * jnp.exp(p["g_last"]) + _mm(k_dec.T, v_new[k])
        for bi in range(nb):
            o_groups = []
            for gi in range(n_groups):
                ob = o_bd[bi, gi]
                ob = ob * lax.rsqrt(jnp.sum(ob * ob, axis=1, keepdims=True) * (1.0 / A_DV) + EPS)
                o_groups.append(sum(ob[h * CHUNK:(h + 1) * CHUNK] for h in range(GROUP_HEADS)))
            outs[bi].append(jnp.concatenate(o_groups, axis=1))

    for i, state in enumerate(states):
        st_ref[i] = state
    for bi in range(nb):
        o = outs[bi][0] if n_chunks == 1 else jnp.concatenate(outs[bi], axis=0)
        zz = zz_ref[bi].astype(F32)
        o_ref[bi] = (o * ag_ref[...] * (zz * jax.nn.sigmoid(zz))).astype(BF16)


def _gdn(z3, zs3, conv_w, a_log, dt_bias, a_norm_g, t=2 * CHUNK, nb=2):
    b, s, _ = z3.shape
    assert b % nb == 0 and s % t == 0 and t % CHUNK == 0 and t >= 8
    consts = _gdn_constants()
    av = jnp.zeros((2, 128), F32)
    av = av.at[0, S_ALPHA:S_ALPHA + A_HEADS].set(a_log.astype(F32))
    av = av.at[1, S_ALPHA:S_ALPHA + A_HEADS].set(dt_bias.astype(F32))
    ag = jnp.tile(a_norm_g.astype(F32), A_HEADS)[None, :]
    conv_cols = 2 * A_HEADS * A_DK + A_HEADS * A_DV

    def const_spec(a):
        nd = a.ndim
        return pl.BlockSpec(a.shape, lambda bi, ci, _n=nd: (0,) * _n)

    small_in = (conv_w.astype(F32), av, ag) + consts
    return pl.pallas_call(
        _gdn_kernel,
        grid=(b // nb, s // t),
        in_specs=[
            pl.BlockSpec((nb, t, conv_cols), lambda bi, ci: (bi, ci, O_AQKV // conv_cols)),
            pl.BlockSpec((nb, t, 512), lambda bi, ci: (bi, ci, O_AZ // 512)),
            pl.BlockSpec((nb, t, 128), lambda bi, ci: (bi, ci, 0)),
        ] + [const_spec(a) for a in small_in],
        out_specs=pl.BlockSpec((nb, t, A_HEADS * A_DV), lambda bi, ci: (bi, ci, 0)),
        out_shape=jax.ShapeDtypeStruct((b, s, A_HEADS * A_DV), BF16),
        scratch_shapes=[pltpu.VMEM((nb, 8 + t, conv_cols), F32),
                        pltpu.VMEM((nb * (A_HEADS // GROUP_HEADS), BD, BD), F32)],
        compiler_params=pltpu.CompilerParams(dimension_semantics=("parallel", "arbitrary"),
                                             vmem_limit_bytes=VMEM_LIMIT),
        name="gdn",
    )(z3, z3, zs3, *small_in)


def _dsa_kernel(topk, tk, q_ref, iq_ref, ka_ref, kb_ref, vd_ref, ik_ref, zs_ref, o_ref, st_ref, vt_ref):
    qb = pl.program_id(1)
    s = ka_ref.shape[0]
    tq = q_ref.shape[0]
    nkt_max = s // tk
    r0 = qb * tq
    n_tiles = (r0 + tq + tk - 1) // tk
    heads_per_kv = B_HEADS // B_KV_HEADS

    @pl.when(qb == 0)
    def _():
        for t in range(nkt_max):
            for g in range(B_KV_HEADS):
                v_tile = vd_ref[t * tk:(t + 1) * tk, g * 128:(g + 1) * 128]
                vt_ref[t, g] = v_tile.astype(F32).T.astype(BF16)

    lane128 = lax.broadcasted_iota(jnp.int32, (tq, 128), 1)
    low_half = lane128 < B_HD
    high_half = lane128 >= B_HD

    def head_slab(x, h):
        slab = x[:, (h // 2) * 128:(h // 2 + 1) * 128]
        return jnp.where(low_half if h % 2 == 0 else high_half, slab, jnp.zeros_like(slab))

    zs_t = zs_ref[...].T
    iw_scale = (IDX_HEADS ** -0.5) * (IDX_HD ** -0.5)
    wrows = [zs_t[S_IW + h:S_IW + h + 1, :] * iw_scale for h in range(IDX_HEADS)]
    qrow = r0 + lax.broadcasted_iota(jnp.int32, (1, tq), 1)
    limit = ((qrow >> 6) + 1) << 6
    small = limit <= topk

    iq = iq_ref[...]
    iq_heads = [head_slab(iq, h) for h in range(IDX_HEADS)]

    def index_tile(kt, carry):
        r_max, r_min = carry
        off = pl.multiple_of(kt * tk, tk)
        ikt = ik_ref[pl.ds(off, tk), :]
        acc = jnp.zeros((tk, tq), F32)
        for h in range(IDX_HEADS):
            sc = lax.dot_general(ikt, iq_heads[h], (((1,), (1,)), ((), ())), preferred_element_type=F32)
            acc = acc + wrows[h] * jnp.maximum(sc, 0.0)
        valid = off + lax.broadcasted_iota(jnp.int32, (tk, tq), 0) < limit
        masked = jnp.where(valid, acc, -jnp.inf)
        st_ref[kt] = masked
        r_max = jnp.maximum(r_max, jnp.max(masked, axis=0, keepdims=True))
        r_min = jnp.minimum(r_min, jnp.min(acc, axis=0, keepdims=True))
        return r_max, r_min

    r_max, r_min = lax.fori_loop(0, n_tiles, index_tile,
                                 (jnp.full((1, tq), -jnp.inf, F32), jnp.full((1, tq), jnp.inf, F32)))

    all_kept = r0 + tq <= min(topk, tk)

    @pl.when(all_kept)
    def _():
        st_ref[0] = jnp.where(st_ref[0] > -jnp.inf, 0.0, NEG_BIG)

    for j in range(nkt_max):
        @pl.when(jnp.logical_and(n_tiles == j + 1, jnp.logical_not(all_kept)))
        def _(j=j):
            _topk_mask(topk, j + 1, small, r_max, r_min, st_ref)

    q = q_ref[...]
    nr = heads_per_kv * tq
    qcols = [jnp.concatenate([head_slab(q, g * heads_per_kv + j) for j in range(heads_per_kv)], axis=0)
             for g in range(B_KV_HEADS)]
    k_refs = (ka_ref, kb_ref)

    def attend_tile(kt, carry):
        off = pl.multiple_of(kt * tk, tk)
        bias = st_ref[kt]
        bias4 = jnp.concatenate([bias] * heads_per_kv, axis=1)
        groups = range(B_KV_HEADS)
        logits = [lax.dot_general(k_refs[g][pl.ds(off, tk), :], qcols[g], (((1,), (1,)), ((), ())),
                                  preferred_element_type=F32) + bias4 for g in groups]
        m_new = [jnp.maximum(carry[g][0], jnp.max(logits[g], axis=0, keepdims=True)) for g in groups]
        p = [jnp.exp2(logits[g] - m_new[g]) for g in groups]
        alpha = [jnp.exp2(carry[g][0] - m_new[g]) for g in groups]
        l_new = [alpha[g] * carry[g][1] + jnp.sum(p[g], axis=0, keepdims=True) for g in groups]
        pv = [_dot(vt_ref[kt, g], p[g].astype(BF16)) for g in groups]
        return tuple((m_new[g], l_new[g], alpha[g] * carry[g][2] + pv[g]) for g in groups)

    init = tuple((jnp.full((1, nr), NEG_BIG, F32), jnp.zeros((1, nr), F32), jnp.zeros((128, nr), F32))
                 for _ in range(B_KV_HEADS))
    fin = lax.fori_loop(0, n_tiles, attend_tile, init)
    for g in range(B_KV_HEADS):
        _, l_run, acc = fin[g]
        og = acc / l_run
        heads = [og[:, j * tq:(j + 1) * tq].T for j in range(heads_per_kv)]
        for pp in range(heads_per_kv // 2):
            col = (g * (heads_per_kv // 2) + pp) * 128
            o_ref[:, col:col + 128] = jnp.where(low_half, heads[2 * pp], heads[2 * pp + 1]).astype(BF16)


def _topk_mask(topk, nt, small, r_max, r_min, st_ref):
    _, tk, tq = st_ref.shape
    kf = float(topk)

    def tiles():
        return [st_ref[t] for t in range(nt)]

    def q_sum(pred):
        tot = None
        for t, x in enumerate(tiles()):
            c = jnp.sum(pred(x, t).astype(F32), axis=0, keepdims=True)
            tot = c if tot is None else tot + c
        return tot

    def q_max(val):
        best = None
        for t, x in enumerate(tiles()):
            c = jnp.max(val(x, t), axis=0, keepdims=True)
            best = c if best is None else jnp.maximum(best, c)
        return best

    hi0 = r_max + jnp.maximum(jnp.abs(r_max), 1e-30) * 1e-6

    def bisect(_, carry):
        lo, hi = carry
        mid = 0.5 * (lo + hi)
        ge = q_sum(lambda x, t: x >= mid) >= kf
        return jnp.where(ge, mid, lo), jnp.where(ge, hi, mid)

    lo, hi = lax.fori_loop(0, BISECT_STEPS, bisect, (r_min, hi0))

    def peel_cond(carry):
        return jnp.sum(1.0 - carry[0]) > 0.0

    def peel(carry):
        done, thr, hi_c, n_ge = carry
        v1 = q_max(lambda x, t: jnp.where(x < hi_c, x, -jnp.inf))
        c1 = q_sum(lambda x, t: x >= v1)
        ok = c1 >= kf
        act = done < 0.5
        thr = jnp.where(act & ok, v1, thr)
        n_ge = jnp.where(act & ok, c1, n_ge)
        hi_c = jnp.where(act & (~ok), v1, hi_c)
        return jnp.where(ok, 1.0, done), thr, hi_c, n_ge

    _, thr, _, n_ge = lax.while_loop(peel_cond, peel, (jnp.where(small, 1.0, 0.0), lo, hi, jnp.full_like(lo, kf)))

    def key_index(t):
        return (t * tk + lax.broadcasted_iota(jnp.int32, (tk, tq), 0)).astype(F32)

    contested = jnp.sum(jnp.where((n_ge > kf) & (~small), 1.0, 0.0)) > 0.0
    last = float(nt * tk - 1)

    def tie_cut():
        need = kf - q_sum(lambda x, t: x > thr)

        def tie_search(_, carry):
            jlo, jhi = carry
            mid = jnp.floor(0.5 * (jlo + jhi))
            ge = q_sum(lambda x, t: (x == thr) & (key_index(t) <= mid)) >= need
            return jnp.where(ge, jlo, mid), jnp.where(ge, mid, jhi)

        n_iter = int(np.ceil(np.log2(nt * tk))) + 1
        return lax.fori_loop(0, n_iter, tie_search,
                             (jnp.full((1, tq), -1.0, F32), jnp.full((1, tq), last, F32)))[1]

    jcut = lax.cond(contested, tie_cut, lambda: jnp.full((1, tq), last, F32))
    for t, x in enumerate(tiles()):
        sel = (x > -jnp.inf) & (small | (x > thr) | ((x == thr) & (key_index(t) <= jcut)))
        st_ref[t] = jnp.where(sel, 0.0, NEG_BIG)


def _dsa(z3, zs3, tq=512, tk=512):
    b, s, _ = z3.shape
    assert s % tq == 0 and s % tk == 0 and tq % CHUNK == 0 and tk % CHUNK == 0
    topk = min(TOPK_MAX, s // 4)
    kernel = lambda *refs: _dsa_kernel(topk, tk, *refs)
    qblock = lambda w, cb: pl.BlockSpec((None, tq, w), lambda bi, qi, _c=cb: (bi, qi, _c))
    keys = lambda w, cb: pl.BlockSpec((None, s, w), lambda bi, qi, _c=cb: (bi, 0, _c))
    return pl.pallas_call(
        kernel,
        grid=(b, s // tq),
        in_specs=[qblock(512, O_BQ // 512), qblock(512, O_IQ // 512), keys(128, O_KD // 128),
                  keys(128, O_KD // 128 + 1), keys(256, O_VD // 256), keys(128, O_IK // 128), qblock(128, 0)],
        out_specs=pl.BlockSpec((None, tq, B_HEADS * B_HD), lambda bi, qi: (bi, qi, 0)),
        out_shape=jax.ShapeDtypeStruct((b, s, B_HEADS * B_HD), BF16),
        scratch_shapes=[pltpu.VMEM((s // tk, tk, tq), F32),
                        pltpu.VMEM((s // tk, B_KV_HEADS, 128, tk), BF16)],
        compiler_params=pltpu.CompilerParams(dimension_semantics=("parallel", "arbitrary"),
                                             vmem_limit_bytes=VMEM_LIMIT),
        name="dsa",
    )(z3, z3, z3, z3, z3, z3, zs3)


def _merge_kernel(x_ref, oa_ref, ob_ref, ga_ref, gb_ref, bg_ref, wa_ref, wb_ref, wo_ref, n2_ref, wr_ref,
                  br_ref, x1_ref, h2_ref, rl_ref):
    pa = _dot(oa_ref[...], wa_ref[...])
    pb = _dot(ob_ref[...], wb_ref[...])
    bgv = bg_ref[...]
    ga = jax.nn.sigmoid(ga_ref[...].astype(F32) + bgv[:, :D_MODEL])
    gb = jax.nn.sigmoid(gb_ref[...].astype(F32) + bgv[:, D_MODEL:])
    merged = ga * pa + gb * pb
    x1 = x_ref[...] + _mm(merged, wo_ref[...])
    x1_ref[...] = x1
    h2 = _rmsnorm_rows(x1, n2_ref[...])
    hb, hl = _split(h2)
    h2_ref[...] = hb
    s1 = _dot(hb, wr_ref[...])
    s2 = _dot(hl, wr_ref[:, :128])
    rl_ref[...] = s1[:, :128] + s1[:, 128:] + s2 + br_ref[...]


def _merge(x2d, oa, ob, z2d, b_gate, wa, wb, wo, n2, wr, br, tm=512):
    n = x2d.shape[0]
    assert n % tm == 0 and oa.shape == ob.shape == (n, 512)
    full =lambda shape: pl.BlockSpec(shape, lambda i: (0, 0))
    return pl.pallas_call(
        _merge_kernel,
        grid=(n // tm,),
        in_specs=[
            pl.BlockSpec((tm, D_MODEL), lambda i: (i, 0)),
            pl.BlockSpec((tm, 512), lambda i: (i, 0)),
            pl.BlockSpec((tm, 512), lambda i: (i, 0)),
            pl.BlockSpec((tm, D_MODEL), lambda i: (i, O_GA // D_MODEL)),
            pl.BlockSpec((tm, D_MODEL), lambda i: (i, O_GB // D_MODEL)),
            full((1, 2 * D_MODEL)), full((512, D_MODEL)), full((512, D_MODEL)), full((D_MODEL, D_MODEL)),
            full((1, D_MODEL)), full((D_MODEL, 256)), full((1, 128)),
        ],
        out_specs=[
            pl.BlockSpec((tm, D_MODEL), lambda i: (i, 0)),
            pl.BlockSpec((tm, D_MODEL), lambda i: (i, 0)),
            pl.BlockSpec((tm, 128), lambda i: (i, 0)),
        ],
        out_shape=[jax.ShapeDtypeStruct((n, D_MODEL), F32), jax.ShapeDtypeStruct((n, D_MODEL), BF16),
                   jax.ShapeDtypeStruct((n, 128), F32)],
        compiler_params=pltpu.CompilerParams(dimension_semantics=("parallel",),
                                             vmem_limit_bytes=VMEM_LIMIT),
        name="merge",
    )(x2d, oa, ob, z2d, z2d, b_gate, wa, wb, wo, n2, wr, br)


R_GROUP, R_EXPERT = 0, 8


def _routing_weights(rl):
    t = rl.T
    n_tok = t.shape[1]
    gl = t[R_GROUP:R_GROUP + N_GROUPS]
    gidx = lax.broadcasted_iota(jnp.int32, gl.shape, 0)
    gmax = jnp.max(gl, axis=0, keepdims=True)
    gsel = jnp.min(jnp.where(gl == gmax, gidx, N_GROUPS), axis=0, keepdims=True)
    ggate = 1.0 / jnp.sum(jnp.exp(gl - gmax), axis=0, keepdims=True)
    el = t[R_EXPERT:R_EXPERT + N_EXPERTS]
    eidx = lax.broadcasted_iota(jnp.int32, el.shape, 0)
    e_lo = gsel * EXPERTS_PER_GROUP
    emask = (eidx >= e_lo) & (eidx < e_lo + EXPERTS_PER_GROUP)
    el = jnp.where(emask, el, -jnp.inf)
    emax = jnp.max(el, axis=0, keepdims=True)
    ee = jnp.where(emask, jnp.exp(el - emax), 0.0)
    ep = jnp.where(emask, ee / jnp.sum(ee, axis=0, keepdims=True), -1.0)
    p1 = jnp.max(ep, axis=0, keepdims=True)
    i1 = jnp.min(jnp.where(ep == p1, eidx, N_EXPERTS), axis=0, keepdims=True)
    ep2 = jnp.where(eidx == i1, -1.0, ep)
    p2 = jnp.max(ep2, axis=0, keepdims=True)
    i2 = jnp.min(jnp.where(ep2 == p2, eidx, N_EXPERTS), axis=0, keepdims=True)
    tot = p1 + p2
    comb_t = (jnp.where(eidx == i1, ggate * (p1 / tot), 0.0)
              + jnp.where(eidx == i2, ggate * (p2 / tot), 0.0))
    full = jnp.concatenate([jnp.zeros((R_EXPERT, n_tok), F32), comb_t,
                            jnp.zeros((128 - R_EXPERT - N_EXPERTS, n_tok), F32)], axis=0)
    return full.T


def _moe_kernel(x1_ref, h2_ref, rl_ref, w1_ref, w3_ref, w2_ref, fg_ref, o_ref, y_ref, comb_ref):
    step = pl.program_id(1)
    per_step = w1_ref.shape[0]

    @pl.when(step == 0)
    def _():
        comb_ref[...] = _routing_weights(rl_ref[...])
        y_ref[...] = jnp.zeros(y_ref.shape, F32)

    h = h2_ref[...]
    comb = comb_ref[...]
    lane = lax.broadcasted_iota(jnp.int32, comb.shape, 1)
    acts = []
    for j in range(per_step):
        a = _dot(h, w1_ref[j])
        b = _dot(h, w3_ref[j])
        ce = jnp.sum(jnp.where(lane == R_EXPERT + step * per_step + j, comb, 0.0), axis=1, keepdims=True)
        acts.append(((a * jax.nn.sigmoid(a)) * b * ce).astype(BF16))
    y_ref[...] += _dot(jnp.concatenate(acts, axis=1), w2_ref[...])

    @pl.when(step == pl.num_programs(1) - 1)
    def _():
        o_ref[...] = _rmsnorm_rows(x1_ref[...] + y_ref[...], fg_ref[...])


def _moe(x1, h2, rl, w1, w3, w2, fg, tm=1024, per_step=4):
    n = x1.shape[0]
    assert n % tm == 0 and N_EXPERTS % per_step == 0 and w1.shape == (N_EXPERTS, D_MODEL, D_EXPERT)
    w2g =w2.reshape(N_EXPERTS // per_step, per_step * D_EXPERT, D_MODEL)
    return pl.pallas_call(
        _moe_kernel,
        grid=(n // tm, N_EXPERTS // per_step),
        in_specs=[
            pl.BlockSpec((tm, D_MODEL), lambda i, e: (i, 0)),
            pl.BlockSpec((tm, D_MODEL), lambda i, e: (i, 0)),
            pl.BlockSpec((tm, 128), lambda i, e: (i, 0)),
            pl.BlockSpec((per_step, D_MODEL, D_EXPERT), lambda i, e: (e, 0, 0)),
            pl.BlockSpec((per_step, D_MODEL, D_EXPERT), lambda i, e: (e, 0, 0)),
            pl.BlockSpec((None, per_step * D_EXPERT, D_MODEL), lambda i, e: (e, 0, 0)),
            pl.BlockSpec((1, D_MODEL), lambda i, e: (0, 0)),
        ],
        out_specs=pl.BlockSpec((tm, D_MODEL), lambda i, e: (i, 0)),
        out_shape=jax.ShapeDtypeStruct((n, D_MODEL), F32),
        scratch_shapes=[pltpu.VMEM((tm, D_MODEL), F32), pltpu.VMEM((tm, 128), F32)],
        compiler_params=pltpu.CompilerParams(dimension_semantics=("parallel", "arbitrary"),
                                             vmem_limit_bytes=VMEM_LIMIT),
        name="moe",
    )(x1, h2, rl, w1, w3, w2g, fg)


_W_OFF = {}
_off = 0
for _name, _n in (("a_q", 512), ("a_k", 512), ("a_v", 512), ("a_z", 512), ("a_beta", 8), ("a_alpha", 8),
                  ("b_q", 512), ("b_k", 128), ("b_v", 128), ("i_q", 512), ("i_k", 64), ("i_w", 8),
                  ("gate_a", 1024), ("gate_b", 1024)):
    _W_OFF[_name] = (_off, _off + _n)
    _off += _n


def _cols(w, name, lo=0, hi=None):
    a, b = _W_OFF[name]
    return w[:, a + lo:(b if hi is None else a + hi)]


def _hi_lo_pair(w_small):
    pad = jnp.zeros((w_small.shape[0], 128 - w_small.shape[1]), F32)
    w = jnp.concatenate([w_small.astype(F32), pad], axis=1)
    hi, lo = _split(w)
    return jnp.concatenate([hi, lo], axis=1)


def _transpose_cast_kernel(off_ref, wt_ref, out_ref):
    del off_ref
    blk = wt_ref[...]
    reps = out_ref.shape[1] // blk.shape[0]
    if reps > 1:
        blk = jnp.concatenate([blk] * reps, axis=0)
    out_ref[...] = blk.T.astype(BF16)


def _gather_columns(wt, offsets, rows, out_cols):
    d = wt.shape[1]
    grid_spec = pltpu.PrefetchScalarGridSpec(
        num_scalar_prefetch=1,
        grid=(len(offsets),),
        in_specs=[pl.BlockSpec((pl.Element(rows), pl.Element(d)), lambda u, off: (pl.multiple_of(off[u], 8), 0))],
        out_specs=pl.BlockSpec((d, out_cols), lambda u, off: (0, u)),
    )
    return pl.pallas_call(
        _transpose_cast_kernel,
        grid_spec=grid_spec,
        out_shape=jax.ShapeDtypeStruct((d, out_cols * len(offsets)), BF16),
        compiler_params=pltpu.CompilerParams(dimension_semantics=("arbitrary",), vmem_limit_bytes=VMEM_LIMIT),
        name="w_layout",
    )(jnp.asarray(offsets, jnp.int32), wt)


def _layout_w_in(w):
    wt = jnp.swapaxes(w, 0, 1)
    start = lambda name: _W_OFF[name][0]
    assert all(start(nm) % 8 == 0 for nm in ("a_q", "gate_a", "b_q", "i_q", "b_k", "b_v", "i_k"))
    main_off = ([start("a_q") + 512 * i for i in range(4)] + [start("gate_a") + 512 * i for i in range(4)]
                + [start("b_q"), start("i_q")])
    tail_off = [start("b_k"), start("b_k") + 64, start("b_v"), start("b_v") + 64, start("i_k")]
    main = _gather_columns(wt, main_off, 512, 512)
    tail = _gather_columns(wt, tail_off, 64, 128)
    small_t = jnp.concatenate([wt[slice(*_W_OFF[nm])] for nm in ("a_beta", "a_alpha", "i_w")], axis=0)
    small = _hi_lo_pair(jnp.swapaxes(small_t, 0, 1))
    return main, tail, small


def kernel(x, positions, norm1_g, w_in, b_gate, conv_w, a_log, dt_bias, a_norm_g, w_proj_a, w_proj_b, w_out,
           norm2_g, w_router_group, b_router_group, w_router_expert, b_router_expert, w_exp_gate, w_exp_up,
           w_exp_down, final_norm_g):
    b, s, d = x.shape
    n = b * s
    assert w_in.shape[0] == 1 and d == D_MODEL, "one layer: the MoE kernel also applies the final norm"
    xc = x.reshape(n, d).astype(F32)
    for l in range(w_in.shape[0]):
        w_main, w_tail, w_small = _layout_w_in(w_in[l])
        z, zs = _in_proj(xc, norm1_g[l][None, :].astype(F32), w_main, w_tail, w_small, positions)
        z3 = z.reshape(b, s, Z_W)
        zs3 = zs.reshape(b, s, 128)
        o_a = _gdn(z3, zs3, conv_w[l], a_log[l], dt_bias[l], a_norm_g[l])
        o_b = _dsa(z3, zs3)
        gap = R_EXPERT - N_GROUPS
        wr = _hi_lo_pair(jnp.concatenate([w_router_group[l], jnp.zeros((d, gap), F32), w_router_expert[l]],
                                         axis=1))
        br = jnp.concatenate([b_router_group[l], jnp.zeros((gap,), F32), b_router_expert[l],
                              jnp.zeros((128 - R_EXPERT - N_EXPERTS,), F32)])[None, :].astype(F32)
        x1, h2, rl = _merge(xc, o_a.reshape(n, -1), o_b.reshape(n, -1), z, b_gate[l][None, :].astype(F32),
                            w_proj_a[l].astype(BF16), w_proj_b[l].astype(BF16), w_out[l].astype(BF16),
                            norm2_g[l][None, :].astype(F32), wr, br)
        xc = _moe(x1, h2, rl, w_exp_gate[l].astype(BF16), w_exp_up[l].astype(BF16),
                  w_exp_down[l].astype(BF16), final_norm_g[None, :].astype(F32))
    return xc.reshape(b, s, d).astype(x.dtype)
```

```python
import numpy as np
import jax
import jax.numpy as jnp
from jax import lax
from jax.experimental import pallas as pl
from jax.experimental.pallas import tpu as pltpu

F32 = jnp.float32
BF16 = jnp.bfloat16

D_MODEL = 1024
CHUNK = 64
EPS = 1e-6
ROPE_THETA = 10000.0
A_HEADS = 8
A_DK = 64
A_DV = 64
CONV_K = 4
B_HEADS = 8
B_KV_HEADS = 2
B_HD = 64
IDX_HEADS = 8
IDX_HD = 64
TOPK_MAX = 256
N_GROUPS = 4
EXPERTS_PER_GROUP = 4
N_EXPERTS = 16
D_EXPERT = 256

O_AQKV, O_AZ, O_GA, O_GB, O_BQ, O_IQ, O_KD, O_VD, O_IK = 0, 1536, 2048, 3072, 4096, 4608, 5120, 5376, 5632
Z_W = 5760
Z_MAIN = O_KD
S_BETA, S_ALPHA, S_IW = 0, 8, 16

GROUP_HEADS = 2
BD = GROUP_HEADS * CHUNK
NEG_BIG = -1e30
LOG2E = 1.4426950408889634
BISECT_STEPS = 18
VMEM_LIMIT = 56 * 1024 * 1024


def _split(x):
    hi = x.astype(BF16)
    lo = (x - hi.astype(F32)).astype(BF16)
    return hi, lo


def _dot(a, b):
    return jnp.dot(a, b, preferred_element_type=F32)


def _mm(a, b):
    return _dot(a.astype(BF16), b.astype(BF16))


def _mm_nt(a, b):
    return lax.dot_general(a.astype(BF16), b.astype(BF16), (((1,), (1,)), ((), ())),
                           preferred_element_type=F32)


def _mm_exact_lhs(a_bf16, x):
    xh, xl = _split(x)
    return _dot(a_bf16, xh) + _dot(a_bf16, xl)


def _rmsnorm_rows(x, g):
    return x * lax.rsqrt(jnp.mean(x * x, axis=-1, keepdims=True) + EPS) * g


_Z_CHUNKS = tuple((o, min(512, Z_W - o)) for o in range(0, Z_W, 512))


def _rope(x, cs, sn, first):
    w = x.shape[1]
    rep = w // 128
    if rep > 1:
        cs, sn, first = (jnp.concatenate([a] * rep, axis=1) for a in (cs, sn, first))
    swapped = jnp.where(first, pltpu.roll(x, w - B_HD // 2, 1), pltpu.roll(x, B_HD // 2, 1))
    return x * cs + swapped * sn


def _in_proj_kernel(x_ref, g_ref, w_ref, wt_ref, ws_ref, pos_ref, inv_ref, z_ref, zs_ref):
    h = _rmsnorm_rows(x_ref[...], g_ref[...])
    hb, hl = _split(h)

    ang = pos_ref[...] * inv_ref[...]
    lane = lax.broadcasted_iota(jnp.int32, ang.shape, 1)
    first = (lane & (B_HD - 1)) < (B_HD // 2)
    cs = jnp.cos(ang)
    sn = jnp.sin(ang)
    sn = jnp.where(first, -sn, sn)

    for o, w in _Z_CHUNKS:
        r = _dot(hb, w_ref[:, o:o + w] if o < Z_MAIN else wt_ref[:, o - Z_MAIN:o - Z_MAIN + w])
        if o == O_BQ:
            r = _rope(r, cs, sn, first) * (B_HD ** -0.5 * LOG2E)
        elif o == O_IQ or o == O_IK:
            r = _rope(r, cs, sn, first)
        elif o == O_KD:
            kw = O_VD - O_KD
            r = jnp.concatenate([_rope(r[:, :kw], cs, sn, first), r[:, kw:]], axis=1)
        z_ref[:, o:o + w] = r.astype(BF16)
    s1 = _dot(hb, ws_ref[...])
    s2 = _dot(hl, ws_ref[:, :128])
    zs_ref[...] = s1[:, :128] + s1[:, 128:] + s2


def _in_proj(x2d, g, w_main, w_tail, w_small, positions, tm=512):
    n = x2d.shape[0]
    assert n % tm == 0 and x2d.shape[1] == D_MODEL
    assert w_main.shape == (D_MODEL, Z_MAIN) and w_tail.shape == (D_MODEL, Z_W - Z_MAIN)
    half = B_HD // 2
    inv = ROPE_THETA ** (-jnp.arange(half, dtype=F32) / half)
    inv128 = jnp.tile(inv, 4)[None, :]
    pos = positions.astype(F32).reshape(n, 1)
    return pl.pallas_call(
        _in_proj_kernel,
        grid=(n // tm,),
        in_specs=[
            pl.BlockSpec((tm, D_MODEL), lambda i: (i, 0)),
            pl.BlockSpec((1, D_MODEL), lambda i: (0, 0)),
            pl.BlockSpec((D_MODEL, Z_MAIN), lambda i: (0, 0)),
            pl.BlockSpec((D_MODEL, Z_W - Z_MAIN), lambda i: (0, 0)),
            pl.BlockSpec((D_MODEL, 256), lambda i: (0, 0)),
            pl.BlockSpec((tm, 1), lambda i: (i, 0)),
            pl.BlockSpec((1, 128), lambda i: (0, 0)),
        ],
        out_specs=[
            pl.BlockSpec((tm, Z_W), lambda i: (i, 0)),
            pl.BlockSpec((tm, 128), lambda i: (i, 0)),
        ],
        out_shape=[jax.ShapeDtypeStruct((n, Z_W), BF16), jax.ShapeDtypeStruct((n, 128), F32)],
        compiler_params=pltpu.CompilerParams(dimension_semantics=("parallel",),
                                             vmem_limit_bytes=VMEM_LIMIT),
        name="in_proj",
    )(x2d, g, w_main, w_tail, w_small, pos, inv128)


def _gdn_constants():
    r = np.arange(BD)
    same = (r[:, None] // CHUNK) == (r[None, :] // CHUNK)
    incl = same & (r[:, None] >= r[None, :])
    strict = same & (r[:, None] > r[None, :])
    eye = np.eye(BD, dtype=np.float32)
    ll = np.concatenate([incl, same], axis=0).astype(np.float32)
    n_groups = A_HEADS // GROUP_HEADS
    sel = np.zeros((2 * n_groups, BD, 128), np.float32)
    for gi in range(n_groups):
        for h in range(GROUP_HEADS):
            sel[gi * 2 + 0, h * CHUNK:(h + 1) * CHUNK, S_BETA + gi * GROUP_HEADS + h] = 1.0
            sel[gi * 2 + 1, h * CHUNK:(h + 1) * CHUNK, S_ALPHA + gi * GROUP_HEADS + h] = 1.0
    return (jnp.asarray(incl, F32), jnp.asarray(strict, F32), jnp.asarray(same, F32), jnp.asarray(eye),
            jnp.asarray(ll, BF16), jnp.asarray(sel))


def _tile_heads(x):
    return jnp.concatenate([x] * GROUP_HEADS, axis=0)


def _gdn_kernel(zq_ref, zz_ref, zs_ref, cw_ref, av_ref, ag_ref, incl_ref, strict_ref, bdm_ref, eye_ref,
                ll_ref, sel_ref, o_ref, ext_ref, st_ref):
    c = pl.program_id(1)
    nb, t = zq_ref.shape[0], zq_ref.shape[1]

    @pl.when(c == 0)
    def _():
        ext_ref[:, 0:8, :] = jnp.zeros((nb, 8, ext_ref.shape[2]), F32)
        st_ref[...] = jnp.zeros(st_ref.shape, F32)

    hw = A_HEADS * A_DK
    cw = cw_ref[...]
    av = av_ref[...]
    q_all, k_all, v_all, bg = [], [], [], []
    for bi in range(nb):
        ext_ref[bi, 8:8 + t, :] = zq_ref[bi].astype(F32)
        y = cw[0:1, :] * ext_ref[bi, pl.ds(8 - (CONV_K - 1), t), :]
        for j in range(1, CONV_K):
            y = y + cw[j:j + 1, :] * ext_ref[bi, pl.ds(8 - (CONV_K - 1) + j, t), :]
        ext_ref[bi, 0:8, :] = ext_ref[bi, t:t + 8, :]
        y = y * jax.nn.sigmoid(y)
        q_all.append(y[:, :hw])
        k_all.append(y[:, hw:2 * hw])
        v_all.append(y[:, 2 * hw:])

        sm = zs_ref[bi]
        lane = lax.broadcasted_iota(jnp.int32, sm.shape, 1)
        xg = sm + av[1:2, :]
        softplus = jnp.maximum(xg, 0.0) + jnp.log1p(jnp.exp(-jnp.abs(xg)))
        g_all = -jnp.exp(av[0:1, :]) * softplus
        bg.append(jnp.where(lane < S_ALPHA, jax.nn.sigmoid(sm), g_all))

    incl = incl_ref[...]
    strict = strict_ref[...]
    bdm = bdm_ref[...]
    eye = eye_ref[...]
    ll = ll_ref[...]
    incl_b = ll[:BD]

    n_chunks = t // CHUNK
    n_groups = A_HEADS // GROUP_HEADS
    chains = [(bi, ci, gi) for bi in range(nb) for ci in range(n_chunks) for gi in range(n_groups)]

    pre = {}
    for bi, ci, gi in chains:
        r0, c0 = ci * CHUNK, gi * BD
        bg4 = _tile_heads(bg[bi][r0:r0 + CHUNK])
        beta = jnp.sum(bg4 * sel_ref[gi * 2 + 0], axis=1, keepdims=True)
        gcol = jnp.sum(bg4 * sel_ref[gi * 2 + 1], axis=1, keepdims=True)
        gs = _mm_exact_lhs(ll, jnp.broadcast_to(gcol, (BD, 128)))
        g_cum = gs[:BD, :1]
        g_last = gs[BD:, :1]
        diff = _mm_exact_lhs(incl_b, gcol * strict)
        decay = jnp.where(incl > 0.0, jnp.exp(diff), 0.0)
        e_cum = jnp.exp(g_cum)
        kr = _tile_heads(k_all[bi][r0:r0 + CHUNK, c0:c0 + BD]) * bdm
        qr = _tile_heads(q_all[bi][r0:r0 + CHUNK, c0:c0 + BD]) * bdm
        vm = _tile_heads(v_all[bi][r0:r0 + CHUNK, c0:c0 + BD]) * bdm
        km = kr * lax.rsqrt(jnp.sum(kr * kr, axis=1, keepdims=True) + EPS)
        qm = qr * (lax.rsqrt(jnp.sum(qr * qr, axis=1, keepdims=True) + EPS) * (A_DK ** -0.5))
        kkqk = _mm_nt(jnp.concatenate([km, qm], axis=0), km)
        m = -(strict * beta * kkqk[:BD] * decay)
        pre[bi, ci, gi] = dict(beta=beta, g_cum=g_cum, g_last=g_last, e_cum=e_cum, km=km, qm=qm, vm=vm,
                               qk=kkqk[BD:] * decay, m=m, inv=eye + m)

    sq = CHUNK
    while sq > 2:
        for key in chains:
            p = pre[key]
            p["m"] = _mm(p["m"], p["m"])
        for key in chains:
            p = pre[key]
            p["inv"] = p["inv"] + _mm(p["inv"], p["m"])
        sq //= 2

    states = [st_ref[i] for i in range(nb * n_groups)]
    outs = [[] for _ in range(nb)]
    lanes = [(bi, gi) for bi in range(nb) for gi in range(n_groups)]
    for ci in range(n_chunks):
        ps = {k: pre[k[0], ci, k[1]] for k in lanes}
        sidx = {k: k[0] * n_groups + k[1] for k in lanes}
        kq_s = {k: _mm(jnp.concatenate([ps[k]["km"] * ps[k]["e_cum"], ps[k]["qm"] * ps[k]["e_cum"]], axis=0),
                       states[sidx[k]]) for k in lanes}
        v_new = {k: _mm(ps[k]["inv"], ps[k]["beta"] * (ps[k]["vm"] - kq_s[k][:BD])) for k in lanes}
        o_bd = {k: kq_s[k][BD:] + _mm(ps[k]["qk"], v_new[k]) for k in lanes}
        for k in lanes:
            p = ps[k]
            k_dec = p["km"] * jnp.exp(p["g_last"] - p["g_cum"])
            states[sidx[k]] = states[sidx[k]] * jnp.exp(p["g_last"]) + _mm(k_dec.T, v_new[k])
        for bi in range(nb):
            o_groups = []
            for gi in range(n_groups):
                ob = o_bd[bi, gi]
                ob = ob * lax.rsqrt(jnp.sum(ob * ob, axis=1, keepdims=True) * (1.0 / A_DV) + EPS)
                o_groups.append(sum(ob[h * CHUNK:(h + 1) * CHUNK] for h in range(GROUP_HEADS)))
            outs[bi].append(jnp.concatenate(o_groups, axis=1))

    for i, state in enumerate(states):
        st_ref[i] = state
    for bi in range(nb):
        o = outs[bi][0] if n_chunks == 1 else jnp.concatenate(outs[bi], axis=0)
        zz = zz_ref[bi].astype(F32)
        o_ref[bi] = (o * ag_ref[...] * (zz * jax.nn.sigmoid(zz))).astype(BF16)


def _gdn(z3, zs3, conv_w, a_log, dt_bias, a_norm_g, t=2 * CHUNK, nb=2):
    b, s, _ = z3.shape
    assert b % nb == 0 and s % t == 0 and t % CHUNK == 0 and t >= 8
    consts = _gdn_constants()
    av = jnp.zeros((2, 128), F32)
    av = av.at[0, S_ALPHA:S_ALPHA + A_HEADS].set(a_log.astype(F32))
    av = av.at[1, S_ALPHA:S_ALPHA + A_HEADS].set(dt_bias.astype(F32))
    ag = jnp.tile(a_norm_g.astype(F32), A_HEADS)[None, :]
    conv_cols = 2 * A_HEADS * A_DK + A_HEADS * A_DV

    def const_spec(a):
        nd = a.ndim
        return pl.BlockSpec(a.shape, lambda bi, ci, _n=nd: (0,) * _n)

    small_in = (conv_w.astype(F32), av, ag) + consts
    return pl.pallas_call(
        _gdn_kernel,
        grid=(b // nb, s // t),
        in_specs=[
            pl.BlockSpec((nb, t, conv_cols), lambda bi, ci: (bi, ci, O_AQKV // conv_cols)),
            pl.BlockSpec((nb, t, 512), lambda bi, ci: (bi, ci, O_AZ // 512)),
            pl.BlockSpec((nb, t, 128), lambda bi, ci: (bi, ci, 0)),
        ] + [const_spec(a) for a in small_in],
        out_specs=pl.BlockSpec((nb, t, A_HEADS * A_DV), lambda bi, ci: (bi, ci, 0)),
        out_shape=jax.ShapeDtypeStruct((b, s, A_HEADS * A_DV), BF16),
        scratch_shapes=[pltpu.VMEM((nb, 8 + t, conv_cols), F32),
                        pltpu.VMEM((nb * (A_HEADS // GROUP_HEADS), BD, BD), F32)],
        compiler_params=pltpu.CompilerParams(dimension_semantics=("parallel", "arbitrary"),
                                             vmem_limit_bytes=VMEM_LIMIT),
        name="gdn",
    )(z3, z3, zs3, *small_in)


def _dsa_kernel(topk, tk, q_ref, iq_ref, ka_ref, kb_ref, vd_ref, ik_ref, zs_ref, o_ref, st_ref, vt_ref):
    qb = pl.program_id(1)
    s = ka_ref.shape[0]
    tq = q_ref.shape[0]
    nkt_max = s // tk
    r0 = qb * tq
    n_tiles = (r0 + tq + tk - 1) // tk
    heads_per_kv = B_HEADS // B_KV_HEADS

    @pl.when(qb == 0)
    def _():
        for t in range(nkt_max):
            for g in range(B_KV_HEADS):
                v_tile = vd_ref[t * tk:(t + 1) * tk, g * 128:(g + 1) * 128]
                vt_ref[t, g] = v_tile.astype(F32).T.astype(BF16)

    lane128 = lax.broadcasted_iota(jnp.int32, (tq, 128), 1)
    low_half = lane128 < B_HD
    high_half = lane128 >= B_HD

    def head_slab(x, h):
        slab = x[:, (h // 2) * 128:(h // 2 + 1) * 128]
        return jnp.where(low_half if h % 2 == 0 else high_half, slab, jnp.zeros_like(slab))

    zs_t = zs_ref[...].T
    iw_scale = (IDX_HEADS ** -0.5) * (IDX_HD ** -0.5)
    wrows = [zs_t[S_IW + h:S_IW + h + 1, :] * iw_scale for h in range(IDX_HEADS)]
    qrow = r0 + lax.broadcasted_iota(jnp.int32, (1, tq), 1)
    limit = ((qrow >> 6) + 1) << 6
    small = limit <= topk

    iq = iq_ref[...]
    iq_heads = [head_slab(iq, h) for h in range(IDX_HEADS)]

    def index_tile(kt, carry):
        r_max, r_min = carry
        off = pl.multiple_of(kt * tk, tk)
        ikt = ik_ref[pl.ds(off, tk), :]
        acc = jnp.zeros((tk, tq), F32)
        for h in range(IDX_HEADS):
            sc = lax.dot_general(ikt, iq_heads[h], (((1,), (1,)), ((), ())), preferred_element_type=F32)
            acc = acc + wrows[h] * jnp.maximum(sc, 0.0)
        valid = off + lax.broadcasted_iota(jnp.int32, (tk, tq), 0) < limit
        masked = jnp.where(valid, acc, -jnp.inf)
        st_ref[kt] = masked
        r_max = jnp.maximum(r_max, jnp.max(masked, axis=0, keepdims=True))
        r_min = jnp.minimum(r_min, jnp.min(acc, axis=0, keepdims=True))
        return r_max, r_min

    r_max, r_min = lax.fori_loop(0, n_tiles, index_tile,
                                 (jnp.full((1, tq), -jnp.inf, F32), jnp.full((1, tq), jnp.inf, F32)))

    all_kept = r0 + tq <= min(topk, tk)

    @pl.when(all_kept)
    def _():
        st_ref[0] = jnp.where(st_ref[0] > -jnp.inf, 0.0, NEG_BIG)

    for j in range(nkt_max):
        @pl.when(jnp.logical_and(n_tiles == j + 1, jnp.logical_not(all_kept)))
        def _(j=j):
            _topk_mask(topk, j + 1, small, r_max, r_min, st_ref)

    q = q_ref[...]
    nr = heads_per_kv * tq
    qcols = [jnp.concatenate([head_slab(q, g * heads_per_kv + j) for j in range(heads_per_kv)], axis=0)
             for g in range(B_KV_HEADS)]
    k_refs = (ka_ref, kb_ref)

    def attend_tile(kt, carry):
        off = pl.multiple_of(kt * tk, tk)
        bias = st_ref[kt]
        bias4 = jnp.concatenate([bias] * heads_per_kv, axis=1)
        groups = range(B_KV_HEADS)
        logits = [lax.dot_general(k_refs[g][pl.ds(off, tk), :], qcols[g], (((1,), (1,)), ((), ())),
                                  preferred_element_type=F32) + bias4 for g in groups]
        m_new = [jnp.maximum(carry[g][0], jnp.max(logits[g], axis=0, keepdims=True)) for g in groups]
        p = [jnp.exp2(logits[g] - m_new[g]) for g in groups]
        alpha = [jnp.exp2(carry[g][0] - m_new[g]) for g in groups]
        l_new = [alpha[g] * carry[g][1] + jnp.sum(p[g], axis=0, keepdims=True) for g in groups]
        pv = [_dot(vt_ref[kt, g], p[g].astype(BF16)) for g in groups]
        return tuple((m_new[g], l_new[g], alpha[g] * carry[g][2] + pv[g]) for g in groups)

    init = tuple((jnp.full((1, nr), NEG_BIG, F32), jnp.zeros((1, nr), F32), jnp.zeros((128, nr), F32))
                 for _ in range(B_KV_HEADS))
    fin = lax.fori_loop(0, n_tiles, attend_tile, init)
    for g in range(B_KV_HEADS):
        _, l_run, acc = fin[g]
        og = acc / l_run
        heads = [og[:, j * tq:(j + 1) * tq].T for j in range(heads_per_kv)]
        for pp in range(heads_per_kv // 2):
            col = (g * (heads_per_kv // 2) + pp) * 128
            o_ref[:, col:col + 128] = jnp.where(low_half, heads[2 * pp], heads[2 * pp + 1]).astype(BF16)


def _topk_mask(topk, nt, small, r_max, r_min, st_ref):
    _, tk, tq = st_ref.shape
    kf = float(topk)

    def tiles():
        return [st_ref[t] for t in range(nt)]

    def q_sum(pred):
        tot = None
        for t, x in enumerate(tiles()):
            c = jnp.sum(pred(x, t).astype(F32), axis=0, keepdims=True)
            tot = c if tot is None else tot + c
        return tot

    def q_max(val):
        best = None
        for t, x in enumerate(tiles()):
            c = jnp.max(val(x, t), axis=0, keepdims=True)
            best = c if best is None else jnp.maximum(best, c)
        return best

    hi0 = r_max + jnp.maximum(jnp.abs(r_max), 1e-30) * 1e-6

    def bisect(_, carry):
        lo, hi = carry
        mid = 0.5 * (lo + hi)
        ge = q_sum(lambda x, t: x >= mid) >= kf
        return jnp.where(ge, mid, lo), jnp.where(ge, hi, mid)

    lo, hi = lax.fori_loop(0, BISECT_STEPS, bisect, (r_min, hi0))

    def peel_cond(carry):
        return jnp.sum(1.0 - carry[0]) > 0.0

    def peel(carry):
        done, thr, hi_c, n_ge = carry
        v1 = q_max(lambda x, t: jnp.where(x < hi_c, x, -jnp.inf))
        c1 = q_sum(lambda x, t: x >= v1)
        ok = c1 >= kf
        act = done < 0.5
        thr = jnp.where(act & ok, v1, thr)
        n_ge = jnp.where(act & ok, c1, n_ge)
        hi_c = jnp.where(act & (~ok), v1, hi_c)
        return jnp.where(ok, 1.0, done), thr, hi_c, n_ge

    _, thr, _, n_ge = lax.while_loop(peel_cond, peel, (jnp.where(small, 1.0, 0.0), lo, hi, jnp.full_like(lo, kf)))

    def key_index(t):
        return (t * tk + lax.broadcasted_iota(jnp.int32, (tk, tq), 0)).astype(F32)

    contested = jnp.sum(jnp.where((n_ge > kf) & (~small), 1.0, 0.0)) > 0.0
    last = float(nt * tk - 1)

    def tie_cut():
        need = kf - q_sum(lambda x, t: x > thr)

        def tie_search(_, carry):
            jlo, jhi = carry
            mid = jnp.floor(0.5 * (jlo + jhi))
            ge = q_sum(lambda x, t: (x == thr) & (key_index(t) <= mid)) >= need
            return jnp.where(ge, jlo, mid), jnp.where(ge, mid, jhi)

        n_iter = int(np.ceil(np.log2(nt * tk))) + 1
        return lax.fori_loop(0, n_iter, tie_search,
                             (jnp.full((1, tq), -1.0, F32), jnp.full((1, tq), last, F32)))[1]

    jcut = lax.cond(contested, tie_cut, lambda: jnp.full((1, tq), last, F32))
    for t, x in enumerate(tiles()):
        sel = (x > -jnp.inf) & (small | (x > thr) | ((x == thr) & (key_index(t) <= jcut)))
        st_ref[t] = jnp.where(sel, 0.0, NEG_BIG)


def _dsa(z3, zs3, tq=512, tk=512):
    b, s, _ = z3.shape
    assert s % tq == 0 and s % tk == 0 and tq % CHUNK == 0 and tk % CHUNK == 0
    topk = min(TOPK_MAX, s // 4)
    kernel = lambda *refs: _dsa_kernel(topk, tk, *refs)
    qblock = lambda w, cb: pl.BlockSpec((None, tq, w), lambda bi, qi, _c=cb: (bi, qi, _c))
    keys = lambda w, cb: pl.BlockSpec((None, s, w), lambda bi, qi, _c=cb: (bi, 0, _c))
    return pl.pallas_call(
        kernel,
        grid=(b, s // tq),
        in_specs=[qblock(512, O_BQ // 512), qblock(512, O_IQ // 512), keys(128, O_KD // 128),
                  keys(128, O_KD // 128 + 1), keys(256, O_VD // 256), keys(128, O_IK // 128), qblock(128, 0)],
        out_specs=pl.BlockSpec((None, tq, B_HEADS * B_HD), lambda bi, qi: (bi, qi, 0)),
        out_shape=jax.ShapeDtypeStruct((b, s, B_HEADS * B_HD), BF16),
        scratch_shapes=[pltpu.VMEM((s // tk, tk, tq), F32),
                        pltpu.VMEM((s // tk, B_KV_HEADS, 128, tk), BF16)],
        compiler_params=pltpu.CompilerParams(dimension_semantics=("parallel", "arbitrary"),
                                             vmem_limit_bytes=VMEM_LIMIT),
        name="dsa",
    )(z3, z3, z3, z3, z3, z3, zs3)


def _merge_kernel(x_ref, oa_ref, ob_ref, ga_ref, gb_ref, bg_ref, wa_ref, wb_ref, wo_ref, n2_ref, wr_ref,
                  br_ref, x1_ref, h2_ref, rl_ref):
    pa = _dot(oa_ref[...], wa_ref[...])
    pb = _dot(ob_ref[...], wb_ref[...])
    bgv = bg_ref[...]
    ga = jax.nn.sigmoid(ga_ref[...].astype(F32) + bgv[:, :D_MODEL])
    gb = jax.nn.sigmoid(gb_ref[...].astype(F32) + bgv[:, D_MODEL:])
    merged = ga * pa + gb * pb
    x1 = x_ref[...] + _mm(merged, wo_ref[...])
    x1_ref[...] = x1
    h2 = _rmsnorm_rows(x1, n2_ref[...])
    hb, hl = _split(h2)
    h2_ref[...] = hb
    s1 = _dot(hb, wr_ref[...])
    s2 = _dot(hl, wr_ref[:, :128])
    rl_ref[...] = s1[:, :128] + s1[:, 128:] + s2 + br_ref[...]


def _merge(x2d, oa, ob, z2d, b_gate, wa, wb, wo, n2, wr, br, tm=512):
    n = x2d.shape[0]
    assert n % tm == 0 and oa.shape == ob.shape == (n, 512)
    full =lambda shape: pl.BlockSpec(shape, lambda i: (0, 0))
    return pl.pallas_call(
        _merge_kernel,
        grid=(n // tm,),
        in_specs=[
            pl.BlockSpec((tm, D_MODEL), lambda i: (i, 0)),
            pl.BlockSpec((tm, 512), lambda i: (i, 0)),
            pl.BlockSpec((tm, 512), lambda i: (i, 0)),
            pl.BlockSpec((tm, D_MODEL), lambda i: (i, O_GA // D_MODEL)),
            pl.BlockSpec((tm, D_MODEL), lambda i: (i, O_GB // D_MODEL)),
            full((1, 2 * D_MODEL)), full((512, D_MODEL)), full((512, D_MODEL)), full((D_MODEL, D_MODEL)),
            full((1, D_MODEL)), full((D_MODEL, 256)), full((1, 128)),
        ],
        out_specs=[
            pl.BlockSpec((tm, D_MODEL), lambda i: (i, 0)),
            pl.BlockSpec((tm, D_MODEL), lambda i: (i, 0)),
            pl.BlockSpec((tm, 128), lambda i: (i, 0)),
        ],
        out_shape=[jax.ShapeDtypeStruct((n, D_MODEL), F32), jax.ShapeDtypeStruct((n, D_MODEL), BF16),
                   jax.ShapeDtypeStruct((n, 128), F32)],
        compiler_params=pltpu.CompilerParams(dimension_semantics=("parallel",),
                                             vmem_limit_bytes=VMEM_LIMIT),
        name="merge",
    )(x2d, oa, ob, z2d, z2d, b_gate, wa, wb, wo, n2, wr, br)


R_GROUP, R_EXPERT = 0, 8


def _routing_weights(rl):
    t = rl.T
    n_tok = t.shape[1]
    gl = t[R_GROUP:R_GROUP + N_GROUPS]
    gidx = lax.broadcasted_iota(jnp.int32, gl.shape, 0)
    gmax = jnp.max(gl, axis=0, keepdims=True)
    gsel = jnp.min(jnp.where(gl == gmax, gidx, N_GROUPS), axis=0, keepdims=True)
    ggate = 1.0 / jnp.sum(jnp.exp(gl - gmax), axis=0, keepdims=True)
    el = t[R_EXPERT:R_EXPERT + N_EXPERTS]
    eidx = lax.broadcasted_iota(jnp.int32, el.shape, 0)
    e_lo = gsel * EXPERTS_PER_GROUP
    emask = (eidx >= e_lo) & (eidx < e_lo + EXPERTS_PER_GROUP)
    el = jnp.where(emask, el, -jnp.inf)
    emax = jnp.max(el, axis=0, keepdims=True)
    ee = jnp.where(emask, jnp.exp(el - emax), 0.0)
    ep = jnp.where(emask, ee / jnp.sum(ee, axis=0, keepdims=True), -1.0)
    p1 = jnp.max(ep, axis=0, keepdims=True)
    i1 = jnp.min(jnp.where(ep == p1, eidx, N_EXPERTS), axis=0, keepdims=True)
    ep2 = jnp.where(eidx == i1, -1.0, ep)
    p2 = jnp.max(ep2, axis=0, keepdims=True)
    i2 = jnp.min(jnp.where(ep2 == p2, eidx, N_EXPERTS), axis=0, keepdims=True)
    tot = p1 + p2
    comb_t = (jnp.where(eidx == i1, ggate * (p1 / tot), 0.0)
              + jnp.where(eidx == i2, ggate * (p2 / tot), 0.0))
    full = jnp.concatenate([jnp.zeros((R_EXPERT, n_tok), F32), comb_t,
                            jnp.zeros((128 - R_EXPERT - N_EXPERTS, n_tok), F32)], axis=0)
    return full.T


def _moe_kernel(x1_ref, h2_ref, rl_ref, w1_ref, w3_ref, w2_ref, fg_ref, o_ref, y_ref, comb_ref):
    step = pl.program_id(1)
    per_step = w1_ref.shape[0]

    @pl.when(step == 0)
    def _():
        comb_ref[...] = _routing_weights(rl_ref[...])
        y_ref[...] = jnp.zeros(y_ref.shape, F32)

    h = h2_ref[...]
    comb = comb_ref[...]
    lane = lax.broadcasted_iota(jnp.int32, comb.shape, 1)
    acts = []
    for j in range(per_step):
        a = _dot(h, w1_ref[j])
        b = _dot(h, w3_ref[j])
        ce = jnp.sum(jnp.where(lane == R_EXPERT + step * per_step + j, comb, 0.0), axis=1, keepdims=True)
        acts.append(((a * jax.nn.sigmoid(a)) * b * ce).astype(BF16))
    y_ref[...] += _dot(jnp.concatenate(acts, axis=1), w2_ref[...])

    @pl.when(step == pl.num_programs(1) - 1)
    def _():
        o_ref[...] = _rmsnorm_rows(x1_ref[...] + y_ref[...], fg_ref[...])


def _moe(x1, h2, rl, w1, w3, w2, fg, tm=1024, per_step=4):
    n = x1.shape[0]
    assert n % tm == 0 and N_EXPERTS % per_step == 0 and w1.shape == (N_EXPERTS, D_MODEL, D_EXPERT)
    w2g =w2.reshape(N_EXPERTS // per_step, per_step * D_EXPERT, D_MODEL)
    return pl.pallas_call(
        _moe_kernel,
        grid=(n // tm, N_EXPERTS // per_step),
        in_specs=[
            pl.BlockSpec((tm, D_MODEL), lambda i, e: (i, 0)),
            pl.BlockSpec((tm, D_MODEL), lambda i, e: (i, 0)),
            pl.BlockSpec((tm, 128), lambda i, e: (i, 0)),
            pl.BlockSpec((per_step, D_MODEL, D_EXPERT), lambda i, e: (e, 0, 0)),
            pl.BlockSpec((per_step, D_MODEL, D_EXPERT), lambda i, e: (e, 0, 0)),
            pl.BlockSpec((None, per_step * D_EXPERT, D_MODEL), lambda i, e: (e, 0, 0)),
            pl.BlockSpec((1, D_MODEL), lambda i, e: (0, 0)),
        ],
        out_specs=pl.BlockSpec((tm, D_MODEL), lambda i, e: (i, 0)),
        out_shape=jax.ShapeDtypeStruct((n, D_MODEL), F32),
        scratch_shapes=[pltpu.VMEM((tm, D_MODEL), F32), pltpu.VMEM((tm, 128), F32)],
        compiler_params=pltpu.CompilerParams(dimension_semantics=("parallel", "arbitrary"),
                                             vmem_limit_bytes=VMEM_LIMIT),
        name="moe",
    )(x1, h2, rl, w1, w3, w2g, fg)


_W_OFF = {}
_off = 0
for _name, _n in (("a_q", 512), ("a_k", 512), ("a_v", 512), ("a_z", 512), ("a_beta", 8), ("a_alpha", 8),
                  ("b_q", 512), ("b_k", 128), ("b_v", 128), ("i_q", 512), ("i_k", 64), ("i_w", 8),
                  ("gate_a", 1024), ("gate_b", 1024)):
    _W_OFF[_name] = (_off, _off + _n)
    _off += _n


def _hi_lo_pair(w_small):
    pad = jnp.zeros((w_small.shape[0], 128 - w_small.shape[1]), F32)
    w = jnp.concatenate([w_small.astype(F32), pad], axis=1)
    hi, lo = _split(w)
    return jnp.concatenate([hi, lo], axis=1)


def _transpose_cast_kernel(off_ref, wt_ref, out_ref):
    del off_ref
    blk = wt_ref[...]
    reps = out_ref.shape[1] // blk.shape[0]
    if reps > 1:
        blk = jnp.concatenate([blk] * reps, axis=0)
    out_ref[...] = blk.T.astype(BF16)


def _gather_columns(wt, offsets, rows, out_cols):
    d = wt.shape[1]
    grid_spec = pltpu.PrefetchScalarGridSpec(
        num_scalar_prefetch=1,
        grid=(len(offsets),),
        in_specs=[pl.BlockSpec((pl.Element(rows), pl.Element(d)), lambda u, off: (pl.multiple_of(off[u], 8), 0))],
        out_specs=pl.BlockSpec((d, out_cols), lambda u, off: (0, u)),
    )
    return pl.pallas_call(
        _transpose_cast_kernel,
        grid_spec=grid_spec,
        out_shape=jax.ShapeDtypeStruct((d, out_cols * len(offsets)), BF16),
        compiler_params=pltpu.CompilerParams(dimension_semantics=("arbitrary",), vmem_limit_bytes=VMEM_LIMIT),
        name="w_layout",
    )(jnp.asarray(offsets, jnp.int32), wt)


def _layout_w_in(w):
    wt = jnp.swapaxes(w, 0, 1)
    start = lambda name: _W_OFF[name][0]
    assert all(start(nm) % 8 == 0 for nm in ("a_q", "gate_a", "b_q", "i_q", "b_k", "b_v", "i_k"))
    main_off = ([start("a_q") + 512 * i for i in range(4)] + [start("gate_a") + 512 * i for i in range(4)]
                + [start("b_q"), start("i_q")])
    tail_off = [start("b_k"), start("b_k") + 64, start("b_v"), start("b_v") + 64, start("i_k")]
    main = _gather_columns(wt, main_off, 512, 512)
    tail = _gather_columns(wt, tail_off, 64, 128)
    small_t = jnp.concatenate([wt[slice(*_W_OFF[nm])] for nm in ("a_beta", "a_alpha", "i_w")], axis=0)
    small = _hi_lo_pair(jnp.swapaxes(small_t, 0, 1))
    return main, tail, small


def kernel(x, positions, norm1_g, w_in, b_gate, conv_w, a_log, dt_bias, a_norm_g, w_proj_a, w_proj_b, w_out,
           norm2_g, w_router_group, b_router_group, w_router_expert, b_router_expert, w_exp_gate, w_exp_up,
           w_exp_down, final_norm_g):
    b, s, d = x.shape
    n = b * s
    assert w_in.shape[0] == 1 and d == D_MODEL, "one layer: the MoE kernel also applies the final norm"
    xc = x.reshape(n, d).astype(F32)
    for l in range(w_in.shape[0]):
        w_main, w_tail, w_small = _layout_w_in(w_in[l])
        z, zs = _in_proj(xc, norm1_g[l][None, :].astype(F32), w_main, w_tail, w_small, positions)
        z3 = z.reshape(b, s, Z_W)
        zs3 = zs.reshape(b, s, 128)
        o_a = _gdn(z3, zs3, conv_w[l], a_log[l], dt_bias[l], a_norm_g[l])
        o_b = _dsa(z3, zs3)
        gap = R_EXPERT - N_GROUPS
        wr = _hi_lo_pair(jnp.concatenate([w_router_group[l], jnp.zeros((d, gap), F32), w_router_expert[l]],
                                         axis=1))
        br = jnp.concatenate([b_router_group[l], jnp.zeros((gap,), F32), b_router_expert[l],
                              jnp.zeros((128 - R_EXPERT - N_EXPERTS,), F32)])[None, :].astype(F32)
        x1, h2, rl = _merge(xc, o_a.reshape(n, -1), o_b.reshape(n, -1), z, b_gate[l][None, :].astype(F32),
                            w_proj_a[l].astype(BF16), w_proj_b[l].astype(BF16), w_out[l].astype(BF16),
                            norm2_g[l][None, :].astype(F32), wr, br)
        xc = _moe(x1, h2, rl, w_exp_gate[l].astype(BF16), w_exp_up[l].astype(BF16),
                  w_exp_down[l].astype(BF16), final_norm_g[None, :].astype(F32))
    return xc.reshape(b, s, d).astype(x.dtype)
```

```python
import numpy as np
import jax
import jax.numpy as jnp
from jax import lax
from jax.experimental import pallas as pl
from jax.experimental.pallas import tpu as pltpu

F32 = jnp.float32
BF16 = jnp.bfloat16

D_MODEL = 1024
CHUNK = 64
EPS = 1e-6
ROPE_THETA = 10000.0
A_HEADS = 8
A_DK = 64
A_DV = 64
CONV_K = 4
B_HEADS = 8
B_KV_HEADS = 2
B_HD = 64
IDX_HEADS = 8
IDX_HD = 64
TOPK_MAX = 256
N_GROUPS = 4
EXPERTS_PER_GROUP = 4
N_EXPERTS = 16
D_EXPERT = 256

O_AQKV, O_AZ, O_GA, O_GB, O_BQ, O_IQ, O_KD, O_VD, O_IK = 0, 1536, 2048, 3072, 4096, 4608, 5120, 5376, 5632
Z_W = 5760
Z_MAIN = O_KD
S_BETA, S_ALPHA, S_IW = 0, 8, 16

GROUP_HEADS = 2
BD = GROUP_HEADS * CHUNK
NEG_BIG = -1e30
LOG2E = 1.4426950408889634
BISECT_STEPS = 18
VMEM_LIMIT = 56 * 1024 * 1024


def _split(x):
    hi = x.astype(BF16)
    lo = (x - hi.astype(F32)).astype(BF16)
    return hi, lo


def _dot(a, b):
    return jnp.dot(a, b, preferred_element_type=F32)


def _mm(a, b):
    return _dot(a.astype(BF16), b.astype(BF16))


def _mm_nt(a, b):
    return lax.dot_general(a.astype(BF16), b.astype(BF16), (((1,), (1,)), ((), ())),
                           preferred_element_type=F32)


def _mm_exact_lhs(a_bf16, x):
    xh, xl = _split(x)
    return _dot(a_bf16, xh) + _dot(a_bf16, xl)


def _rmsnorm_rows(x, g):
    return x * lax.rsqrt(jnp.mean(x * x, axis=-1, keepdims=True) + EPS) * g


_Z_CHUNKS = tuple((o, min(512, Z_W - o)) for o in range(0, Z_W, 512))


def _rope(x, cs, sn, first):
    w = x.shape[1]
    rep = w // 128
    if rep > 1:
        cs, sn, first = (jnp.concatenate([a] * rep, axis=1) for a in (cs, sn, first))
    swapped = jnp.where(first, pltpu.roll(x, w - B_HD // 2, 1), pltpu.roll(x, B_HD // 2, 1))
    return x * cs + swapped * sn


def _in_proj_kernel(x_ref, g_ref, w_ref, wt_ref, ws_ref, pos_ref, inv_ref, z_ref, zs_ref):
    h = _rmsnorm_rows(x_ref[...], g_ref[...])
    hb, hl = _split(h)

    pos = jnp.broadcast_to(pos_ref[...].astype(F32), (128, pos_ref.shape[1]))
    ang = jnp.transpose(pos) * inv_ref[...]
    lane = lax.broadcasted_iota(jnp.int32, ang.shape, 1)
    first = (lane & (B_HD - 1)) < (B_HD // 2)
    cs = jnp.cos(ang)
    sn = jnp.sin(ang)
    sn = jnp.where(first, -sn, sn)

    for o, w in _Z_CHUNKS:
        r = _dot(hb, w_ref[:, o:o + w] if o < Z_MAIN else wt_ref[:, o - Z_MAIN:o - Z_MAIN + w])
        if o == O_BQ:
            r = _rope(r, cs, sn, first) * (B_HD ** -0.5 * LOG2E)
        elif o == O_IQ or o == O_IK:
            r = _rope(r, cs, sn, first)
        elif o == O_KD:
            kw = O_VD - O_KD
            r = jnp.concatenate([_rope(r[:, :kw], cs, sn, first), r[:, kw:]], axis=1)
        z_ref[:, o:o + w] = r.astype(BF16)
    s1 = _dot(hb, ws_ref[...])
    s2 = _dot(hl, ws_ref[:, :128])
    zs_ref[...] = s1[:, :128] + s1[:, 128:] + s2


def _in_proj(x2d, g, w_main, w_tail, w_small, positions, tm=512):
    n = x2d.shape[0]
    assert n % tm == 0 and x2d.shape[1] == D_MODEL
    assert w_main.shape == (D_MODEL, Z_MAIN) and w_tail.shape == (D_MODEL, Z_W - Z_MAIN)
    half = B_HD // 2
    inv = ROPE_THETA ** (-jnp.arange(half, dtype=F32) / half)
    inv128 = jnp.tile(inv, 4)[None, :]
    pos = positions.astype(jnp.int32).reshape(1, n)
    return pl.pallas_call(
        _in_proj_kernel,
        grid=(n // tm,),
        in_specs=[
            pl.BlockSpec((tm, D_MODEL), lambda i: (i, 0)),
            pl.BlockSpec((1, D_MODEL), lambda i: (0, 0)),
            pl.BlockSpec((D_MODEL, Z_MAIN), lambda i: (0, 0)),
            pl.BlockSpec((D_MODEL, Z_W - Z_MAIN), lambda i: (0, 0)),
            pl.BlockSpec((D_MODEL, 256), lambda i: (0, 0)),
            pl.BlockSpec((1, tm), lambda i: (0, i)),
            pl.BlockSpec((1, 128), lambda i: (0, 0)),
        ],
        out_specs=[
            pl.BlockSpec((tm, Z_W), lambda i: (i, 0)),
            pl.BlockSpec((tm, 128), lambda i: (i, 0)),
        ],
        out_shape=[jax.ShapeDtypeStruct((n, Z_W), BF16), jax.ShapeDtypeStruct((n, 128), F32)],
        compiler_params=pltpu.CompilerParams(dimension_semantics=("parallel",),
                                             vmem_limit_bytes=VMEM_LIMIT),
        name="in_proj",
    )(x2d, g, w_main, w_tail, w_small, pos, inv128)


def _gdn_constants():
    r = np.arange(BD)
    same = (r[:, None] // CHUNK) == (r[None, :] // CHUNK)
    incl = same & (r[:, None] >= r[None, :])
    strict = same & (r[:, None] > r[None, :])
    eye = np.eye(BD, dtype=np.float32)
    ll = np.concatenate([incl, same], axis=0).astype(np.float32)
    n_groups = A_HEADS // GROUP_HEADS
    sel = np.zeros((2 * n_groups, BD, 128), np.float32)
    for gi in range(n_groups):
        for h in range(GROUP_HEADS):
            sel[gi * 2 + 0, h * CHUNK:(h + 1) * CHUNK, S_BETA + gi * GROUP_HEADS + h] = 1.0
            sel[gi * 2 + 1, h * CHUNK:(h + 1) * CHUNK, S_ALPHA + gi * GROUP_HEADS + h] = 1.0
    return (jnp.asarray(incl, F32), jnp.asarray(strict, F32), jnp.asarray(same, F32), jnp.asarray(eye),
            jnp.asarray(ll, BF16), jnp.asarray(sel))


def _tile_heads(x):
    return jnp.concatenate([x] * GROUP_HEADS, axis=0)


def _gdn_kernel(zq_ref, zz_ref, zs_ref, cw_ref, av_ref, ag_ref, incl_ref, strict_ref, bdm_ref, eye_ref,
                ll_ref, sel_ref, o_ref, ext_ref, st_ref):
    c = pl.program_id(1)
    nb, t = zq_ref.shape[0], zq_ref.shape[1]

    @pl.when(c == 0)
    def _():
        ext_ref[:, 0:8, :] = jnp.zeros((nb, 8, ext_ref.shape[2]), F32)
        st_ref[...] = jnp.zeros(st_ref.shape, F32)

    hw = A_HEADS * A_DK
    cw = cw_ref[...]
    av = av_ref[...]
    q_all, k_all, v_all, bg = [], [], [], []
    for bi in range(nb):
        ext_ref[bi, 8:8 + t, :] = zq_ref[bi].astype(F32)
        y = cw[0:1, :] * ext_ref[bi, pl.ds(8 - (CONV_K - 1), t), :]
        for j in range(1, CONV_K):
            y = y + cw[j:j + 1, :] * ext_ref[bi, pl.ds(8 - (CONV_K - 1) + j, t), :]
        ext_ref[bi, 0:8, :] = ext_ref[bi, t:t + 8, :]
        y = y * jax.nn.sigmoid(y)
        q_all.append(y[:, :hw])
        k_all.append(y[:, hw:2 * hw])
        v_all.append(y[:, 2 * hw:])

        sm = zs_ref[bi]
        lane = lax.broadcasted_iota(jnp.int32, sm.shape, 1)
        xg = sm + av[1:2, :]
        softplus = jnp.maximum(xg, 0.0) + jnp.log1p(jnp.exp(-jnp.abs(xg)))
        g_all = -jnp.exp(av[0:1, :]) * softplus
        bg.append(jnp.where(lane < S_ALPHA, jax.nn.sigmoid(sm), g_all))

    incl = incl_ref[...]
    strict = strict_ref[...]
    bdm = bdm_ref[...]
    eye = eye_ref[...]
    ll = ll_ref[...]
    incl_b = ll[:BD]

    n_chunks = t // CHUNK
    n_groups = A_HEADS // GROUP_HEADS
    chains = [(bi, ci, gi) for bi in range(nb) for ci in range(n_chunks) for gi in range(n_groups)]

    pre = {}
    for bi, ci, gi in chains:
        r0, c0 = ci * CHUNK, gi * BD
        bg4 = _tile_heads(bg[bi][r0:r0 + CHUNK])
        beta = jnp.sum(bg4 * sel_ref[gi * 2 + 0], axis=1, keepdims=True)
        gcol = jnp.sum(bg4 * sel_ref[gi * 2 + 1], axis=1, keepdims=True)
        gs = _mm_exact_lhs(ll, jnp.broadcast_to(gcol, (BD, 128)))
        g_cum = gs[:BD, :1]
        g_last = gs[BD:, :1]
        diff = _mm_exact_lhs(incl_b, gcol * strict)
        decay = jnp.where(incl > 0.0, jnp.exp(diff), 0.0)
        e_cum = jnp.exp(g_cum)
        kr = _tile_heads(k_all[bi][r0:r0 + CHUNK, c0:c0 + BD]) * bdm
        qr = _tile_heads(q_all[bi][r0:r0 + CHUNK, c0:c0 + BD]) * bdm
        vm = _tile_heads(v_all[bi][r0:r0 + CHUNK, c0:c0 + BD]) * bdm
        km = kr * lax.rsqrt(jnp.sum(kr * kr, axis=1, keepdims=True) + EPS)
        qm = qr * (lax.rsqrt(jnp.sum(qr * qr, axis=1, keepdims=True) + EPS) * (A_DK ** -0.5))
        kkqk = _mm_nt(jnp.concatenate([km, qm], axis=0), km)
        m = -(strict * beta * kkqk[:BD] * decay)
        pre[bi, ci, gi] = dict(beta=beta, g_cum=g_cum, g_last=g_last, e_cum=e_cum, km=km, qm=qm, vm=vm,
                               qk=kkqk[BD:] * decay, m=m, inv=eye + m)

    sq = CHUNK
    while sq > 2:
        for key in chains:
            p = pre[key]
            p["m"] = _mm(p["m"], p["m"])
        for key in chains:
            p = pre[key]
            p["inv"] = p["inv"] + _mm(p["inv"], p["m"])
        sq //= 2

    states = [st_ref[i] for i in range(nb * n_groups)]
    outs = [[] for _ in range(nb)]
    lanes = [(bi, gi) for bi in range(nb) for gi in range(n_groups)]
    for ci in range(n_chunks):
        ps = {k: pre[k[0], ci, k[1]] for k in lanes}
        sidx = {k: k[0] * n_groups + k[1] for k in lanes}
        kq_s = {k: _mm(jnp.concatenate([ps[k]["km"] * ps[k]["e_cum"], ps[k]["qm"] * ps[k]["e_cum"]], axis=0),
                       states[sidx[k]]) for k in lanes}
        v_new = {k: _mm(ps[k]["inv"], ps[k]["beta"] * (ps[k]["vm"] - kq_s[k][:BD])) for k in lanes}
        o_bd = {k: kq_s[k][BD:] + _mm(ps[k]["qk"], v_new[k]) for k in lanes}
        for k in lanes:
            p = ps[k]
            k_dec = p["km"] * jnp.exp(p["g_last"] - p["g_cum"])
            states[sidx[k]] = states[sidx[k]] * jnp.exp(p["g_last"]) + _mm(k_dec.T, v_new[k])
        for bi in range(nb):
            o_groups = []
            for gi in range(n_groups):
                ob = o_bd[bi, gi]
                ob = ob * lax.rsqrt(jnp.sum(ob * ob, axis=1, keepdims=True) * (1.0 / A_DV) + EPS)
                o_groups.append(sum(ob[h * CHUNK:(h + 1) * CHUNK] for h in range(GROUP_HEADS)))
            outs[bi].append(jnp.concatenate(o_groups, axis=1))

    for i, state in enumerate(states):
        st_ref[i] = state
    for bi in range(nb):
        o = outs[bi][0] if n_chunks == 1 else jnp.concatenate(outs[bi], axis=0)
        zz = zz_ref[bi].astype(F32)
        o_ref[bi] = (o * ag_ref[...] * (zz * jax.nn.sigmoid(zz))).astype(BF16)


def _gdn(z3, zs3, conv_w, a_log, dt_bias, a_norm_g, t=2 * CHUNK, nb=2):
    b, s, _ = z3.shape
    assert b % nb == 0 and s % t == 0 and t % CHUNK == 0 and t >= 8
    consts = _gdn_constants()
    av = jnp.zeros((2, 128), F32)
    av = av.at[0, S_ALPHA:S_ALPHA + A_HEADS].set(a_log.astype(F32))
    av = av.at[1, S_ALPHA:S_ALPHA + A_HEADS].set(dt_bias.astype(F32))
    ag = jnp.tile(a_norm_g.astype(F32), A_HEADS)[None, :]
    conv_cols = 2 * A_HEADS * A_DK + A_HEADS * A_DV

    def const_spec(a):
        nd = a.ndim
        return pl.BlockSpec(a.shape, lambda bi, ci, _n=nd: (0,) * _n)

    small_in = (conv_w.astype(F32), av, ag) + consts
    return pl.pallas_call(
        _gdn_kernel,
        grid=(b // nb, s // t),
        in_specs=[
            pl.BlockSpec((nb, t, conv_cols), lambda bi, ci: (bi, ci, O_AQKV // conv_cols)),
            pl.BlockSpec((nb, t, 512), lambda bi, ci: (bi, ci, O_AZ // 512)),
            pl.BlockSpec((nb, t, 128), lambda bi, ci: (bi, ci, 0)),
        ] + [const_spec(a) for a in small_in],
        out_specs=pl.BlockSpec((nb, t, A_HEADS * A_DV), lambda bi, ci: (bi, ci, 0)),
        out_shape=jax.ShapeDtypeStruct((b, s, A_HEADS * A_DV), BF16),
        scratch_shapes=[pltpu.VMEM((nb, 8 + t, conv_cols), F32),
                        pltpu.VMEM((nb * (A_HEADS // GROUP_HEADS), BD, BD), F32)],
        compiler_params=pltpu.CompilerParams(dimension_semantics=("parallel", "arbitrary"),
                                             vmem_limit_bytes=VMEM_LIMIT),
        name="gdn",
    )(z3, z3, zs3, *small_in)


def _dsa_kernel(topk, tk, q_ref, iq_ref, ka_ref, kb_ref, vd_ref, ik_ref, zs_ref, o_ref, st_ref, vt_ref):
    qb = pl.program_id(1)
    s = ka_ref.shape[0]
    tq = q_ref.shape[0]
    nkt_max = s // tk
    r0 = qb * tq
    n_tiles = (r0 + tq + tk - 1) // tk
    heads_per_kv = B_HEADS // B_KV_HEADS

    @pl.when(qb == 0)
    def _():
        for t in range(nkt_max):
            for g in range(B_KV_HEADS):
                v_tile = vd_ref[t * tk:(t + 1) * tk, g * 128:(g + 1) * 128]
                vt_ref[t, g] = v_tile.astype(F32).T.astype(BF16)

    lane128 = lax.broadcasted_iota(jnp.int32, (tq, 128), 1)
    low_half = lane128 < B_HD
    high_half = lane128 >= B_HD

    def head_slab(x, h):
        slab = x[:, (h // 2) * 128:(h // 2 + 1) * 128]
        return jnp.where(low_half if h % 2 == 0 else high_half, slab, jnp.zeros_like(slab))

    zs_t = zs_ref[...].T
    iw_scale = (IDX_HEADS ** -0.5) * (IDX_HD ** -0.5)
    wrows = [zs_t[S_IW + h:S_IW + h + 1, :] * iw_scale for h in range(IDX_HEADS)]
    qrow = r0 + lax.broadcasted_iota(jnp.int32, (1, tq), 1)
    limit = ((qrow >> 6) + 1) << 6
    small = limit <= topk

    iq = iq_ref[...]
    iq_heads = [head_slab(iq, h) for h in range(IDX_HEADS)]

    def index_tile(kt, carry):
        r_max, r_min = carry
        off = pl.multiple_of(kt * tk, tk)
        ikt = ik_ref[pl.ds(off, tk), :]
        acc = jnp.zeros((tk, tq), F32)
        for h in range(IDX_HEADS):
            sc = lax.dot_general(ikt, iq_heads[h], (((1,), (1,)), ((), ())), preferred_element_type=F32)
            acc = acc + wrows[h] * jnp.maximum(sc, 0.0)
        valid = off + lax.broadcasted_iota(jnp.int32, (tk, tq), 0) < limit
        masked = jnp.where(valid, acc, -jnp.inf)
        st_ref[kt] = masked
        r_max = jnp.maximum(r_max, jnp.max(masked, axis=0, keepdims=True))
        r_min = jnp.minimum(r_min, jnp.min(acc, axis=0, keepdims=True))
        return r_max, r_min

    r_max, r_min = lax.fori_loop(0, n_tiles, index_tile,
                                 (jnp.full((1, tq), -jnp.inf, F32), jnp.full((1, tq), jnp.inf, F32)))

    all_kept = r0 + tq <= min(topk, tk)

    @pl.when(all_kept)
    def _():
        st_ref[0] = jnp.where(st_ref[0] > -jnp.inf, 0.0, NEG_BIG)

    for j in range(nkt_max):
        @pl.when(jnp.logical_and(n_tiles == j + 1, jnp.logical_not(all_kept)))
        def _(j=j):
            _topk_mask(topk, j + 1, small, r_max, r_min, st_ref)

    q = q_ref[...]
    nr = heads_per_kv * tq
    qcols = [jnp.concatenate([head_slab(q, g * heads_per_kv + j) for j in range(heads_per_kv)], axis=0)
             for g in range(B_KV_HEADS)]
    k_refs = (ka_ref, kb_ref)

    def attend_tile(kt, carry):
        off = pl.multiple_of(kt * tk, tk)
        bias = st_ref[kt]
        bias4 = jnp.concatenate([bias] * heads_per_kv, axis=1)
        groups = range(B_KV_HEADS)
        logits = [lax.dot_general(k_refs[g][pl.ds(off, tk), :], qcols[g], (((1,), (1,)), ((), ())),
                                  preferred_element_type=F32) + bias4 for g in groups]
        m_new = [jnp.maximum(carry[g][0], jnp.max(logits[g], axis=0, keepdims=True)) for g in groups]
        p = [jnp.exp2(logits[g] - m_new[g]) for g in groups]
        alpha = [jnp.exp2(carry[g][0] - m_new[g]) for g in groups]
        l_new = [alpha[g] * carry[g][1] + jnp.sum(p[g], axis=0, keepdims=True) for g in groups]
        pv = [_dot(vt_ref[kt, g], p[g].astype(BF16)) for g in groups]
        return tuple((m_new[g], l_new[g], alpha[g] * carry[g][2] + pv[g]) for g in groups)

    init = tuple((jnp.full((1, nr), NEG_BIG, F32), jnp.zeros((1, nr), F32), jnp.zeros((128, nr), F32))
                 for _ in range(B_KV_HEADS))
    fin = lax.fori_loop(0, n_tiles, attend_tile, init)
    for g in range(B_KV_HEADS):
        _, l_run, acc = fin[g]
        og = acc / l_run
        heads = [og[:, j * tq:(j + 1) * tq].T for j in range(heads_per_kv)]
        for pp in range(heads_per_kv // 2):
            col = (g * (heads_per_kv // 2) + pp) * 128
            o_ref[:, col:col + 128] = jnp.where(low_half, heads[2 * pp], heads[2 * pp + 1]).astype(BF16)


def _topk_mask(topk, nt, small, r_max, r_min, st_ref):
    _, tk, tq = st_ref.shape
    kf = float(topk)

    def tiles():
        return [st_ref[t] for t in range(nt)]

    def q_sum(pred):
        tot = None
        for t, x in enumerate(tiles()):
            c = jnp.sum(pred(x, t).astype(F32), axis=0, keepdims=True)
            tot = c if tot is None else tot + c
        return tot

    def q_max(val):
        best = None
        for t, x in enumerate(tiles()):
            c = jnp.max(val(x, t), axis=0, keepdims=True)
            best = c if best is None else jnp.maximum(best, c)
        return best

    hi0 = r_max + jnp.maximum(jnp.abs(r_max), 1e-30) * 1e-6

    def bisect(_, carry):
        lo, hi = carry
        mid = 0.5 * (lo + hi)
        ge = q_sum(lambda x, t: x >= mid) >= kf
        return jnp.where(ge, mid, lo), jnp.where(ge, hi, mid)

    lo, hi = lax.fori_loop(0, BISECT_STEPS, bisect, (r_min, hi0))

    def peel_cond(carry):
        return jnp.sum(1.0 - carry[0]) > 0.0

    def peel(carry):
        done, thr, hi_c, n_ge = carry
        v1 = q_max(lambda x, t: jnp.where(x < hi_c, x, -jnp.inf))
        c1 = q_sum(lambda x, t: x >= v1)
        ok = c1 >= kf
        act = done < 0.5
        thr = jnp.where(act & ok, v1, thr)
        n_ge = jnp.where(act & ok, c1, n_ge)
        hi_c = jnp.where(act & (~ok), v1, hi_c)
        return jnp.where(ok, 1.0, done), thr, hi_c, n_ge

    _, thr, _, n_ge = lax.while_loop(peel_cond, peel, (jnp.where(small, 1.0, 0.0), lo, hi, jnp.full_like(lo, kf)))

    def key_index(t):
        return (t * tk + lax.broadcasted_iota(jnp.int32, (tk, tq), 0)).astype(F32)

    contested = jnp.sum(jnp.where((n_ge > kf) & (~small), 1.0, 0.0)) > 0.0
    last = float(nt * tk - 1)

    def tie_cut():
        need = kf - q_sum(lambda x, t: x > thr)

        def tie_search(_, carry):
            jlo, jhi = carry
            mid = jnp.floor(0.5 * (jlo + jhi))
            ge = q_sum(lambda x, t: (x == thr) & (key_index(t) <= mid)) >= need
            return jnp.where(ge, jlo, mid), jnp.where(ge, mid, jhi)

        n_iter = int(np.ceil(np.log2(nt * tk))) + 1
        return lax.fori_loop(0, n_iter, tie_search,
                             (jnp.full((1, tq), -1.0, F32), jnp.full((1, tq), last, F32)))[1]

    jcut = lax.cond(contested, tie_cut, lambda: jnp.full((1, tq), last, F32))
    for t, x in enumerate(tiles()):
        sel = (x > -jnp.inf) & (small | (x > thr) | ((x == thr) & (key_index(t) <= jcut)))
        st_ref[t] = jnp.where(sel, 0.0, NEG_BIG)


def _dsa(z3, zs3, tq=512, tk=512):
    b, s, _ = z3.shape
    assert s % tq == 0 and s % tk == 0 and tq % CHUNK == 0 and tk % CHUNK == 0
    topk = min(TOPK_MAX, s // 4)
    kernel = lambda *refs: _dsa_kernel(topk, tk, *refs)
    qblock = lambda w, cb: pl.BlockSpec((None, tq, w), lambda bi, qi, _c=cb: (bi, qi, _c))
    keys = lambda w, cb: pl.BlockSpec((None, s, w), lambda bi, qi, _c=cb: (bi, 0, _c))
    return pl.pallas_call(
        kernel,
        grid=(b, s // tq),
        in_specs=[qblock(512, O_BQ // 512), qblock(512, O_IQ // 512), keys(128, O_KD // 128),
                  keys(128, O_KD // 128 + 1), keys(256, O_VD // 256), keys(128, O_IK // 128), qblock(128, 0)],
        out_specs=pl.BlockSpec((None, tq, B_HEADS * B_HD), lambda bi, qi: (bi, qi, 0)),
        out_shape=jax.ShapeDtypeStruct((b, s, B_HEADS * B_HD), BF16),
        scratch_shapes=[pltpu.VMEM((s // tk, tk, tq), F32),
                        pltpu.VMEM((s // tk, B_KV_HEADS, 128, tk), BF16)],
        compiler_params=pltpu.CompilerParams(dimension_semantics=("parallel", "arbitrary"),
                                             vmem_limit_bytes=VMEM_LIMIT),
        name="dsa",
    )(z3, z3, z3, z3, z3, z3, zs3)


def _merge_kernel(x_ref, oa_ref, ob_ref, ga_ref, gb_ref, bg_ref, wa_ref, wb_ref, wo_ref, n2_ref, wr_ref,
                  br_ref, x1_ref, h2_ref, rl_ref):
    pa = _dot(oa_ref[...], wa_ref[...])
    pb = _dot(ob_ref[...], wb_ref[...])
    bgv = bg_ref[...]
    ga = jax.nn.sigmoid(ga_ref[...].astype(F32) + bgv[:, :D_MODEL])
    gb = jax.nn.sigmoid(gb_ref[...].astype(F32) + bgv[:, D_MODEL:])
    merged = ga * pa + gb * pb
    x1 = x_ref[...] + _mm(merged, wo_ref[...])
    x1_ref[...] = x1
    h2 = _rmsnorm_rows(x1, n2_ref[...])
    hb, hl = _split(h2)
    h2_ref[...] = hb
    s1 = _dot(hb, wr_ref[...])
    s2 = _dot(hl, wr_ref[:, :128])
    rl_ref[...] = s1[:, :128] + s1[:, 128:] + s2 + br_ref[...]


def _merge(x2d, oa, ob, z2d, b_gate, wa, wb, wo, n2, wr, br, tm=512):
    n = x2d.shape[0]
    assert n % tm == 0 and oa.shape == ob.shape == (n, 512)
    full =lambda shape: pl.BlockSpec(shape, lambda i: (0, 0))
    return pl.pallas_call(
        _merge_kernel,
        grid=(n // tm,),
        in_specs=[
            pl.BlockSpec((tm, D_MODEL), lambda i: (i, 0)),
            pl.BlockSpec((tm, 512), lambda i: (i, 0)),
            pl.BlockSpec((tm, 512), lambda i: (i, 0)),
            pl.BlockSpec((tm, D_MODEL), lambda i: (i, O_GA // D_MODEL)),
            pl.BlockSpec((tm, D_MODEL), lambda i: (i, O_GB // D_MODEL)),
            full((1, 2 * D_MODEL)), full((512, D_MODEL)), full((512, D_MODEL)), full((D_MODEL, D_MODEL)),
            full((1, D_MODEL)), full((D_MODEL, 256)), full((1, 128)),
        ],
        out_specs=[
            pl.BlockSpec((tm, D_MODEL), lambda i: (i, 0)),
            pl.BlockSpec((tm, D_MODEL), lambda i: (i, 0)),
            pl.BlockSpec((tm, 128), lambda i: (i, 0)),
        ],
        out_shape=[jax.ShapeDtypeStruct((n, D_MODEL), F32), jax.ShapeDtypeStruct((n, D_MODEL), BF16),
                   jax.ShapeDtypeStruct((n, 128), F32)],
        compiler_params=pltpu.CompilerParams(dimension_semantics=("parallel",),
                                             vmem_limit_bytes=VMEM_LIMIT),
        name="merge",
    )(x2d, oa, ob, z2d, z2d, b_gate, wa, wb, wo, n2, wr, br)


R_GROUP, R_EXPERT = 0, 8


def _routing_weights(rl):
    t = rl.T
    n_tok = t.shape[1]
    gl = t[R_GROUP:R_GROUP + N_GROUPS]
    gidx = lax.broadcasted_iota(jnp.int32, gl.shape, 0)
    gmax = jnp.max(gl, axis=0, keepdims=True)
    gsel = jnp.min(jnp.where(gl == gmax, gidx, N_GROUPS), axis=0, keepdims=True)
    ggate = 1.0 / jnp.sum(jnp.exp(gl - gmax), axis=0, keepdims=True)
    el = t[R_EXPERT:R_EXPERT + N_EXPERTS]
    eidx = lax.broadcasted_iota(jnp.int32, el.shape, 0)
    e_lo = gsel * EXPERTS_PER_GROUP
    emask = (eidx >= e_lo) & (eidx < e_lo + EXPERTS_PER_GROUP)
    el = jnp.where(emask, el, -jnp.inf)
    emax = jnp.max(el, axis=0, keepdims=True)
    ee = jnp.where(emask, jnp.exp(el - emax), 0.0)
    ep = jnp.where(emask, ee / jnp.sum(ee, axis=0, keepdims=True), -1.0)
    p1 = jnp.max(ep, axis=0, keepdims=True)
    i1 = jnp.min(jnp.where(ep == p1, eidx, N_EXPERTS), axis=0, keepdims=True)
    ep2 = jnp.where(eidx == i1, -1.0, ep)
    p2 = jnp.max(ep2, axis=0, keepdims=True)
    i2 = jnp.min(jnp.where(ep2 == p2, eidx, N_EXPERTS), axis=0, keepdims=True)
    tot = p1 + p2
    comb_t = (jnp.where(eidx == i1, ggate * (p1 / tot), 0.0)
              + jnp.where(eidx == i2, ggate * (p2 / tot), 0.0))
    full = jnp.concatenate([jnp.zeros((R_EXPERT, n_tok), F32), comb_t,
                            jnp.zeros((128 - R_EXPERT - N_EXPERTS, n_tok), F32)], axis=0)
    return full.T


def _moe_kernel(x1_ref, h2_ref, rl_ref, w1_ref, w3_ref, w2_ref, fg_ref, o_ref, y_ref, comb_ref):
    step = pl.program_id(1)
    per_step = w1_ref.shape[0]

    @pl.when(step == 0)
    def _():
        comb_ref[...] = _routing_weights(rl_ref[...])
        y_ref[...] = jnp.zeros(y_ref.shape, F32)

    h = h2_ref[...]
    comb = comb_ref[...]
    lane = lax.broadcasted_iota(jnp.int32, comb.shape, 1)
    acts = []
    for j in range(per_step):
        a = _dot(h, w1_ref[j])
        b = _dot(h, w3_ref[j])
        ce = jnp.sum(jnp.where(lane == R_EXPERT + step * per_step + j, comb, 0.0), axis=1, keepdims=True)
        acts.append(((a * jax.nn.sigmoid(a)) * b * ce).astype(BF16))
    y_ref[...] += _dot(jnp.concatenate(acts, axis=1), w2_ref[...])

    @pl.when(step == pl.num_programs(1) - 1)
    def _():
        o_ref[...] = _rmsnorm_rows(x1_ref[...] + y_ref[...], fg_ref[...])


def _moe(x1, h2, rl, w1, w3, w2, fg, tm=1024, per_step=4):
    n = x1.shape[0]
    assert n % tm == 0 and N_EXPERTS % per_step == 0 and w1.shape == (N_EXPERTS, D_MODEL, D_EXPERT)
    w2g =w2.reshape(N_EXPERTS // per_step, per_step * D_EXPERT, D_MODEL)
    return pl.pallas_call(
        _moe_kernel,
        grid=(n // tm, N_EXPERTS // per_step),
        in_specs=[
            pl.BlockSpec((tm, D_MODEL), lambda i, e: (i, 0)),
            pl.BlockSpec((tm, D_MODEL), lambda i, e: (i, 0)),
            pl.BlockSpec((tm, 128), lambda i, e: (i, 0)),
            pl.BlockSpec((per_step, D_MODEL, D_EXPERT), lambda i, e: (e, 0, 0)),
            pl.BlockSpec((per_step, D_MODEL, D_EXPERT), lambda i, e: (e, 0, 0)),
            pl.BlockSpec((None, per_step * D_EXPERT, D_MODEL), lambda i, e: (e, 0, 0)),
            pl.BlockSpec((1, D_MODEL), lambda i, e: (0, 0)),
        ],
        out_specs=pl.BlockSpec((tm, D_MODEL), lambda i, e: (i, 0)),
        out_shape=jax.ShapeDtypeStruct((n, D_MODEL), F32),
        scratch_shapes=[pltpu.VMEM((tm, D_MODEL), F32), pltpu.VMEM((tm, 128), F32)],
        compiler_params=pltpu.CompilerParams(dimension_semantics=("parallel", "arbitrary"),
                                             vmem_limit_bytes=VMEM_LIMIT),
        name="moe",
    )(x1, h2, rl, w1, w3, w2g, fg)


_W_OFF = {}
_off = 0
for _name, _n in (("a_q", 512), ("a_k", 512), ("a_v", 512), ("a_z", 512), ("a_beta", 8), ("a_alpha", 8),
                  ("b_q", 512), ("b_k", 128), ("b_v", 128), ("i_q", 512), ("i_k", 64), ("i_w", 8),
                  ("gate_a", 1024), ("gate_b", 1024)):
    _W_OFF[_name] = (_off, _off + _n)
    _off += _n


def _hi_lo_pair(w_small):
    pad = jnp.zeros((w_small.shape[0], 128 - w_small.shape[1]), F32)
    w = jnp.concatenate([w_small.astype(F32), pad], axis=1)
    hi, lo = _split(w)
    return jnp.concatenate([hi, lo], axis=1)


def _transpose_cast_kernel(off_ref, wt_ref, out_ref):
    del off_ref
    blk = wt_ref[...]
    reps = out_ref.shape[1] // blk.shape[0]
    if reps > 1:
        blk = jnp.concatenate([blk] * reps, axis=0)
    out_ref[...] = blk.T.astype(BF16)


def _gather_columns(wt, offsets, rows, out_cols):
    d = wt.shape[1]
    grid_spec = pltpu.PrefetchScalarGridSpec(
        num_scalar_prefetch=1,
        grid=(len(offsets),),
        in_specs=[pl.BlockSpec((pl.Element(rows), pl.Element(d)), lambda u, off: (pl.multiple_of(off[u], 8), 0))],
        out_specs=pl.BlockSpec((d, out_cols), lambda u, off: (0, u)),
    )
    return pl.pallas_call(
        _transpose_cast_kernel,
        grid_spec=grid_spec,
        out_shape=jax.ShapeDtypeStruct((d, out_cols * len(offsets)), BF16),
        compiler_params=pltpu.CompilerParams(dimension_semantics=("arbitrary",), vmem_limit_bytes=VMEM_LIMIT),
        name="w_layout",
    )(jnp.asarray(offsets, jnp.int32), wt)


def _layout_w_in(w):
    wt = jnp.swapaxes(w, 0, 1)
    start = lambda name: _W_OFF[name][0]
    assert all(start(nm) % 8 == 0 for nm in ("a_q", "gate_a", "b_q", "i_q", "b_k", "b_v", "i_k"))
    main_off = ([start("a_q") + 512 * i for i in range(4)] + [start("gate_a") + 512 * i for i in range(4)]
                + [start("b_q"), start("i_q")])
    tail_off = [start("b_k"), start("b_k") + 64, start("b_v"), start("b_v") + 64, start("i_k")]
    main = _gather_columns(wt, main_off, 512, 512)
    tail = _gather_columns(wt, tail_off, 64, 128)
    small_t = jnp.concatenate([wt[slice(*_W_OFF[nm])] for nm in ("a_beta", "a_alpha", "i_w")], axis=0)
    small = _hi_lo_pair(jnp.swapaxes(small_t, 0, 1))
    return main, tail, small


def kernel(x, positions, norm1_g, w_in, b_gate, conv_w, a_log, dt_bias, a_norm_g, w_proj_a, w_proj_b, w_out,
           norm2_g, w_router_group, b_router_group, w_router_expert, b_router_expert, w_exp_gate, w_exp_up,
           w_exp_down, final_norm_g):
    b, s, d = x.shape
    n = b * s
    assert w_in.shape[0] == 1 and d == D_MODEL, "one layer: the MoE kernel also applies the final norm"
    xc = x.reshape(n, d).astype(F32)
    for l in range(w_in.shape[0]):
        w_main, w_tail, w_small = _layout_w_in(w_in[l])
        z, zs = _in_proj(xc, norm1_g[l][None, :].astype(F32), w_main, w_tail, w_small, positions)
        z3 = z.reshape(b, s, Z_W)
        zs3 = zs.reshape(b, s, 128)
        o_a = _gdn(z3, zs3, conv_w[l], a_log[l], dt_bias[l], a_norm_g[l])
        o_b = _dsa(z3, zs3)
        gap = R_EXPERT - N_GROUPS
        wr = _hi_lo_pair(jnp.concatenate([w_router_group[l], jnp.zeros((d, gap), F32), w_router_expert[l]],
                                         axis=1))
        br = jnp.concatenate([b_router_group[l], jnp.zeros((gap,), F32), b_router_expert[l],
                              jnp.zeros((128 - R_EXPERT - N_EXPERTS,), F32)])[None, :].astype(F32)
        x1, h2, rl = _merge(xc, o_a.reshape(n, -1), o_b.reshape(n, -1), z, b_gate[l][None, :].astype(F32),
                            w_proj_a[l].astype(BF16), w_proj_b[l].astype(BF16), w_out[l].astype(BF16),
                            norm2_g[l][None, :].astype(F32), wr, br)
        xc = _moe(x1, h2, rl, w_exp_gate[l].astype(BF16), w_exp_up[l].astype(BF16),
                  w_exp_down[l].astype(BF16), final_norm_g[None, :].astype(F32))
    return xc.reshape(b, s, d).astype(x.dtype)
```

```python
import numpy as np
import jax
import jax.numpy as jnp
from jax import lax
from jax.experimental import pallas as pl
from jax.experimental.pallas import tpu as pltpu

F32 = jnp.float32
BF16 = jnp.bfloat16

D_MODEL = 1024
CHUNK = 64
EPS = 1e-6
ROPE_THETA = 10000.0
A_HEADS = 8
A_DK = 64
A_DV = 64
CONV_K = 4
B_HEADS = 8
B_KV_HEADS = 2
B_HD = 64
IDX_HEADS = 8
IDX_HD = 64
TOPK_MAX = 256
N_GROUPS = 4
EXPERTS_PER_GROUP = 4
N_EXPERTS = 16
D_EXPERT = 256

O_AQKV, O_AZ, O_GA, O_GB, O_BQ, O_IQ, O_KD, O_VD, O_IK = 0, 1536, 2048, 3072, 4096, 4608, 5120, 5376, 5632
Z_W = 5760
Z_MAIN = O_KD
S_BETA, S_ALPHA, S_IW = 0, 8, 16

GROUP_HEADS = 2
BD = GROUP_HEADS * CHUNK
NEG_BIG = -1e30
LOG2E = 1.4426950408889634
BISECT_STEPS = 18
VMEM_LIMIT = 56 * 1024 * 1024


def _split(x):
    hi = x.astype(BF16)
    lo = (x - hi.astype(F32)).astype(BF16)
    return hi, lo


def _dot(a, b):
    return jnp.dot(a, b, preferred_element_type=F32)


def _mm(a, b):
    return _dot(a.astype(BF16), b.astype(BF16))


def _mm_nt(a, b):
    return lax.dot_general(a.astype(BF16), b.astype(BF16), (((1,), (1,)), ((), ())),
                           preferred_element_type=F32)


def _mm_exact_lhs(a_bf16, x):
    xh, xl = _split(x)
    return _dot(a_bf16, xh) + _dot(a_bf16, xl)


def _rmsnorm_rows(x, g):
    return x * lax.rsqrt(jnp.mean(x * x, axis=-1, keepdims=True) + EPS) * g


_Z_CHUNKS = tuple((o, min(512, Z_W - o)) for o in range(0, Z_W, 512))


def _rope(x, cs, sn, first):
    w = x.shape[1]
    rep = w // 128
    if rep > 1:
        cs, sn, first = (jnp.concatenate([a] * rep, axis=1) for a in (cs, sn, first))
    swapped = jnp.where(first, pltpu.roll(x, w - B_HD // 2, 1), pltpu.roll(x, B_HD // 2, 1))
    return x * cs + swapped * sn


def _in_proj_kernel(x_ref, g_ref, w_ref, wt_ref, ws_ref, pos_ref, inv_ref, z_ref, zs_ref):
    h = _rmsnorm_rows(x_ref[...], g_ref[...])
    hb, hl = _split(h)

    pos = jnp.broadcast_to(pos_ref[...].astype(F32), (128, pos_ref.shape[1]))
    ang = jnp.transpose(pos) * inv_ref[...]
    lane = lax.broadcasted_iota(jnp.int32, ang.shape, 1)
    first = (lane & (B_HD - 1)) < (B_HD // 2)
    cs = jnp.cos(ang)
    sn = jnp.sin(ang)
    sn = jnp.where(first, -sn, sn)

    for o, w in _Z_CHUNKS:
        r = _dot(hb, w_ref[:, o:o + w] if o < Z_MAIN else wt_ref[:, o - Z_MAIN:o - Z_MAIN + w])
        if o == O_BQ:
            r = _rope(r, cs, sn, first) * (B_HD ** -0.5 * LOG2E)
        elif o == O_IQ or o == O_IK:
            r = _rope(r, cs, sn, first)
        elif o == O_KD:
            kw = O_VD - O_KD
            r = jnp.concatenate([_rope(r[:, :kw], cs, sn, first), r[:, kw:]], axis=1)
        z_ref[:, o:o + w] = r.astype(BF16)
    s1 = _dot(hb, ws_ref[...])
    s2 = _dot(hl, ws_ref[:, :128])
    zs_ref[...] = s1[:, :128] + s1[:, 128:] + s2


def _in_proj(x2d, g, w_main, w_tail, w_small, positions, tm=512):
    n = x2d.shape[0]
    assert n % tm == 0 and x2d.shape[1] == D_MODEL
    assert w_main.shape == (D_MODEL, Z_MAIN) and w_tail.shape == (D_MODEL, Z_W - Z_MAIN)
    half = B_HD // 2
    inv = ROPE_THETA ** (-jnp.arange(half, dtype=F32) / half)
    inv128 = jnp.tile(inv, 4)[None, :]
    pos = positions.astype(jnp.int32).reshape(1, n)
    return pl.pallas_call(
        _in_proj_kernel,
        grid=(n // tm,),
        in_specs=[
            pl.BlockSpec((tm, D_MODEL), lambda i: (i, 0)),
            pl.BlockSpec((1, D_MODEL), lambda i: (0, 0)),
            pl.BlockSpec((D_MODEL, Z_MAIN), lambda i: (0, 0)),
            pl.BlockSpec((D_MODEL, Z_W - Z_MAIN), lambda i: (0, 0)),
            pl.BlockSpec((D_MODEL, 256), lambda i: (0, 0)),
            pl.BlockSpec((1, tm), lambda i: (0, i)),
            pl.BlockSpec((1, 128), lambda i: (0, 0)),
        ],
        out_specs=[
            pl.BlockSpec((tm, Z_W), lambda i: (i, 0)),
            pl.BlockSpec((tm, 128), lambda i: (i, 0)),
        ],
        out_shape=[jax.ShapeDtypeStruct((n, Z_W), BF16), jax.ShapeDtypeStruct((n, 128), F32)],
        compiler_params=pltpu.CompilerParams(dimension_semantics=("parallel",),
                                             vmem_limit_bytes=VMEM_LIMIT),
        name="in_proj",
    )(x2d, g, w_main, w_tail, w_small, pos, inv128)


def _gdn_constants():
    r = np.arange(BD)
    same = (r[:, None] // CHUNK) == (r[None, :] // CHUNK)
    incl = same & (r[:, None] >= r[None, :])
    strict = same & (r[:, None] > r[None, :])
    eye = np.eye(BD, dtype=np.float32)
    ll = np.concatenate([incl, same], axis=0).astype(np.float32)
    n_groups = A_HEADS // GROUP_HEADS
    sel = np.zeros((2 * n_groups, BD, 128), np.float32)
    for gi in range(n_groups):
        for h in range(GROUP_HEADS):
            sel[gi * 2 + 0, h * CHUNK:(h + 1) * CHUNK, S_BETA + gi * GROUP_HEADS + h] = 1.0
            sel[gi * 2 + 1, h * CHUNK:(h + 1) * CHUNK, S_ALPHA + gi * GROUP_HEADS + h] = 1.0
    return (jnp.asarray(incl, F32), jnp.asarray(strict, F32), jnp.asarray(same, F32), jnp.asarray(eye),
            jnp.asarray(ll, BF16), jnp.asarray(sel))


def _tile_heads(x):
    return jnp.concatenate([x] * GROUP_HEADS, axis=0)


def _gdn_kernel(zq_ref, zz_ref, zs_ref, cw_ref, av_ref, ag_ref, incl_ref, strict_ref, bdm_ref, eye_ref,
                ll_ref, sel_ref, o_ref, ext_ref, st_ref):
    c = pl.program_id(1)
    nb, t = zq_ref.shape[0], zq_ref.shape[1]

    @pl.when(c == 0)
    def _():
        ext_ref[:, 0:8, :] = jnp.zeros((nb, 8, ext_ref.shape[2]), F32)
        st_ref[...] = jnp.zeros(st_ref.shape, F32)

    hw = A_HEADS * A_DK
    cw = cw_ref[...]
    av = av_ref[...]
    q_all, k_all, v_all, bg = [], [], [], []
    for bi in range(nb):
        ext_ref[bi, 8:8 + t, :] = zq_ref[bi].astype(F32)
        y = cw[0:1, :] * ext_ref[bi, pl.ds(8 - (CONV_K - 1), t), :]
        for j in range(1, CONV_K):
            y = y + cw[j:j + 1, :] * ext_ref[bi, pl.ds(8 - (CONV_K - 1) + j, t), :]
        ext_ref[bi, 0:8, :] = ext_ref[bi, t:t + 8, :]
        y = y * jax.nn.sigmoid(y)
        q_all.append(y[:, :hw])
        k_all.append(y[:, hw:2 * hw])
        v_all.append(y[:, 2 * hw:])

        sm = zs_ref[bi]
        lane = lax.broadcasted_iota(jnp.int32, sm.shape, 1)
        xg = sm + av[1:2, :]
        softplus = jnp.maximum(xg, 0.0) + jnp.log1p(jnp.exp(-jnp.abs(xg)))
        g_all = -jnp.exp(av[0:1, :]) * softplus
        bg.append(jnp.where(lane < S_ALPHA, jax.nn.sigmoid(sm), g_all))

    incl = incl_ref[...]
    strict = strict_ref[...]
    bdm = bdm_ref[...]
    eye = eye_ref[...]
    ll = ll_ref[...]
    incl_b = ll[:BD]

    n_chunks = t // CHUNK
    n_groups = A_HEADS // GROUP_HEADS
    chains = [(bi, ci, gi) for bi in range(nb) for ci in range(n_chunks) for gi in range(n_groups)]

    pre = {}
    for bi, ci, gi in chains:
        r0, c0 = ci * CHUNK, gi * BD
        bg4 = _tile_heads(bg[bi][r0:r0 + CHUNK])
        beta = jnp.sum(bg4 * sel_ref[gi * 2 + 0], axis=1, keepdims=True)
        gcol = jnp.sum(bg4 * sel_ref[gi * 2 + 1], axis=1, keepdims=True)
        gs = _mm_exact_lhs(ll, jnp.broadcast_to(gcol, (BD, 128)))
        g_cum = gs[:BD, :1]
        g_last = gs[BD:, :1]
        diff = _mm_exact_lhs(incl_b, gcol * strict)
        decay = jnp.where(incl > 0.0, jnp.exp(diff), 0.0)
        e_cum = jnp.exp(g_cum)
        kr = _tile_heads(k_all[bi][r0:r0 + CHUNK, c0:c0 + BD]) * bdm
        qr = _tile_heads(q_all[bi][r0:r0 + CHUNK, c0:c0 + BD]) * bdm
        vm = _tile_heads(v_all[bi][r0:r0 + CHUNK, c0:c0 + BD]) * bdm
        km = kr * lax.rsqrt(jnp.sum(kr * kr, axis=1, keepdims=True) + EPS)
        qm = qr * (lax.rsqrt(jnp.sum(qr * qr, axis=1, keepdims=True) + EPS) * (A_DK ** -0.5))
        kkqk = _mm_nt(jnp.concatenate([km, qm], axis=0), km)
        m = -(strict * beta * kkqk[:BD] * decay)
        pre[bi, ci, gi] = dict(beta=beta, g_cum=g_cum, g_last=g_last, e_cum=e_cum, km=km, qm=qm, vm=vm,
                               qk=kkqk[BD:] * decay, m=m, inv=eye + m)

    sq = CHUNK
    while sq > 2:
        for key in chains:
            p = pre[key]
            p["m"] = _mm(p["m"], p["m"])
        for key in chains:
            p = pre[key]
            p["inv"] = p["inv"] + _mm(p["inv"], p["m"])
        sq //= 2

    states = [st_ref[i] for i in range(nb * n_groups)]
    outs = [[] for _ in range(nb)]
    lanes = [(bi, gi) for bi in range(nb) for gi in range(n_groups)]
    for ci in range(n_chunks):
        ps = {k: pre[k[0], ci, k[1]] for k in lanes}
        sidx = {k: k[0] * n_groups + k[1] for k in lanes}
        kq_s = {k: _mm(jnp.concatenate([ps[k]["km"] * ps[k]["e_cum"], ps[k]["qm"] * ps[k]["e_cum"]], axis=0),
                       states[sidx[k]]) for k in lanes}
        v_new = {k: _mm(ps[k]["inv"], ps[k]["beta"] * (ps[k]["vm"] - kq_s[k][:BD])) for k in lanes}
        o_bd = {k: kq_s[k][BD:] + _mm(ps[k]["qk"], v_new[k]) for k in lanes}
        for k in lanes:
            p = ps[k]
            k_dec = p["km"] * jnp.exp(p["g_last"] - p["g_cum"])
            states[sidx[k]] = states[sidx[k]] * jnp.exp(p["g_last"]) + _mm(k_dec.T, v_new[k])
        for bi in range(nb):
            o_groups = []
            for gi in range(n_groups):
                ob = o_bd[bi, gi]
                ob = ob * lax.rsqrt(jnp.sum(ob * ob, axis=1, keepdims=True) * (1.0 / A_DV) + EPS)
                o_groups.append(sum(ob[h * CHUNK:(h + 1) * CHUNK] for h in range(GROUP_HEADS)))
            outs[bi].append(jnp.concatenate(o_groups, axis=1))

    for i, state in enumerate(states):
        st_ref[i] = state
    for bi in range(nb):
        o = outs[bi][0] if n_chunks == 1 else jnp.concatenate(outs[bi], axis=0)
        zz = zz_ref[bi].astype(F32)
        o_ref[bi] = (o * ag_ref[...] * (zz * jax.nn.sigmoid(zz))).astype(BF16)


def _gdn(z3, zs3, conv_w, a_log, dt_bias, a_norm_g, t=2 * CHUNK, nb=2):
    b, s, _ = z3.shape
    assert b % nb == 0 and s % t == 0 and t % CHUNK == 0 and t >= 8
    consts = _gdn_constants()
    av = jnp.zeros((2, 128), F32)
    av = av.at[0, S_ALPHA:S_ALPHA + A_HEADS].set(a_log.astype(F32))
    av = av.at[1, S_ALPHA:S_ALPHA + A_HEADS].set(dt_bias.astype(F32))
    ag = jnp.tile(a_norm_g.astype(F32), A_HEADS)[None, :]
    conv_cols = 2 * A_HEADS * A_DK + A_HEADS * A_DV

    def const_spec(a):
        nd = a.ndim
        return pl.BlockSpec(a.shape, lambda bi, ci, _n=nd: (0,) * _n)

    small_in = (conv_w.astype(F32), av, ag) + consts
    return pl.pallas_call(
        _gdn_kernel,
        grid=(b // nb, s // t),
        in_specs=[
            pl.BlockSpec((nb, t, conv_cols), lambda bi, ci: (bi, ci, O_AQKV // conv_cols)),
            pl.BlockSpec((nb, t, 512), lambda bi, ci: (bi, ci, O_AZ // 512)),
            pl.BlockSpec((nb, t, 128), lambda bi, ci: (bi, ci, 0)),
        ] + [const_spec(a) for a in small_in],
        out_specs=pl.BlockSpec((nb, t, A_HEADS * A_DV), lambda bi, ci: (bi, ci, 0)),
        out_shape=jax.ShapeDtypeStruct((b, s, A_HEADS * A_DV), BF16),
        scratch_shapes=[pltpu.VMEM((nb, 8 + t, conv_cols), F32),
                        pltpu.VMEM((nb * (A_HEADS // GROUP_HEADS), BD, BD), F32)],
        compiler_params=pltpu.CompilerParams(dimension_semantics=("parallel", "arbitrary"),
                                             vmem_limit_bytes=VMEM_LIMIT),
        name="gdn",
    )(z3, z3, zs3, *small_in)


def _dsa_kernel(topk, tk, q_ref, iq_ref, ka_ref, kb_ref, vd_ref, ik_ref, zs_ref, o_ref, st_ref, vt_ref):
    qb = pl.program_id(1)
    s = ka_ref.shape[0]
    tq = q_ref.shape[0]
    nkt_max = s // tk
    r0 = qb * tq
    n_tiles = (r0 + tq + tk - 1) // tk
    heads_per_kv = B_HEADS // B_KV_HEADS

    @pl.when(qb == 0)
    def _():
        for t in range(nkt_max):
            for g in range(B_KV_HEADS):
                v_tile = vd_ref[t * tk:(t + 1) * tk, g * 128:(g + 1) * 128]
                vt_ref[t, g] = v_tile.astype(F32).T.astype(BF16)

    lane128 = lax.broadcasted_iota(jnp.int32, (tq, 128), 1)
    low_half = lane128 < B_HD
    high_half = lane128 >= B_HD

    def head_slab(x, h):
        slab = x[:, (h // 2) * 128:(h // 2 + 1) * 128]
        return jnp.where(low_half if h % 2 == 0 else high_half, slab, jnp.zeros_like(slab))

    zs_t = zs_ref[...].T
    iw_scale = (IDX_HEADS ** -0.5) * (IDX_HD ** -0.5)
    wrows = [zs_t[S_IW + h:S_IW + h + 1, :] * iw_scale for h in range(IDX_HEADS)]
    qrow = r0 + lax.broadcasted_iota(jnp.int32, (1, tq), 1)
    limit = ((qrow >> 6) + 1) << 6
    small = limit <= topk

    iq = iq_ref[...]
    iq_heads = [head_slab(iq, h) for h in range(IDX_HEADS)]

    def index_tile(kt, carry):
        r_max, r_min = carry
        off = pl.multiple_of(kt * tk, tk)
        ikt = ik_ref[pl.ds(off, tk), :]
        acc = jnp.zeros((tk, tq), F32)
        for h in range(IDX_HEADS):
            sc = lax.dot_general(ikt, iq_heads[h], (((1,), (1,)), ((), ())), preferred_element_type=F32)
            acc = acc + wrows[h] * jnp.maximum(sc, 0.0)
        valid = off + lax.broadcasted_iota(jnp.int32, (tk, tq), 0) < limit
        masked = jnp.where(valid, acc, -jnp.inf)
        st_ref[kt] = masked
        r_max = jnp.maximum(r_max, jnp.max(masked, axis=0, keepdims=True))
        r_min = jnp.minimum(r_min, jnp.min(acc, axis=0, keepdims=True))
        return r_max, r_min

    r_max, r_min = lax.fori_loop(0, n_tiles, index_tile,
                                 (jnp.full((1, tq), -jnp.inf, F32), jnp.full((1, tq), jnp.inf, F32)))

    all_kept = r0 + tq <= min(topk, tk)

    @pl.when(all_kept)
    def _():
        st_ref[0] = jnp.where(st_ref[0] > -jnp.inf, 0.0, NEG_BIG)

    for j in range(nkt_max):
        @pl.when(jnp.logical_and(n_tiles == j + 1, jnp.logical_not(all_kept)))
        def _(j=j):
            _topk_mask(topk, j + 1, small, r_max, r_min, st_ref)

    q = q_ref[...]
    nr = heads_per_kv * tq
    qcols = [jnp.concatenate([head_slab(q, g * heads_per_kv + j) for j in range(heads_per_kv)], axis=0)
             for g in range(B_KV_HEADS)]
    k_refs = (ka_ref, kb_ref)

    def attend_tile(kt, carry):
        off = pl.multiple_of(kt * tk, tk)
        bias = st_ref[kt]
        bias4 = jnp.concatenate([bias] * heads_per_kv, axis=1)
        groups = range(B_KV_HEADS)
        logits = [lax.dot_general(k_refs[g][pl.ds(off, tk), :], qcols[g], (((1,), (1,)), ((), ())),
                                  preferred_element_type=F32) + bias4 for g in groups]
        m_new = [jnp.maximum(carry[g][0], jnp.max(logits[g], axis=0, keepdims=True)) for g in groups]
        p = [jnp.exp2(logits[g] - m_new[g]) for g in groups]
        alpha = [jnp.exp2(carry[g][0] - m_new[g]) for g in groups]
        l_new = [alpha[g] * carry[g][1] + jnp.sum(p[g], axis=0, keepdims=True) for g in groups]
        pv = [_dot(vt_ref[kt, g], p[g].astype(BF16)) for g in groups]
        return tuple((m_new[g], l_new[g], alpha[g] * carry[g][2] + pv[g]) for g in groups)

    init = tuple((jnp.full((1, nr), NEG_BIG, F32), jnp.zeros((1, nr), F32), jnp.zeros((128, nr), F32))
                 for _ in range(B_KV_HEADS))
    fin = lax.fori_loop(0, n_tiles, attend_tile, init)
    for g in range(B_KV_HEADS):
        _, l_run, acc = fin[g]
        og = acc / l_run
        heads = [og[:, j * tq:(j + 1) * tq].T for j in range(heads_per_kv)]
        for pp in range(heads_per_kv // 2):
            col = (g * (heads_per_kv // 2) + pp) * 128
            o_ref[:, col:col + 128] = jnp.where(low_half, heads[2 * pp], heads[2 * pp + 1]).astype(BF16)


def _topk_mask(topk, nt, small, r_max, r_min, st_ref):
    _, tk, tq = st_ref.shape
    kf = float(topk)

    def tiles():
        return [st_ref[t] for t in range(nt)]

    def q_sum(pred):
        tot = None
        for t, x in enumerate(tiles()):
            c = jnp.sum(pred(x, t).astype(F32), axis=0, keepdims=True)
            tot = c if tot is None else tot + c
        return tot

    def q_max(val):
        best = None
        for t, x in enumerate(tiles()):
            c = jnp.max(val(x, t), axis=0, keepdims=True)
            best = c if best is None else jnp.maximum(best, c)
        return best

    hi0 = r_max + jnp.maximum(jnp.abs(r_max), 1e-30) * 1e-6

    def bisect(_, carry):
        lo, hi = carry
        mid = 0.5 * (lo + hi)
        ge = q_sum(lambda x, t: x >= mid) >= kf
        return jnp.where(ge, mid, lo), jnp.where(ge, hi, mid)

    lo, hi = lax.fori_loop(0, BISECT_STEPS, bisect, (r_min, hi0))

    def peel_cond(carry):
        return jnp.sum(1.0 - carry[0]) > 0.0

    def peel(carry):
        done, thr, hi_c, n_ge = carry
        v1 = q_max(lambda x, t: jnp.where(x < hi_c, x, -jnp.inf))
        c1 = q_sum(lambda x, t: x >= v1)
        ok = c1 >= kf
        act = done < 0.5
        thr = jnp.where(act & ok, v1, thr)
        n_ge = jnp.where(act & ok, c1, n_ge)
        hi_c = jnp.where(act & (~ok), v1, hi_c)
        return jnp.where(ok, 1.0, done), thr, hi_c, n_ge

    _, thr, _, n_ge = lax.while_loop(peel_cond, peel, (jnp.where(small, 1.0, 0.0), lo, hi, jnp.full_like(lo, kf)))

    def key_index(t):
        return (t * tk + lax.broadcasted_iota(jnp.int32, (tk, tq), 0)).astype(F32)

    contested = jnp.sum(jnp.where((n_ge > kf) & (~small), 1.0, 0.0)) > 0.0
    last = float(nt * tk - 1)

    def tie_cut():
        need = kf - q_sum(lambda x, t: x > thr)

        def tie_search(_, carry):
            jlo, jhi = carry
            mid = jnp.floor(0.5 * (jlo + jhi))
            ge = q_sum(lambda x, t: (x == thr) & (key_index(t) <= mid)) >= need
            return jnp.where(ge, jlo, mid), jnp.where(ge, mid, jhi)

        n_iter = int(np.ceil(np.log2(nt * tk))) + 1
        return lax.fori_loop(0, n_iter, tie_search,
                             (jnp.full((1, tq), -1.0, F32), jnp.full((1, tq), last, F32)))[1]

    jcut = lax.cond(contested, tie_cut, lambda: jnp.full((1, tq), last, F32))
    for t, x in enumerate(tiles()):
        sel = (x > -jnp.inf) & (small | (x > thr) | ((x == thr) & (key_index(t) <= jcut)))
        st_ref[t] = jnp.where(sel, 0.0, NEG_BIG)


def _dsa(z3, zs3, tq=512, tk=512):
    b, s, _ = z3.shape
    assert s % tq == 0 and s % tk == 0 and tq % CHUNK == 0 and tk % CHUNK == 0
    topk = min(TOPK_MAX, s // 4)
    kernel = lambda *refs: _dsa_kernel(topk, tk, *refs)
    qblock = lambda w, cb: pl.BlockSpec((None, tq, w), lambda bi, qi, _c=cb: (bi, qi, _c))
    keys = lambda w, cb: pl.BlockSpec((None, s, w), lambda bi, qi, _c=cb: (bi, 0, _c))
    return pl.pallas_call(
        kernel,
        grid=(b, s // tq),
        in_specs=[qblock(512, O_BQ // 512), qblock(512, O_IQ // 512), keys(128, O_KD // 128),
                  keys(128, O_KD // 128 + 1), keys(256, O_VD // 256), keys(128, O_IK // 128), qblock(128, 0)],
        out_specs=pl.BlockSpec((None, tq, B_HEADS * B_HD), lambda bi, qi: (bi, qi, 0)),
        out_shape=jax.ShapeDtypeStruct((b, s, B_HEADS * B_HD), BF16),
        scratch_shapes=[pltpu.VMEM((s // tk, tk, tq), F32),
                        pltpu.VMEM((s // tk, B_KV_HEADS, 128, tk), BF16)],
        compiler_params=pltpu.CompilerParams(dimension_semantics=("parallel", "arbitrary"),
                                             vmem_limit_bytes=VMEM_LIMIT),
        name="dsa",
    )(z3, z3, z3, z3, z3, z3, zs3)


def _merge_kernel(x_ref, oa_ref, ob_ref, ga_ref, gb_ref, bg_ref, wa_ref, wb_ref, wo_ref, n2_ref, wr_ref,
                  br_ref, x1_ref, h2_ref, rl_ref):
    pa = _dot(oa_ref[...], wa_ref[...])
    pb = _dot(ob_ref[...], wb_ref[...])
    bgv = bg_ref[...]
    ga = jax.nn.sigmoid(ga_ref[...].astype(F32) + bgv[:, :D_MODEL])
    gb = jax.nn.sigmoid(gb_ref[...].astype(F32) + bgv[:, D_MODEL:])
    merged = ga * pa + gb * pb
    x1 = x_ref[...] + _mm(merged, wo_ref[...])
    x1_ref[...] = x1
    h2 = _rmsnorm_rows(x1, n2_ref[...])
    hb, hl = _split(h2)
    h2_ref[...] = hb
    s1 = _dot(hb, wr_ref[...])
    s2 = _dot(hl, wr_ref[:, :128])
    rl_ref[...] = s1[:, :128] + s1[:, 128:] + s2 + br_ref[...]


def _merge(x2d, oa, ob, z2d, b_gate, wa, wb, wo, n2, wr, br, tm=1024):
    n = x2d.shape[0]
    assert n % tm == 0 and oa.shape == ob.shape == (n, 512)
    full =lambda shape: pl.BlockSpec(shape, lambda i: (0, 0))
    return pl.pallas_call(
        _merge_kernel,
        grid=(n // tm,),
        in_specs=[
            pl.BlockSpec((tm, D_MODEL), lambda i: (i, 0)),
            pl.BlockSpec((tm, 512), lambda i: (i, 0)),
            pl.BlockSpec((tm, 512), lambda i: (i, 0)),
            pl.BlockSpec((tm, D_MODEL), lambda i: (i, O_GA // D_MODEL)),
            pl.BlockSpec((tm, D_MODEL), lambda i: (i, O_GB // D_MODEL)),
            full((1, 2 * D_MODEL)), full((512, D_MODEL)), full((512, D_MODEL)), full((D_MODEL, D_MODEL)),
            full((1, D_MODEL)), full((D_MODEL, 256)), full((1, 128)),
        ],
        out_specs=[
            pl.BlockSpec((tm, D_MODEL), lambda i: (i, 0)),
            pl.BlockSpec((tm, D_MODEL), lambda i: (i, 0)),
            pl.BlockSpec((tm, 128), lambda i: (i, 0)),
        ],
        out_shape=[jax.ShapeDtypeStruct((n, D_MODEL), F32), jax.ShapeDtypeStruct((n, D_MODEL), BF16),
                   jax.ShapeDtypeStruct((n, 128), F32)],
        compiler_params=pltpu.CompilerParams(dimension_semantics=("parallel",),
                                             vmem_limit_bytes=VMEM_LIMIT),
        name="merge",
    )(x2d, oa, ob, z2d, z2d, b_gate, wa, wb, wo, n2, wr, br)


R_GROUP, R_EXPERT = 0, 8


def _routing_weights(rl):
    t = rl.T
    n_tok = t.shape[1]
    gl = t[R_GROUP:R_GROUP + N_GROUPS]
    gidx = lax.broadcasted_iota(jnp.int32, gl.shape, 0)
    gmax = jnp.max(gl, axis=0, keepdims=True)
    gsel = jnp.min(jnp.where(gl == gmax, gidx, N_GROUPS), axis=0, keepdims=True)
    ggate = 1.0 / jnp.sum(jnp.exp(gl - gmax), axis=0, keepdims=True)
    el = t[R_EXPERT:R_EXPERT + N_EXPERTS]
    eidx = lax.broadcasted_iota(jnp.int32, el.shape, 0)
    e_lo = gsel * EXPERTS_PER_GROUP
    emask = (eidx >= e_lo) & (eidx < e_lo + EXPERTS_PER_GROUP)
    el = jnp.where(emask, el, -jnp.inf)
    emax = jnp.max(el, axis=0, keepdims=True)
    ee = jnp.where(emask, jnp.exp(el - emax), 0.0)
    ep = jnp.where(emask, ee / jnp.sum(ee, axis=0, keepdims=True), -1.0)
    p1 = jnp.max(ep, axis=0, keepdims=True)
    i1 = jnp.min(jnp.where(ep == p1, eidx, N_EXPERTS), axis=0, keepdims=True)
    ep2 = jnp.where(eidx == i1, -1.0, ep)
    p2 = jnp.max(ep2, axis=0, keepdims=True)
    i2 = jnp.min(jnp.where(ep2 == p2, eidx, N_EXPERTS), axis=0, keepdims=True)
    tot = p1 + p2
    comb_t = (jnp.where(eidx == i1, ggate * (p1 / tot), 0.0)
              + jnp.where(eidx == i2, ggate * (p2 / tot), 0.0))
    full = jnp.concatenate([jnp.zeros((R_EXPERT, n_tok), F32), comb_t,
                            jnp.zeros((128 - R_EXPERT - N_EXPERTS, n_tok), F32)], axis=0)
    return full.T


def _moe_kernel(x1_ref, h2_ref, rl_ref, w1_ref, w3_ref, w2_ref, fg_ref, o_ref, y_ref, comb_ref):
    step = pl.program_id(1)
    per_step = w1_ref.shape[0]

    @pl.when(step == 0)
    def _():
        comb_ref[...] = _routing_weights(rl_ref[...])
        y_ref[...] = jnp.zeros(y_ref.shape, F32)

    h = h2_ref[...]
    comb = comb_ref[...]
    lane = lax.broadcasted_iota(jnp.int32, comb.shape, 1)
    acts = []
    for j in range(per_step):
        a = _dot(h, w1_ref[j])
        b = _dot(h, w3_ref[j])
        ce = jnp.sum(jnp.where(lane == R_EXPERT + step * per_step + j, comb, 0.0), axis=1, keepdims=True)
        acts.append(((a * jax.nn.sigmoid(a)) * b * ce).astype(BF16))
    y_ref[...] += _dot(jnp.concatenate(acts, axis=1), w2_ref[...])

    @pl.when(step == pl.num_programs(1) - 1)
    def _():
        o_ref[...] = _rmsnorm_rows(x1_ref[...] + y_ref[...], fg_ref[...])


def _moe(x1, h2, rl, w1, w3, w2, fg, tm=1024, per_step=4):
    n = x1.shape[0]
    assert n % tm == 0 and N_EXPERTS % per_step == 0 and w1.shape == (N_EXPERTS, D_MODEL, D_EXPERT)
    w2g =w2.reshape(N_EXPERTS // per_step, per_step * D_EXPERT, D_MODEL)
    return pl.pallas_call(
        _moe_kernel,
        grid=(n // tm, N_EXPERTS // per_step),
        in_specs=[
            pl.BlockSpec((tm, D_MODEL), lambda i, e: (i, 0)),
            pl.BlockSpec((tm, D_MODEL), lambda i, e: (i, 0)),
            pl.BlockSpec((tm, 128), lambda i, e: (i, 0)),
            pl.BlockSpec((per_step, D_MODEL, D_EXPERT), lambda i, e: (e, 0, 0)),
            pl.BlockSpec((per_step, D_MODEL, D_EXPERT), lambda i, e: (e, 0, 0)),
            pl.BlockSpec((None, per_step * D_EXPERT, D_MODEL), lambda i, e: (e, 0, 0)),
            pl.BlockSpec((1, D_MODEL), lambda i, e: (0, 0)),
        ],
        out_specs=pl.BlockSpec((tm, D_MODEL), lambda i, e: (i, 0)),
        out_shape=jax.ShapeDtypeStruct((n, D_MODEL), F32),
        scratch_shapes=[pltpu.VMEM((tm, D_MODEL), F32), pltpu.VMEM((tm, 128), F32)],
        compiler_params=pltpu.CompilerParams(dimension_semantics=("parallel", "arbitrary"),
                                             vmem_limit_bytes=VMEM_LIMIT),
        name="moe",
    )(x1, h2, rl, w1, w3, w2g, fg)


_W_OFF = {}
_off = 0
for _name, _n in (("a_q", 512), ("a_k", 512), ("a_v", 512), ("a_z", 512), ("a_beta", 8), ("a_alpha", 8),
                  ("b_q", 512), ("b_k", 128), ("b_v", 128), ("i_q", 512), ("i_k", 64), ("i_w", 8),
                  ("gate_a", 1024), ("gate_b", 1024)):
    _W_OFF[_name] = (_off, _off + _n)
    _off += _n


def _hi_lo_pair(w_small):
    pad = jnp.zeros((w_small.shape[0], 128 - w_small.shape[1]), F32)
    w = jnp.concatenate([w_small.astype(F32), pad], axis=1)
    hi, lo = _split(w)
    return jnp.concatenate([hi, lo], axis=1)


def _transpose_cast_kernel(off_ref, wt_ref, out_ref):
    del off_ref
    blk = wt_ref[...]
    reps = out_ref.shape[1] // blk.shape[0]
    if reps > 1:
        blk = jnp.concatenate([blk] * reps, axis=0)
    out_ref[...] = blk.T.astype(BF16)


def _gather_columns(wt, offsets, rows, out_cols):
    d = wt.shape[1]
    grid_spec = pltpu.PrefetchScalarGridSpec(
        num_scalar_prefetch=1,
        grid=(len(offsets),),
        in_specs=[pl.BlockSpec((pl.Element(rows), pl.Element(d)), lambda u, off: (pl.multiple_of(off[u], 8), 0))],
        out_specs=pl.BlockSpec((d, out_cols), lambda u, off: (0, u)),
    )
    return pl.pallas_call(
        _transpose_cast_kernel,
        grid_spec=grid_spec,
        out_shape=jax.ShapeDtypeStruct((d, out_cols * len(offsets)), BF16),
        compiler_params=pltpu.CompilerParams(dimension_semantics=("arbitrary",), vmem_limit_bytes=VMEM_LIMIT),
        name="w_layout",
    )(jnp.asarray(offsets, jnp.int32), wt)


def _layout_w_in(w):
    wt = jnp.swapaxes(w, 0, 1)
    start = lambda name: _W_OFF[name][0]
    assert all(start(nm) % 8 == 0 for nm in ("a_q", "gate_a", "b_q", "i_q", "b_k", "b_v", "i_k"))
    main_off = ([start("a_q") + 512 * i for i in range(4)] + [start("gate_a") + 512 * i for i in range(4)]
                + [start("b_q"), start("i_q")])
    tail_off = [start("b_k"), start("b_k") + 64, start("b_v"), start("b_v") + 64, start("i_k")]
    main = _gather_columns(wt, main_off, 512, 512)
    tail = _gather_columns(wt, tail_off, 64, 128)
    small_t = jnp.concatenate([wt[slice(*_W_OFF[nm])] for nm in ("a_beta", "a_alpha", "i_w")], axis=0)
    small = _hi_lo_pair(jnp.swapaxes(small_t, 0, 1))
    return main, tail, small


def kernel(x, positions, norm1_g, w_in, b_gate, conv_w, a_log, dt_bias, a_norm_g, w_proj_a, w_proj_b, w_out,
           norm2_g, w_router_group, b_router_group, w_router_expert, b_router_expert, w_exp_gate, w_exp_up,
           w_exp_down, final_norm_g):
    b, s, d = x.shape
    n = b * s
    assert w_in.shape[0] == 1 and d == D_MODEL, "one layer: the MoE kernel also applies the final norm"
    xc = x.reshape(n, d).astype(F32)
    for l in range(w_in.shape[0]):
        w_main, w_tail, w_small = _layout_w_in(w_in[l])
        z, zs = _in_proj(xc, norm1_g[l][None, :].astype(F32), w_main, w_tail, w_small, positions)
        z3 = z.reshape(b, s, Z_W)
        zs3 = zs.reshape(b, s, 128)
        o_a = _gdn(z3, zs3, conv_w[l], a_log[l], dt_bias[l], a_norm_g[l])
        o_b = _dsa(z3, zs3)
        gap = R_EXPERT - N_GROUPS
        wr = _hi_lo_pair(jnp.concatenate([w_router_group[l], jnp.zeros((d, gap), F32), w_router_expert[l]],
                                         axis=1))
        br = jnp.concatenate([b_router_group[l], jnp.zeros((gap,), F32), b_router_expert[l],
                              jnp.zeros((128 - R_EXPERT - N_EXPERTS,), F32)])[None, :].astype(F32)
        x1, h2, rl = _merge(xc, o_a.reshape(n, -1), o_b.reshape(n, -1), z, b_gate[l][None, :].astype(F32),
                            w_proj_a[l].astype(BF16), w_proj_b[l].astype(BF16), w_out[l].astype(BF16),
                            norm2_g[l][None, :].astype(F32), wr, br)
        xc = _moe(x1, h2, rl, w_exp_gate[l].astype(BF16), w_exp_up[l].astype(BF16),
                  w_exp_down[l].astype(BF16), final_norm_g[None, :].astype(F32))
    return xc.reshape(b, s, d).astype(x.dtype)
```

```python
import numpy as np
import jax
import jax.numpy as jnp
from jax import lax
from jax.experimental import pallas as pl
from jax.experimental.pallas import tpu as pltpu

F32 = jnp.float32
BF16 = jnp.bfloat16

D_MODEL = 1024
CHUNK = 64
EPS = 1e-6
ROPE_THETA = 10000.0
A_HEADS = 8
A_DK = 64
A_DV = 64
CONV_K = 4
B_HEADS = 8
B_KV_HEADS = 2
B_HD = 64
IDX_HEADS = 8
IDX_HD = 64
TOPK_MAX = 256
N_GROUPS = 4
EXPERTS_PER_GROUP = 4
N_EXPERTS = 16
D_EXPERT = 256

O_AQKV, O_AZ, O_GA, O_GB, O_BQ, O_IQ, O_KD, O_VD, O_IK = 0, 1536, 2048, 3072, 4096, 4608, 5120, 5376, 5632
Z_W = 5760
Z_MAIN = O_KD
S_BETA, S_ALPHA, S_IW = 0, 8, 16

GROUP_HEADS = 2
BD = GROUP_HEADS * CHUNK
NEG_BIG = -1e30
LOG2E = 1.4426950408889634
BISECT_STEPS = 18
VMEM_LIMIT = 56 * 1024 * 1024


def _split(x):
    hi = x.astype(BF16)
    lo = (x - hi.astype(F32)).astype(BF16)
    return hi, lo


def _dot(a, b):
    return jnp.dot(a, b, preferred_element_type=F32)


def _mm(a, b):
    return _dot(a.astype(BF16), b.astype(BF16))


def _mm_nt(a, b):
    return lax.dot_general(a.astype(BF16), b.astype(BF16), (((1,), (1,)), ((), ())),
                           preferred_element_type=F32)


def _mm_exact_lhs(a_bf16, x):
    xh, xl = _split(x)
    return _dot(a_bf16, xh) + _dot(a_bf16, xl)


def _rmsnorm_rows(x, g):
    return x * lax.rsqrt(jnp.mean(x * x, axis=-1, keepdims=True) + EPS) * g


_Z_CHUNKS = tuple((o, min(512, Z_W - o)) for o in range(0, Z_W, 512))


def _rope(x, cs, sn, first):
    w = x.shape[1]
    rep = w // 128
    if rep > 1:
        cs, sn, first = (jnp.concatenate([a] * rep, axis=1) for a in (cs, sn, first))
    swapped = jnp.where(first, pltpu.roll(x, w - B_HD // 2, 1), pltpu.roll(x, B_HD // 2, 1))
    return x * cs + swapped * sn


def _in_proj_kernel(x_ref, g_ref, w_ref, wt_ref, ws_ref, pos_ref, inv_ref, z_ref, zs_ref):
    h = _rmsnorm_rows(x_ref[...], g_ref[...])
    hb, hl = _split(h)

    pos = jnp.broadcast_to(pos_ref[...].astype(F32), (128, pos_ref.shape[1]))
    ang = jnp.transpose(pos) * inv_ref[...]
    lane = lax.broadcasted_iota(jnp.int32, ang.shape, 1)
    first = (lane & (B_HD - 1)) < (B_HD // 2)
    cs = jnp.cos(ang)
    sn = jnp.sin(ang)
    sn = jnp.where(first, -sn, sn)

    for o, w in _Z_CHUNKS:
        r = _dot(hb, w_ref[:, o:o + w] if o < Z_MAIN else wt_ref[:, o - Z_MAIN:o - Z_MAIN + w])
        if o == O_BQ:
            r = _rope(r, cs, sn, first) * (B_HD ** -0.5 * LOG2E)
        elif o == O_IQ or o == O_IK:
            r = _rope(r, cs, sn, first)
        elif o == O_KD:
            kw = O_VD - O_KD
            r = jnp.concatenate([_rope(r[:, :kw], cs, sn, first), r[:, kw:]], axis=1)
        z_ref[:, o:o + w] = r.astype(BF16)
    s1 = _dot(hb, ws_ref[...])
    s2 = _dot(hl, ws_ref[:, :128])
    zs_ref[...] = s1[:, :128] + s1[:, 128:] + s2


def _in_proj(x2d, g, w_main, w_tail, w_small, positions, tm=512):
    n = x2d.shape[0]
    assert n % tm == 0 and x2d.shape[1] == D_MODEL
    assert w_main.shape == (D_MODEL, Z_MAIN) and w_tail.shape == (D_MODEL, Z_W - Z_MAIN)
    half = B_HD // 2
    inv = ROPE_THETA ** (-jnp.arange(half, dtype=F32) / half)
    inv128 = jnp.tile(inv, 4)[None, :]
    pos = positions.astype(jnp.int32).reshape(1, n)
    return pl.pallas_call(
        _in_proj_kernel,
        grid=(n // tm,),
        in_specs=[
            pl.BlockSpec((tm, D_MODEL), lambda i: (i, 0)),
            pl.BlockSpec((1, D_MODEL), lambda i: (0, 0)),
            pl.BlockSpec((D_MODEL, Z_MAIN), lambda i: (0, 0)),
            pl.BlockSpec((D_MODEL, Z_W - Z_MAIN), lambda i: (0, 0)),
            pl.BlockSpec((D_MODEL, 256), lambda i: (0, 0)),
            pl.BlockSpec((1, tm), lambda i: (0, i)),
            pl.BlockSpec((1, 128), lambda i: (0, 0)),
        ],
        out_specs=[
            pl.BlockSpec((tm, Z_W), lambda i: (i, 0)),
            pl.BlockSpec((tm, 128), lambda i: (i, 0)),
        ],
        out_shape=[jax.ShapeDtypeStruct((n, Z_W), BF16), jax.ShapeDtypeStruct((n, 128), F32)],
        compiler_params=pltpu.CompilerParams(dimension_semantics=("parallel",),
                                             vmem_limit_bytes=VMEM_LIMIT),
        name="in_proj",
    )(x2d, g, w_main, w_tail, w_small, pos, inv128)


def _gdn_constants():
    r = np.arange(BD)
    same = (r[:, None] // CHUNK) == (r[None, :] // CHUNK)
    incl = same & (r[:, None] >= r[None, :])
    strict = same & (r[:, None] > r[None, :])
    eye = np.eye(BD, dtype=np.float32)
    ll = np.concatenate([incl, same], axis=0).astype(np.float32)
    n_groups = A_HEADS // GROUP_HEADS
    sel = np.zeros((2 * n_groups, BD, 128), np.float32)
    for gi in range(n_groups):
        for h in range(GROUP_HEADS):
            sel[gi * 2 + 0, h * CHUNK:(h + 1) * CHUNK, S_BETA + gi * GROUP_HEADS + h] = 1.0
            sel[gi * 2 + 1, h * CHUNK:(h + 1) * CHUNK, S_ALPHA + gi * GROUP_HEADS + h] = 1.0
    return (jnp.asarray(incl, F32), jnp.asarray(strict, F32), jnp.asarray(same, F32), jnp.asarray(eye),
            jnp.asarray(ll, BF16), jnp.asarray(sel))


def _tile_heads(x):
    return jnp.concatenate([x] * GROUP_HEADS, axis=0)


def _gdn_kernel(zq_ref, zz_ref, zs_ref, cw_ref, av_ref, ag_ref, incl_ref, strict_ref, bdm_ref, eye_ref,
                ll_ref, sel_ref, o_ref, ext_ref, st_ref):
    c = pl.program_id(1)
    nb, t = zq_ref.shape[0], zq_ref.shape[1]

    @pl.when(c == 0)
    def _():
        ext_ref[:, 0:8, :] = jnp.zeros((nb, 8, ext_ref.shape[2]), F32)
        st_ref[...] = jnp.zeros(st_ref.shape, F32)

    hw = A_HEADS * A_DK
    cw = cw_ref[...]
    av = av_ref[...]
    q_all, k_all, v_all, bg = [], [], [], []
    for bi in range(nb):
        ext_ref[bi, 8:8 + t, :] = zq_ref[bi].astype(F32)
        y = cw[0:1, :] * ext_ref[bi, pl.ds(8 - (CONV_K - 1), t), :]
        for j in range(1, CONV_K):
            y = y + cw[j:j + 1, :] * ext_ref[bi, pl.ds(8 - (CONV_K - 1) + j, t), :]
        ext_ref[bi, 0:8, :] = ext_ref[bi, t:t + 8, :]
        y = y * jax.nn.sigmoid(y)
        q_all.append(y[:, :hw])
        k_all.append(y[:, hw:2 * hw])
        v_all.append(y[:, 2 * hw:])

        sm = zs_ref[bi]
        lane = lax.broadcasted_iota(jnp.int32, sm.shape, 1)
        xg = sm + av[1:2, :]
        softplus = jnp.maximum(xg, 0.0) + jnp.log1p(jnp.exp(-jnp.abs(xg)))
        g_all = -jnp.exp(av[0:1, :]) * softplus
        bg.append(jnp.where(lane < S_ALPHA, jax.nn.sigmoid(sm), g_all))

    incl = incl_ref[...]
    strict = strict_ref[...]
    bdm = bdm_ref[...]
    eye = eye_ref[...]
    ll = ll_ref[...]
    incl_b = ll[:BD]

    n_chunks = t // CHUNK
    n_groups = A_HEADS // GROUP_HEADS
    chains = [(bi, ci, gi) for bi in range(nb) for ci in range(n_chunks) for gi in range(n_groups)]

    pre = {}
    for bi, ci, gi in chains:
        r0, c0 = ci * CHUNK, gi * BD
        bg4 = _tile_heads(bg[bi][r0:r0 + CHUNK])
        beta = jnp.sum(bg4 * sel_ref[gi * 2 + 0], axis=1, keepdims=True)
        gcol = jnp.sum(bg4 * sel_ref[gi * 2 + 1], axis=1, keepdims=True)
        gs = _mm_exact_lhs(ll, jnp.broadcast_to(gcol, (BD, 128)))
        g_cum = gs[:BD, :1]
        g_last = gs[BD:, :1]
        diff = _mm_exact_lhs(incl_b, gcol * strict)
        decay = jnp.where(incl > 0.0, jnp.exp(diff), 0.0)
        e_cum = jnp.exp(g_cum)
        kr = _tile_heads(k_all[bi][r0:r0 + CHUNK, c0:c0 + BD]) * bdm
        qr = _tile_heads(q_all[bi][r0:r0 + CHUNK, c0:c0 + BD]) * bdm
        vm = _tile_heads(v_all[bi][r0:r0 + CHUNK, c0:c0 + BD]) * bdm
        km = kr * lax.rsqrt(jnp.sum(kr * kr, axis=1, keepdims=True) + EPS)
        qm = qr * (lax.rsqrt(jnp.sum(qr * qr, axis=1, keepdims=True) + EPS) * (A_DK ** -0.5))
        kkqk = _mm_nt(jnp.concatenate([km, qm], axis=0), km)
        m = -(strict * beta * kkqk[:BD] * decay)
        pre[bi, ci, gi] = dict(beta=beta, g_cum=g_cum, g_last=g_last, e_cum=e_cum, km=km, qm=qm, vm=vm,
                               qk=kkqk[BD:] * decay, m=m, inv=eye + m)

    sq = CHUNK
    while sq > 2:
        for key in chains:
            p = pre[key]
            p["m"] = _mm(p["m"], p["m"])
        for key in chains:
            p = pre[key]
            p["inv"] = p["inv"] + _mm(p["inv"], p["m"])
        sq //= 2

    states = [st_ref[i] for i in range(nb * n_groups)]
    outs = [[] for _ in range(nb)]
    lanes = [(bi, gi) for bi in range(nb) for gi in range(n_groups)]
    for ci in range(n_chunks):
        ps = {k: pre[k[0], ci, k[1]] for k in lanes}
        sidx = {k: k[0] * n_groups + k[1] for k in lanes}
        kq_s = {k: _mm(jnp.concatenate([ps[k]["km"] * ps[k]["e_cum"], ps[k]["qm"] * ps[k]["e_cum"]], axis=0),
                       states[sidx[k]]) for k in lanes}
        v_new = {k: _mm(ps[k]["inv"], ps[k]["beta"] * (ps[k]["vm"] - kq_s[k][:BD])) for k in lanes}
        o_bd = {k: kq_s[k][BD:] + _mm(ps[k]["qk"], v_new[k]) for k in lanes}
        for k in lanes:
            p = ps[k]
            k_dec = p["km"] * jnp.exp(p["g_last"] - p["g_cum"])
            states[sidx[k]] = states[sidx[k]] * jnp.exp(p["g_last"]) + _mm(k_dec.T, v_new[k])
        for bi in range(nb):
            o_groups = []
            for gi in range(n_groups):
                ob = o_bd[bi, gi]
                ob = ob * lax.rsqrt(jnp.sum(ob * ob, axis=1, keepdims=True) * (1.0 / A_DV) + EPS)
                o_groups.append(sum(ob[h * CHUNK:(h + 1) * CHUNK] for h in range(GROUP_HEADS)))
            outs[bi].append(jnp.concatenate(o_groups, axis=1))

    for i, state in enumerate(states):
        st_ref[i] = state
    for bi in range(nb):
        o = outs[bi][0] if n_chunks == 1 else jnp.concatenate(outs[bi], axis=0)
        zz = zz_ref[bi].astype(F32)
        o_ref[bi] = (o * ag_ref[...] * (zz * jax.nn.sigmoid(zz))).astype(BF16)


def _gdn(z3, zs3, conv_w, a_log, dt_bias, a_norm_g, t=4 * CHUNK, nb=2):
    b, s, _ = z3.shape
    assert b % nb == 0 and s % t == 0 and t % CHUNK == 0 and t >= 8
    consts = _gdn_constants()
    av = jnp.zeros((2, 128), F32)
    av = av.at[0, S_ALPHA:S_ALPHA + A_HEADS].set(a_log.astype(F32))
    av = av.at[1, S_ALPHA:S_ALPHA + A_HEADS].set(dt_bias.astype(F32))
    ag = jnp.tile(a_norm_g.astype(F32), A_HEADS)[None, :]
    conv_cols = 2 * A_HEADS * A_DK + A_HEADS * A_DV

    def const_spec(a):
        nd = a.ndim
        return pl.BlockSpec(a.shape, lambda bi, ci, _n=nd: (0,) * _n)

    small_in = (conv_w.astype(F32), av, ag) + consts
    return pl.pallas_call(
        _gdn_kernel,
        grid=(b // nb, s // t),
        in_specs=[
            pl.BlockSpec((nb, t, conv_cols), lambda bi, ci: (bi, ci, O_AQKV // conv_cols)),
            pl.BlockSpec((nb, t, 512), lambda bi, ci: (bi, ci, O_AZ // 512)),
            pl.BlockSpec((nb, t, 128), lambda bi, ci: (bi, ci, 0)),
        ] + [const_spec(a) for a in small_in],
        out_specs=pl.BlockSpec((nb, t, A_HEADS * A_DV), lambda bi, ci: (bi, ci, 0)),
        out_shape=jax.ShapeDtypeStruct((b, s, A_HEADS * A_DV), BF16),
        scratch_shapes=[pltpu.VMEM((nb, 8 + t, conv_cols), F32),
                        pltpu.VMEM((nb * (A_HEADS // GROUP_HEADS), BD, BD), F32)],
        compiler_params=pltpu.CompilerParams(dimension_semantics=("parallel", "arbitrary"),
                                             vmem_limit_bytes=VMEM_LIMIT),
        name="gdn",
    )(z3, z3, zs3, *small_in)


def _dsa_kernel(topk, tk, q_ref, iq_ref, ka_ref, kb_ref, vd_ref, ik_ref, zs_ref, o_ref, st_ref, vt_ref):
    qb = pl.program_id(1)
    s = ka_ref.shape[0]
    tq = q_ref.shape[0]
    nkt_max = s // tk
    r0 = qb * tq
    n_tiles = (r0 + tq + tk - 1) // tk
    heads_per_kv = B_HEADS // B_KV_HEADS

    @pl.when(qb == 0)
    def _():
        for t in range(nkt_max):
            for g in range(B_KV_HEADS):
                v_tile = vd_ref[t * tk:(t + 1) * tk, g * 128:(g + 1) * 128]
                vt_ref[t, g] = v_tile.astype(F32).T.astype(BF16)

    lane128 = lax.broadcasted_iota(jnp.int32, (tq, 128), 1)
    low_half = lane128 < B_HD
    high_half = lane128 >= B_HD

    def head_slab(x, h):
        slab = x[:, (h // 2) * 128:(h // 2 + 1) * 128]
        return jnp.where(low_half if h % 2 == 0 else high_half, slab, jnp.zeros_like(slab))

    zs_t = zs_ref[...].T
    iw_scale = (IDX_HEADS ** -0.5) * (IDX_HD ** -0.5)
    wrows = [zs_t[S_IW + h:S_IW + h + 1, :] * iw_scale for h in range(IDX_HEADS)]
    qrow = r0 + lax.broadcasted_iota(jnp.int32, (1, tq), 1)
    limit = ((qrow >> 6) + 1) << 6
    small = limit <= topk

    iq = iq_ref[...]
    iq_heads = [head_slab(iq, h) for h in range(IDX_HEADS)]

    def index_tile(kt, carry):
        r_max, r_min = carry
        off = pl.multiple_of(kt * tk, tk)
        ikt = ik_ref[pl.ds(off, tk), :]
        acc = jnp.zeros((tk, tq), F32)
        for h in range(IDX_HEADS):
            sc = lax.dot_general(ikt, iq_heads[h], (((1,), (1,)), ((), ())), preferred_element_type=F32)
            acc = acc + wrows[h] * jnp.maximum(sc, 0.0)
        valid = off + lax.broadcasted_iota(jnp.int32, (tk, tq), 0) < limit
        masked = jnp.where(valid, acc, -jnp.inf)
        st_ref[kt] = masked
        r_max = jnp.maximum(r_max, jnp.max(masked, axis=0, keepdims=True))
        r_min = jnp.minimum(r_min, jnp.min(acc, axis=0, keepdims=True))
        return r_max, r_min

    r_max, r_min = lax.fori_loop(0, n_tiles, index_tile,
                                 (jnp.full((1, tq), -jnp.inf, F32), jnp.full((1, tq), jnp.inf, F32)))

    all_kept = r0 + tq <= min(topk, tk)

    @pl.when(all_kept)
    def _():
        st_ref[0] = jnp.where(st_ref[0] > -jnp.inf, 0.0, NEG_BIG)

    for j in range(nkt_max):
        @pl.when(jnp.logical_and(n_tiles == j + 1, jnp.logical_not(all_kept)))
        def _(j=j):
            _topk_mask(topk, j + 1, small, r_max, r_min, st_ref)

    q = q_ref[...]
    nr = heads_per_kv * tq
    qcols = [jnp.concatenate([head_slab(q, g * heads_per_kv + j) for j in range(heads_per_kv)], axis=0)
             for g in range(B_KV_HEADS)]
    k_refs = (ka_ref, kb_ref)

    def attend_tile(kt, carry):
        off = pl.multiple_of(kt * tk, tk)
        bias = st_ref[kt]
        bias4 = jnp.concatenate([bias] * heads_per_kv, axis=1)
        groups = range(B_KV_HEADS)
        logits = [lax.dot_general(k_refs[g][pl.ds(off, tk), :], qcols[g], (((1,), (1,)), ((), ())),
                                  preferred_element_type=F32) + bias4 for g in groups]
        m_new = [jnp.maximum(carry[g][0], jnp.max(logits[g], axis=0, keepdims=True)) for g in groups]
        p = [jnp.exp2(logits[g] - m_new[g]) for g in groups]
        alpha = [jnp.exp2(carry[g][0] - m_new[g]) for g in groups]
        l_new = [alpha[g] * carry[g][1] + jnp.sum(p[g], axis=0, keepdims=True) for g in groups]
        pv = [_dot(vt_ref[kt, g], p[g].astype(BF16)) for g in groups]
        return tuple((m_new[g], l_new[g], alpha[g] * carry[g][2] + pv[g]) for g in groups)

    init = tuple((jnp.full((1, nr), NEG_BIG, F32), jnp.zeros((1, nr), F32), jnp.zeros((128, nr), F32))
                 for _ in range(B_KV_HEADS))
    fin = lax.fori_loop(0, n_tiles, attend_tile, init)
    for g in range(B_KV_HEADS):
        _, l_run, acc = fin[g]
        og = acc / l_run
        heads = [og[:, j * tq:(j + 1) * tq].T for j in range(heads_per_kv)]
        for pp in range(heads_per_kv // 2):
            col = (g * (heads_per_kv // 2) + pp) * 128
            o_ref[:, col:col + 128] = jnp.where(low_half, heads[2 * pp], heads[2 * pp + 1]).astype(BF16)


def _topk_mask(topk, nt, small, r_max, r_min, st_ref):
    _, tk, tq = st_ref.shape
    kf = float(topk)

    def tiles():
        return [st_ref[t] for t in range(nt)]

    def q_sum(pred):
        tot = None
        for t, x in enumerate(tiles()):
            c = jnp.sum(pred(x, t).astype(F32), axis=0, keepdims=True)
            tot = c if tot is None else tot + c
        return tot

    def q_max(val):
        best = None
        for t, x in enumerate(tiles()):
            c = jnp.max(val(x, t), axis=0, keepdims=True)
            best = c if best is None else jnp.maximum(best, c)
        return best

    hi0 = r_max + jnp.maximum(jnp.abs(r_max), 1e-30) * 1e-6

    def bisect(_, carry):
        lo, hi = carry
        mid = 0.5 * (lo + hi)
        ge = q_sum(lambda x, t: x >= mid) >= kf
        return jnp.where(ge, mid, lo), jnp.where(ge, hi, mid)

    lo, hi = lax.fori_loop(0, BISECT_STEPS, bisect, (r_min, hi0))

    def peel_cond(carry):
        return jnp.sum(1.0 - carry[0]) > 0.0

    def peel(carry):
        done, thr, hi_c, n_ge = carry
        v1 = q_max(lambda x, t: jnp.where(x < hi_c, x, -jnp.inf))
        c1 = q_sum(lambda x, t: x >= v1)
        ok = c1 >= kf
        act = done < 0.5
        thr = jnp.where(act & ok, v1, thr)
        n_ge = jnp.where(act & ok, c1, n_ge)
        hi_c = jnp.where(act & (~ok), v1, hi_c)
        return jnp.where(ok, 1.0, done), thr, hi_c, n_ge

    _, thr, _, n_ge = lax.while_loop(peel_cond, peel, (jnp.where(small, 1.0, 0.0), lo, hi, jnp.full_like(lo, kf)))

    def key_index(t):
        return (t * tk + lax.broadcasted_iota(jnp.int32, (tk, tq), 0)).astype(F32)

    contested = jnp.sum(jnp.where((n_ge > kf) & (~small), 1.0, 0.0)) > 0.0
    last = float(nt * tk - 1)

    def tie_cut():
        need = kf - q_sum(lambda x, t: x > thr)

        def tie_search(_, carry):
            jlo, jhi = carry
            mid = jnp.floor(0.5 * (jlo + jhi))
            ge = q_sum(lambda x, t: (x == thr) & (key_index(t) <= mid)) >= need
            return jnp.where(ge, jlo, mid), jnp.where(ge, mid, jhi)

        n_iter = int(np.ceil(np.log2(nt * tk))) + 1
        return lax.fori_loop(0, n_iter, tie_search,
                             (jnp.full((1, tq), -1.0, F32), jnp.full((1, tq), last, F32)))[1]

    jcut = lax.cond(contested, tie_cut, lambda: jnp.full((1, tq), last, F32))
    for t, x in enumerate(tiles()):
        sel = (x > -jnp.inf) & (small | (x > thr) | ((x == thr) & (key_index(t) <= jcut)))
        st_ref[t] = jnp.where(sel, 0.0, NEG_BIG)


def _dsa(z3, zs3, tq=512, tk=512):
    b, s, _ = z3.shape
    assert s % tq == 0 and s % tk == 0 and tq % CHUNK == 0 and tk % CHUNK == 0
    topk = min(TOPK_MAX, s // 4)
    kernel = lambda *refs: _dsa_kernel(topk, tk, *refs)
    qblock = lambda w, cb: pl.BlockSpec((None, tq, w), lambda bi, qi, _c=cb: (bi, qi, _c))
    keys = lambda w, cb: pl.BlockSpec((None, s, w), lambda bi, qi, _c=cb: (bi, 0, _c))
    return pl.pallas_call(
        kernel,
        grid=(b, s // tq),
        in_specs=[qblock(512, O_BQ // 512), qblock(512, O_IQ // 512), keys(128, O_KD // 128),
                  keys(128, O_KD // 128 + 1), keys(256, O_VD // 256), keys(128, O_IK // 128), qblock(128, 0)],
        out_specs=pl.BlockSpec((None, tq, B_HEADS * B_HD), lambda bi, qi: (bi, qi, 0)),
        out_shape=jax.ShapeDtypeStruct((b, s, B_HEADS * B_HD), BF16),
        scratch_shapes=[pltpu.VMEM((s // tk, tk, tq), F32),
                        pltpu.VMEM((s // tk, B_KV_HEADS, 128, tk), BF16)],
        compiler_params=pltpu.CompilerParams(dimension_semantics=("parallel", "arbitrary"),
                                             vmem_limit_bytes=VMEM_LIMIT),
        name="dsa",
    )(z3, z3, z3, z3, z3, z3, zs3)


def _merge_kernel(x_ref, oa_ref, ob_ref, ga_ref, gb_ref, bg_ref, wa_ref, wb_ref, wo_ref, n2_ref, wr_ref,
                  br_ref, x1_ref, h2_ref, rl_ref):
    pa = _dot(oa_ref[...], wa_ref[...])
    pb = _dot(ob_ref[...], wb_ref[...])
    bgv = bg_ref[...]
    ga = jax.nn.sigmoid(ga_ref[...].astype(F32) + bgv[:, :D_MODEL])
    gb = jax.nn.sigmoid(gb_ref[...].astype(F32) + bgv[:, D_MODEL:])
    merged = ga * pa + gb * pb
    x1 = x_ref[...] + _mm(merged, wo_ref[...])
    x1_ref[...] = x1
    h2 = _rmsnorm_rows(x1, n2_ref[...])
    hb, hl = _split(h2)
    h2_ref[...] = hb
    s1 = _dot(hb, wr_ref[...])
    s2 = _dot(hl, wr_ref[:, :128])
    rl_ref[...] = s1[:, :128] + s1[:, 128:] + s2 + br_ref[...]


def _merge(x2d, oa, ob, z2d, b_gate, wa, wb, wo, n2, wr, br, tm=1024):
    n = x2d.shape[0]
    assert n % tm == 0 and oa.shape == ob.shape == (n, 512)
    full =lambda shape: pl.BlockSpec(shape, lambda i: (0, 0))
    return pl.pallas_call(
        _merge_kernel,
        grid=(n // tm,),
        in_specs=[
            pl.BlockSpec((tm, D_MODEL), lambda i: (i, 0)),
            pl.BlockSpec((tm, 512), lambda i: (i, 0)),
            pl.BlockSpec((tm, 512), lambda i: (i, 0)),
            pl.BlockSpec((tm, D_MODEL), lambda i: (i, O_GA // D_MODEL)),
            pl.BlockSpec((tm, D_MODEL), lambda i: (i, O_GB // D_MODEL)),
            full((1, 2 * D_MODEL)), full((512, D_MODEL)), full((512, D_MODEL)), full((D_MODEL, D_MODEL)),
            full((1, D_MODEL)), full((D_MODEL, 256)), full((1, 128)),
        ],
        out_specs=[
            pl.BlockSpec((tm, D_MODEL), lambda i: (i, 0)),
            pl.BlockSpec((tm, D_MODEL), lambda i: (i, 0)),
            pl.BlockSpec((tm, 128), lambda i: (i, 0)),
        ],
        out_shape=[jax.ShapeDtypeStruct((n, D_MODEL), F32), jax.ShapeDtypeStruct((n, D_MODEL), BF16),
                   jax.ShapeDtypeStruct((n, 128), F32)],
        compiler_params=pltpu.CompilerParams(dimension_semantics=("parallel",),
                                             vmem_limit_bytes=VMEM_LIMIT),
        name="merge",
    )(x2d, oa, ob, z2d, z2d, b_gate, wa, wb, wo, n2, wr, br)


R_GROUP, R_EXPERT = 0, 8


def _routing_weights(rl):
    t = rl.T
    n_tok = t.shape[1]
    gl = t[R_GROUP:R_GROUP + N_GROUPS]
    gidx = lax.broadcasted_iota(jnp.int32, gl.shape, 0)
    gmax = jnp.max(gl, axis=0, keepdims=True)
    gsel = jnp.min(jnp.where(gl == gmax, gidx, N_GROUPS), axis=0, keepdims=True)
    ggate = 1.0 / jnp.sum(jnp.exp(gl - gmax), axis=0, keepdims=True)
    el = t[R_EXPERT:R_EXPERT + N_EXPERTS]
    eidx = lax.broadcasted_iota(jnp.int32, el.shape, 0)
    e_lo = gsel * EXPERTS_PER_GROUP
    emask = (eidx >= e_lo) & (eidx < e_lo + EXPERTS_PER_GROUP)
    el = jnp.where(emask, el, -jnp.inf)
    emax = jnp.max(el, axis=0, keepdims=True)
    ee = jnp.where(emask, jnp.exp(el - emax), 0.0)
    ep = jnp.where(emask, ee / jnp.sum(ee, axis=0, keepdims=True), -1.0)
    p1 = jnp.max(ep, axis=0, keepdims=True)
    i1 = jnp.min(jnp.where(ep == p1, eidx, N_EXPERTS), axis=0, keepdims=True)
    ep2 = jnp.where(eidx == i1, -1.0, ep)
    p2 = jnp.max(ep2, axis=0, keepdims=True)
    i2 = jnp.min(jnp.where(ep2 == p2, eidx, N_EXPERTS), axis=0, keepdims=True)
    tot = p1 + p2
    comb_t = (jnp.where(eidx == i1, ggate * (p1 / tot), 0.0)
              + jnp.where(eidx == i2, ggate * (p2 / tot), 0.0))
    full = jnp.concatenate([jnp.zeros((R_EXPERT, n_tok), F32), comb_t,
                            jnp.zeros((128 - R_EXPERT - N_EXPERTS, n_tok), F32)], axis=0)
    return full.T


def _moe_kernel(x1_ref, h2_ref, rl_ref, w1_ref, w3_ref, w2_ref, fg_ref, o_ref, y_ref, comb_ref):
    step = pl.program_id(1)
    per_step = w1_ref.shape[0]

    @pl.when(step == 0)
    def _():
        comb_ref[...] = _routing_weights(rl_ref[...])
        y_ref[...] = jnp.zeros(y_ref.shape, F32)

    h = h2_ref[...]
    comb = comb_ref[...]
    lane = lax.broadcasted_iota(jnp.int32, comb.shape, 1)
    acts = []
    for j in range(per_step):
        a = _dot(h, w1_ref[j])
        b = _dot(h, w3_ref[j])
        ce = jnp.sum(jnp.where(lane == R_EXPERT + step * per_step + j, comb, 0.0), axis=1, keepdims=True)
        acts.append(((a * jax.nn.sigmoid(a)) * b * ce).astype(BF16))
    y_ref[...] += _dot(jnp.concatenate(acts, axis=1), w2_ref[...])

    @pl.when(step == pl.num_programs(1) - 1)
    def _():
        o_ref[...] = _rmsnorm_rows(x1_ref[...] + y_ref[...], fg_ref[...])


def _moe(x1, h2, rl, w1, w3, w2, fg, tm=1024, per_step=4):
    n = x1.shape[0]
    assert n % tm == 0 and N_EXPERTS % per_step == 0 and w1.shape == (N_EXPERTS, D_MODEL, D_EXPERT)
    w2g =w2.reshape(N_EXPERTS // per_step, per_step * D_EXPERT, D_MODEL)
    return pl.pallas_call(
        _moe_kernel,
        grid=(n // tm, N_EXPERTS // per_step),
        in_specs=[
            pl.BlockSpec((tm, D_MODEL), lambda i, e: (i, 0)),
            pl.BlockSpec((tm, D_MODEL), lambda i, e: (i, 0)),
            pl.BlockSpec((tm, 128), lambda i, e: (i, 0)),
            pl.BlockSpec((per_step, D_MODEL, D_EXPERT), lambda i, e: (e, 0, 0)),
            pl.BlockSpec((per_step, D_MODEL, D_EXPERT), lambda i, e: (e, 0, 0)),
            pl.BlockSpec((None, per_step * D_EXPERT, D_MODEL), lambda i, e: (e, 0, 0)),
            pl.BlockSpec((1, D_MODEL), lambda i, e: (0, 0)),
        ],
        out_specs=pl.BlockSpec((tm, D_MODEL), lambda i, e: (i, 0)),
        out_shape=jax.ShapeDtypeStruct((n, D_MODEL), F32),
        scratch_shapes=[pltpu.VMEM((tm, D_MODEL), F32), pltpu.VMEM((tm, 128), F32)],
        compiler_params=pltpu.CompilerParams(dimension_semantics=("parallel", "arbitrary"),
                                             vmem_limit_bytes=VMEM_LIMIT),
        name="moe",
    )(x1, h2, rl, w1, w3, w2g, fg)


_W_OFF = {}
_off = 0
for _name, _n in (("a_q", 512), ("a_k", 512), ("a_v", 512), ("a_z", 512), ("a_beta", 8), ("a_alpha", 8),
                  ("b_q", 512), ("b_k", 128), ("b_v", 128), ("i_q", 512), ("i_k", 64), ("i_w", 8),
                  ("gate_a", 1024), ("gate_b", 1024)):
    _W_OFF[_name] = (_off, _off + _n)
    _off += _n


def _hi_lo_pair(w_small):
    pad = jnp.zeros((w_small.shape[0], 128 - w_small.shape[1]), F32)
    w = jnp.concatenate([w_small.astype(F32), pad], axis=1)
    hi, lo = _split(w)
    return jnp.concatenate([hi, lo], axis=1)


def _transpose_cast_kernel(off_ref, wt_ref, out_ref):
    del off_ref
    blk = wt_ref[...]
    reps = out_ref.shape[1] // blk.shape[0]
    if reps > 1:
        blk = jnp.concatenate([blk] * reps, axis=0)
    out_ref[...] = blk.T.astype(BF16)


def _gather_columns(wt, offsets, rows, out_cols):
    d = wt.shape[1]
    grid_spec = pltpu.PrefetchScalarGridSpec(
        num_scalar_prefetch=1,
        grid=(len(offsets),),
        in_specs=[pl.BlockSpec((pl.Element(rows), pl.Element(d)), lambda u, off: (pl.multiple_of(off[u], 8), 0))],
        out_specs=pl.BlockSpec((d, out_cols), lambda u, off: (0, u)),
    )
    return pl.pallas_call(
        _transpose_cast_kernel,
        grid_spec=grid_spec,
        out_shape=jax.ShapeDtypeStruct((d, out_cols * len(offsets)), BF16),
        compiler_params=pltpu.CompilerParams(dimension_semantics=("arbitrary",), vmem_limit_bytes=VMEM_LIMIT),
        name="w_layout",
    )(jnp.asarray(offsets, jnp.int32), wt)


def _layout_w_in(w):
    wt = jnp.swapaxes(w, 0, 1)
    start = lambda name: _W_OFF[name][0]
    assert all(start(nm) % 8 == 0 for nm in ("a_q", "gate_a", "b_q", "i_q", "b_k", "b_v", "i_k"))
    main_off = ([start("a_q") + 512 * i for i in range(4)] + [start("gate_a") + 512 * i for i in range(4)]
                + [start("b_q"), start("i_q")])
    tail_off = [start("b_k"), start("b_k") + 64, start("b_v"), start("b_v") + 64, start("i_k")]
    main = _gather_columns(wt, main_off, 512, 512)
    tail = _gather_columns(wt, tail_off, 64, 128)
    small_t = jnp.concatenate([wt[slice(*_W_OFF[nm])] for nm in ("a_beta", "a_alpha", "i_w")], axis=0)
    small = _hi_lo_pair(jnp.swapaxes(small_t, 0, 1))
    return main, tail, small


def kernel(x, positions, norm1_g, w_in, b_gate, conv_w, a_log, dt_bias, a_norm_g, w_proj_a, w_proj_b, w_out,
           norm2_g, w_router_group, b_router_group, w_router_expert, b_router_expert, w_exp_gate, w_exp_up,
           w_exp_down, final_norm_g):
    b, s, d = x.shape
    n = b * s
    assert w_in.shape[0] == 1 and d == D_MODEL, "one layer: the MoE kernel also applies the final norm"
    xc = x.reshape(n, d).astype(F32)
    for l in range(w_in.shape[0]):
        w_main, w_tail, w_small = _layout_w_in(w_in[l])
        z, zs = _in_proj(xc, norm1_g[l][None, :].astype(F32), w_main, w_tail, w_small, positions)
        z3 = z.reshape(b, s, Z_W)
        zs3 = zs.reshape(b, s, 128)
        o_a = _gdn(z3, zs3, conv_w[l], a_log[l], dt_bias[l], a_norm_g[l])
        o_b = _dsa(z3, zs3)
        gap = R_EXPERT - N_GROUPS
        wr = _hi_lo_pair(jnp.concatenate([w_router_group[l], jnp.zeros((d, gap), F32), w_router_expert[l]],
                                         axis=1))
        br = jnp.concatenate([b_router_group[l], jnp.zeros((gap,), F32), b_router_expert[l],
                              jnp.zeros((128 - R_EXPERT - N_EXPERTS,), F32)])[None, :].astype(F32)
        x1, h2, rl = _merge(xc, o_a.reshape(n, -1), o_b.reshape(n, -1), z, b_gate[l][None, :].astype(F32),
                            w_proj_a[l].astype(BF16), w_proj_b[l].astype(BF16), w_out[l].astype(BF16),
                            norm2_g[l][None, :].astype(F32), wr, br)
        xc = _moe(x1, h2, rl, w_exp_gate[l].astype(BF16), w_exp_up[l].astype(BF16),
                  w_exp_down[l].astype(BF16), final_norm_g[None, :].astype(F32))
    return xc.reshape(b, s, d).astype(x.dtype)
```

```python
import numpy as np
import jax
import jax.numpy as jnp
from jax import lax
from jax.experimental import pallas as pl
from jax.experimental.pallas import tpu as pltpu

F32 = jnp.float32
BF16 = jnp.bfloat16

D_MODEL = 1024
CHUNK = 64
EPS = 1e-6
ROPE_THETA = 10000.0
A_HEADS = 8
A_DK = 64
A_DV = 64
CONV_K = 4
B_HEADS = 8
B_KV_HEADS = 2
B_HD = 64
IDX_HEADS = 8
IDX_HD = 64
TOPK_MAX = 256
N_GROUPS = 4
EXPERTS_PER_GROUP = 4
N_EXPERTS = 16
D_EXPERT = 256

O_AQKV, O_AZ, O_GA, O_GB, O_BQ, O_IQ, O_KD, O_VD, O_IK = 0, 1536, 2048, 3072, 4096, 4608, 5120, 5376, 5632
Z_W = 5760
Z_MAIN = O_KD
S_BETA, S_ALPHA, S_IW = 0, 8, 16

GROUP_HEADS = 2
BD = GROUP_HEADS * CHUNK
NEG_BIG = -1e30
LOG2E = 1.4426950408889634
BISECT_STEPS = 18
VMEM_LIMIT = 56 * 1024 * 1024


def _split(x):
    hi = x.astype(BF16)
    lo = (x - hi.astype(F32)).astype(BF16)
    return hi, lo


def _dot(a, b):
    return jnp.dot(a, b, preferred_element_type=F32)


def _mm(a, b):
    return _dot(a.astype(BF16), b.astype(BF16))


def _mm_nt(a, b):
    return lax.dot_general(a.astype(BF16), b.astype(BF16), (((1,), (1,)), ((), ())),
                           preferred_element_type=F32)


def _mm_exact_lhs(a_bf16, x):
    xh, xl = _split(x)
    return _dot(a_bf16, xh) + _dot(a_bf16, xl)


def _rmsnorm_rows(x, g):
    return x * lax.rsqrt(jnp.mean(x * x, axis=-1, keepdims=True) + EPS) * g


_Z_CHUNKS = tuple((o, min(512, Z_W - o)) for o in range(0, Z_W, 512))


def _rope(x, cs, sn, first):
    w = x.shape[1]
    rep = w // 128
    if rep > 1:
        cs, sn, first = (jnp.concatenate([a] * rep, axis=1) for a in (cs, sn, first))
    swapped = jnp.where(first, pltpu.roll(x, w - B_HD // 2, 1), pltpu.roll(x, B_HD // 2, 1))
    return x * cs + swapped * sn


def _in_proj_kernel(x_ref, g_ref, w_ref, wt_ref, ws_ref, pos_ref, inv_ref, z_ref, zs_ref):
    h = _rmsnorm_rows(x_ref[...], g_ref[...])
    hb, hl = _split(h)

    pos = jnp.broadcast_to(pos_ref[...].astype(F32), (128, pos_ref.shape[1]))
    ang = jnp.transpose(pos) * inv_ref[...]
    lane = lax.broadcasted_iota(jnp.int32, ang.shape, 1)
    first = (lane & (B_HD - 1)) < (B_HD // 2)
    cs = jnp.cos(ang)
    sn = jnp.sin(ang)
    sn = jnp.where(first, -sn, sn)

    for o, w in _Z_CHUNKS:
        r = _dot(hb, w_ref[:, o:o + w] if o < Z_MAIN else wt_ref[:, o - Z_MAIN:o - Z_MAIN + w])
        if o == O_BQ:
            r = _rope(r, cs, sn, first) * (B_HD ** -0.5 * LOG2E)
        elif o == O_IQ or o == O_IK:
            r = _rope(r, cs, sn, first)
        elif o == O_KD:
            kw = O_VD - O_KD
            r = jnp.concatenate([_rope(r[:, :kw], cs, sn, first), r[:, kw:]], axis=1)
        z_ref[:, o:o + w] = r.astype(BF16)
    s1 = _dot(hb, ws_ref[...])
    s2 = _dot(hl, ws_ref[:, :128])
    zs_ref[...] = s1[:, :128] + s1[:, 128:] + s2


def _in_proj(x2d, g, w_main, w_tail, w_small, positions, tm=512):
    n = x2d.shape[0]
    assert n % tm == 0 and x2d.shape[1] == D_MODEL
    assert w_main.shape == (D_MODEL, Z_MAIN) and w_tail.shape == (D_MODEL, Z_W - Z_MAIN)
    half = B_HD // 2
    inv = ROPE_THETA ** (-jnp.arange(half, dtype=F32) / half)
    inv128 = jnp.tile(inv, 4)[None, :]
    pos = positions.astype(jnp.int32).reshape(1, n)
    return pl.pallas_call(
        _in_proj_kernel,
        grid=(n // tm,),
        in_specs=[
            pl.BlockSpec((tm, D_MODEL), lambda i: (i, 0)),
            pl.BlockSpec((1, D_MODEL), lambda i: (0, 0)),
            pl.BlockSpec((D_MODEL, Z_MAIN), lambda i: (0, 0)),
            pl.BlockSpec((D_MODEL, Z_W - Z_MAIN), lambda i: (0, 0)),
            pl.BlockSpec((D_MODEL, 256), lambda i: (0, 0)),
            pl.BlockSpec((1, tm), lambda i: (0, i)),
            pl.BlockSpec((1, 128), lambda i: (0, 0)),
        ],
        out_specs=[
            pl.BlockSpec((tm, Z_W), lambda i: (i, 0)),
            pl.BlockSpec((tm, 128), lambda i: (i, 0)),
        ],
        out_shape=[jax.ShapeDtypeStruct((n, Z_W), BF16), jax.ShapeDtypeStruct((n, 128), F32)],
        compiler_params=pltpu.CompilerParams(dimension_semantics=("parallel",),
                                             vmem_limit_bytes=VMEM_LIMIT),
        name="in_proj",
    )(x2d, g, w_main, w_tail, w_small, pos, inv128)


def _gdn_constants():
    r = np.arange(BD)
    same = (r[:, None] // CHUNK) == (r[None, :] // CHUNK)
    incl = same & (r[:, None] >= r[None, :])
    strict = same & (r[:, None] > r[None, :])
    eye = np.eye(BD, dtype=np.float32)
    ll = np.concatenate([incl, same], axis=0).astype(np.float32)
    n_groups = A_HEADS // GROUP_HEADS
    sel = np.zeros((2 * n_groups, BD, 128), np.float32)
    for gi in range(n_groups):
        for h in range(GROUP_HEADS):
            sel[gi * 2 + 0, h * CHUNK:(h + 1) * CHUNK, S_BETA + gi * GROUP_HEADS + h] = 1.0
            sel[gi * 2 + 1, h * CHUNK:(h + 1) * CHUNK, S_ALPHA + gi * GROUP_HEADS + h] = 1.0
    return (jnp.asarray(incl, F32), jnp.asarray(strict, F32), jnp.asarray(same, F32), jnp.asarray(eye),
            jnp.asarray(ll, BF16), jnp.asarray(sel))


def _tile_heads(x):
    return jnp.concatenate([x] * GROUP_HEADS, axis=0)


def _gdn_kernel(zq_ref, zz_ref, zs_ref, cw_ref, av_ref, ag_ref, incl_ref, strict_ref, bdm_ref, eye_ref,
                ll_ref, sel_ref, o_ref, ext_ref, st_ref):
    c = pl.program_id(1)
    nb, t = zq_ref.shape[0], zq_ref.shape[1]

    @pl.when(c == 0)
    def _():
        ext_ref[:, 0:8, :] = jnp.zeros((nb, 8, ext_ref.shape[2]), F32)
        st_ref[...] = jnp.zeros(st_ref.shape, F32)

    hw = A_HEADS * A_DK
    cw = cw_ref[...]
    av = av_ref[...]
    q_all, k_all, v_all, bg = [], [], [], []
    for bi in range(nb):
        ext_ref[bi, 8:8 + t, :] = zq_ref[bi].astype(F32)
        y = cw[0:1, :] * ext_ref[bi, pl.ds(8 - (CONV_K - 1), t), :]
        for j in range(1, CONV_K):
            y = y + cw[j:j + 1, :] * ext_ref[bi, pl.ds(8 - (CONV_K - 1) + j, t), :]
        ext_ref[bi, 0:8, :] = ext_ref[bi, t:t + 8, :]
        y = y * jax.nn.sigmoid(y)
        q_all.append(y[:, :hw])
        k_all.append(y[:, hw:2 * hw])
        v_all.append(y[:, 2 * hw:])

        sm = zs_ref[bi]
        lane = lax.broadcasted_iota(jnp.int32, sm.shape, 1)
        xg = sm + av[1:2, :]
        softplus = jnp.maximum(xg, 0.0) + jnp.log1p(jnp.exp(-jnp.abs(xg)))
        g_all = -jnp.exp(av[0:1, :]) * softplus
        bg.append(jnp.where(lane < S_ALPHA, jax.nn.sigmoid(sm), g_all))

    incl = incl_ref[...]
    strict = strict_ref[...]
    bdm = bdm_ref[...]
    eye = eye_ref[...]
    ll = ll_ref[...]
    incl_b = ll[:BD]

    n_chunks = t // CHUNK
    n_groups = A_HEADS // GROUP_HEADS
    chains = [(bi, ci, gi) for bi in range(nb) for ci in range(n_chunks) for gi in range(n_groups)]

    pre = {}
    for bi, ci, gi in chains:
        r0, c0 = ci * CHUNK, gi * BD
        bg4 = _tile_heads(bg[bi][r0:r0 + CHUNK])
        beta = jnp.sum(bg4 * sel_ref[gi * 2 + 0], axis=1, keepdims=True)
        gcol = jnp.sum(bg4 * sel_ref[gi * 2 + 1], axis=1, keepdims=True)
        gs = _mm_exact_lhs(ll, jnp.broadcast_to(gcol, (BD, 128)))
        g_cum = gs[:BD, :1]
        g_last = gs[BD:, :1]
        diff = _mm_exact_lhs(incl_b, gcol * strict)
        decay = jnp.where(incl > 0.0, jnp.exp(diff), 0.0)
        e_cum = jnp.exp(g_cum)
        kr = _tile_heads(k_all[bi][r0:r0 + CHUNK, c0:c0 + BD]) * bdm
        qr = _tile_heads(q_all[bi][r0:r0 + CHUNK, c0:c0 + BD]) * bdm
        vm = _tile_heads(v_all[bi][r0:r0 + CHUNK, c0:c0 + BD]) * bdm
        km = kr * lax.rsqrt(jnp.sum(kr * kr, axis=1, keepdims=True) + EPS)
        qm = qr * (lax.rsqrt(jnp.sum(qr * qr, axis=1, keepdims=True) + EPS) * (A_DK ** -0.5))
        kkqk = _mm_nt(jnp.concatenate([km, qm], axis=0), km)
        m = -(strict * beta * kkqk[:BD] * decay)
        pre[bi, ci, gi] = dict(beta=beta, g_cum=g_cum, g_last=g_last, e_cum=e_cum, km=km, qm=qm, vm=vm,
                               qk=kkqk[BD:] * decay, m=m, inv=eye + m)

    sq = CHUNK
    while sq > 2:
        for key in chains:
            p = pre[key]
            p["m"] = _mm(p["m"], p["m"])
        for key in chains:
            p = pre[key]
            p["inv"] = p["inv"] + _mm(p["inv"], p["m"])
        sq //= 2

    states = [st_ref[i] for i in range(nb * n_groups)]
    outs = [[] for _ in range(nb)]
    lanes = [(bi, gi) for bi in range(nb) for gi in range(n_groups)]
    for ci in range(n_chunks):
        ps = {k: pre[k[0], ci, k[1]] for k in lanes}
        sidx = {k: k[0] * n_groups + k[1] for k in lanes}
        kq_s = {k: _mm(jnp.concatenate([ps[k]["km"] * ps[k]["e_cum"], ps[k]["qm"] * ps[k]["e_cum"]], axis=0),
                       states[sidx[k]]) for k in lanes}
        v_new = {k: _mm(ps[k]["inv"], ps[k]["beta"] * (ps[k]["vm"] - kq_s[k][:BD])) for k in lanes}
        o_bd = {k: kq_s[k][BD:] + _mm(ps[k]["qk"], v_new[k]) for k in lanes}
        for k in lanes:
            p = ps[k]
            k_dec = p["km"] * jnp.exp(p["g_last"] - p["g_cum"])
            states[sidx[k]] = states[sidx[k]] * jnp.exp(p["g_last"]) + _mm(k_dec.T, v_new[k])
        for bi in range(nb):
            o_groups = []
            for gi in range(n_groups):
                ob = o_bd[bi, gi]
                ob = ob * lax.rsqrt(jnp.sum(ob * ob, axis=1, keepdims=True) * (1.0 / A_DV) + EPS)
                o_groups.append(sum(ob[h * CHUNK:(h + 1) * CHUNK] for h in range(GROUP_HEADS)))
            outs[bi].append(jnp.concatenate(o_groups, axis=1))

    for i, state in enumerate(states):
        st_ref[i] = state
    for bi in range(nb):
        o = outs[bi][0] if n_chunks == 1 else jnp.concatenate(outs[bi], axis=0)
        zz = zz_ref[bi].astype(F32)
        o_ref[bi] = (o * ag_ref[...] * (zz * jax.nn.sigmoid(zz))).astype(BF16)


def _gdn(z3, zs3, conv_w, a_log, dt_bias, a_norm_g, t=2 * CHUNK, nb=2):
    b, s, _ = z3.shape
    assert b % nb == 0 and s % t == 0 and t % CHUNK == 0 and t >= 8
    consts = _gdn_constants()
    av = jnp.zeros((2, 128), F32)
    av = av.at[0, S_ALPHA:S_ALPHA + A_HEADS].set(a_log.astype(F32))
    av = av.at[1, S_ALPHA:S_ALPHA + A_HEADS].set(dt_bias.astype(F32))
    ag = jnp.tile(a_norm_g.astype(F32), A_HEADS)[None, :]
    conv_cols = 2 * A_HEADS * A_DK + A_HEADS * A_DV

    def const_spec(a):
        nd = a.ndim
        return pl.BlockSpec(a.shape, lambda bi, ci, _n=nd: (0,) * _n)

    small_in = (conv_w.astype(F32), av, ag) + consts
    return pl.pallas_call(
        _gdn_kernel,
        grid=(b // nb, s // t),
        in_specs=[
            pl.BlockSpec((nb, t, conv_cols), lambda bi, ci: (bi, ci, O_AQKV // conv_cols)),
            pl.BlockSpec((nb, t, 512), lambda bi, ci: (bi, ci, O_AZ // 512)),
            pl.BlockSpec((nb, t, 128), lambda bi, ci: (bi, ci, 0)),
        ] + [const_spec(a) for a in small_in],
        out_specs=pl.BlockSpec((nb, t, A_HEADS * A_DV), lambda bi, ci: (bi, ci, 0)),
        out_shape=jax.ShapeDtypeStruct((b, s, A_HEADS * A_DV), BF16),
        scratch_shapes=[pltpu.VMEM((nb, 8 + t, conv_cols), F32),
                        pltpu.VMEM((nb * (A_HEADS // GROUP_HEADS), BD, BD), F32)],
        compiler_params=pltpu.CompilerParams(dimension_semantics=("parallel", "arbitrary"),
                                             vmem_limit_bytes=VMEM_LIMIT),
        name="gdn",
    )(z3, z3, zs3, *small_in)


def _dsa_kernel(topk, tk, q_ref, iq_ref, ka_ref, kb_ref, vd_ref, ik_ref, zs_ref, o_ref, st_ref, vt_ref):
    qb = pl.program_id(1)
    s = ka_ref.shape[0]
    tq = q_ref.shape[0]
    nkt_max = s // tk
    r0 = qb * tq
    n_tiles = (r0 + tq + tk - 1) // tk
    heads_per_kv = B_HEADS // B_KV_HEADS

    @pl.when(qb == 0)
    def _():
        for t in range(nkt_max):
            for g in range(B_KV_HEADS):
                v_tile = vd_ref[t * tk:(t + 1) * tk, g * 128:(g + 1) * 128]
                vt_ref[t, g] = v_tile.astype(F32).T.astype(BF16)

    lane128 = lax.broadcasted_iota(jnp.int32, (tq, 128), 1)
    low_half = lane128 < B_HD
    high_half = lane128 >= B_HD

    def head_slab(x, h):
        slab = x[:, (h // 2) * 128:(h // 2 + 1) * 128]
        return jnp.where(low_half if h % 2 == 0 else high_half, slab, jnp.zeros_like(slab))

    zs_t = zs_ref[...].T
    iw_scale = (IDX_HEADS ** -0.5) * (IDX_HD ** -0.5)
    wrows = [zs_t[S_IW + h:S_IW + h + 1, :] * iw_scale for h in range(IDX_HEADS)]
    qrow = r0 + lax.broadcasted_iota(jnp.int32, (1, tq), 1)
    limit = ((qrow >> 6) + 1) << 6
    small = limit <= topk

    iq = iq_ref[...]
    iq_heads = [head_slab(iq, h) for h in range(IDX_HEADS)]

    def index_tile(kt, carry):
        r_max, r_min = carry
        off = pl.multiple_of(kt * tk, tk)
        ikt = ik_ref[pl.ds(off, tk), :]
        acc = jnp.zeros((tk, tq), F32)
        for h in range(IDX_HEADS):
            sc = lax.dot_general(ikt, iq_heads[h], (((1,), (1,)), ((), ())), preferred_element_type=F32)
            acc = acc + wrows[h] * jnp.maximum(sc, 0.0)
        valid = off + lax.broadcasted_iota(jnp.int32, (tk, tq), 0) < limit
        masked = jnp.where(valid, acc, -jnp.inf)
        st_ref[kt] = masked
        r_max = jnp.maximum(r_max, jnp.max(masked, axis=0, keepdims=True))
        r_min = jnp.minimum(r_min, jnp.min(acc, axis=0, keepdims=True))
        return r_max, r_min

    r_max, r_min = lax.fori_loop(0, n_tiles, index_tile,
                                 (jnp.full((1, tq), -jnp.inf, F32), jnp.full((1, tq), jnp.inf, F32)))

    all_kept = r0 + tq <= min(topk, tk)

    @pl.when(all_kept)
    def _():
        st_ref[0] = jnp.where(st_ref[0] > -jnp.inf, 0.0, NEG_BIG)

    for j in range(nkt_max):
        @pl.when(jnp.logical_and(n_tiles == j + 1, jnp.logical_not(all_kept)))
        def _(j=j):
            _topk_mask(topk, j + 1, small, r_max, r_min, st_ref)

    q = q_ref[...]
    nr = heads_per_kv * tq
    qcols = [jnp.concatenate([head_slab(q, g * heads_per_kv + j) for j in range(heads_per_kv)], axis=0)
             for g in range(B_KV_HEADS)]
    k_refs = (ka_ref, kb_ref)

    def attend_tile(kt, carry):
        off = pl.multiple_of(kt * tk, tk)
        bias = st_ref[kt]
        bias4 = jnp.concatenate([bias] * heads_per_kv, axis=1)
        groups = range(B_KV_HEADS)
        logits = [lax.dot_general(k_refs[g][pl.ds(off, tk), :], qcols[g], (((1,), (1,)), ((), ())),
                                  preferred_element_type=F32) + bias4 for g in groups]
        m_new = [jnp.maximum(carry[g][0], jnp.max(logits[g], axis=0, keepdims=True)) for g in groups]
        p = [jnp.exp2(logits[g] - m_new[g]) for g in groups]
        alpha = [jnp.exp2(carry[g][0] - m_new[g]) for g in groups]
        l_new = [alpha[g] * carry[g][1] + jnp.sum(p[g], axis=0, keepdims=True) for g in groups]
        pv = [_dot(vt_ref[kt, g], p[g].astype(BF16)) for g in groups]
        return tuple((m_new[g], l_new[g], alpha[g] * carry[g][2] + pv[g]) for g in groups)

    init = tuple((jnp.full((1, nr), NEG_BIG, F32), jnp.zeros((1, nr), F32), jnp.zeros((128, nr), F32))
                 for _ in range(B_KV_HEADS))
    fin = lax.fori_loop(0, n_tiles, attend_tile, init)
    for g in range(B_KV_HEADS):
        _, l_run, acc = fin[g]
        og = acc / l_run
        heads = [og[:, j * tq:(j + 1) * tq].T for j in range(heads_per_kv)]
        for pp in range(heads_per_kv // 2):
            col = (g * (heads_per_kv // 2) + pp) * 128
            o_ref[:, col:col + 128] = jnp.where(low_half, heads[2 * pp], heads[2 * pp + 1]).astype(BF16)


def _topk_mask(topk, nt, small, r_max, r_min, st_ref):
    _, tk, tq = st_ref.shape
    kf = float(topk)

    def tiles():
        return [st_ref[t] for t in range(nt)]

    def q_sum(pred):
        tot = None
        for t, x in enumerate(tiles()):
            c = jnp.sum(pred(x, t).astype(F32), axis=0, keepdims=True)
            tot = c if tot is None else tot + c
        return tot

    def q_max(val):
        best = None
        for t, x in enumerate(tiles()):
            c = jnp.max(val(x, t), axis=0, keepdims=True)
            best = c if best is None else jnp.maximum(best, c)
        return best

    hi0 = r_max + jnp.maximum(jnp.abs(r_max), 1e-30) * 1e-6

    def bisect(_, carry):
        lo, hi = carry
        mid = 0.5 * (lo + hi)
        ge = q_sum(lambda x, t: x >= mid) >= kf
        return jnp.where(ge, mid, lo), jnp.where(ge, hi, mid)

    lo, hi = lax.fori_loop(0, BISECT_STEPS, bisect, (r_min, hi0))

    def peel_cond(carry):
        return jnp.sum(1.0 - carry[0]) > 0.0

    def peel(carry):
        done, thr, hi_c, n_ge = carry
        v1 = q_max(lambda x, t: jnp.where(x < hi_c, x, -jnp.inf))
        c1 = q_sum(lambda x, t: x >= v1)
        ok = c1 >= kf
        act = done < 0.5
        thr = jnp.where(act & ok, v1, thr)
        n_ge = jnp.where(act & ok, c1, n_ge)
        hi_c = jnp.where(act & (~ok), v1, hi_c)
        return jnp.where(ok, 1.0, done), thr, hi_c, n_ge

    _, thr, _, n_ge = lax.while_loop(peel_cond, peel, (jnp.where(small, 1.0, 0.0), lo, hi, jnp.full_like(lo, kf)))

    def key_index(t):
        return (t * tk + lax.broadcasted_iota(jnp.int32, (tk, tq), 0)).astype(F32)

    contested = jnp.sum(jnp.where((n_ge > kf) & (~small), 1.0, 0.0)) > 0.0
    last = float(nt * tk - 1)

    def tie_cut():
        need = kf - q_sum(lambda x, t: x > thr)

        def tie_search(_, carry):
            jlo, jhi = carry
            mid = jnp.floor(0.5 * (jlo + jhi))
            ge = q_sum(lambda x, t: (x == thr) & (key_index(t) <= mid)) >= need
            return jnp.where(ge, jlo, mid), jnp.where(ge, mid, jhi)

        n_iter = int(np.ceil(np.log2(nt * tk))) + 1
        return lax.fori_loop(0, n_iter, tie_search,
                             (jnp.full((1, tq), -1.0, F32), jnp.full((1, tq), last, F32)))[1]

    jcut = lax.cond(contested, tie_cut, lambda: jnp.full((1, tq), last, F32))
    for t, x in enumerate(tiles()):
        sel = (x > -jnp.inf) & (small | (x > thr) | ((x == thr) & (key_index(t) <= jcut)))
        st_ref[t] = jnp.where(sel, 0.0, NEG_BIG)


def _dsa(z3, zs3, tq=512, tk=512):
    b, s, _ = z3.shape
    assert s % tq == 0 and s % tk == 0 and tq % CHUNK == 0 and tk % CHUNK == 0
    topk = min(TOPK_MAX, s // 4)
    kernel = lambda *refs: _dsa_kernel(topk, tk, *refs)
    qblock = lambda w, cb: pl.BlockSpec((None, tq, w), lambda bi, qi, _c=cb: (bi, qi, _c))
    keys = lambda w, cb: pl.BlockSpec((None, s, w), lambda bi, qi, _c=cb: (bi, 0, _c))
    return pl.pallas_call(
        kernel,
        grid=(b, s // tq),
        in_specs=[qblock(512, O_BQ // 512), qblock(512, O_IQ // 512), keys(128, O_KD // 128),
                  keys(128, O_KD // 128 + 1), keys(256, O_VD // 256), keys(128, O_IK // 128), qblock(128, 0)],
        out_specs=pl.BlockSpec((None, tq, B_HEADS * B_HD), lambda bi, qi: (bi, qi, 0)),
        out_shape=jax.ShapeDtypeStruct((b, s, B_HEADS * B_HD), BF16),
        scratch_shapes=[pltpu.VMEM((s // tk, tk, tq), F32),
                        pltpu.VMEM((s // tk, B_KV_HEADS, 128, tk), BF16)],
        compiler_params=pltpu.CompilerParams(dimension_semantics=("parallel", "arbitrary"),
                                             vmem_limit_bytes=VMEM_LIMIT),
        name="dsa",
    )(z3, z3, z3, z3, z3, z3, zs3)


def _merge_kernel(x_ref, oa_ref, ob_ref, ga_ref, gb_ref, bg_ref, wa_ref, wb_ref, wo_ref, n2_ref, wr_ref,
                  br_ref, x1_ref, h2_ref, rl_ref, was_ref, wbs_ref, wos_ref):
    @pl.when(pl.program_id(0) == 0)
    def _():
        was_ref[...] = wa_ref[...].astype(BF16)
        wbs_ref[...] = wb_ref[...].astype(BF16)
        wos_ref[...] = wo_ref[...].astype(BF16)

    pa = _dot(oa_ref[...], was_ref[...])
    pb = _dot(ob_ref[...], wbs_ref[...])
    bgv = bg_ref[...]
    ga = jax.nn.sigmoid(ga_ref[...].astype(F32) + bgv[:, :D_MODEL])
    gb = jax.nn.sigmoid(gb_ref[...].astype(F32) + bgv[:, D_MODEL:])
    merged = ga * pa + gb * pb
    x1 = x_ref[...] + _mm(merged, wos_ref[...])
    x1_ref[...] = x1
    h2 = _rmsnorm_rows(x1, n2_ref[...])
    hb, hl = _split(h2)
    h2_ref[...] = hb
    s1 = _dot(hb, wr_ref[...])
    s2 = _dot(hl, wr_ref[:, :128])
    rl_ref[...] = s1[:, :128] + s1[:, 128:] + s2 + br_ref[...]


def _merge(x2d, oa, ob, z2d, b_gate, wa, wb, wo, n2, wr, br, tm=1024):
    n = x2d.shape[0]
    assert n % tm == 0 and oa.shape == ob.shape == (n, 512)
    full =lambda shape: pl.BlockSpec(shape, lambda i: (0, 0))
    return pl.pallas_call(
        _merge_kernel,
        grid=(n // tm,),
        in_specs=[
            pl.BlockSpec((tm, D_MODEL), lambda i: (i, 0)),
            pl.BlockSpec((tm, 512), lambda i: (i, 0)),
            pl.BlockSpec((tm, 512), lambda i: (i, 0)),
            pl.BlockSpec((tm, D_MODEL), lambda i: (i, O_GA // D_MODEL)),
            pl.BlockSpec((tm, D_MODEL), lambda i: (i, O_GB // D_MODEL)),
            full((1, 2 * D_MODEL)), full((512, D_MODEL)), full((512, D_MODEL)), full((D_MODEL, D_MODEL)),
            full((1, D_MODEL)), full((D_MODEL, 256)), full((1, 128)),
        ],
        out_specs=[
            pl.BlockSpec((tm, D_MODEL), lambda i: (i, 0)),
            pl.BlockSpec((tm, D_MODEL), lambda i: (i, 0)),
            pl.BlockSpec((tm, 128), lambda i: (i, 0)),
        ],
        out_shape=[jax.ShapeDtypeStruct((n, D_MODEL), F32), jax.ShapeDtypeStruct((n, D_MODEL), BF16),
                   jax.ShapeDtypeStruct((n, 128), F32)],
        scratch_shapes=[pltpu.VMEM((512, D_MODEL), BF16), pltpu.VMEM((512, D_MODEL), BF16),
                        pltpu.VMEM((D_MODEL, D_MODEL), BF16)],
        compiler_params=pltpu.CompilerParams(dimension_semantics=("arbitrary",),
                                             vmem_limit_bytes=VMEM_LIMIT),
        name="merge",
    )(x2d, oa, ob, z2d, z2d, b_gate, wa, wb, wo, n2, wr, br)


R_GROUP, R_EXPERT = 0, 8


def _routing_weights(rl):
    t = rl.T
    n_tok = t.shape[1]
    gl = t[R_GROUP:R_GROUP + N_GROUPS]
    gidx = lax.broadcasted_iota(jnp.int32, gl.shape, 0)
    gmax = jnp.max(gl, axis=0, keepdims=True)
    gsel = jnp.min(jnp.where(gl == gmax, gidx, N_GROUPS), axis=0, keepdims=True)
    ggate = 1.0 / jnp.sum(jnp.exp(gl - gmax), axis=0, keepdims=True)
    el = t[R_EXPERT:R_EXPERT + N_EXPERTS]
    eidx = lax.broadcasted_iota(jnp.int32, el.shape, 0)
    e_lo = gsel * EXPERTS_PER_GROUP
    emask = (eidx >= e_lo) & (eidx < e_lo + EXPERTS_PER_GROUP)
    el = jnp.where(emask, el, -jnp.inf)
    emax = jnp.max(el, axis=0, keepdims=True)
    ee = jnp.where(emask, jnp.exp(el - emax), 0.0)
    ep = jnp.where(emask, ee / jnp.sum(ee, axis=0, keepdims=True), -1.0)
    p1 = jnp.max(ep, axis=0, keepdims=True)
    i1 = jnp.min(jnp.where(ep == p1, eidx, N_EXPERTS), axis=0, keepdims=True)
    ep2 = jnp.where(eidx == i1, -1.0, ep)
    p2 = jnp.max(ep2, axis=0, keepdims=True)
    i2 = jnp.min(jnp.where(ep2 == p2, eidx, N_EXPERTS), axis=0, keepdims=True)
    tot = p1 + p2
    comb_t = (jnp.where(eidx == i1, ggate * (p1 / tot), 0.0)
              + jnp.where(eidx == i2, ggate * (p2 / tot), 0.0))
    full = jnp.concatenate([jnp.zeros((R_EXPERT, n_tok), F32), comb_t,
                            jnp.zeros((128 - R_EXPERT - N_EXPERTS, n_tok), F32)], axis=0)
    return full.T


def _moe_kernel(x1_ref, h2_ref, rl_ref, w1_ref, w3_ref, w2_ref, fg_ref, o_ref, y_ref, comb_ref):
    step = pl.program_id(1)
    per_step = w1_ref.shape[0]

    @pl.when(step == 0)
    def _():
        comb_ref[...] = _routing_weights(rl_ref[...])
        y_ref[...] = jnp.zeros(y_ref.shape, F32)

    h = h2_ref[...]
    comb = comb_ref[...]
    lane = lax.broadcasted_iota(jnp.int32, comb.shape, 1)
    acts = []
    for j in range(per_step):
        a = _dot(h, w1_ref[j])
        b = _dot(h, w3_ref[j])
        ce = jnp.sum(jnp.where(lane == R_EXPERT + step * per_step + j, comb, 0.0), axis=1, keepdims=True)
        acts.append(((a * jax.nn.sigmoid(a)) * b * ce).astype(BF16))
    y_ref[...] += _dot(jnp.concatenate(acts, axis=1), w2_ref[...])

    @pl.when(step == pl.num_programs(1) - 1)
    def _():
        o_ref[...] = _rmsnorm_rows(x1_ref[...] + y_ref[...], fg_ref[...])


def _moe(x1, h2, rl, w1, w3, w2, fg, tm=1024, per_step=4):
    n = x1.shape[0]
    assert n % tm == 0 and N_EXPERTS % per_step == 0 and w1.shape == (N_EXPERTS, D_MODEL, D_EXPERT)
    w2g =w2.reshape(N_EXPERTS // per_step, per_step * D_EXPERT, D_MODEL)
    return pl.pallas_call(
        _moe_kernel,
        grid=(n // tm, N_EXPERTS // per_step),
        in_specs=[
            pl.BlockSpec((tm, D_MODEL), lambda i, e: (i, 0)),
            pl.BlockSpec((tm, D_MODEL), lambda i, e: (i, 0)),
            pl.BlockSpec((tm, 128), lambda i, e: (i, 0)),
            pl.BlockSpec((per_step, D_MODEL, D_EXPERT), lambda i, e: (e, 0, 0)),
            pl.BlockSpec((per_step, D_MODEL, D_EXPERT), lambda i, e: (e, 0, 0)),
            pl.BlockSpec((None, per_step * D_EXPERT, D_MODEL), lambda i, e: (e, 0, 0)),
            pl.BlockSpec((1, D_MODEL), lambda i, e: (0, 0)),
        ],
        out_specs=pl.BlockSpec((tm, D_MODEL), lambda i, e: (i, 0)),
        out_shape=jax.ShapeDtypeStruct((n, D_MODEL), F32),
        scratch_shapes=[pltpu.VMEM((tm, D_MODEL), F32), pltpu.VMEM((tm, 128), F32)],
        compiler_params=pltpu.CompilerParams(dimension_semantics=("parallel", "arbitrary"),
                                             vmem_limit_bytes=VMEM_LIMIT),
        name="moe",
    )(x1, h2, rl, w1, w3, w2g, fg)


_W_OFF = {}
_off = 0
for _name, _n in (("a_q", 512), ("a_k", 512), ("a_v", 512), ("a_z", 512), ("a_beta", 8), ("a_alpha", 8),
                  ("b_q", 512), ("b_k", 128), ("b_v", 128), ("i_q", 512), ("i_k", 64), ("i_w", 8),
                  ("gate_a", 1024), ("gate_b", 1024)):
    _W_OFF[_name] = (_off, _off + _n)
    _off += _n


def _hi_lo_pair(w_small):
    pad = jnp.zeros((w_small.shape[0], 128 - w_small.shape[1]), F32)
    w = jnp.concatenate([w_small.astype(F32), pad], axis=1)
    hi, lo = _split(w)
    return jnp.concatenate([hi, lo], axis=1)


def _transpose_cast_kernel(off_ref, wt_ref, out_ref):
    del off_ref
    blk = wt_ref[...]
    reps = out_ref.shape[1] // blk.shape[0]
    if reps > 1:
        blk = jnp.concatenate([blk] * reps, axis=0)
    out_ref[...] = blk.T.astype(BF16)


def _gather_columns(wt, offsets, rows, out_cols):
    d = wt.shape[1]
    grid_spec = pltpu.PrefetchScalarGridSpec(
        num_scalar_prefetch=1,
        grid=(len(offsets),),
        in_specs=[pl.BlockSpec((pl.Element(rows), pl.Element(d)), lambda u, off: (pl.multiple_of(off[u], 8), 0))],
        out_specs=pl.BlockSpec((d, out_cols), lambda u, off: (0, u)),
    )
    return pl.pallas_call(
        _transpose_cast_kernel,
        grid_spec=grid_spec,
        out_shape=jax.ShapeDtypeStruct((d, out_cols * len(offsets)), BF16),
        compiler_params=pltpu.CompilerParams(dimension_semantics=("arbitrary",), vmem_limit_bytes=VMEM_LIMIT),
        name="w_layout",
    )(jnp.asarray(offsets, jnp.int32), wt)


def _layout_w_in(w):
    wt = jnp.swapaxes(w, 0, 1)
    start = lambda name: _W_OFF[name][0]
    assert all(start(nm) % 8 == 0 for nm in ("a_q", "gate_a", "b_q", "i_q", "b_k", "b_v", "i_k"))
    main_off = ([start("a_q") + 512 * i for i in range(4)] + [start("gate_a") + 512 * i for i in range(4)]
                + [start("b_q"), start("i_q")])
    tail_off = [start("b_k"), start("b_k") + 64, start("b_v"), start("b_v") + 64, start("i_k")]
    main = _gather_columns(wt, main_off, 512, 512)
    tail = _gather_columns(wt, tail_off, 64, 128)
    small_t = jnp.concatenate([wt[slice(*_W_OFF[nm])] for nm in ("a_beta", "a_alpha", "i_w")], axis=0)
    small = _hi_lo_pair(jnp.swapaxes(small_t, 0, 1))
    return main, tail, small


def kernel(x, positions, norm1_g, w_in, b_gate, conv_w, a_log, dt_bias, a_norm_g, w_proj_a, w_proj_b, w_out,
           norm2_g, w_router_group, b_router_group, w_router_expert, b_router_expert, w_exp_gate, w_exp_up,
           w_exp_down, final_norm_g):
    b, s, d = x.shape
    n = b * s
    assert w_in.shape[0] == 1 and d == D_MODEL, "one layer: the MoE kernel also applies the final norm"
    xc = x.reshape(n, d).astype(F32)
    for l in range(w_in.shape[0]):
        w_main, w_tail, w_small = _layout_w_in(w_in[l])
        z, zs = _in_proj(xc, norm1_g[l][None, :].astype(F32), w_main, w_tail, w_small, positions)
        z3 = z.reshape(b, s, Z_W)
        zs3 = zs.reshape(b, s, 128)
        o_a = _gdn(z3, zs3, conv_w[l], a_log[l], dt_bias[l], a_norm_g[l])
        o_b = _dsa(z3, zs3)
        gap = R_EXPERT - N_GROUPS
        wr = _hi_lo_pair(jnp.concatenate([w_router_group[l], jnp.zeros((d, gap), F32), w_router_expert[l]],
                                         axis=1))
        br = jnp.concatenate([b_router_group[l], jnp.zeros((gap,), F32), b_router_expert[l],
                              jnp.zeros((128 - R_EXPERT - N_EXPERTS,), F32)])[None, :].astype(F32)
        x1, h2, rl = _merge(xc, o_a.reshape(n, -1), o_b.reshape(n, -1), z, b_gate[l][None, :].astype(F32),
                            w_proj_a[l], w_proj_b[l], w_out[l],
                            norm2_g[l][None, :].astype(F32), wr, br)
        xc = _moe(x1, h2, rl, w_exp_gate[l].astype(BF16), w_exp_up[l].astype(BF16),
                  w_exp_down[l].astype(BF16), final_norm_g[None, :].astype(F32))
    return xc.reshape(b, s, d).astype(x.dtype)
```
